```python
import math
import jax, jax.numpy as jnp
from jax import lax
import numpy as np

D_MODEL = 1024
BATCH = 32
SEQ = 256
DEPTH = 2
DEC_BATCH = 8
DEC_SEQ = 1024
PAST_LEN = 512

GRID_W = 64
HEAD_DIM = 64
A_HEADS = 8
B_HEADS = 8
B_KV_HEADS = 2
C_HEADS = 16
C_KV_HEADS = 4
NA_WIN_H = 8
NA_WIN_W = 16
B_WINDOW = 128
Q_BLOCK = 128
D_FF = ((8 * D_MODEL // 3 + 255) // 256) * 256
ROPE_THETA = 10000.0
RMS_EPS = 1e-6
NEG_INF = -1e30
N_EVEN = (DEPTH + 1) // 2
N_ODD = DEPTH // 2
EVEN_HEADS = (A_HEADS, A_HEADS, A_HEADS, B_HEADS, B_KV_HEADS, B_KV_HEADS)
ODD_HEADS = (C_HEADS, C_KV_HEADS, C_KV_HEADS)
EVEN_IN = sum(EVEN_HEADS) * HEAD_DIM
ODD_IN = sum(ODD_HEADS) * HEAD_DIM
EVEN_OUT = (A_HEADS + B_HEADS) * HEAD_DIM
ODD_OUT = C_HEADS * HEAD_DIM

kernel_name = 'hybrid_prefix_diffusion_step'


def rms_norm(x, g):
    xf = x.astype(jnp.float32)
    y = xf * lax.rsqrt(jnp.mean(xf * xf, axis=-1, keepdims=True) + RMS_EPS)
    return y.astype(x.dtype) * g


def modulation(cvec, w_mod, b_mod):
    m = jax.nn.silu(cvec) @ w_mod + b_mod
    return jnp.split(m[:, None, :], 6, axis=-1)


def adaln(x, g, shift, scale):
    return rms_norm(x, g) * (1 + scale) + shift


def split_heads(p, heads):
    b, n, _ = p.shape
    bounds = [int(i) for i in np.cumsum([h * HEAD_DIM for h in heads])[:-1]]
    parts = jnp.split(p, bounds, axis=-1)
    return [t.reshape(b, n, h, HEAD_DIM) for t, h in zip(parts, heads)]


def axial_rope(x):
    n = x.shape[1]
    t = jnp.arange(n)
    row = (t // GRID_W).astype(jnp.float32)
    col = (t % GRID_W).astype(jnp.float32)
    half = HEAD_DIM // 2
    inv_freq = ROPE_THETA ** (-jnp.arange(0, half, 2, dtype=jnp.float32) / half)

    def rot(xa, pos):
        ang = pos[:, None] * inv_freq[None, :]
        cos = jnp.cos(ang)[None, :, None, :].astype(x.dtype)
        sin = jnp.sin(ang)[None, :, None, :].astype(x.dtype)
        x1, x2 = xa[..., :half // 2], xa[..., half // 2:]
        return jnp.concatenate([x1 * cos - x2 * sin, x2 * cos + x1 * sin], axis=-1)

    return jnp.concatenate([rot(x[..., :half], row), rot(x[..., half:], col)], axis=-1)


def softmax_sink(s, sink):
    if sink is None:
        return jax.nn.softmax(s, axis=-1)
    sk = jnp.broadcast_to(sink.astype(jnp.float32), s.shape[:-1] + (1,))
    p = jax.nn.softmax(jnp.concatenate([s, sk], axis=-1), axis=-1)
    return p[..., :-1]


def block_attention(q, k, v, sink=None):
    b, nq, hq, d = q.shape
    hkv = k.shape[2]
    g = hq // hkv
    nb = nq // Q_BLOCK
    scale = 1.0 / math.sqrt(d)
    qb = q.reshape(b, nb, Q_BLOCK, hkv, g, d).transpose(1, 0, 2, 3, 4, 5)
    sink_b = None if sink is None else sink.reshape(hkv, g)[None, :, :, None, None]

    def one(qblk):
        s = jnp.einsum('bqhgd,bkhd->bhgqk', qblk, k).astype(jnp.float32) * scale
        p = softmax_sink(s, sink_b)
        return jnp.einsum('bhgqk,bkhd->bqhgd', p.astype(v.dtype), v)

    out = lax.map(one, qb)
    return out.transpose(1, 0, 2, 3, 4, 5).reshape(b, nq, hq, d)


def window_attention(q, k, v, k_ctx, v_ctx, sink):
    b, n, hq, d = q.shape
    hkv = k.shape[2]
    g = hq // hkv
    nb = n // Q_BLOCK
    scale = 1.0 / math.sqrt(d)
    pad = ((0, 0), (Q_BLOCK, Q_BLOCK), (0, 0), (0, 0))
    kp = jnp.pad(k, pad).reshape(b, nb + 2, Q_BLOCK, hkv, d)
    vp = jnp.pad(v, pad).reshape(b, nb + 2, Q_BLOCK, hkv, d)

    def band(t):
        return jnp.concatenate([t[:, :-2], t[:, 1:-1], t[:, 2:]], axis=2)

    kb, vb = band(kp), band(vp)
    qb = q.reshape(b, nb, Q_BLOCK, hkv, g, d)
    s_loc = jnp.einsum('bnqhgd,bnjhd->bnhgqj', qb, kb).astype(jnp.float32) * scale
    blk = jnp.arange(nb)[:, None, None]
    qpos = blk * Q_BLOCK + jnp.arange(Q_BLOCK)[None, :, None]
    kpos = (blk - 1) * Q_BLOCK + jnp.arange(3 * Q_BLOCK)[None, None, :]
    valid = (kpos >= 0) & (kpos < n) & (jnp.abs(qpos - kpos) <= B_WINDOW)
    s_loc = jnp.where(valid[None, :, None, None], s_loc, NEG_INF)
    s_ctx = jnp.einsum('bnqhgd,blhd->bnhgql', qb, k_ctx).astype(jnp.float32) * scale
    p = softmax_sink(jnp.concatenate([s_loc, s_ctx], axis=-1), sink.reshape(hkv, g)[None, None, :, :, None, None])
    p = p.astype(v.dtype)
    p_loc, p_ctx = p[..., :3 * Q_BLOCK], p[..., 3 * Q_BLOCK:]
    o = (jnp.einsum('bnhgqj,bnjhd->bnqhgd', p_loc, vb)
         + jnp.einsum('bnhgql,blhd->bnqhgd', p_ctx, v_ctx))
    return o.reshape(b, n, hq, d)


def neighbourhood_attention(q, k, v, k_ctx, v_ctx, rpb):
    b, n, h, d = q.shape
    rows = n // GRID_W
    wh = min(NA_WIN_H, rows)
    ww = NA_WIN_W
    scale = 1.0 / math.sqrt(d)
    qg = q.reshape(b, rows, GRID_W, h, d)
    kg = k.reshape(b, rows, GRID_W, h, d)
    vg = v.reshape(b, rows, GRID_W, h, d)
    cols = jnp.arange(GRID_W)
    cstart = jnp.clip(cols - ww // 2, 0, GRID_W - ww)
    col_idx = cstart[:, None] + jnp.arange(ww)[None, :]
    col_off = col_idx - cols[:, None] + (NA_WIN_W - 1)
    bias_cols = rpb[:, :, col_off]

    def one_row(r):
        rs = jnp.clip(r - wh // 2, 0, rows - wh)
        q_r = lax.dynamic_index_in_dim(qg, r, axis=1, keepdims=False)
        k_r = lax.dynamic_slice_in_dim(kg, rs, wh, axis=1)[:, :, col_idx]
        v_r = lax.dynamic_slice_in_dim(vg, rs, wh, axis=1)[:, :, col_idx]
        row_off = rs + jnp.arange(wh) - r + (NA_WIN_H - 1)
        bias = bias_cols[:, row_off].transpose(0, 2, 1, 3)
        s_nb = jnp.einsum('bchd,bwcxhd->bhcwx', q_r, k_r).astype(jnp.float32) * scale + bias[None].astype(jnp.float32)
        s_nb = s_nb.reshape(b, h, GRID_W, wh * ww)
        s_ctx = jnp.einsum('bchd,blhd->bhcl', q_r, k_ctx).astype(jnp.float32) * scale
        p = jax.nn.softmax(jnp.concatenate([s_nb, s_ctx], axis=-1), axis=-1).astype(v.dtype)
        p_nb = p[..., :wh * ww].reshape(b, h, GRID_W, wh, ww)
        p_ctx = p[..., wh * ww:]
        return (jnp.einsum('bhcwx,bwcxhd->bchd', p_nb, v_r)
                + jnp.einsum('bhcl,blhd->bchd', p_ctx, v_ctx))

    out = lax.map(one_row, jnp.arange(rows))
    return out.transpose(1, 0, 2, 3, 4).reshape(b, n, h, d)


def even_mix_context(h, w_in, w_out, sink):
    b, n, _ = h.shape
    qa, ka, va, qb, kb, vb = split_heads(h @ w_in, EVEN_HEADS)
    oa = block_attention(qa, ka, va)
    ob = block_attention(qb, kb, vb, sink)
    out = jnp.concatenate([oa.reshape(b, n, -1), ob.reshape(b, n, -1)], axis=-1) @ w_out
    return out, ka, va, kb, vb


def even_mix_latent(h, ka_ctx, va_ctx, kb_ctx, vb_ctx, w_in, w_out, rpb, sink):
    b, n, _ = h.shape
    qa, ka, va, qb, kb, vb = split_heads(h @ w_in, EVEN_HEADS)
    oa = neighbourhood_attention(qa, ka, va, ka_ctx, va_ctx, rpb)
    ob = window_attention(axial_rope(qb), axial_rope(kb), vb, kb_ctx, vb_ctx, sink)
    return jnp.concatenate([oa.reshape(b, n, -1), ob.reshape(b, n, -1)], axis=-1) @ w_out


def odd_mix_context(h, w_in, w_out, q_norm, k_norm):
    b, n, _ = h.shape
    q, k, v = split_heads(h @ w_in, ODD_HEADS)
    q = rms_norm(q, q_norm)
    k = rms_norm(k, k_norm)
    o = block_attention(q, k, v)
    return o.reshape(b, n, -1) @ w_out, k, v


def odd_mix_latent(h, k_ctx, v_ctx, w_in, w_out, q_norm, k_norm):
    b, n, _ = h.shape
    q, k, v = split_heads(h @ w_in, ODD_HEADS)
    q = axial_rope(rms_norm(q, q_norm))
    k = axial_rope(rms_norm(k, k_norm))
    o = block_attention(q, jnp.concatenate([k_ctx, k], axis=1), jnp.concatenate([v_ctx, v], axis=1))
    return o.reshape(b, n, -1) @ w_out


def swiglu(h, w_gate_up, w_down):
    gate, up = jnp.split(h @ w_gate_up, 2, axis=-1)
    return (jax.nn.silu(gate) * up) @ w_down


def setup_inputs(seed: int = 0) -> dict:
    key = jax.random.key(seed)
    ks = jax.random.split(key, 26)

    def nrm(k, shape, s):
        return jax.random.normal(k, shape, jnp.float32) * s

    D = D_MODEL
    return {
        'x_prompt': nrm(ks[0], (BATCH, SEQ, D), 1.0),
        'x_sample': nrm(ks[1], (DEC_BATCH, DEC_SEQ, D), 1.0),
        'cache_a_k': nrm(ks[2], (DEC_BATCH, N_EVEN, PAST_LEN, A_HEADS, HEAD_DIM), 1.0),
        'cache_a_v': nrm(ks[3], (DEC_BATCH, N_EVEN, PAST_LEN, A_HEADS, HEAD_DIM), 1.0),
        'cache_b_k': nrm(ks[4], (DEC_BATCH, N_EVEN, PAST_LEN, B_KV_HEADS, HEAD_DIM), 1.0),
        'cache_b_v': nrm(ks[5], (DEC_BATCH, N_EVEN, PAST_LEN, B_KV_HEADS, HEAD_DIM), 1.0),
        'cache_c_k': nrm(ks[6], (DEC_BATCH, N_ODD, PAST_LEN, C_KV_HEADS, HEAD_DIM), 1.0),
        'cache_c_v': nrm(ks[7], (DEC_BATCH, N_ODD, PAST_LEN, C_KV_HEADS, HEAD_DIM), 1.0),
        'c': nrm(ks[8], (DEC_BATCH, D), 1.0),
        'c_ctx': nrm(ks[9], (D,), 1.0),
        'norm_gain': 1.0 + nrm(ks[10], (DEPTH, 2, D), 0.02),
        'w_mod': nrm(ks[11], (DEPTH, D, 6 * D), D ** -0.5),
        'b_mod': nrm(ks[12], (DEPTH, 6 * D), 0.02),
        'w_in_even': nrm(ks[13], (N_EVEN, D, EVEN_IN), D ** -0.5),
        'w_out_even': nrm(ks[14], (N_EVEN, EVEN_OUT, D), EVEN_OUT ** -0.5),
        'rpb_a': nrm(ks[15], (N_EVEN, A_HEADS, 2 * NA_WIN_H - 1, 2 * NA_WIN_W - 1), 0.1),
        'sink_b': nrm(ks[16], (N_EVEN, B_HEADS), 0.5),
        'w_in_odd': nrm(ks[17], (N_ODD, D, ODD_IN), D ** -0.5),
        'w_out_odd': nrm(ks[18], (N_ODD, ODD_OUT, D), ODD_OUT ** -0.5),
        'q_norm_c': 1.0 + nrm(ks[19], (N_ODD, HEAD_DIM), 0.02),
        'k_norm_c': 1.0 + nrm(ks[20], (N_ODD, HEAD_DIM), 0.02),
        'w_gate_up': nrm(ks[21], (DEPTH, D, 2 * D_FF), D ** -0.5),
        'w_down': nrm(ks[22], (DEPTH, D_FF, D), D_FF ** -0.5),
        'final_gain': 1.0 + nrm(ks[23], (D,), 0.02),
    }


def reference(x_prompt, x_sample, cache_a_k, cache_a_v, cache_b_k, cache_b_v, cache_c_k, cache_c_v,
              c, c_ctx, norm_gain, w_mod, b_mod, w_in_even, w_out_even, rpb_a, sink_b,
              w_in_odd, w_out_odd, q_norm_c, k_norm_c, w_gate_up, w_down, final_gain):
    ctx = x_prompt
    lat = x_sample
    a_k, a_v, b_k, b_v, c_k, c_v = [], [], [], [], [], []
    for l in range(DEPTH):
        csh1, csc1, cg1, csh2, csc2, cg2 = modulation(c_ctx[None, :], w_mod[l], b_mod[l])
        lsh1, lsc1, lg1, lsh2, lsc2, lg2 = modulation(c, w_mod[l], b_mod[l])
        hc = adaln(ctx, norm_gain[l, 0], csh1, csc1)
        hl = adaln(lat, norm_gain[l, 0], lsh1, lsc1)
        if l % 2 == 0:
            e = l // 2
            mix_c, ka, va, kb, vb = even_mix_context(hc, w_in_even[e], w_out_even[e], sink_b[e])
            mix_l = even_mix_latent(hl, cache_a_k[:, e], cache_a_v[:, e], cache_b_k[:, e], cache_b_v[:, e],
                                    w_in_even[e], w_out_even[e], rpb_a[e], sink_b[e])
            a_k.append(ka)
            a_v.append(va)
            b_k.append(kb)
            b_v.append(vb)
        else:
            o = l // 2
            mix_c, kc, vc = odd_mix_context(hc, w_in_odd[o], w_out_odd[o], q_norm_c[o], k_norm_c[o])
            mix_l = odd_mix_latent(hl, cache_c_k[:, o], cache_c_v[:, o], w_in_odd[o], w_out_odd[o],
                                   q_norm_c[o], k_norm_c[o])
            c_k.append(kc)
            c_v.append(vc)
        ctx = ctx + cg1 * mix_c
        lat = lat + lg1 * mix_l
        ctx = ctx + cg2 * swiglu(adaln(ctx, norm_gain[l, 1], csh2, csc2), w_gate_up[l], w_down[l])
        lat = lat + lg2 * swiglu(adaln(lat, norm_gain[l, 1], lsh2, lsc2), w_gate_up[l], w_down[l])
    y_prompt = rms_norm(ctx, final_gain)
    y_sample = rms_norm(lat, final_gain)
    state_a_k = jnp.stack(a_k, axis=1)
    state_a_v = jnp.stack(a_v, axis=1)
    state_b_k = jnp.stack(b_k, axis=1)
    state_b_v = jnp.stack(b_v, axis=1)
    state_c_k = jnp.stack(c_k, axis=1)
    state_c_v = jnp.stack(c_v, axis=1)
    return (y_prompt, y_sample, state_a_k, state_a_v, state_b_k, state_b_v, state_c_k, state_c_v)
```

```python
import functools
import math

import jax
import jax.numpy as jnp
from jax import lax
from jax.experimental import pallas as pl
from jax.experimental.pallas import tpu as pltpu

F32 = jnp.float32
BF16 = jnp.bfloat16

D_MODEL = 1024
GRID_W = 64
HEAD_DIM = 64
A_HEADS = 8
B_HEADS = 8
B_KV_HEADS = 2
C_HEADS = 16
C_KV_HEADS = 4
NA_WIN_H = 8
NA_WIN_W = 16
B_WINDOW = 128
ROPE_THETA = 10000.0
RMS_EPS = 1e-6
NEG_INF = -1e30
QK_SCALE = 1.0 / math.sqrt(HEAD_DIM)

LANES = 128
TOKEN_TILE = 512
Q_TILE = 256
FF_CHUNK = 256
MOD_GROUPS = 16
VMEM_LIMIT = 56 * 1024 * 1024


def _params(n_axes, vmem=VMEM_LIMIT):
    return pltpu.CompilerParams(
        dimension_semantics=("arbitrary",) * n_axes, vmem_limit_bytes=vmem)


def _resident(shape):
    nd = len(shape)
    return pl.BlockSpec(shape, lambda *_: (0,) * nd, pipeline_mode=pl.Buffered(1))


def _mod_kernel(c_ref, w_ref, b_ref, o_ref):
    c = c_ref[...]
    s = (c * jax.nn.sigmoid(c)).astype(BF16)
    o_ref[0] = jnp.dot(s, w_ref[0].astype(BF16), preferred_element_type=F32) + b_ref[0]


def _modulation(cvec, w_mod, b_mod):
    depth, d, n = w_mod.shape
    tn = 1536
    return pl.pallas_call(
        _mod_kernel,
        grid=(depth, n // tn),
        in_specs=[
            pl.BlockSpec((MOD_GROUPS, d), lambda l, j: (0, 0)),
            pl.BlockSpec((1, d, tn), lambda l, j: (l, 0, j)),
            pl.BlockSpec((1, 1, tn), lambda l, j: (l, 0, j)),
        ],
        out_specs=pl.BlockSpec((1, MOD_GROUPS, tn), lambda l, j: (l, 0, j)),
        out_shape=jax.ShapeDtypeStruct((depth, MOD_GROUPS, n), F32),
        compiler_params=_params(2),
        name="modulation",
    )(cvec, w_mod, b_mod.reshape(depth, 1, n))


def _mod_spec(layer, which, group_of_step):
    return pl.BlockSpec((1, 1, 1, D_MODEL), lambda i: (layer, group_of_step(i), 0, which))


def _adaln(x, gain, shift, scale):
    ms = jnp.mean(x * x, axis=-1, keepdims=True)
    return (x * lax.rsqrt(ms + RMS_EPS)) * gain * (1.0 + scale) + shift


def _rope(y, cos, sin_lo, sin_hi):
    outs = []
    for c in range(y.shape[1] // LANES):
        yc = y[:, c * LANES:(c + 1) * LANES]
        outs.append(yc * cos
                    + pltpu.roll(yc, LANES - 16, 1) * sin_lo
                    + pltpu.roll(yc, 16, 1) * sin_hi)
    return outs[0] if len(outs) == 1 else jnp.concatenate(outs, axis=1)


def _head_rms_norm(y, gain):
    first = lax.broadcasted_iota(jnp.int32, (1, LANES), 1) < HEAD_DIM
    outs = []
    for c in range(y.shape[1] // LANES):
        yc = y[:, c * LANES:(c + 1) * LANES]
        sq = yc * yc
        s0 = jnp.sum(jnp.where(first, sq, 0.0), axis=-1, keepdims=True)
        s1 = jnp.sum(jnp.where(first, 0.0, sq), axis=-1, keepdims=True)
        ms = jnp.where(first, s0, s1) * (1.0 / HEAD_DIM)
        outs.append(yc * lax.rsqrt(ms + RMS_EPS) * gain)
    return outs[0] if len(outs) == 1 else jnp.concatenate(outs, axis=1)


def _attend(q, segments, sink=None):
    scores = []
    for k, _, bias, valid in segments:
        s = lax.dot_general(q, k, (((1,), (1,)), ((), ())), preferred_element_type=F32)
        if bias is not None:
            s = s + bias
        if valid is not None:
            s = jnp.where(valid, s, NEG_INF)
        scores.append(s)
    m = functools.reduce(jnp.maximum, [jnp.max(s, axis=-1, keepdims=True) for s in scores])
    if sink is not None:
        m = jnp.maximum(m, sink)
    denom = None
    out = None
    for s, (_, v, _, _) in zip(scores, segments):
        p = jnp.exp(s - m)
        part = jnp.sum(p, axis=-1, keepdims=True)
        pv = jnp.dot(p.astype(BF16), v, preferred_element_type=F32)
        denom = part if denom is None else denom + part
        out = pv if out is None else out + pv
    if sink is not None:
        denom = denom + jnp.exp(sink - m)
    return out / denom


def _stack_group(ref, rows, col0, group):
    return jnp.concatenate(
        [ref[rows, col0 + g * HEAD_DIM: col0 + (g + 1) * HEAD_DIM] for g in range(group)], axis=0)


def _unstack_group(o, rows, group):
    return jnp.concatenate([o[g * rows:(g + 1) * rows] for g in range(group)], axis=1)


def _pre_even_kernel(x_ref, g_ref, sh_ref, sc_ref, w_ref, *rest, is_lat):
    if is_lat:
        cos_ref, slo_ref, shi_ref, qkv_ref = rest
    else:
        qkv_ref, ka_ref, va_ref, kb_ref, vb_ref = rest
    h = _adaln(x_ref[...], g_ref[...], sh_ref[0, 0], sc_ref[0, 0]).astype(BF16)

    def proj(c0, c1):
        return jnp.dot(h, w_ref[:, c0:c1], preferred_element_type=F32)

    na = A_HEADS * HEAD_DIM
    nb = B_HEADS * HEAD_DIM
    nkv = B_KV_HEADS * HEAD_DIM
    c = 0
    qkv_ref[:, c:c + na] = (proj(c, c + na) * QK_SCALE).astype(BF16)
    c += na
    ka = proj(c, c + na)
    qkv_ref[:, c:c + na] = ka.astype(BF16)
    c += na
    va = proj(c, c + na)
    qkv_ref[:, c:c + na] = va.astype(BF16)
    c += na
    qb = proj(c, c + nb)
    if is_lat:
        qb = _rope(qb, cos_ref[...], slo_ref[...], shi_ref[...])
    qkv_ref[:, c:c + nb] = (qb * QK_SCALE).astype(BF16)
    c += nb
    kb = proj(c, c + nkv)
    if is_lat:
        kb_out = _rope(kb, cos_ref[...], slo_ref[...], shi_ref[...])
    else:
        kb_out = kb
    qkv_ref[:, c:c + nkv] = kb_out.astype(BF16)
    c += nkv
    vb = proj(c, c + nkv)
    qkv_ref[:, c:c + nkv] = vb.astype(BF16)
    if not is_lat:
        ka_ref[...] = ka
        va_ref[...] = va
        kb_ref[...] = kb
        vb_ref[...] = vb


def _pre_odd_kernel(x_ref, g_ref, sh_ref, sc_ref, w_ref, qn_ref, kn_ref, *rest, is_lat):
    if is_lat:
        cos_ref, slo_ref, shi_ref, qkv_ref = rest
    else:
        qkv_ref, kc_ref, vc_ref = rest
    h = _adaln(x_ref[...], g_ref[...], sh_ref[0, 0], sc_ref[0, 0]).astype(BF16)

    def proj(c0, c1):
        return jnp.dot(h, w_ref[:, c0:c1], preferred_element_type=F32)

    nq = C_HEADS * HEAD_DIM
    nkv = C_KV_HEADS * HEAD_DIM
    q = _head_rms_norm(proj(0, nq), qn_ref[...])
    k = _head_rms_norm(proj(nq, nq + nkv), kn_ref[...])
    v = proj(nq + nkv, nq + 2 * nkv)
    if is_lat:
        q = _rope(q, cos_ref[...], slo_ref[...], shi_ref[...])
        k_out = _rope(k, cos_ref[...], slo_ref[...], shi_ref[...])
    else:
        k_out = k
        kc_ref[...] = k
        vc_ref[...] = v
    qkv_ref[:, 0:nq] = (q * QK_SCALE).astype(BF16)
    qkv_ref[:, nq:nq + nkv] = k_out.astype(BF16)
    qkv_ref[:, nq + nkv:nq + 2 * nkv] = v.astype(BF16)


def _pre_attention(x, mods, layer, gain, w, *, odd, is_lat, seq, rope=None, head_gains=None):
    t, d = x.shape
    n = w.shape[1]
    tm = TOKEN_TILE
    per_seq = seq // tm
    group = (lambda i: 1 + i // per_seq) if is_lat else (lambda i: 0)
    row = lambda i: (i, 0)
    in_specs = [
        pl.BlockSpec((tm, d), row),
        _resident((1, d)),
        _mod_spec(layer, 0, group),
        _mod_spec(layer, 1, group),
        _resident((d, n)),
    ]
    args = [x, gain.reshape(1, d), mods, mods, w]
    if odd:
        in_specs += [_resident((1, LANES)), _resident((1, LANES))]
        args += list(head_gains)
    if is_lat:
        in_specs += [pl.BlockSpec((tm, LANES), lambda i: (i % per_seq, 0))] * 3
        args += list(rope)
    out_specs = [pl.BlockSpec((tm, n), row)]
    out_shape = [jax.ShapeDtypeStruct((t, n), BF16)]
    if not is_lat:
        if odd:
            widths = [C_KV_HEADS * HEAD_DIM] * 2
        else:
            widths = [A_HEADS * HEAD_DIM] * 2 + [B_KV_HEADS * HEAD_DIM] * 2
        out_specs += [pl.BlockSpec((tm, wd), row) for wd in widths]
        out_shape += [jax.ShapeDtypeStruct((t, wd), F32) for wd in widths]
    body = _pre_odd_kernel if odd else _pre_even_kernel
    return pl.pallas_call(
        functools.partial(body, is_lat=is_lat),
        grid=(t // tm,),
        in_specs=in_specs,
        out_specs=out_specs,
        out_shape=out_shape,
        compiler_params=_params(1),
        name=f"pre_{'odd' if odd else 'even'}_{'lat' if is_lat else 'ctx'}",
    )(*args)


def _ctx_even_attn_kernel(sink_ref, qkv_ref, o_ref):
    rows = slice(None)
    n = qkv_ref.shape[0]
    na = A_HEADS * HEAD_DIM
    for pair in range(A_HEADS // 2):
        outs = []
        for h in (2 * pair, 2 * pair + 1):
            c = h * HEAD_DIM
            q = qkv_ref[:, c:c + HEAD_DIM]
            k = qkv_ref[:, na + c:na + c + HEAD_DIM]
            v = qkv_ref[:, 2 * na + c:2 * na + c + HEAD_DIM]
            outs.append(_attend(q, [(k, v, None, None)]))
        o_ref[:, pair * LANES:(pair + 1) * LANES] = jnp.concatenate(outs, axis=1).astype(BF16)
    group = B_HEADS // B_KV_HEADS
    qb0 = 3 * na
    kb0 = qb0 + B_HEADS * HEAD_DIM
    vb0 = kb0 + B_KV_HEADS * HEAD_DIM
    for kv in range(B_KV_HEADS):
        q = _stack_group(qkv_ref, rows, qb0 + kv * group * HEAD_DIM, group)
        k = qkv_ref[:, kb0 + kv * HEAD_DIM:kb0 + (kv + 1) * HEAD_DIM]
        v = qkv_ref[:, vb0 + kv * HEAD_DIM:vb0 + (kv + 1) * HEAD_DIM]
        sink = jnp.concatenate(
            [jnp.full((n, 1), sink_ref[kv * group + g], F32) for g in range(group)], axis=0)
        o = _attend(q, [(k, v, None, None)], sink)
        c0 = na + kv * group * HEAD_DIM
        o_ref[:, c0:c0 + group * HEAD_DIM] = _unstack_group(o, n, group).astype(BF16)


def _ctx_even_attention(qkv, sink, seq):
    t, n = qkv.shape
    width = (A_HEADS + B_HEADS) * HEAD_DIM
    return pl.pallas_call(
        _ctx_even_attn_kernel,
        grid=(t // seq,),
        in_specs=[pl.BlockSpec(memory_space=pltpu.SMEM),
                  pl.BlockSpec((seq, n), lambda b: (b, 0))],
        out_specs=pl.BlockSpec((seq, width), lambda b: (b, 0)),
        out_shape=jax.ShapeDtypeStruct((t, width), BF16),
        compiler_params=_params(1),
        name="attn_even_ctx",
    )(sink, qkv)


def _na_key_window(j):
    return min(max(Q_TILE * j - Q_TILE, 0), Q_TILE)


def _lat_a_kernel(q_ref, k_ref, v_ref, ck_ref, cv_ref, bias_ref, o_ref):
    ck = ck_ref[0, 0].astype(BF16)
    cv = cv_ref[0, 0].astype(BF16)
    n = q_ref.shape[0]
    win = 3 * Q_TILE
    for j in range(n // Q_TILE):
        lo = _na_key_window(j)
        q_rows = slice(j * Q_TILE, (j + 1) * Q_TILE)
        outs = []
        for hh in range(2):
            cols = slice(hh * HEAD_DIM, (hh + 1) * HEAD_DIM)
            local = (k_ref[lo:lo + win, cols], v_ref[lo:lo + win, cols],
                     bias_ref[hh, q_rows, lo:lo + win], None)
            ctx = (ck[:, cols], cv[:, cols], None, None)
            outs.append(_attend(q_ref[q_rows, cols], [local, ctx]))
        o_ref[q_rows, :] = jnp.concatenate(outs, axis=1).astype(BF16)


def _flat_heads(cache):
    b, layers, past, heads, hd = cache.shape
    return cache.reshape(b, layers, past, heads * hd)


def _lat_a_attention(qkv, cache_k, cache_v, e, bias, seq):
    t = qkv.shape[0]
    past = cache_k.shape[2]
    pairs = A_HEADS // 2
    return pl.pallas_call(
        _lat_a_kernel,
        grid=(pairs, t // seq),
        in_specs=[
            pl.BlockSpec((seq, LANES), lambda p, b: (b, p)),
            pl.BlockSpec((seq, LANES), lambda p, b: (b, pairs + p)),
            pl.BlockSpec((seq, LANES), lambda p, b: (b, 2 * pairs + p)),
            pl.BlockSpec((1, 1, past, LANES), lambda p, b: (b, e, 0, p)),
            pl.BlockSpec((1, 1, past, LANES), lambda p, b: (b, e, 0, p)),
            pl.BlockSpec((2, seq, seq), lambda p, b: (p, 0, 0)),
        ],
        out_specs=pl.BlockSpec((seq, LANES), lambda p, b: (b, p)),
        out_shape=jax.ShapeDtypeStruct((t, A_HEADS * HEAD_DIM), BF16),
        compiler_params=_params(2),
        name="attn_even_lat_a",
    )(qkv, qkv, qkv, cache_k, cache_v, bias)


def _lat_b_kernel(sink_ref, q_ref, k_ref, v_ref, ck_ref, cv_ref, o_ref):
    ck = ck_ref[0, 0].astype(BF16)
    cv = cv_ref[0, 0].astype(BF16)
    n = q_ref.shape[0]
    group = B_HEADS // B_KV_HEADS
    win = 2 * Q_TILE
    for j in range(n // Q_TILE):
        lo = min(max(Q_TILE * j - B_WINDOW, 0), n - win)
        q_rows = slice(j * Q_TILE, (j + 1) * Q_TILE)
        qpos = j * Q_TILE + lax.broadcasted_iota(jnp.int32, (group * Q_TILE, 1), 0) % Q_TILE
        kpos = lo + lax.broadcasted_iota(jnp.int32, (1, win), 1)
        valid = jnp.abs(qpos - kpos) <= B_WINDOW
        for kv in range(B_KV_HEADS):
            cols = slice(kv * HEAD_DIM, (kv + 1) * HEAD_DIM)
            q = _stack_group(q_ref, q_rows, kv * group * HEAD_DIM, group)
            sink = jnp.concatenate(
                [jnp.full((Q_TILE, 1), sink_ref[kv * group + g], F32) for g in range(group)], axis=0)
            local = (k_ref[lo:lo + win, cols], v_ref[lo:lo + win, cols], None, valid)
            ctx = (ck[:, cols], cv[:, cols], None, None)
            o = _attend(q, [local, ctx], sink)
            c0 = kv * group * HEAD_DIM
            o_ref[q_rows, c0:c0 + group * HEAD_DIM] = _unstack_group(o, Q_TILE, group).astype(BF16)


def _lat_b_attention(qkv, cache_k, cache_v, e, sink, seq):
    t = qkv.shape[0]
    past = cache_k.shape[2]
    na = A_HEADS * HEAD_DIM
    nb = B_HEADS * HEAD_DIM
    nkv = B_KV_HEADS * HEAD_DIM
    return pl.pallas_call(
        _lat_b_kernel,
        grid=(t // seq,),
        in_specs=[
            pl.BlockSpec(memory_space=pltpu.SMEM),
            pl.BlockSpec((seq, nb), lambda b: (b, 3 * na // nb)),
            pl.BlockSpec((seq, nkv), lambda b: (b, (3 * na + nb) // nkv)),
            pl.BlockSpec((seq, nkv), lambda b: (b, (3 * na + nb) // nkv + 1)),
            pl.BlockSpec((1, 1, past, nkv), lambda b: (b, e, 0, 0)),
            pl.BlockSpec((1, 1, past, nkv), lambda b: (b, e, 0, 0)),
        ],
        out_specs=pl.BlockSpec((seq, nb), lambda b: (b, 0)),
        out_shape=jax.ShapeDtypeStruct((t, nb), BF16),
        compiler_params=_params(1),
        name="attn_even_lat_b",
    )(sink, qkv, qkv, qkv, cache_k, cache_v)


def _ctx_odd_attn_kernel(qkv_ref, o_ref):
    n = qkv_ref.shape[0]
    group = C_HEADS // C_KV_HEADS
    nq = C_HEADS * HEAD_DIM
    nkv = C_KV_HEADS * HEAD_DIM
    for kv in range(C_KV_HEADS):
        q = _stack_group(qkv_ref, slice(None), kv * group * HEAD_DIM, group)
        k = qkv_ref[:, nq + kv * HEAD_DIM:nq + (kv + 1) * HEAD_DIM]
        v = qkv_ref[:, nq + nkv + kv * HEAD_DIM:nq + nkv + (kv + 1) * HEAD_DIM]
        o = _attend(q, [(k, v, None, None)])
        c0 = kv * group * HEAD_DIM
        o_ref[:, c0:c0 + group * HEAD_DIM] = _unstack_group(o, n, group).astype(BF16)


def _ctx_odd_attention(qkv, seq):
    t, n = qkv.shape
    width = C_HEADS * HEAD_DIM
    return pl.pallas_call(
        _ctx_odd_attn_kernel,
        grid=(t // seq,),
        in_specs=[pl.BlockSpec((seq, n), lambda b: (b, 0))],
        out_specs=pl.BlockSpec((seq, width), lambda b: (b, 0)),
        out_shape=jax.ShapeDtypeStruct((t, width), BF16),
        compiler_params=_params(1),
        name="attn_odd_ctx",
    )(qkv)


def _lat_c_kernel(qkv_ref, ck_ref, cv_ref, o_ref):
    ck = ck_ref[0, 0].astype(BF16)
    cv = cv_ref[0, 0].astype(BF16)
    n = qkv_ref.shape[0]
    group = C_HEADS // C_KV_HEADS
    nq = C_HEADS * HEAD_DIM
    nkv = C_KV_HEADS * HEAD_DIM

    def q_block(j, carry):
        q_rows = pl.ds(pl.multiple_of(j * Q_TILE, Q_TILE), Q_TILE)
        for kv in range(C_KV_HEADS):
            cols = slice(kv * HEAD_DIM, (kv + 1) * HEAD_DIM)
            q = _stack_group(qkv_ref, q_rows, kv * group * HEAD_DIM, group)
            k = qkv_ref[:, nq + kv * HEAD_DIM:nq + (kv + 1) * HEAD_DIM]
            v = qkv_ref[:, nq + nkv + kv * HEAD_DIM:nq + nkv + (kv + 1) * HEAD_DIM]
            o = _attend(q, [(ck[:, cols], cv[:, cols], None, None), (k, v, None, None)])
            c0 = kv * group * HEAD_DIM
            o_ref[q_rows, c0:c0 + group * HEAD_DIM] = _unstack_group(o, Q_TILE, group).astype(BF16)
        return carry

    lax.fori_loop(0, n // Q_TILE, q_block, 0)


def _lat_c_attention(qkv, cache_k, cache_v, o, seq):
    t, n = qkv.shape
    past = cache_k.shape[2]
    nkv = C_KV_HEADS * HEAD_DIM
    width = C_HEADS * HEAD_DIM
    return pl.pallas_call(
        _lat_c_kernel,
        grid=(t // seq,),
        in_specs=[
            pl.BlockSpec((seq, n), lambda b: (b, 0)),
            pl.BlockSpec((1, 1, past, nkv), lambda b: (b, o, 0, 0)),
            pl.BlockSpec((1, 1, past, nkv), lambda b: (b, o, 0, 0)),
        ],
        out_specs=pl.BlockSpec((seq, width), lambda b: (b, 0)),
        out_shape=jax.ShapeDtypeStruct((t, width), BF16),
        compiler_params=_params(1),
        name="attn_odd_lat",
    )(qkv, cache_k, cache_v)


def _post_kernel(*refs, n_parts, final):
    o_refs = refs[:n_parts]
    (x_ref, wo_ref, g1_ref, sh_ref, sc_ref, g2_ref, gain_ref, wgu_ref, wd_ref) = refs[n_parts:n_parts + 9]
    rest = refs[n_parts + 9:]
    if final:
        fg_ref, out_ref, act_ref = rest
    else:
        out_ref, act_ref = rest
    mix = None
    r0 = 0
    for o_ref in o_refs:
        kk = o_ref.shape[1]
        part = jnp.dot(o_ref[...], wo_ref[r0:r0 + kk, :], preferred_element_type=F32)
        mix = part if mix is None else mix + part
        r0 += kk
    x1 = x_ref[...] + g1_ref[0, 0] * mix
    h = _adaln(x1, gain_ref[...], sh_ref[0, 0], sc_ref[0, 0]).astype(BF16)
    d_ff = wd_ref.shape[0]
    for j in range(d_ff // FF_CHUNK):
        c0 = j * FF_CHUNK
        gate = jnp.dot(h, wgu_ref[:, c0:c0 + FF_CHUNK], preferred_element_type=F32)
        up = jnp.dot(h, wgu_ref[:, d_ff + c0:d_ff + c0 + FF_CHUNK], preferred_element_type=F32)
        act_ref[:, c0:c0 + FF_CHUNK] = (gate * jax.nn.sigmoid(gate) * up).astype(BF16)
    ffn = jnp.dot(act_ref[...], wd_ref[...], preferred_element_type=F32)
    x2 = x1 + g2_ref[0, 0] * ffn
    if final:
        ms = jnp.mean(x2 * x2, axis=-1, keepdims=True)
        x2 = (x2 * lax.rsqrt(ms + RMS_EPS)) * fg_ref[...]
    out_ref[...] = x2


def _post_attention(o_parts, x, mods, layer, gain, w_out, w_gu, w_down, *, is_lat, seq,
                    final_gain=None):
    t, d = x.shape
    tm = TOKEN_TILE
    per_seq = seq // tm
    group = (lambda i: 1 + i // per_seq) if is_lat else (lambda i: 0)
    row = lambda i: (i, 0)
    d_ff = w_down.shape[0]
    final = final_gain is not None
    in_specs = [pl.BlockSpec((tm, o.shape[1]), row) for o in o_parts]
    in_specs += [
        pl.BlockSpec((tm, d), row),
        _resident(w_out.shape),
        _mod_spec(layer, 2, group),
        _mod_spec(layer, 3, group),
        _mod_spec(layer, 4, group),
        _mod_spec(layer, 5, group),
        _resident((1, d)),
        _resident(w_gu.shape),
        _resident(w_down.shape),
    ]
    args = list(o_parts) + [x, w_out, mods, mods, mods, mods, gain.reshape(1, d), w_gu, w_down]
    if final:
        in_specs.append(_resident((1, d)))
        args.append(final_gain.reshape(1, d))
    return pl.pallas_call(
        functools.partial(_post_kernel, n_parts=len(o_parts), final=final),
        grid=(t // tm,),
        in_specs=in_specs,
        out_specs=pl.BlockSpec((tm, d), row),
        out_shape=jax.ShapeDtypeStruct((t, d), F32),
        scratch_shapes=[pltpu.VMEM((tm, d_ff), BF16)],
        compiler_params=_params(1),
        name=f"post_{'lat' if is_lat else 'ctx'}{'_final' if final else ''}",
    )(*args)


def _rope_tables(n):
    t = jnp.arange(n)
    row = (t // GRID_W).astype(F32)
    col = (t % GRID_W).astype(F32)
    half = HEAD_DIM // 2
    inv_freq = ROPE_THETA ** (-jnp.arange(0, half, 2, dtype=F32) / half)
    lane = jnp.arange(LANES)
    in_head = lane % HEAD_DIM
    pos = jnp.where((in_head < half)[None, :], row[:, None], col[:, None])
    ang = pos * inv_freq[in_head % (half // 2)][None, :]
    first = ((in_head % half) < half // 2)[None, :]
    cos = jnp.cos(ang)
    sin = jnp.sin(ang)
    return cos, jnp.where(first, -sin, 0.0), jnp.where(first, 0.0, sin)


def _na_bias(rpb, n):
    rows = n // GRID_W
    wh = min(NA_WIN_H, rows)
    r = jnp.arange(rows)
    rs = jnp.clip(r - wh // 2, 0, rows - wh)
    row_ok = (r[None, :] >= rs[:, None]) & (r[None, :] < rs[:, None] + wh)
    row_off = r[None, :] - r[:, None] + (NA_WIN_H - 1)
    row_sel = (row_ok[:, :, None]
               & (row_off[:, :, None] == jnp.arange(2 * NA_WIN_H - 1)[None, None, :])).astype(F32)
    c = jnp.arange(GRID_W)
    cs = jnp.clip(c - NA_WIN_W // 2, 0, GRID_W - NA_WIN_W)
    col_ok = (c[None, :] >= cs[:, None]) & (c[None, :] < cs[:, None] + NA_WIN_W)
    col_off = c[None, :] - c[:, None] + (NA_WIN_W - 1)
    col_sel = (col_ok[:, :, None]
               & (col_off[:, :, None] == jnp.arange(2 * NA_WIN_W - 1)[None, None, :])).astype(F32)
    hi = lax.Precision.HIGHEST
    by_row = jnp.einsum('rka,hab->hrkb', row_sel, rpb, precision=hi)
    bias = jnp.einsum('hrkb,cxb->hrckx', by_row, col_sel, precision=hi)
    ok = row_ok[:, None, :, None] & col_ok[None, :, None, :]
    return jnp.where(ok[None], bias, NEG_INF).reshape(rpb.shape[0], n, n)


def kernel(x_prompt, x_sample, cache_a_k, cache_a_v, cache_b_k, cache_b_v, cache_c_k, cache_c_v,
           c, c_ctx, norm_gain, w_mod, b_mod, w_in_even, w_out_even, rpb_a, sink_b,
           w_in_odd, w_out_odd, q_norm_c, k_norm_c, w_gate_up, w_down, final_gain):
    batch, seq, d = x_prompt.shape
    dec_batch, dec_seq, _ = x_sample.shape
    depth = w_mod.shape[0]

    cvec = jnp.concatenate(
        [c_ctx[None, :], c, jnp.zeros((MOD_GROUPS - 1 - dec_batch, d), F32)], axis=0)
    mods = _modulation(cvec, w_mod, b_mod).reshape(depth, MOD_GROUPS, 1, 6 * d)
    rope = _rope_tables(dec_seq)

    ctx = x_prompt.reshape(batch * seq, d)
    lat = x_sample.reshape(dec_batch * dec_seq, d)
    states = {name: [] for name in ("a_k", "a_v", "b_k", "b_v", "c_k", "c_v")}

    for layer in range(depth):
        last = layer == depth - 1
        gain1, gain2 = norm_gain[layer, 0], norm_gain[layer, 1]
        if layer % 2 == 0:
            e = layer // 2
            w_in = w_in_even[e].astype(BF16)
            w_out = w_out_even[e].astype(BF16)
            qkv_c, ka, va, kb, vb = _pre_attention(
                ctx, mods, layer, gain1, w_in, odd=False, is_lat=False, seq=seq)
            (qkv_l,) = _pre_attention(
                lat, mods, layer, gain1, w_in, odd=False, is_lat=True, seq=dec_seq, rope=rope)
            states["a_k"].append(ka.reshape(batch, seq, A_HEADS, HEAD_DIM))
            states["a_v"].append(va.reshape(batch, seq, A_HEADS, HEAD_DIM))
            states["b_k"].append(kb.reshape(batch, seq, B_KV_HEADS, HEAD_DIM))
            states["b_v"].append(vb.reshape(batch, seq, B_KV_HEADS, HEAD_DIM))
            o_ctx = [_ctx_even_attention(qkv_c, sink_b[e], seq)]
            bias = _na_bias(rpb_a[e], dec_seq)
            o_lat = [
                _lat_a_attention(qkv_l, _flat_heads(cache_a_k), _flat_heads(cache_a_v), e,
                                 bias, dec_seq),
                _lat_b_attention(qkv_l, _flat_heads(cache_b_k), _flat_heads(cache_b_v), e,
                                 sink_b[e], dec_seq),
            ]
        else:
            o = layer // 2
            w_in = w_in_odd[o].astype(BF16)
            w_out = w_out_odd[o].astype(BF16)
            head_gains = (jnp.tile(q_norm_c[o], LANES // HEAD_DIM).reshape(1, LANES),
                          jnp.tile(k_norm_c[o], LANES // HEAD_DIM).reshape(1, LANES))
            qkv_c, kc, vc = _pre_attention(
                ctx, mods, layer, gain1, w_in, odd=True, is_lat=False, seq=seq,
                head_gains=head_gains)
            (qkv_l,) = _pre_attention(
                lat, mods, layer, gain1, w_in, odd=True, is_lat=True, seq=dec_seq, rope=rope,
                head_gains=head_gains)
            states["c_k"].append(kc.reshape(batch, seq, C_KV_HEADS, HEAD_DIM))
            states["c_v"].append(vc.reshape(batch, seq, C_KV_HEADS, HEAD_DIM))
            o_ctx = [_ctx_odd_attention(qkv_c, seq)]
            o_lat = [_lat_c_attention(qkv_l, _flat_heads(cache_c_k), _flat_heads(cache_c_v), o,
                                      dec_seq)]
        w_gu = w_gate_up[layer].astype(BF16)
        w_dn = w_down[layer].astype(BF16)
        fg = final_gain if last else None
        ctx = _post_attention(o_ctx, ctx, mods, layer, gain2, w_out, w_gu, w_dn,
                              is_lat=False, seq=seq, final_gain=fg)
        lat = _post_attention(o_lat, lat, mods, layer, gain2, w_out, w_gu, w_dn,
                              is_lat=True, seq=dec_seq, final_gain=fg)

    return (ctx.reshape(batch, seq, d), lat.reshape(dec_batch, dec_seq, d),
            jnp.stack(states["a_k"], axis=1), jnp.stack(states["a_v"], axis=1),
            jnp.stack(states["b_k"], axis=1), jnp.stack(states["b_v"], axis=1),
            jnp.stack(states["c_k"], axis=1), jnp.stack(states["c_v"], axis=1))
```

```python
import functools
import math

import jax
import jax.numpy as jnp
from jax import lax
from jax.experimental import pallas as pl
from jax.experimental.pallas import tpu as pltpu

F32 = jnp.float32
BF16 = jnp.bfloat16

D_MODEL = 1024
GRID_W = 64
HEAD_DIM = 64
A_HEADS = 8
B_HEADS = 8
B_KV_HEADS = 2
C_HEADS = 16
C_KV_HEADS = 4
NA_WIN_H = 8
NA_WIN_W = 16
B_WINDOW = 128
ROPE_THETA = 10000.0
RMS_EPS = 1e-6
NEG_INF = -1e30
QK_SCALE = 1.0 / math.sqrt(HEAD_DIM)

LANES = 128
TOKEN_TILE = 512
Q_TILE = 256
FF_CHUNK = 256
MOD_GROUPS = 16
VMEM_LIMIT = 56 * 1024 * 1024


def _params(n_axes, vmem=VMEM_LIMIT):
    return pltpu.CompilerParams(
        dimension_semantics=("arbitrary",) * n_axes, vmem_limit_bytes=vmem)


def _resident(shape):
    nd = len(shape)
    return pl.BlockSpec(shape, lambda *_: (0,) * nd, pipeline_mode=pl.Buffered(1))


def _mod_kernel(c_ref, w_ref, b_ref, o_ref):
    c = c_ref[...]
    s = (c * jax.nn.sigmoid(c)).astype(BF16)
    o_ref[0] = jnp.dot(s, w_ref[0].astype(BF16), preferred_element_type=F32) + b_ref[0]


def _modulation(cvec, w_mod, b_mod):
    depth, d, n = w_mod.shape
    tn = 1536
    return pl.pallas_call(
        _mod_kernel,
        grid=(depth, n // tn),
        in_specs=[
            pl.BlockSpec((MOD_GROUPS, d), lambda l, j: (0, 0)),
            pl.BlockSpec((1, d, tn), lambda l, j: (l, 0, j)),
            pl.BlockSpec((1, 1, tn), lambda l, j: (l, 0, j)),
        ],
        out_specs=pl.BlockSpec((1, MOD_GROUPS, tn), lambda l, j: (l, 0, j)),
        out_shape=jax.ShapeDtypeStruct((depth, MOD_GROUPS, n), F32),
        compiler_params=_params(2),
        name="modulation",
    )(cvec, w_mod, b_mod.reshape(depth, 1, n))


def _mod_spec(layer, which, group_of_step):
    return pl.BlockSpec((1, 1, 1, D_MODEL), lambda i: (layer, group_of_step(i), 0, which))


def _adaln(x, gain, shift, scale):
    ms = jnp.mean(x * x, axis=-1, keepdims=True)
    return (x * lax.rsqrt(ms + RMS_EPS)) * gain * (1.0 + scale) + shift


def _rope(y, cos, sin_lo, sin_hi):
    outs = []
    for c in range(y.shape[1] // LANES):
        yc = y[:, c * LANES:(c + 1) * LANES]
        outs.append(yc * cos
                    + pltpu.roll(yc, LANES - 16, 1) * sin_lo
                    + pltpu.roll(yc, 16, 1) * sin_hi)
    return outs[0] if len(outs) == 1 else jnp.concatenate(outs, axis=1)


def _head_rms_norm(y, gain):
    first = lax.broadcasted_iota(jnp.int32, (1, LANES), 1) < HEAD_DIM
    outs = []
    for c in range(y.shape[1] // LANES):
        yc = y[:, c * LANES:(c + 1) * LANES]
        sq = yc * yc
        s0 = jnp.sum(jnp.where(first, sq, 0.0), axis=-1, keepdims=True)
        s1 = jnp.sum(jnp.where(first, 0.0, sq), axis=-1, keepdims=True)
        ms = jnp.where(first, s0, s1) * (1.0 / HEAD_DIM)
        outs.append(yc * lax.rsqrt(ms + RMS_EPS) * gain)
    return outs[0] if len(outs) == 1 else jnp.concatenate(outs, axis=1)


def _attend(q, segments, sink=None):
    scores = []
    for k, _, bias, valid in segments:
        s = lax.dot_general(q, k, (((1,), (1,)), ((), ())), preferred_element_type=F32)
        if bias is not None:
            s = s + bias
        if valid is not None:
            s = jnp.where(valid, s, NEG_INF)
        scores.append(s)
    m = functools.reduce(jnp.maximum, [jnp.max(s, axis=-1, keepdims=True) for s in scores])
    if sink is not None:
        m = jnp.maximum(m, sink)
    denom = None
    out = None
    for s, (_, v, _, _) in zip(scores, segments):
        p = jnp.exp(s - m)
        part = jnp.sum(p, axis=-1, keepdims=True)
        pv = jnp.dot(p.astype(BF16), v, preferred_element_type=F32)
        denom = part if denom is None else denom + part
        out = pv if out is None else out + pv
    if sink is not None:
        denom = denom + jnp.exp(sink - m)
    return out / denom


def _stack_group(ref, rows, col0, group):
    return jnp.concatenate(
        [ref[rows, col0 + g * HEAD_DIM: col0 + (g + 1) * HEAD_DIM] for g in range(group)], axis=0)


def _unstack_group(o, rows, group):
    return jnp.concatenate([o[g * rows:(g + 1) * rows] for g in range(group)], axis=1)


def _store_token_head_rows(ref, y):
    heads = y.shape[1] // HEAD_DIM
    for h in range(heads):
        ref[pl.ds(h, y.shape[0], stride=heads), :] = y[:, h * HEAD_DIM:(h + 1) * HEAD_DIM]


def _pre_even_kernel(x_ref, g_ref, sh_ref, sc_ref, w_ref, *rest, is_lat):
    if is_lat:
        cos_ref, slo_ref, shi_ref, qkv_ref = rest
    else:
        qkv_ref, ka_ref, va_ref, kb_ref, vb_ref = rest
    h = _adaln(x_ref[...], g_ref[...], sh_ref[0, 0], sc_ref[0, 0]).astype(BF16)

    def proj(c0, c1):
        return jnp.dot(h, w_ref[:, c0:c1], preferred_element_type=F32)

    na = A_HEADS * HEAD_DIM
    nb = B_HEADS * HEAD_DIM
    nkv = B_KV_HEADS * HEAD_DIM
    c = 0
    qkv_ref[:, c:c + na] = (proj(c, c + na) * QK_SCALE).astype(BF16)
    c += na
    ka = proj(c, c + na)
    qkv_ref[:, c:c + na] = ka.astype(BF16)
    c += na
    va = proj(c, c + na)
    qkv_ref[:, c:c + na] = va.astype(BF16)
    c += na
    qb = proj(c, c + nb)
    if is_lat:
        qb = _rope(qb, cos_ref[...], slo_ref[...], shi_ref[...])
    qkv_ref[:, c:c + nb] = (qb * QK_SCALE).astype(BF16)
    c += nb
    kb = proj(c, c + nkv)
    if is_lat:
        kb_out = _rope(kb, cos_ref[...], slo_ref[...], shi_ref[...])
    else:
        kb_out = kb
    qkv_ref[:, c:c + nkv] = kb_out.astype(BF16)
    c += nkv
    vb = proj(c, c + nkv)
    qkv_ref[:, c:c + nkv] = vb.astype(BF16)
    if not is_lat:
        _store_token_head_rows(ka_ref, ka)
        _store_token_head_rows(va_ref, va)
        _store_token_head_rows(kb_ref, kb)
        _store_token_head_rows(vb_ref, vb)


def _pre_odd_kernel(x_ref, g_ref, sh_ref, sc_ref, w_ref, qn_ref, kn_ref, *rest, is_lat):
    if is_lat:
        cos_ref, slo_ref, shi_ref, qkv_ref = rest
    else:
        qkv_ref, kc_ref, vc_ref = rest
    h = _adaln(x_ref[...], g_ref[...], sh_ref[0, 0], sc_ref[0, 0]).astype(BF16)

    def proj(c0, c1):
        return jnp.dot(h, w_ref[:, c0:c1], preferred_element_type=F32)

    nq = C_HEADS * HEAD_DIM
    nkv = C_KV_HEADS * HEAD_DIM
    q = _head_rms_norm(proj(0, nq), qn_ref[...])
    k = _head_rms_norm(proj(nq, nq + nkv), kn_ref[...])
    v = proj(nq + nkv, nq + 2 * nkv)
    if is_lat:
        q = _rope(q, cos_ref[...], slo_ref[...], shi_ref[...])
        k_out = _rope(k, cos_ref[...], slo_ref[...], shi_ref[...])
    else:
        k_out = k
        _store_token_head_rows(kc_ref, k)
        _store_token_head_rows(vc_ref, v)
    qkv_ref[:, 0:nq] = (q * QK_SCALE).astype(BF16)
    qkv_ref[:, nq:nq + nkv] = k_out.astype(BF16)
    qkv_ref[:, nq + nkv:nq + 2 * nkv] = v.astype(BF16)


def _pre_attention(x, mods, layer, gain, w, *, odd, is_lat, seq, rope=None, head_gains=None):
    t, d = x.shape
    n = w.shape[1]
    tm = TOKEN_TILE
    per_seq = seq // tm
    group = (lambda i: 1 + i // per_seq) if is_lat else (lambda i: 0)
    row = lambda i: (i, 0)
    in_specs = [
        pl.BlockSpec((tm, d), row),
        _resident((1, d)),
        _mod_spec(layer, 0, group),
        _mod_spec(layer, 1, group),
        _resident((d, n)),
    ]
    args = [x, gain.reshape(1, d), mods, mods, w]
    if odd:
        in_specs += [_resident((1, LANES)), _resident((1, LANES))]
        args += list(head_gains)
    if is_lat:
        in_specs += [pl.BlockSpec((tm, LANES), lambda i: (i % per_seq, 0))] * 3
        args += list(rope)
    out_specs = [pl.BlockSpec((tm, n), row)]
    out_shape = [jax.ShapeDtypeStruct((t, n), BF16)]
    if not is_lat:
        if odd:
            widths = [C_KV_HEADS * HEAD_DIM] * 2
        else:
            widths = [A_HEADS * HEAD_DIM] * 2 + [B_KV_HEADS * HEAD_DIM] * 2
        for wd in widths:
            heads = wd // HEAD_DIM
            out_specs.append(pl.BlockSpec((tm * heads, HEAD_DIM), row))
            out_shape.append(jax.ShapeDtypeStruct((t * heads, HEAD_DIM), F32))
    body = _pre_odd_kernel if odd else _pre_even_kernel
    return pl.pallas_call(
        functools.partial(body, is_lat=is_lat),
        grid=(t // tm,),
        in_specs=in_specs,
        out_specs=out_specs,
        out_shape=out_shape,
        compiler_params=_params(1),
        name=f"pre_{'odd' if odd else 'even'}_{'lat' if is_lat else 'ctx'}",
    )(*args)


def _ctx_even_attn_kernel(sink_ref, qkv_ref, o_ref):
    rows = slice(None)
    n = qkv_ref.shape[0]
    na = A_HEADS * HEAD_DIM
    for pair in range(A_HEADS // 2):
        outs = []
        for h in (2 * pair, 2 * pair + 1):
            c = h * HEAD_DIM
            q = qkv_ref[:, c:c + HEAD_DIM]
            k = qkv_ref[:, na + c:na + c + HEAD_DIM]
            v = qkv_ref[:, 2 * na + c:2 * na + c + HEAD_DIM]
            outs.append(_attend(q, [(k, v, None, None)]))
        o_ref[:, pair * LANES:(pair + 1) * LANES] = jnp.concatenate(outs, axis=1).astype(BF16)
    group = B_HEADS // B_KV_HEADS
    qb0 = 3 * na
    kb0 = qb0 + B_HEADS * HEAD_DIM
    vb0 = kb0 + B_KV_HEADS * HEAD_DIM
    for kv in range(B_KV_HEADS):
        q = _stack_group(qkv_ref, rows, qb0 + kv * group * HEAD_DIM, group)
        k = qkv_ref[:, kb0 + kv * HEAD_DIM:kb0 + (kv + 1) * HEAD_DIM]
        v = qkv_ref[:, vb0 + kv * HEAD_DIM:vb0 + (kv + 1) * HEAD_DIM]
        sink = jnp.concatenate(
            [jnp.full((n, 1), sink_ref[kv * group + g], F32) for g in range(group)], axis=0)
        o = _attend(q, [(k, v, None, None)], sink)
        c0 = na + kv * group * HEAD_DIM
        o_ref[:, c0:c0 + group * HEAD_DIM] = _unstack_group(o, n, group).astype(BF16)


def _ctx_even_attention(qkv, sink, seq):
    t, n = qkv.shape
    width = (A_HEADS + B_HEADS) * HEAD_DIM
    return pl.pallas_call(
        _ctx_even_attn_kernel,
        grid=(t // seq,),
        in_specs=[pl.BlockSpec(memory_space=pltpu.SMEM),
                  pl.BlockSpec((seq, n), lambda b: (b, 0))],
        out_specs=pl.BlockSpec((seq, width), lambda b: (b, 0)),
        out_shape=jax.ShapeDtypeStruct((t, width), BF16),
        compiler_params=_params(1),
        name="attn_even_ctx",
    )(sink, qkv)


def _na_key_window(j):
    return min(max(Q_TILE * j - Q_TILE, 0), Q_TILE)


def _na_bias_tiles(src_ref, h):
    qc = lax.broadcasted_iota(jnp.int32, (GRID_W, LANES), 0)
    kc = lax.rem(lax.broadcasted_iota(jnp.int32, (GRID_W, LANES), 1), GRID_W)
    cs = jnp.clip(qc - NA_WIN_W // 2, 0, GRID_W - NA_WIN_W)
    col_ok = (kc >= cs) & (kc < cs + NA_WIN_W)
    tiles = []
    for i in range(2 * NA_WIN_H):
        src = jnp.broadcast_to(src_ref[h, i:i + 1, :], (GRID_W, LANES))
        tiles.append(jnp.where(col_ok, pltpu.roll(src, 0, 1, stride=1, stride_axis=0), NEG_INF))
    return tiles


def _na_bias_block(tiles, j, grid_rows):
    wh = min(NA_WIN_H, grid_rows)
    first_row = lax.broadcasted_iota(jnp.int32, (GRID_W, LANES), 1) < GRID_W
    lo_r = _na_key_window(j) // GRID_W
    strips = []
    for qr in range(j * Q_TILE // GRID_W, (j + 1) * Q_TILE // GRID_W):
        rs = min(max(qr - wh // 2, 0), grid_rows - wh)
        parts = []
        for t in range(3 * Q_TILE // LANES):
            kr = lo_r + 2 * t
            ok0 = rs <= kr < rs + wh
            ok1 = rs <= kr + 1 < rs + wh
            i = kr - qr + NA_WIN_H
            if ok0 and ok1:
                parts.append(tiles[i])
            elif ok0:
                parts.append(jnp.where(first_row, tiles[i], NEG_INF))
            elif ok1:
                parts.append(jnp.where(first_row, NEG_INF, tiles[i]))
            else:
                parts.append(jnp.full((GRID_W, LANES), NEG_INF, F32))
        strips.append(jnp.concatenate(parts, axis=1))
    return jnp.concatenate(strips, axis=0)


def _cache_head(ref, h, heads):
    past = ref.shape[2] // heads
    return ref[0, 0, pl.ds(h, past, stride=heads), :].astype(BF16)


def _lat_a_kernel(src_ref, q_ref, k_ref, v_ref, ck_ref, cv_ref, o_ref):
    n = q_ref.shape[0]
    win = 3 * Q_TILE
    for pair in range(A_HEADS // 2):
        outs = [[] for _ in range(n // Q_TILE)]
        for h in (2 * pair, 2 * pair + 1):
            cols = slice(h * HEAD_DIM, (h + 1) * HEAD_DIM)
            ctx = (_cache_head(ck_ref, h, A_HEADS), _cache_head(cv_ref, h, A_HEADS), None, None)
            tiles = _na_bias_tiles(src_ref, h)
            for j in range(n // Q_TILE):
                lo = _na_key_window(j)
                q_rows = slice(j * Q_TILE, (j + 1) * Q_TILE)
                local = (k_ref[lo:lo + win, cols], v_ref[lo:lo + win, cols],
                         _na_bias_block(tiles, j, n // GRID_W), None)
                outs[j].append(_attend(q_ref[q_rows, cols], [local, ctx]))
        for j in range(n // Q_TILE):
            o_ref[j * Q_TILE:(j + 1) * Q_TILE, pair * LANES:(pair + 1) * LANES] = (
                jnp.concatenate(outs[j], axis=1).astype(BF16))


def _token_head_rows(cache):
    b, layers, past, heads, hd = cache.shape
    return cache.reshape(b, layers, past * heads, hd)


def _lat_a_attention(qkv, cache_k, cache_v, e, bias_src, seq):
    t = qkv.shape[0]
    rows = cache_k.shape[2]
    na = A_HEADS * HEAD_DIM
    return pl.pallas_call(
        _lat_a_kernel,
        grid=(t // seq,),
        in_specs=[
            _resident(bias_src.shape),
            pl.BlockSpec((seq, na), lambda b: (b, 0)),
            pl.BlockSpec((seq, na), lambda b: (b, 1)),
            pl.BlockSpec((seq, na), lambda b: (b, 2)),
            pl.BlockSpec((1, 1, rows, HEAD_DIM), lambda b: (b, e, 0, 0)),
            pl.BlockSpec((1, 1, rows, HEAD_DIM), lambda b: (b, e, 0, 0)),
        ],
        out_specs=pl.BlockSpec((seq, na), lambda b: (b, 0)),
        out_shape=jax.ShapeDtypeStruct((t, na), BF16),
        compiler_params=_params(1),
        name="attn_even_lat_a",
    )(bias_src, qkv, qkv, qkv, cache_k, cache_v)


def _lat_b_kernel(sink_ref, q_ref, k_ref, v_ref, ck_ref, cv_ref, o_ref):
    ctx = [(_cache_head(ck_ref, kv, B_KV_HEADS), _cache_head(cv_ref, kv, B_KV_HEADS), None, None)
           for kv in range(B_KV_HEADS)]
    n = q_ref.shape[0]
    group = B_HEADS // B_KV_HEADS
    win = 2 * Q_TILE
    for j in range(n // Q_TILE):
        lo = min(max(Q_TILE * j - B_WINDOW, 0), n - win)
        q_rows = slice(j * Q_TILE, (j + 1) * Q_TILE)
        qpos = j * Q_TILE + lax.broadcasted_iota(jnp.int32, (group * Q_TILE, 1), 0) % Q_TILE
        kpos = lo + lax.broadcasted_iota(jnp.int32, (1, win), 1)
        valid = jnp.abs(qpos - kpos) <= B_WINDOW
        for kv in range(B_KV_HEADS):
            cols = slice(kv * HEAD_DIM, (kv + 1) * HEAD_DIM)
            q = _stack_group(q_ref, q_rows, kv * group * HEAD_DIM, group)
            sink = jnp.concatenate(
                [jnp.full((Q_TILE, 1), sink_ref[kv * group + g], F32) for g in range(group)], axis=0)
            local = (k_ref[lo:lo + win, cols], v_ref[lo:lo + win, cols], None, valid)
            o = _attend(q, [local, ctx[kv]], sink)
            c0 = kv * group * HEAD_DIM
            o_ref[q_rows, c0:c0 + group * HEAD_DIM] = _unstack_group(o, Q_TILE, group).astype(BF16)


def _lat_b_attention(qkv, cache_k, cache_v, e, sink, seq):
    t = qkv.shape[0]
    rows = cache_k.shape[2]
    na = A_HEADS * HEAD_DIM
    nb = B_HEADS * HEAD_DIM
    nkv = B_KV_HEADS * HEAD_DIM
    return pl.pallas_call(
        _lat_b_kernel,
        grid=(t // seq,),
        in_specs=[
            pl.BlockSpec(memory_space=pltpu.SMEM),
            pl.BlockSpec((seq, nb), lambda b: (b, 3 * na // nb)),
            pl.BlockSpec((seq, nkv), lambda b: (b, (3 * na + nb) // nkv)),
            pl.BlockSpec((seq, nkv), lambda b: (b, (3 * na + nb) // nkv + 1)),
            pl.BlockSpec((1, 1, rows, HEAD_DIM), lambda b: (b, e, 0, 0)),
            pl.BlockSpec((1, 1, rows, HEAD_DIM), lambda b: (b, e, 0, 0)),
        ],
        out_specs=pl.BlockSpec((seq, nb), lambda b: (b, 0)),
        out_shape=jax.ShapeDtypeStruct((t, nb), BF16),
        compiler_params=_params(1),
        name="attn_even_lat_b",
    )(sink, qkv, qkv, qkv, cache_k, cache_v)


def _ctx_odd_attn_kernel(qkv_ref, o_ref):
    n = qkv_ref.shape[0]
    group = C_HEADS // C_KV_HEADS
    nq = C_HEADS * HEAD_DIM
    nkv = C_KV_HEADS * HEAD_DIM
    for kv in range(C_KV_HEADS):
        q = _stack_group(qkv_ref, slice(None), kv * group * HEAD_DIM, group)
        k = qkv_ref[:, nq + kv * HEAD_DIM:nq + (kv + 1) * HEAD_DIM]
        v = qkv_ref[:, nq + nkv + kv * HEAD_DIM:nq + nkv + (kv + 1) * HEAD_DIM]
        o = _attend(q, [(k, v, None, None)])
        c0 = kv * group * HEAD_DIM
        o_ref[:, c0:c0 + group * HEAD_DIM] = _unstack_group(o, n, group).astype(BF16)


def _ctx_odd_attention(qkv, seq):
    t, n = qkv.shape
    width = C_HEADS * HEAD_DIM
    return pl.pallas_call(
        _ctx_odd_attn_kernel,
        grid=(t // seq,),
        in_specs=[pl.BlockSpec((seq, n), lambda b: (b, 0))],
        out_specs=pl.BlockSpec((seq, width), lambda b: (b, 0)),
        out_shape=jax.ShapeDtypeStruct((t, width), BF16),
        compiler_params=_params(1),
        name="attn_odd_ctx",
    )(qkv)


def _lat_c_kernel(qkv_ref, ck_ref, cv_ref, o_ref, ckb_ref, cvb_ref):
    n = qkv_ref.shape[0]
    group = C_HEADS // C_KV_HEADS
    nq = C_HEADS * HEAD_DIM
    nkv = C_KV_HEADS * HEAD_DIM
    for kv in range(C_KV_HEADS):
        ckb_ref[kv] = _cache_head(ck_ref, kv, C_KV_HEADS)
        cvb_ref[kv] = _cache_head(cv_ref, kv, C_KV_HEADS)

    def q_block(j, carry):
        q_rows = pl.ds(pl.multiple_of(j * Q_TILE, Q_TILE), Q_TILE)
        for kv in range(C_KV_HEADS):
            q = _stack_group(qkv_ref, q_rows, kv * group * HEAD_DIM, group)
            k = qkv_ref[:, nq + kv * HEAD_DIM:nq + (kv + 1) * HEAD_DIM]
            v = qkv_ref[:, nq + nkv + kv * HEAD_DIM:nq + nkv + (kv + 1) * HEAD_DIM]
            o = _attend(q, [(ckb_ref[kv], cvb_ref[kv], None, None), (k, v, None, None)])
            c0 = kv * group * HEAD_DIM
            o_ref[q_rows, c0:c0 + group * HEAD_DIM] = _unstack_group(o, Q_TILE, group).astype(BF16)
        return carry

    lax.fori_loop(0, n // Q_TILE, q_block, 0)


def _lat_c_attention(qkv, cache_k, cache_v, o, seq):
    t, n = qkv.shape
    rows = cache_k.shape[2]
    past = rows // C_KV_HEADS
    width = C_HEADS * HEAD_DIM
    return pl.pallas_call(
        _lat_c_kernel,
        grid=(t // seq,),
        in_specs=[
            pl.BlockSpec((seq, n), lambda b: (b, 0)),
            pl.BlockSpec((1, 1, rows, HEAD_DIM), lambda b: (b, o, 0, 0)),
            pl.BlockSpec((1, 1, rows, HEAD_DIM), lambda b: (b, o, 0, 0)),
        ],
        out_specs=pl.BlockSpec((seq, width), lambda b: (b, 0)),
        out_shape=jax.ShapeDtypeStruct((t, width), BF16),
        scratch_shapes=[pltpu.VMEM((C_KV_HEADS, past, HEAD_DIM), BF16)] * 2,
        compiler_params=_params(1),
        name="attn_odd_lat",
    )(qkv, cache_k, cache_v)


def _post_kernel(*refs, n_parts, final):
    o_refs = refs[:n_parts]
    (x_ref, wo_ref, g1_ref, sh_ref, sc_ref, g2_ref, gain_ref, wgu_ref, wd_ref) = refs[n_parts:n_parts + 9]
    rest = refs[n_parts + 9:]
    if final:
        fg_ref, out_ref, act_ref = rest
    else:
        out_ref, act_ref = rest
    mix = None
    r0 = 0
    for o_ref in o_refs:
        kk = o_ref.shape[1]
        part = jnp.dot(o_ref[...], wo_ref[r0:r0 + kk, :], preferred_element_type=F32)
        mix = part if mix is None else mix + part
        r0 += kk
    x1 = x_ref[...] + g1_ref[0, 0] * mix
    h = _adaln(x1, gain_ref[...], sh_ref[0, 0], sc_ref[0, 0]).astype(BF16)
    d_ff = wd_ref.shape[0]
    for j in range(d_ff // FF_CHUNK):
        c0 = j * FF_CHUNK
        gate = jnp.dot(h, wgu_ref[:, c0:c0 + FF_CHUNK], preferred_element_type=F32)
        up = jnp.dot(h, wgu_ref[:, d_ff + c0:d_ff + c0 + FF_CHUNK], preferred_element_type=F32)
        act_ref[:, c0:c0 + FF_CHUNK] = (gate * jax.nn.sigmoid(gate) * up).astype(BF16)
    ffn = jnp.dot(act_ref[...], wd_ref[...], preferred_element_type=F32)
    x2 = x1 + g2_ref[0, 0] * ffn
    if final:
        ms = jnp.mean(x2 * x2, axis=-1, keepdims=True)
        x2 = (x2 * lax.rsqrt(ms + RMS_EPS)) * fg_ref[...]
    out_ref[...] = x2


def _post_attention(o_parts, x, mods, layer, gain, w_out, w_gu, w_down, *, is_lat, seq,
                    final_gain=None):
    t, d = x.shape
    tm = TOKEN_TILE
    per_seq = seq // tm
    group = (lambda i: 1 + i // per_seq) if is_lat else (lambda i: 0)
    row = lambda i: (i, 0)
    d_ff = w_down.shape[0]
    final = final_gain is not None
    in_specs = [pl.BlockSpec((tm, o.shape[1]), row) for o in o_parts]
    in_specs += [
        pl.BlockSpec((tm, d), row),
        _resident(w_out.shape),
        _mod_spec(layer, 2, group),
        _mod_spec(layer, 3, group),
        _mod_spec(layer, 4, group),
        _mod_spec(layer, 5, group),
        _resident((1, d)),
        _resident(w_gu.shape),
        _resident(w_down.shape),
    ]
    args = list(o_parts) + [x, w_out, mods, mods, mods, mods, gain.reshape(1, d), w_gu, w_down]
    if final:
        in_specs.append(_resident((1, d)))
        args.append(final_gain.reshape(1, d))
    return pl.pallas_call(
        functools.partial(_post_kernel, n_parts=len(o_parts), final=final),
        grid=(t // tm,),
        in_specs=in_specs,
        out_specs=pl.BlockSpec((tm, d), row),
        out_shape=jax.ShapeDtypeStruct((t, d), F32),
        scratch_shapes=[pltpu.VMEM((tm, d_ff), BF16)],
        compiler_params=_params(1),
        name=f"post_{'lat' if is_lat else 'ctx'}{'_final' if final else ''}",
    )(*args)


def _rope_tables(n):
    t = jnp.arange(n)
    row = (t // GRID_W).astype(F32)
    col = (t % GRID_W).astype(F32)
    half = HEAD_DIM // 2
    inv_freq = ROPE_THETA ** (-jnp.arange(0, half, 2, dtype=F32) / half)
    lane = jnp.arange(LANES)
    in_head = lane % HEAD_DIM
    pos = jnp.where((in_head < half)[None, :], row[:, None], col[:, None])
    ang = pos * inv_freq[in_head % (half // 2)][None, :]
    first = ((in_head % half) < half // 2)[None, :]
    cos = jnp.cos(ang)
    sin = jnp.sin(ang)
    return cos, jnp.where(first, -sin, 0.0), jnp.where(first, 0.0, sin)


def _na_bias_sources(rpb):
    h, _, nb = rpb.shape
    w = NA_WIN_W - 1
    rp = jnp.pad(rpb, ((0, 0), (1, 1), (0, 0)))
    lo, hi = rp[:, :-1], rp[:, 1:]
    z = jnp.zeros((h, 2 * NA_WIN_H, LANES // 2 - nb), F32)
    return jnp.concatenate([lo[:, :, w:], z, hi, z, lo[:, :, :w]], axis=-1)


def kernel(x_prompt, x_sample, cache_a_k, cache_a_v, cache_b_k, cache_b_v, cache_c_k, cache_c_v,
           c, c_ctx, norm_gain, w_mod, b_mod, w_in_even, w_out_even, rpb_a, sink_b,
           w_in_odd, w_out_odd, q_norm_c, k_norm_c, w_gate_up, w_down, final_gain):
    batch, seq, d = x_prompt.shape
    dec_batch, dec_seq, _ = x_sample.shape
    depth = w_mod.shape[0]

    cvec = jnp.concatenate(
        [c_ctx[None, :], c, jnp.zeros((MOD_GROUPS - 1 - dec_batch, d), F32)], axis=0)
    mods = _modulation(cvec, w_mod, b_mod).reshape(depth, MOD_GROUPS, 1, 6 * d)
    rope = _rope_tables(dec_seq)

    ctx = x_prompt.reshape(batch * seq, d)
    lat = x_sample.reshape(dec_batch * dec_seq, d)
    states = {name: [] for name in ("a_k", "a_v", "b_k", "b_v", "c_k", "c_v")}

    for layer in range(depth):
        last = layer == depth - 1
        gain1, gain2 = norm_gain[layer, 0], norm_gain[layer, 1]
        if layer % 2 == 0:
            e = layer // 2
            w_in = w_in_even[e].astype(BF16)
            w_out = w_out_even[e].astype(BF16)
            qkv_c, ka, va, kb, vb = _pre_attention(
                ctx, mods, layer, gain1, w_in, odd=False, is_lat=False, seq=seq)
            (qkv_l,) = _pre_attention(
                lat, mods, layer, gain1, w_in, odd=False, is_lat=True, seq=dec_seq, rope=rope)
            states["a_k"].append(ka.reshape(batch, seq, A_HEADS, HEAD_DIM))
            states["a_v"].append(va.reshape(batch, seq, A_HEADS, HEAD_DIM))
            states["b_k"].append(kb.reshape(batch, seq, B_KV_HEADS, HEAD_DIM))
            states["b_v"].append(vb.reshape(batch, seq, B_KV_HEADS, HEAD_DIM))
            o_ctx = [_ctx_even_attention(qkv_c, sink_b[e], seq)]
            o_lat = [
                _lat_a_attention(qkv_l, _token_head_rows(cache_a_k), _token_head_rows(cache_a_v),
                                 e, _na_bias_sources(rpb_a[e]), dec_seq),
                _lat_b_attention(qkv_l, _token_head_rows(cache_b_k), _token_head_rows(cache_b_v),
                                 e, sink_b[e], dec_seq),
            ]
        else:
            o = layer // 2
            w_in = w_in_odd[o].astype(BF16)
            w_out = w_out_odd[o].astype(BF16)
            head_gains = (jnp.tile(q_norm_c[o], LANES // HEAD_DIM).reshape(1, LANES),
                          jnp.tile(k_norm_c[o], LANES // HEAD_DIM).reshape(1, LANES))
            qkv_c, kc, vc = _pre_attention(
                ctx, mods, layer, gain1, w_in, odd=True, is_lat=False, seq=seq,
                head_gains=head_gains)
            (qkv_l,) = _pre_attention(
                lat, mods, layer, gain1, w_in, odd=True, is_lat=True, seq=dec_seq, rope=rope,
                head_gains=head_gains)
            states["c_k"].append(kc.reshape(batch, seq, C_KV_HEADS, HEAD_DIM))
            states["c_v"].append(vc.reshape(batch, seq, C_KV_HEADS, HEAD_DIM))
            o_ctx = [_ctx_odd_attention(qkv_c, seq)]
            o_lat = [_lat_c_attention(qkv_l, _token_head_rows(cache_c_k),
                                      _token_head_rows(cache_c_v), o, dec_seq)]
        w_gu = w_gate_up[layer].astype(BF16)
        w_dn = w_down[layer].astype(BF16)
        fg = final_gain if last else None
        ctx = _post_attention(o_ctx, ctx, mods, layer, gain2, w_out, w_gu, w_dn,
                              is_lat=False, seq=seq, final_gain=fg)
        lat = _post_attention(o_lat, lat, mods, layer, gain2, w_out, w_gu, w_dn,
                              is_lat=True, seq=dec_seq, final_gain=fg)

    return (ctx.reshape(batch, seq, d), lat.reshape(dec_batch, dec_seq, d),
            jnp.stack(states["a_k"], axis=1), jnp.stack(states["a_v"], axis=1),
            jnp.stack(states["b_k"], axis=1), jnp.stack(states["b_v"], axis=1),
            jnp.stack(states["c_k"], axis=1), jnp.stack(states["c_v"], axis=1))
```

```python
import functools
import math

import jax
import jax.numpy as jnp
from jax import lax
from jax.experimental import pallas as pl
from jax.experimental.pallas import tpu as pltpu

F32 = jnp.float32
BF16 = jnp.bfloat16

D_MODEL = 1024
GRID_W = 64
HEAD_DIM = 64
A_HEADS = 8
B_HEADS = 8
B_KV_HEADS = 2
C_HEADS = 16
C_KV_HEADS = 4
NA_WIN_H = 8
NA_WIN_W = 16
B_WINDOW = 128
ROPE_THETA = 10000.0
RMS_EPS = 1e-6
NEG_INF = -1e30
QK_SCALE = 1.0 / math.sqrt(HEAD_DIM)

LANES = 128
TOKEN_TILE = 512
Q_TILE = 256
FF_CHUNK = 256
MOD_GROUPS = 16
VMEM_LIMIT = 56 * 1024 * 1024


def _params(n_axes, vmem=VMEM_LIMIT):
    return pltpu.CompilerParams(
        dimension_semantics=("arbitrary",) * n_axes, vmem_limit_bytes=vmem)


def _resident(shape):
    nd = len(shape)
    return pl.BlockSpec(shape, lambda *_: (0,) * nd, pipeline_mode=pl.Buffered(1))


def _mod_kernel(c_ref, w_ref, b_ref, o_ref):
    c = c_ref[...]
    s = (c * jax.nn.sigmoid(c)).astype(BF16)
    o_ref[0] = jnp.dot(s, w_ref[0].astype(BF16), preferred_element_type=F32) + b_ref[0]


def _modulation(cvec, w_mod, b_mod):
    depth, d, n = w_mod.shape
    tn = 1536
    return pl.pallas_call(
        _mod_kernel,
        grid=(depth, n // tn),
        in_specs=[
            pl.BlockSpec((MOD_GROUPS, d), lambda l, j: (0, 0)),
            pl.BlockSpec((1, d, tn), lambda l, j: (l, 0, j)),
            pl.BlockSpec((1, 1, tn), lambda l, j: (l, 0, j)),
        ],
        out_specs=pl.BlockSpec((1, MOD_GROUPS, tn), lambda l, j: (l, 0, j)),
        out_shape=jax.ShapeDtypeStruct((depth, MOD_GROUPS, n), F32),
        compiler_params=_params(2),
        name="modulation",
    )(cvec, w_mod, b_mod.reshape(depth, 1, n))


def _mod_spec(layer, which, group_of_step):
    return pl.BlockSpec((1, 1, 1, D_MODEL), lambda i: (layer, group_of_step(i), 0, which))


def _adaln(x, gain, shift, scale):
    ms = jnp.mean(x * x, axis=-1, keepdims=True)
    return (x * lax.rsqrt(ms + RMS_EPS)) * gain * (1.0 + scale) + shift


def _rope(y, cos, sin_lo, sin_hi):
    outs = []
    for c in range(y.shape[1] // LANES):
        yc = y[:, c * LANES:(c + 1) * LANES]
        outs.append(yc * cos
                    + pltpu.roll(yc, LANES - 16, 1) * sin_lo
                    + pltpu.roll(yc, 16, 1) * sin_hi)
    return outs[0] if len(outs) == 1 else jnp.concatenate(outs, axis=1)


def _head_rms_norm(y, gain):
    first = lax.broadcasted_iota(jnp.int32, (1, LANES), 1) < HEAD_DIM
    outs = []
    for c in range(y.shape[1] // LANES):
        yc = y[:, c * LANES:(c + 1) * LANES]
        sq = yc * yc
        s0 = jnp.sum(jnp.where(first, sq, 0.0), axis=-1, keepdims=True)
        s1 = jnp.sum(jnp.where(first, 0.0, sq), axis=-1, keepdims=True)
        ms = jnp.where(first, s0, s1) * (1.0 / HEAD_DIM)
        outs.append(yc * lax.rsqrt(ms + RMS_EPS) * gain)
    return outs[0] if len(outs) == 1 else jnp.concatenate(outs, axis=1)


def _attend(q, segments, sink=None):
    scores = []
    for k, _, bias, valid in segments:
        s = lax.dot_general(q, k, (((1,), (1,)), ((), ())), preferred_element_type=F32)
        if bias is not None:
            s = s + bias
        if valid is not None:
            s = jnp.where(valid, s, NEG_INF)
        scores.append(s)
    m = functools.reduce(jnp.maximum, [jnp.max(s, axis=-1, keepdims=True) for s in scores])
    if sink is not None:
        m = jnp.maximum(m, sink)
    denom = None
    out = None
    for s, (_, v, _, _) in zip(scores, segments):
        p = jnp.exp(s - m)
        part = jnp.sum(p, axis=-1, keepdims=True)
        pv = jnp.dot(p.astype(BF16), v, preferred_element_type=F32)
        denom = part if denom is None else denom + part
        out = pv if out is None else out + pv
    if sink is not None:
        denom = denom + jnp.exp(sink - m)
    return out / denom


def _stack_group(ref, rows, col0, group):
    return jnp.concatenate(
        [ref[rows, col0 + g * HEAD_DIM: col0 + (g + 1) * HEAD_DIM] for g in range(group)], axis=0)


def _unstack_group(o, rows, group):
    return jnp.concatenate([o[g * rows:(g + 1) * rows] for g in range(group)], axis=1)


def _store_head_major(ref, y):
    nb, heads, seq, _ = ref.shape
    for b in range(nb):
        for h in range(heads):
            ref[b, h] = y[b * seq:(b + 1) * seq, h * HEAD_DIM:(h + 1) * HEAD_DIM]


def _pre_even_kernel(x_ref, g_ref, sh_ref, sc_ref, w_ref, *rest, is_lat):
    if is_lat:
        cos_ref, slo_ref, shi_ref, qkv_ref = rest
    else:
        qkv_ref, ka_ref, va_ref, kb_ref, vb_ref = rest
    h = _adaln(x_ref[...], g_ref[...], sh_ref[0, 0], sc_ref[0, 0]).astype(BF16)

    def proj(c0, c1):
        return jnp.dot(h, w_ref[:, c0:c1], preferred_element_type=F32)

    na = A_HEADS * HEAD_DIM
    nb = B_HEADS * HEAD_DIM
    nkv = B_KV_HEADS * HEAD_DIM
    c = 0
    qkv_ref[:, c:c + na] = (proj(c, c + na) * QK_SCALE).astype(BF16)
    c += na
    ka = proj(c, c + na)
    qkv_ref[:, c:c + na] = ka.astype(BF16)
    c += na
    va = proj(c, c + na)
    qkv_ref[:, c:c + na] = va.astype(BF16)
    c += na
    qb = proj(c, c + nb)
    if is_lat:
        qb = _rope(qb, cos_ref[...], slo_ref[...], shi_ref[...])
    qkv_ref[:, c:c + nb] = (qb * QK_SCALE).astype(BF16)
    c += nb
    kb = proj(c, c + nkv)
    if is_lat:
        kb_out = _rope(kb, cos_ref[...], slo_ref[...], shi_ref[...])
    else:
        kb_out = kb
    qkv_ref[:, c:c + nkv] = kb_out.astype(BF16)
    c += nkv
    vb = proj(c, c + nkv)
    qkv_ref[:, c:c + nkv] = vb.astype(BF16)
    if not is_lat:
        _store_head_major(ka_ref, ka)
        _store_head_major(va_ref, va)
        _store_head_major(kb_ref, kb)
        _store_head_major(vb_ref, vb)


def _pre_odd_kernel(x_ref, g_ref, sh_ref, sc_ref, w_ref, qn_ref, kn_ref, *rest, is_lat):
    if is_lat:
        cos_ref, slo_ref, shi_ref, qkv_ref = rest
    else:
        qkv_ref, kc_ref, vc_ref = rest
    h = _adaln(x_ref[...], g_ref[...], sh_ref[0, 0], sc_ref[0, 0]).astype(BF16)

    def proj(c0, c1):
        return jnp.dot(h, w_ref[:, c0:c1], preferred_element_type=F32)

    nq = C_HEADS * HEAD_DIM
    nkv = C_KV_HEADS * HEAD_DIM
    q = _head_rms_norm(proj(0, nq), qn_ref[...])
    k = _head_rms_norm(proj(nq, nq + nkv), kn_ref[...])
    v = proj(nq + nkv, nq + 2 * nkv)
    if is_lat:
        q = _rope(q, cos_ref[...], slo_ref[...], shi_ref[...])
        k_out = _rope(k, cos_ref[...], slo_ref[...], shi_ref[...])
    else:
        k_out = k
        _store_head_major(kc_ref, k)
        _store_head_major(vc_ref, v)
    qkv_ref[:, 0:nq] = (q * QK_SCALE).astype(BF16)
    qkv_ref[:, nq:nq + nkv] = k_out.astype(BF16)
    qkv_ref[:, nq + nkv:nq + 2 * nkv] = v.astype(BF16)


def _pre_attention(x, mods, layer, gain, w, *, odd, is_lat, seq, rope=None, head_gains=None):
    t, d = x.shape
    n = w.shape[1]
    tm = TOKEN_TILE
    per_seq = seq // tm
    group = (lambda i: 1 + i // per_seq) if is_lat else (lambda i: 0)
    row = lambda i: (i, 0)
    in_specs = [
        pl.BlockSpec((tm, d), row),
        _resident((1, d)),
        _mod_spec(layer, 0, group),
        _mod_spec(layer, 1, group),
        _resident((d, n)),
    ]
    args = [x, gain.reshape(1, d), mods, mods, w]
    if odd:
        in_specs += [_resident((1, LANES)), _resident((1, LANES))]
        args += list(head_gains)
    if is_lat:
        in_specs += [pl.BlockSpec((tm, LANES), lambda i: (i % per_seq, 0))] * 3
        args += list(rope)
    out_specs = [pl.BlockSpec((tm, n), row)]
    out_shape = [jax.ShapeDtypeStruct((t, n), BF16)]
    if not is_lat:
        if odd:
            widths = [C_KV_HEADS * HEAD_DIM] * 2
        else:
            widths = [A_HEADS * HEAD_DIM] * 2 + [B_KV_HEADS * HEAD_DIM] * 2
        for wd in widths:
            heads = wd // HEAD_DIM
            out_specs.append(pl.BlockSpec((tm // seq, heads, seq, HEAD_DIM), lambda i: (i, 0, 0, 0)))
            out_shape.append(jax.ShapeDtypeStruct((t // seq, heads, seq, HEAD_DIM), F32))
    body = _pre_odd_kernel if odd else _pre_even_kernel
    return pl.pallas_call(
        functools.partial(body, is_lat=is_lat),
        grid=(t // tm,),
        in_specs=in_specs,
        out_specs=out_specs,
        out_shape=out_shape,
        compiler_params=_params(1),
        name=f"pre_{'odd' if odd else 'even'}_{'lat' if is_lat else 'ctx'}",
    )(*args)


def _ctx_even_attn_kernel(sink_ref, qkv_ref, o_ref):
    rows = slice(None)
    n = qkv_ref.shape[0]
    na = A_HEADS * HEAD_DIM
    for pair in range(A_HEADS // 2):
        outs = []
        for h in (2 * pair, 2 * pair + 1):
            c = h * HEAD_DIM
            q = qkv_ref[:, c:c + HEAD_DIM]
            k = qkv_ref[:, na + c:na + c + HEAD_DIM]
            v = qkv_ref[:, 2 * na + c:2 * na + c + HEAD_DIM]
            outs.append(_attend(q, [(k, v, None, None)]))
        o_ref[:, pair * LANES:(pair + 1) * LANES] = jnp.concatenate(outs, axis=1).astype(BF16)
    group = B_HEADS // B_KV_HEADS
    qb0 = 3 * na
    kb0 = qb0 + B_HEADS * HEAD_DIM
    vb0 = kb0 + B_KV_HEADS * HEAD_DIM
    for kv in range(B_KV_HEADS):
        q = _stack_group(qkv_ref, rows, qb0 + kv * group * HEAD_DIM, group)
        k = qkv_ref[:, kb0 + kv * HEAD_DIM:kb0 + (kv + 1) * HEAD_DIM]
        v = qkv_ref[:, vb0 + kv * HEAD_DIM:vb0 + (kv + 1) * HEAD_DIM]
        sink = jnp.concatenate(
            [jnp.full((n, 1), sink_ref[kv * group + g], F32) for g in range(group)], axis=0)
        o = _attend(q, [(k, v, None, None)], sink)
        c0 = na + kv * group * HEAD_DIM
        o_ref[:, c0:c0 + group * HEAD_DIM] = _unstack_group(o, n, group).astype(BF16)


def _ctx_even_attention(qkv, sink, seq):
    t, n = qkv.shape
    width = (A_HEADS + B_HEADS) * HEAD_DIM
    return pl.pallas_call(
        _ctx_even_attn_kernel,
        grid=(t // seq,),
        in_specs=[pl.BlockSpec(memory_space=pltpu.SMEM),
                  pl.BlockSpec((seq, n), lambda b: (b, 0))],
        out_specs=pl.BlockSpec((seq, width), lambda b: (b, 0)),
        out_shape=jax.ShapeDtypeStruct((t, width), BF16),
        compiler_params=_params(1),
        name="attn_even_ctx",
    )(sink, qkv)


def _na_key_window(j):
    return min(max(Q_TILE * j - Q_TILE, 0), Q_TILE)


def _na_bias_tiles(src_ref, h):
    qc = lax.broadcasted_iota(jnp.int32, (GRID_W, LANES), 0)
    kc = lax.rem(lax.broadcasted_iota(jnp.int32, (GRID_W, LANES), 1), GRID_W)
    cs = jnp.clip(qc - NA_WIN_W // 2, 0, GRID_W - NA_WIN_W)
    col_ok = (kc >= cs) & (kc < cs + NA_WIN_W)
    tiles = []
    for i in range(2 * NA_WIN_H):
        src = jnp.broadcast_to(src_ref[h, i:i + 1, :], (GRID_W, LANES))
        tiles.append(jnp.where(col_ok, pltpu.roll(src, 0, 1, stride=1, stride_axis=0), NEG_INF))
    return tiles


def _na_bias_block(tiles, j, grid_rows):
    wh = min(NA_WIN_H, grid_rows)
    first_row = lax.broadcasted_iota(jnp.int32, (GRID_W, LANES), 1) < GRID_W
    lo_r = _na_key_window(j) // GRID_W
    strips = []
    for qr in range(j * Q_TILE // GRID_W, (j + 1) * Q_TILE // GRID_W):
        rs = min(max(qr - wh // 2, 0), grid_rows - wh)
        parts = []
        for t in range(3 * Q_TILE // LANES):
            kr = lo_r + 2 * t
            ok0 = rs <= kr < rs + wh
            ok1 = rs <= kr + 1 < rs + wh
            i = kr - qr + NA_WIN_H
            if ok0 and ok1:
                parts.append(tiles[i])
            elif ok0:
                parts.append(jnp.where(first_row, tiles[i], NEG_INF))
            elif ok1:
                parts.append(jnp.where(first_row, NEG_INF, tiles[i]))
            else:
                parts.append(jnp.full((GRID_W, LANES), NEG_INF, F32))
        strips.append(jnp.concatenate(parts, axis=1))
    return jnp.concatenate(strips, axis=0)


def _cache_head(ref, h, heads):
    assert ref.shape[2] == heads
    return ref[0, 0, h].astype(BF16)


def _lat_a_kernel(src_ref, q_ref, k_ref, v_ref, ck_ref, cv_ref, o_ref):
    n = q_ref.shape[0]
    win = 3 * Q_TILE
    for pair in range(A_HEADS // 2):
        outs = [[] for _ in range(n // Q_TILE)]
        for h in (2 * pair, 2 * pair + 1):
            cols = slice(h * HEAD_DIM, (h + 1) * HEAD_DIM)
            ctx = (_cache_head(ck_ref, h, A_HEADS), _cache_head(cv_ref, h, A_HEADS), None, None)
            tiles = _na_bias_tiles(src_ref, h)
            for j in range(n // Q_TILE):
                lo = _na_key_window(j)
                q_rows = slice(j * Q_TILE, (j + 1) * Q_TILE)
                local = (k_ref[lo:lo + win, cols], v_ref[lo:lo + win, cols],
                         _na_bias_block(tiles, j, n // GRID_W), None)
                outs[j].append(_attend(q_ref[q_rows, cols], [local, ctx]))
        for j in range(n // Q_TILE):
            o_ref[j * Q_TILE:(j + 1) * Q_TILE, pair * LANES:(pair + 1) * LANES] = (
                jnp.concatenate(outs[j], axis=1).astype(BF16))


def _head_major(cache):
    return cache.transpose(0, 1, 3, 2, 4)


def _cache_spec(cache, layer):
    blk = (1, 1) + cache.shape[2:]
    return pl.BlockSpec(blk, lambda b: (b, layer, 0, 0, 0))


def _lat_a_attention(qkv, cache_k, cache_v, e, bias_src, seq):
    t = qkv.shape[0]
    na = A_HEADS * HEAD_DIM
    return pl.pallas_call(
        _lat_a_kernel,
        grid=(t // seq,),
        in_specs=[
            _resident(bias_src.shape),
            pl.BlockSpec((seq, na), lambda b: (b, 0)),
            pl.BlockSpec((seq, na), lambda b: (b, 1)),
            pl.BlockSpec((seq, na), lambda b: (b, 2)),
            _cache_spec(cache_k, e),
            _cache_spec(cache_v, e),
        ],
        out_specs=pl.BlockSpec((seq, na), lambda b: (b, 0)),
        out_shape=jax.ShapeDtypeStruct((t, na), BF16),
        compiler_params=_params(1),
        name="attn_even_lat_a",
    )(bias_src, qkv, qkv, qkv, cache_k, cache_v)


def _lat_b_kernel(sink_ref, q_ref, k_ref, v_ref, ck_ref, cv_ref, o_ref):
    ctx = [(_cache_head(ck_ref, kv, B_KV_HEADS), _cache_head(cv_ref, kv, B_KV_HEADS), None, None)
           for kv in range(B_KV_HEADS)]
    n = q_ref.shape[0]
    group = B_HEADS // B_KV_HEADS
    win = 2 * Q_TILE
    for j in range(n // Q_TILE):
        lo = min(max(Q_TILE * j - B_WINDOW, 0), n - win)
        q_rows = slice(j * Q_TILE, (j + 1) * Q_TILE)
        qpos = j * Q_TILE + lax.broadcasted_iota(jnp.int32, (group * Q_TILE, 1), 0) % Q_TILE
        kpos = lo + lax.broadcasted_iota(jnp.int32, (1, win), 1)
        valid = jnp.abs(qpos - kpos) <= B_WINDOW
        for kv in range(B_KV_HEADS):
            cols = slice(kv * HEAD_DIM, (kv + 1) * HEAD_DIM)
            q = _stack_group(q_ref, q_rows, kv * group * HEAD_DIM, group)
            sink = jnp.concatenate(
                [jnp.full((Q_TILE, 1), sink_ref[kv * group + g], F32) for g in range(group)], axis=0)
            local = (k_ref[lo:lo + win, cols], v_ref[lo:lo + win, cols], None, valid)
            o = _attend(q, [local, ctx[kv]], sink)
            c0 = kv * group * HEAD_DIM
            o_ref[q_rows, c0:c0 + group * HEAD_DIM] = _unstack_group(o, Q_TILE, group).astype(BF16)


def _lat_b_attention(qkv, cache_k, cache_v, e, sink, seq):
    t = qkv.shape[0]
    na = A_HEADS * HEAD_DIM
    nb = B_HEADS * HEAD_DIM
    nkv = B_KV_HEADS * HEAD_DIM
    return pl.pallas_call(
        _lat_b_kernel,
        grid=(t // seq,),
        in_specs=[
            pl.BlockSpec(memory_space=pltpu.SMEM),
            pl.BlockSpec((seq, nb), lambda b: (b, 3 * na // nb)),
            pl.BlockSpec((seq, nkv), lambda b: (b, (3 * na + nb) // nkv)),
            pl.BlockSpec((seq, nkv), lambda b: (b, (3 * na + nb) // nkv + 1)),
            _cache_spec(cache_k, e),
            _cache_spec(cache_v, e),
        ],
        out_specs=pl.BlockSpec((seq, nb), lambda b: (b, 0)),
        out_shape=jax.ShapeDtypeStruct((t, nb), BF16),
        compiler_params=_params(1),
        name="attn_even_lat_b",
    )(sink, qkv, qkv, qkv, cache_k, cache_v)


def _ctx_odd_attn_kernel(qkv_ref, o_ref):
    n = qkv_ref.shape[0]
    group = C_HEADS // C_KV_HEADS
    nq = C_HEADS * HEAD_DIM
    nkv = C_KV_HEADS * HEAD_DIM
    for kv in range(C_KV_HEADS):
        q = _stack_group(qkv_ref, slice(None), kv * group * HEAD_DIM, group)
        k = qkv_ref[:, nq + kv * HEAD_DIM:nq + (kv + 1) * HEAD_DIM]
        v = qkv_ref[:, nq + nkv + kv * HEAD_DIM:nq + nkv + (kv + 1) * HEAD_DIM]
        o = _attend(q, [(k, v, None, None)])
        c0 = kv * group * HEAD_DIM
        o_ref[:, c0:c0 + group * HEAD_DIM] = _unstack_group(o, n, group).astype(BF16)


def _ctx_odd_attention(qkv, seq):
    t, n = qkv.shape
    width = C_HEADS * HEAD_DIM
    return pl.pallas_call(
        _ctx_odd_attn_kernel,
        grid=(t // seq,),
        in_specs=[pl.BlockSpec((seq, n), lambda b: (b, 0))],
        out_specs=pl.BlockSpec((seq, width), lambda b: (b, 0)),
        out_shape=jax.ShapeDtypeStruct((t, width), BF16),
        compiler_params=_params(1),
        name="attn_odd_ctx",
    )(qkv)


def _lat_c_kernel(qkv_ref, ck_ref, cv_ref, o_ref, ckb_ref, cvb_ref):
    n = qkv_ref.shape[0]
    group = C_HEADS // C_KV_HEADS
    nq = C_HEADS * HEAD_DIM
    nkv = C_KV_HEADS * HEAD_DIM
    for kv in range(C_KV_HEADS):
        ckb_ref[kv] = _cache_head(ck_ref, kv, C_KV_HEADS)
        cvb_ref[kv] = _cache_head(cv_ref, kv, C_KV_HEADS)

    def q_block(j, carry):
        q_rows = pl.ds(pl.multiple_of(j * Q_TILE, Q_TILE), Q_TILE)
        for kv in range(C_KV_HEADS):
            q = _stack_group(qkv_ref, q_rows, kv * group * HEAD_DIM, group)
            k = qkv_ref[:, nq + kv * HEAD_DIM:nq + (kv + 1) * HEAD_DIM]
            v = qkv_ref[:, nq + nkv + kv * HEAD_DIM:nq + nkv + (kv + 1) * HEAD_DIM]
            o = _attend(q, [(ckb_ref[kv], cvb_ref[kv], None, None), (k, v, None, None)])
            c0 = kv * group * HEAD_DIM
            o_ref[q_rows, c0:c0 + group * HEAD_DIM] = _unstack_group(o, Q_TILE, group).astype(BF16)
        return carry

    lax.fori_loop(0, n // Q_TILE, q_block, 0)


def _lat_c_attention(qkv, cache_k, cache_v, o, seq):
    t, n = qkv.shape
    past = cache_k.shape[3]
    width = C_HEADS * HEAD_DIM
    return pl.pallas_call(
        _lat_c_kernel,
        grid=(t // seq,),
        in_specs=[
            pl.BlockSpec((seq, n), lambda b: (b, 0)),
            _cache_spec(cache_k, o),
            _cache_spec(cache_v, o),
        ],
        out_specs=pl.BlockSpec((seq, width), lambda b: (b, 0)),
        out_shape=jax.ShapeDtypeStruct((t, width), BF16),
        scratch_shapes=[pltpu.VMEM((C_KV_HEADS, past, HEAD_DIM), BF16)] * 2,
        compiler_params=_params(1),
        name="attn_odd_lat",
    )(qkv, cache_k, cache_v)


def _post_kernel(*refs, n_parts, final):
    o_refs = refs[:n_parts]
    (x_ref, wo_ref, g1_ref, sh_ref, sc_ref, g2_ref, gain_ref, wgu_ref, wd_ref) = refs[n_parts:n_parts + 9]
    rest = refs[n_parts + 9:]
    if final:
        fg_ref, out_ref, act_ref = rest
    else:
        out_ref, act_ref = rest
    mix = None
    r0 = 0
    for o_ref in o_refs:
        kk = o_ref.shape[1]
        part = jnp.dot(o_ref[...], wo_ref[r0:r0 + kk, :], preferred_element_type=F32)
        mix = part if mix is None else mix + part
        r0 += kk
    x1 = x_ref[...] + g1_ref[0, 0] * mix
    h = _adaln(x1, gain_ref[...], sh_ref[0, 0], sc_ref[0, 0]).astype(BF16)
    d_ff = wd_ref.shape[0]
    for j in range(d_ff // FF_CHUNK):
        c0 = j * FF_CHUNK
        gate = jnp.dot(h, wgu_ref[:, c0:c0 + FF_CHUNK], preferred_element_type=F32)
        up = jnp.dot(h, wgu_ref[:, d_ff + c0:d_ff + c0 + FF_CHUNK], preferred_element_type=F32)
        act_ref[:, c0:c0 + FF_CHUNK] = (gate * jax.nn.sigmoid(gate) * up).astype(BF16)
    ffn = jnp.dot(act_ref[...], wd_ref[...], preferred_element_type=F32)
    x2 = x1 + g2_ref[0, 0] * ffn
    if final:
        ms = jnp.mean(x2 * x2, axis=-1, keepdims=True)
        x2 = (x2 * lax.rsqrt(ms + RMS_EPS)) * fg_ref[...]
    out_ref[...] = x2


def _post_attention(o_parts, x, mods, layer, gain, w_out, w_gu, w_down, *, is_lat, seq,
                    final_gain=None):
    t, d = x.shape
    tm = TOKEN_TILE
    per_seq = seq // tm
    group = (lambda i: 1 + i // per_seq) if is_lat else (lambda i: 0)
    row = lambda i: (i, 0)
    d_ff = w_down.shape[0]
    final = final_gain is not None
    in_specs = [pl.BlockSpec((tm, o.shape[1]), row) for o in o_parts]
    in_specs += [
        pl.BlockSpec((tm, d), row),
        _resident(w_out.shape),
        _mod_spec(layer, 2, group),
        _mod_spec(layer, 3, group),
        _mod_spec(layer, 4, group),
        _mod_spec(layer, 5, group),
        _resident((1, d)),
        _resident(w_gu.shape),
        _resident(w_down.shape),
    ]
    args = list(o_parts) + [x, w_out, mods, mods, mods, mods, gain.reshape(1, d), w_gu, w_down]
    if final:
        in_specs.append(_resident((1, d)))
        args.append(final_gain.reshape(1, d))
    return pl.pallas_call(
        functools.partial(_post_kernel, n_parts=len(o_parts), final=final),
        grid=(t // tm,),
        in_specs=in_specs,
        out_specs=pl.BlockSpec((tm, d), row),
        out_shape=jax.ShapeDtypeStruct((t, d), F32),
        scratch_shapes=[pltpu.VMEM((tm, d_ff), BF16)],
        compiler_params=_params(1),
        name=f"post_{'lat' if is_lat else 'ctx'}{'_final' if final else ''}",
    )(*args)


def _rope_tables(n):
    t = jnp.arange(n)
    row = (t // GRID_W).astype(F32)
    col = (t % GRID_W).astype(F32)
    half = HEAD_DIM // 2
    inv_freq = ROPE_THETA ** (-jnp.arange(0, half, 2, dtype=F32) / half)
    lane = jnp.arange(LANES)
    in_head = lane % HEAD_DIM
    pos = jnp.where((in_head < half)[None, :], row[:, None], col[:, None])
    ang = pos * inv_freq[in_head % (half // 2)][None, :]
    first = ((in_head % half) < half // 2)[None, :]
    cos = jnp.cos(ang)
    sin = jnp.sin(ang)
    return cos, jnp.where(first, -sin, 0.0), jnp.where(first, 0.0, sin)


def _na_bias_sources(rpb):
    h, _, nb = rpb.shape
    w = NA_WIN_W - 1
    rp = jnp.pad(rpb, ((0, 0), (1, 1), (0, 0)))
    lo, hi = rp[:, :-1], rp[:, 1:]
    z = jnp.zeros((h, 2 * NA_WIN_H, LANES // 2 - nb), F32)
    return jnp.concatenate([lo[:, :, w:], z, hi, z, lo[:, :, :w]], axis=-1)


def kernel(x_prompt, x_sample, cache_a_k, cache_a_v, cache_b_k, cache_b_v, cache_c_k, cache_c_v,
           c, c_ctx, norm_gain, w_mod, b_mod, w_in_even, w_out_even, rpb_a, sink_b,
           w_in_odd, w_out_odd, q_norm_c, k_norm_c, w_gate_up, w_down, final_gain):
    batch, seq, d = x_prompt.shape
    dec_batch, dec_seq, _ = x_sample.shape
    depth = w_mod.shape[0]

    cvec = jnp.concatenate(
        [c_ctx[None, :], c, jnp.zeros((MOD_GROUPS - 1 - dec_batch, d), F32)], axis=0)
    mods = _modulation(cvec, w_mod, b_mod).reshape(depth, MOD_GROUPS, 1, 6 * d)
    rope = _rope_tables(dec_seq)

    ctx = x_prompt.reshape(batch * seq, d)
    lat = x_sample.reshape(dec_batch * dec_seq, d)
    states = {name: [] for name in ("a_k", "a_v", "b_k", "b_v", "c_k", "c_v")}

    for layer in range(depth):
        last = layer == depth - 1
        gain1, gain2 = norm_gain[layer, 0], norm_gain[layer, 1]
        if layer % 2 == 0:
            e = layer // 2
            w_in = w_in_even[e].astype(BF16)
            w_out = w_out_even[e].astype(BF16)
            qkv_c, ka, va, kb, vb = _pre_attention(
                ctx, mods, layer, gain1, w_in, odd=False, is_lat=False, seq=seq)
            (qkv_l,) = _pre_attention(
                lat, mods, layer, gain1, w_in, odd=False, is_lat=True, seq=dec_seq, rope=rope)
            states["a_k"].append(ka.transpose(0, 2, 1, 3))
            states["a_v"].append(va.transpose(0, 2, 1, 3))
            states["b_k"].append(kb.transpose(0, 2, 1, 3))
            states["b_v"].append(vb.transpose(0, 2, 1, 3))
            o_ctx = [_ctx_even_attention(qkv_c, sink_b[e], seq)]
            o_lat = [
                _lat_a_attention(qkv_l, _head_major(cache_a_k), _head_major(cache_a_v),
                                 e, _na_bias_sources(rpb_a[e]), dec_seq),
                _lat_b_attention(qkv_l, _head_major(cache_b_k), _head_major(cache_b_v),
                                 e, sink_b[e], dec_seq),
            ]
        else:
            o = layer // 2
            w_in = w_in_odd[o].astype(BF16)
            w_out = w_out_odd[o].astype(BF16)
            head_gains = (jnp.tile(q_norm_c[o], LANES // HEAD_DIM).reshape(1, LANES),
                          jnp.tile(k_norm_c[o], LANES // HEAD_DIM).reshape(1, LANES))
            qkv_c, kc, vc = _pre_attention(
                ctx, mods, layer, gain1, w_in, odd=True, is_lat=False, seq=seq,
                head_gains=head_gains)
            (qkv_l,) = _pre_attention(
                lat, mods, layer, gain1, w_in, odd=True, is_lat=True, seq=dec_seq, rope=rope,
                head_gains=head_gains)
            states["c_k"].append(kc.transpose(0, 2, 1, 3))
            states["c_v"].append(vc.transpose(0, 2, 1, 3))
            o_ctx = [_ctx_odd_attention(qkv_c, seq)]
            o_lat = [_lat_c_attention(qkv_l, _head_major(cache_c_k),
                                      _head_major(cache_c_v), o, dec_seq)]
        w_gu = w_gate_up[layer].astype(BF16)
        w_dn = w_down[layer].astype(BF16)
        fg = final_gain if last else None
        ctx = _post_attention(o_ctx, ctx, mods, layer, gain2, w_out, w_gu, w_dn,
                              is_lat=False, seq=seq, final_gain=fg)
        lat = _post_attention(o_lat, lat, mods, layer, gain2, w_out, w_gu, w_dn,
                              is_lat=True, seq=dec_seq, final_gain=fg)

    return (ctx.reshape(batch, seq, d), lat.reshape(dec_batch, dec_seq, d),
            jnp.stack(states["a_k"], axis=1), jnp.stack(states["a_v"], axis=1),
            jnp.stack(states["b_k"], axis=1), jnp.stack(states["b_v"], axis=1),
            jnp.stack(states["c_k"], axis=1), jnp.stack(states["c_v"], axis=1))
```

```python
import functools
import math

import jax
import jax.numpy as jnp
from jax import lax
from jax.experimental import pallas as pl
from jax.experimental.pallas import tpu as pltpu

F32 = jnp.float32
BF16 = jnp.bfloat16

D_MODEL = 1024
GRID_W = 64
HEAD_DIM = 64
A_HEADS = 8
B_HEADS = 8
B_KV_HEADS = 2
C_HEADS = 16
C_KV_HEADS = 4
NA_WIN_H = 8
NA_WIN_W = 16
B_WINDOW = 128
ROPE_THETA = 10000.0
RMS_EPS = 1e-6
NEG_INF = -1e30
QK_SCALE = 1.0 / math.sqrt(HEAD_DIM)

LANES = 128
TOKEN_TILE = 512
Q_TILE = 256
ROW_CHUNK = 32
FF_CHUNK = 256
MOD_GROUPS = 16
VMEM_LIMIT = 56 * 1024 * 1024


def _params(n_axes, vmem=VMEM_LIMIT):
    return pltpu.CompilerParams(
        dimension_semantics=("arbitrary",) * n_axes, vmem_limit_bytes=vmem)


def _resident(shape):
    nd = len(shape)
    return pl.BlockSpec(shape, lambda *_: (0,) * nd, pipeline_mode=pl.Buffered(1))


def _mod_kernel(c_ref, w_ref, b_ref, o_ref):
    c = c_ref[...]
    s = (c * jax.nn.sigmoid(c)).astype(BF16)
    o_ref[0] = jnp.dot(s, w_ref[0].astype(BF16), preferred_element_type=F32) + b_ref[0]


def _modulation(cvec, w_mod, b_mod):
    depth, d, n = w_mod.shape
    tn = 1536
    return pl.pallas_call(
        _mod_kernel,
        grid=(depth, n // tn),
        in_specs=[
            pl.BlockSpec((MOD_GROUPS, d), lambda l, j: (0, 0)),
            pl.BlockSpec((1, d, tn), lambda l, j: (l, 0, j)),
            pl.BlockSpec((1, 1, tn), lambda l, j: (l, 0, j)),
        ],
        out_specs=pl.BlockSpec((1, MOD_GROUPS, tn), lambda l, j: (l, 0, j)),
        out_shape=jax.ShapeDtypeStruct((depth, MOD_GROUPS, n), F32),
        compiler_params=_params(2),
        name="modulation",
    )(cvec, w_mod, b_mod.reshape(depth, 1, n))


def _mod_spec(layer, which, group_of_step):
    return pl.BlockSpec((1, 1, 1, D_MODEL), lambda i: (layer, group_of_step(i), 0, which))


def _adaln(x, gain, shift, scale):
    ms = jnp.mean(x * x, axis=-1, keepdims=True)
    return (x * lax.rsqrt(ms + RMS_EPS)) * gain * (1.0 + scale) + shift


def _rope(y, cos, sin_lo, sin_hi):
    outs = []
    for c in range(y.shape[1] // LANES):
        yc = y[:, c * LANES:(c + 1) * LANES]
        outs.append(yc * cos
                    + pltpu.roll(yc, LANES - 16, 1) * sin_lo
                    + pltpu.roll(yc, 16, 1) * sin_hi)
    return outs[0] if len(outs) == 1 else jnp.concatenate(outs, axis=1)


def _head_rms_norm(y, gain):
    first = lax.broadcasted_iota(jnp.int32, (1, LANES), 1) < HEAD_DIM
    outs = []
    for c in range(y.shape[1] // LANES):
        yc = y[:, c * LANES:(c + 1) * LANES]
        sq = yc * yc
        s0 = jnp.sum(jnp.where(first, sq, 0.0), axis=-1, keepdims=True)
        s1 = jnp.sum(jnp.where(first, 0.0, sq), axis=-1, keepdims=True)
        ms = jnp.where(first, s0, s1) * (1.0 / HEAD_DIM)
        outs.append(yc * lax.rsqrt(ms + RMS_EPS) * gain)
    return outs[0] if len(outs) == 1 else jnp.concatenate(outs, axis=1)


_NT = (((1,), (1,)), ((), ()))
_NN = (((1,), (0,)), ((), ()))


def _attn_scratch(m, n):
    return [pltpu.VMEM((2, m, n), F32), pltpu.VMEM((2, m, n), BF16),
            pltpu.VMEM((2, 3, m, LANES), F32)]


def _slot(scratch, step):
    return tuple(r.at[step % 2] for r in scratch)


def _put_scores(s_ref, c0, q, k, *, k_is_transposed, bias=None, valid=None):
    s = lax.dot_general(q, k, _NN if k_is_transposed else _NT, preferred_element_type=F32)
    if bias is not None:
        s = s + bias
    if valid is not None:
        s = jnp.where(valid, s, NEG_INF)
    s_ref[:q.shape[0], c0:c0 + s.shape[1]] = s


def _softmax_pv(s_ref, p_ref, stat_ref, m_rows, values, sink=None):
    n = max(c1 for _, c1, _, _ in values)
    chunks = range(0, m_rows, ROW_CHUNK)
    lanes = [slice(c, c + LANES) for c in range(0, n, LANES)]
    for r in chunks:
        rows = slice(r, r + ROW_CHUNK)
        acc = s_ref[rows, lanes[0]]
        for cols in lanes[1:]:
            acc = jnp.maximum(acc, s_ref[rows, cols])
        stat_ref[0, rows, :] = acc
    m = jnp.max(stat_ref[0, :m_rows, :], axis=-1, keepdims=True)
    if sink is not None:
        m = jnp.maximum(m, sink)
    stat_ref[1, :m_rows, :] = jnp.broadcast_to(m, (m_rows, LANES))
    for r in chunks:
        rows = slice(r, r + ROW_CHUNK)
        mb = stat_ref[1, rows, :]
        acc = None
        for cols in lanes:
            p = jnp.exp(s_ref[rows, cols] - mb)
            acc = p if acc is None else acc + p
            p_ref[rows, cols] = p.astype(BF16)
        stat_ref[2, rows, :] = acc
    denom = jnp.sum(stat_ref[2, :m_rows, :], axis=-1, keepdims=True)
    if sink is not None:
        denom = denom + jnp.exp(sink - m)
    out = None
    for c0, c1, v, v_is_transposed in values:
        pv = lax.dot_general(p_ref[:m_rows, c0:c1], v, _NT if v_is_transposed else _NN,
                             preferred_element_type=F32)
        out = pv if out is None else out + pv
    return out / denom


def _stack_group(ref, rows, col0, group):
    return jnp.concatenate(
        [ref[rows, col0 + g * HEAD_DIM: col0 + (g + 1) * HEAD_DIM] for g in range(group)], axis=0)


def _unstack_group(o, rows, group):
    return jnp.concatenate([o[g * rows:(g + 1) * rows] for g in range(group)], axis=1)


def _sink_column(sink_ref, h0, group, rows):
    return jnp.concatenate(
        [jnp.full((rows, 1), sink_ref[h0 + g], F32) for g in range(group)], axis=0)


def _store_state(ref, y):
    nb, heads, _, seq = ref.shape
    yt = y.T
    for b in range(nb):
        for h in range(heads):
            ref[b, h] = yt[h * HEAD_DIM:(h + 1) * HEAD_DIM, b * seq:(b + 1) * seq]


def _pre_even_kernel(x_ref, g_ref, sh_ref, sc_ref, w_ref, *rest, is_lat):
    if is_lat:
        cos_ref, slo_ref, shi_ref, qkv_ref = rest
    else:
        qkv_ref, ka_ref, va_ref, kb_ref, vb_ref = rest
    h = _adaln(x_ref[...], g_ref[...], sh_ref[0, 0], sc_ref[0, 0]).astype(BF16)

    def proj(c0, c1):
        return jnp.dot(h, w_ref[:, c0:c1], preferred_element_type=F32)

    na = A_HEADS * HEAD_DIM
    nb = B_HEADS * HEAD_DIM
    nkv = B_KV_HEADS * HEAD_DIM
    c = 0
    qkv_ref[:, c:c + na] = (proj(c, c + na) * QK_SCALE).astype(BF16)
    c += na
    ka = proj(c, c + na)
    qkv_ref[:, c:c + na] = ka.astype(BF16)
    c += na
    va = proj(c, c + na)
    qkv_ref[:, c:c + na] = va.astype(BF16)
    c += na
    qb = proj(c, c + nb)
    if is_lat:
        qb = _rope(qb, cos_ref[...], slo_ref[...], shi_ref[...])
    qkv_ref[:, c:c + nb] = (qb * QK_SCALE).astype(BF16)
    c += nb
    kb = proj(c, c + nkv)
    if is_lat:
        kb_out = _rope(kb, cos_ref[...], slo_ref[...], shi_ref[...])
    else:
        kb_out = kb
    qkv_ref[:, c:c + nkv] = kb_out.astype(BF16)
    c += nkv
    vb = proj(c, c + nkv)
    qkv_ref[:, c:c + nkv] = vb.astype(BF16)
    if not is_lat:
        _store_state(ka_ref, ka)
        _store_state(va_ref, va)
        _store_state(kb_ref, kb)
        _store_state(vb_ref, vb)


def _pre_odd_kernel(x_ref, g_ref, sh_ref, sc_ref, w_ref, qn_ref, kn_ref, *rest, is_lat):
    if is_lat:
        cos_ref, slo_ref, shi_ref, qkv_ref = rest
    else:
        qkv_ref, kc_ref, vc_ref = rest
    h = _adaln(x_ref[...], g_ref[...], sh_ref[0, 0], sc_ref[0, 0]).astype(BF16)

    def proj(c0, c1):
        return jnp.dot(h, w_ref[:, c0:c1], preferred_element_type=F32)

    nq = C_HEADS * HEAD_DIM
    nkv = C_KV_HEADS * HEAD_DIM
    q = _head_rms_norm(proj(0, nq), qn_ref[...])
    k = _head_rms_norm(proj(nq, nq + nkv), kn_ref[...])
    v = proj(nq + nkv, nq + 2 * nkv)
    if is_lat:
        q = _rope(q, cos_ref[...], slo_ref[...], shi_ref[...])
        k_out = _rope(k, cos_ref[...], slo_ref[...], shi_ref[...])
    else:
        k_out = k
        _store_state(kc_ref, k)
        _store_state(vc_ref, v)
    qkv_ref[:, 0:nq] = (q * QK_SCALE).astype(BF16)
    qkv_ref[:, nq:nq + nkv] = k_out.astype(BF16)
    qkv_ref[:, nq + nkv:nq + 2 * nkv] = v.astype(BF16)


def _pre_attention(x, mods, layer, gain, w, *, odd, is_lat, seq, rope=None, head_gains=None):
    t, d = x.shape
    n = w.shape[1]
    tm = TOKEN_TILE
    per_seq = seq // tm
    group = (lambda i: 1 + i // per_seq) if is_lat else (lambda i: 0)
    row = lambda i: (i, 0)
    in_specs = [
        pl.BlockSpec((tm, d), row),
        _resident((1, d)),
        _mod_spec(layer, 0, group),
        _mod_spec(layer, 1, group),
        _resident((d, n)),
    ]
    args = [x, gain.reshape(1, d), mods, mods, w]
    if odd:
        in_specs += [_resident((1, LANES)), _resident((1, LANES))]
        args += list(head_gains)
    if is_lat:
        in_specs += [pl.BlockSpec((tm, LANES), lambda i: (i % per_seq, 0))] * 3
        args += list(rope)
    out_specs = [pl.BlockSpec((tm, n), row)]
    out_shape = [jax.ShapeDtypeStruct((t, n), BF16)]
    if not is_lat:
        if odd:
            widths = [C_KV_HEADS * HEAD_DIM] * 2
        else:
            widths = [A_HEADS * HEAD_DIM] * 2 + [B_KV_HEADS * HEAD_DIM] * 2
        for wd in widths:
            heads = wd // HEAD_DIM
            out_specs.append(pl.BlockSpec((tm // seq, heads, HEAD_DIM, seq), lambda i: (i, 0, 0, 0)))
            out_shape.append(jax.ShapeDtypeStruct((t // seq, heads, HEAD_DIM, seq), F32))
    body = _pre_odd_kernel if odd else _pre_even_kernel
    return pl.pallas_call(
        functools.partial(body, is_lat=is_lat),
        grid=(t // tm,),
        in_specs=in_specs,
        out_specs=out_specs,
        out_shape=out_shape,
        compiler_params=_params(1),
        name=f"pre_{'odd' if odd else 'even'}_{'lat' if is_lat else 'ctx'}",
    )(*args)


def _ctx_even_attn_kernel(sink_ref, qkv_ref, o_ref, *scratch):
    rows = slice(None)
    n = qkv_ref.shape[0]
    na = A_HEADS * HEAD_DIM
    step = 0
    for pair in range(A_HEADS // 2):
        outs = []
        for h in (2 * pair, 2 * pair + 1):
            slot = _slot(scratch, step)
            step += 1
            c = h * HEAD_DIM
            _put_scores(slot[0], 0, qkv_ref[:, c:c + HEAD_DIM],
                        qkv_ref[:, na + c:na + c + HEAD_DIM], k_is_transposed=False)
            v = qkv_ref[:, 2 * na + c:2 * na + c + HEAD_DIM]
            outs.append(_softmax_pv(*slot, n, [(0, n, v, False)]))
        o_ref[:, pair * LANES:(pair + 1) * LANES] = jnp.concatenate(outs, axis=1).astype(BF16)
    group = B_HEADS // B_KV_HEADS
    qb0 = 3 * na
    kb0 = qb0 + B_HEADS * HEAD_DIM
    vb0 = kb0 + B_KV_HEADS * HEAD_DIM
    for kv in range(B_KV_HEADS):
        slot = _slot(scratch, step)
        step += 1
        q = _stack_group(qkv_ref, rows, qb0 + kv * group * HEAD_DIM, group)
        _put_scores(slot[0], 0, q, qkv_ref[:, kb0 + kv * HEAD_DIM:kb0 + (kv + 1) * HEAD_DIM],
                    k_is_transposed=False)
        v = qkv_ref[:, vb0 + kv * HEAD_DIM:vb0 + (kv + 1) * HEAD_DIM]
        o = _softmax_pv(*slot, group * n, [(0, n, v, False)],
                        _sink_column(sink_ref, kv * group, group, n))
        c0 = na + kv * group * HEAD_DIM
        o_ref[:, c0:c0 + group * HEAD_DIM] = _unstack_group(o, n, group).astype(BF16)


def _ctx_even_attention(qkv, sink, seq):
    t, n = qkv.shape
    width = (A_HEADS + B_HEADS) * HEAD_DIM
    return pl.pallas_call(
        _ctx_even_attn_kernel,
        grid=(t // seq,),
        in_specs=[pl.BlockSpec(memory_space=pltpu.SMEM),
                  pl.BlockSpec((seq, n), lambda b: (b, 0))],
        out_specs=pl.BlockSpec((seq, width), lambda b: (b, 0)),
        out_shape=jax.ShapeDtypeStruct((t, width), BF16),
        scratch_shapes=_attn_scratch(B_HEADS // B_KV_HEADS * seq, seq),
        compiler_params=_params(1),
        name="attn_even_ctx",
    )(sink, qkv)


def _na_key_window(j):
    return min(max(Q_TILE * j - Q_TILE, 0), Q_TILE)


def _na_bias_tiles(src_ref, h):
    qc = lax.broadcasted_iota(jnp.int32, (GRID_W, LANES), 0)
    kc = lax.rem(lax.broadcasted_iota(jnp.int32, (GRID_W, LANES), 1), GRID_W)
    cs = jnp.clip(qc - NA_WIN_W // 2, 0, GRID_W - NA_WIN_W)
    col_ok = (kc >= cs) & (kc < cs + NA_WIN_W)
    tiles = []
    for i in range(2 * NA_WIN_H):
        src = jnp.broadcast_to(src_ref[h, i:i + 1, :], (GRID_W, LANES))
        tiles.append(jnp.where(col_ok, pltpu.roll(src, 0, 1, stride=1, stride_axis=0), NEG_INF))
    return tiles


def _na_bias_block(tiles, j, grid_rows):
    wh = min(NA_WIN_H, grid_rows)
    first_row = lax.broadcasted_iota(jnp.int32, (GRID_W, LANES), 1) < GRID_W
    lo_r = _na_key_window(j) // GRID_W
    strips = []
    for qr in range(j * Q_TILE // GRID_W, (j + 1) * Q_TILE // GRID_W):
        rs = min(max(qr - wh // 2, 0), grid_rows - wh)
        parts = []
        for t in range(3 * Q_TILE // LANES):
            kr = lo_r + 2 * t
            ok0 = rs <= kr < rs + wh
            ok1 = rs <= kr + 1 < rs + wh
            i = kr - qr + NA_WIN_H
            if ok0 and ok1:
                parts.append(tiles[i])
            elif ok0:
                parts.append(jnp.where(first_row, tiles[i], NEG_INF))
            elif ok1:
                parts.append(jnp.where(first_row, NEG_INF, tiles[i]))
            else:
                parts.append(jnp.full((GRID_W, LANES), NEG_INF, F32))
        strips.append(jnp.concatenate(parts, axis=1))
    return jnp.concatenate(strips, axis=0)


def _cache_head(ref, h):
    return ref[0, 0, h].astype(BF16)


def _lat_a_kernel(src_ref, q_ref, k_ref, v_ref, ck_ref, cv_ref, o_ref, *scratch):
    n = q_ref.shape[0]
    past = ck_ref.shape[4]
    win = 3 * Q_TILE
    step = 0
    for pair in range(A_HEADS // 2):
        outs = [[] for _ in range(n // Q_TILE)]
        for h in (2 * pair, 2 * pair + 1):
            cols = slice(h * HEAD_DIM, (h + 1) * HEAD_DIM)
            ck = _cache_head(ck_ref, h)
            cv = _cache_head(cv_ref, h)
            tiles = _na_bias_tiles(src_ref, h)
            for j in range(n // Q_TILE):
                slot = _slot(scratch, step)
                step += 1
                lo = _na_key_window(j)
                q = q_ref[j * Q_TILE:(j + 1) * Q_TILE, cols]
                _put_scores(slot[0], 0, q, k_ref[lo:lo + win, cols], k_is_transposed=False,
                            bias=_na_bias_block(tiles, j, n // GRID_W))
                _put_scores(slot[0], win, q, ck, k_is_transposed=True)
                outs[j].append(_softmax_pv(
                    *slot, Q_TILE,
                    [(0, win, v_ref[lo:lo + win, cols], False), (win, win + past, cv, True)]))
        for j in range(n // Q_TILE):
            o_ref[j * Q_TILE:(j + 1) * Q_TILE, pair * LANES:(pair + 1) * LANES] = (
                jnp.concatenate(outs[j], axis=1).astype(BF16))


def _feature_major(cache):
    return cache.transpose(0, 1, 3, 4, 2)


def _cache_spec(cache, layer):
    blk = (1, 1) + cache.shape[2:]
    return pl.BlockSpec(blk, lambda b: (b, layer, 0, 0, 0))


def _lat_a_attention(qkv, cache_k, cache_v, e, bias_src, seq):
    t = qkv.shape[0]
    past = cache_k.shape[4]
    na = A_HEADS * HEAD_DIM
    return pl.pallas_call(
        _lat_a_kernel,
        grid=(t // seq,),
        in_specs=[
            _resident(bias_src.shape),
            pl.BlockSpec((seq, na), lambda b: (b, 0)),
            pl.BlockSpec((seq, na), lambda b: (b, 1)),
            pl.BlockSpec((seq, na), lambda b: (b, 2)),
            _cache_spec(cache_k, e),
            _cache_spec(cache_v, e),
        ],
        out_specs=pl.BlockSpec((seq, na), lambda b: (b, 0)),
        out_shape=jax.ShapeDtypeStruct((t, na), BF16),
        scratch_shapes=_attn_scratch(Q_TILE, 3 * Q_TILE + past),
        compiler_params=_params(1),
        name="attn_even_lat_a",
    )(bias_src, qkv, qkv, qkv, cache_k, cache_v)


def _lat_b_kernel(sink_ref, q_ref, k_ref, v_ref, ck_ref, cv_ref, o_ref, *scratch):
    n = q_ref.shape[0]
    past = ck_ref.shape[4]
    group = B_HEADS // B_KV_HEADS
    win = 2 * Q_TILE
    ctx = [(_cache_head(ck_ref, kv), _cache_head(cv_ref, kv)) for kv in range(B_KV_HEADS)]
    step = 0
    for j in range(n // Q_TILE):
        lo = min(max(Q_TILE * j - B_WINDOW, 0), n - win)
        q_rows = slice(j * Q_TILE, (j + 1) * Q_TILE)
        qpos = j * Q_TILE + lax.broadcasted_iota(jnp.int32, (group * Q_TILE, 1), 0) % Q_TILE
        kpos = lo + lax.broadcasted_iota(jnp.int32, (1, win), 1)
        valid = jnp.abs(qpos - kpos) <= B_WINDOW
        for kv in range(B_KV_HEADS):
            slot = _slot(scratch, step)
            step += 1
            cols = slice(kv * HEAD_DIM, (kv + 1) * HEAD_DIM)
            q = _stack_group(q_ref, q_rows, kv * group * HEAD_DIM, group)
            _put_scores(slot[0], 0, q, k_ref[lo:lo + win, cols], k_is_transposed=False,
                        valid=valid)
            _put_scores(slot[0], win, q, ctx[kv][0], k_is_transposed=True)
            o = _softmax_pv(
                *slot, group * Q_TILE,
                [(0, win, v_ref[lo:lo + win, cols], False), (win, win + past, ctx[kv][1], True)],
                _sink_column(sink_ref, kv * group, group, Q_TILE))
            c0 = kv * group * HEAD_DIM
            o_ref[q_rows, c0:c0 + group * HEAD_DIM] = _unstack_group(o, Q_TILE, group).astype(BF16)


def _lat_b_attention(qkv, cache_k, cache_v, e, sink, seq):
    t = qkv.shape[0]
    past = cache_k.shape[4]
    na = A_HEADS * HEAD_DIM
    nb = B_HEADS * HEAD_DIM
    nkv = B_KV_HEADS * HEAD_DIM
    group = B_HEADS // B_KV_HEADS
    return pl.pallas_call(
        _lat_b_kernel,
        grid=(t // seq,),
        in_specs=[
            pl.BlockSpec(memory_space=pltpu.SMEM),
            pl.BlockSpec((seq, nb), lambda b: (b, 3 * na // nb)),
            pl.BlockSpec((seq, nkv), lambda b: (b, (3 * na + nb) // nkv)),
            pl.BlockSpec((seq, nkv), lambda b: (b, (3 * na + nb) // nkv + 1)),
            _cache_spec(cache_k, e),
            _cache_spec(cache_v, e),
        ],
        out_specs=pl.BlockSpec((seq, nb), lambda b: (b, 0)),
        out_shape=jax.ShapeDtypeStruct((t, nb), BF16),
        scratch_shapes=_attn_scratch(group * Q_TILE, 2 * Q_TILE + past),
        compiler_params=_params(1),
        name="attn_even_lat_b",
    )(sink, qkv, qkv, qkv, cache_k, cache_v)


def _ctx_odd_attn_kernel(qkv_ref, o_ref, *scratch):
    n = qkv_ref.shape[0]
    group = C_HEADS // C_KV_HEADS
    nq = C_HEADS * HEAD_DIM
    nkv = C_KV_HEADS * HEAD_DIM
    for kv in range(C_KV_HEADS):
        slot = _slot(scratch, kv)
        q = _stack_group(qkv_ref, slice(None), kv * group * HEAD_DIM, group)
        _put_scores(slot[0], 0, q, qkv_ref[:, nq + kv * HEAD_DIM:nq + (kv + 1) * HEAD_DIM],
                    k_is_transposed=False)
        v = qkv_ref[:, nq + nkv + kv * HEAD_DIM:nq + nkv + (kv + 1) * HEAD_DIM]
        o = _softmax_pv(*slot, group * n, [(0, n, v, False)])
        c0 = kv * group * HEAD_DIM
        o_ref[:, c0:c0 + group * HEAD_DIM] = _unstack_group(o, n, group).astype(BF16)


def _ctx_odd_attention(qkv, seq):
    t, n = qkv.shape
    width = C_HEADS * HEAD_DIM
    return pl.pallas_call(
        _ctx_odd_attn_kernel,
        grid=(t // seq,),
        in_specs=[pl.BlockSpec((seq, n), lambda b: (b, 0))],
        out_specs=pl.BlockSpec((seq, width), lambda b: (b, 0)),
        out_shape=jax.ShapeDtypeStruct((t, width), BF16),
        scratch_shapes=_attn_scratch(C_HEADS // C_KV_HEADS * seq, seq),
        compiler_params=_params(1),
        name="attn_odd_ctx",
    )(qkv)


def _lat_c_kernel(qkv_ref, ck_ref, cv_ref, o_ref, ckb_ref, cvb_ref, *scratch):
    n = qkv_ref.shape[0]
    past = ck_ref.shape[4]
    group = C_HEADS // C_KV_HEADS
    nq = C_HEADS * HEAD_DIM
    nkv = C_KV_HEADS * HEAD_DIM
    for kv in range(C_KV_HEADS):
        ckb_ref[kv] = _cache_head(ck_ref, kv)
        cvb_ref[kv] = _cache_head(cv_ref, kv)

    def q_block(j, carry):
        q_rows = pl.ds(pl.multiple_of(j * Q_TILE, Q_TILE), Q_TILE)
        for kv in range(C_KV_HEADS):
            slot = _slot(scratch, kv)
            q = _stack_group(qkv_ref, q_rows, kv * group * HEAD_DIM, group)
            _put_scores(slot[0], 0, q, ckb_ref[kv], k_is_transposed=True)
            _put_scores(slot[0], past, q,
                        qkv_ref[:, nq + kv * HEAD_DIM:nq + (kv + 1) * HEAD_DIM],
                        k_is_transposed=False)
            v = qkv_ref[:, nq + nkv + kv * HEAD_DIM:nq + nkv + (kv + 1) * HEAD_DIM]
            o = _softmax_pv(*slot, group * Q_TILE,
                            [(0, past, cvb_ref[kv], True), (past, past + n, v, False)])
            c0 = kv * group * HEAD_DIM
            o_ref[q_rows, c0:c0 + group * HEAD_DIM] = _unstack_group(o, Q_TILE, group).astype(BF16)
        return carry

    lax.fori_loop(0, n // Q_TILE, q_block, 0)


def _lat_c_attention(qkv, cache_k, cache_v, o, seq):
    t, n = qkv.shape
    past = cache_k.shape[4]
    width = C_HEADS * HEAD_DIM
    group = C_HEADS // C_KV_HEADS
    return pl.pallas_call(
        _lat_c_kernel,
        grid=(t // seq,),
        in_specs=[
            pl.BlockSpec((seq, n), lambda b: (b, 0)),
            _cache_spec(cache_k, o),
            _cache_spec(cache_v, o),
        ],
        out_specs=pl.BlockSpec((seq, width), lambda b: (b, 0)),
        out_shape=jax.ShapeDtypeStruct((t, width), BF16),
        scratch_shapes=([pltpu.VMEM((C_KV_HEADS, HEAD_DIM, past), BF16)] * 2
                        + _attn_scratch(group * Q_TILE, past + seq)),
        compiler_params=_params(1),
        name="attn_odd_lat",
    )(qkv, cache_k, cache_v)


def _post_kernel(*refs, n_parts, final):
    o_refs = refs[:n_parts]
    (x_ref, wo_ref, g1_ref, sh_ref, sc_ref, g2_ref, gain_ref, wgu_ref, wd_ref) = refs[n_parts:n_parts + 9]
    rest = refs[n_parts + 9:]
    if final:
        fg_ref, out_ref, act_ref = rest
    else:
        out_ref, act_ref = rest
    mix = None
    r0 = 0
    for o_ref in o_refs:
        kk = o_ref.shape[1]
        part = jnp.dot(o_ref[...], wo_ref[r0:r0 + kk, :], preferred_element_type=F32)
        mix = part if mix is None else mix + part
        r0 += kk
    x1 = x_ref[...] + g1_ref[0, 0] * mix
    h = _adaln(x1, gain_ref[...], sh_ref[0, 0], sc_ref[0, 0]).astype(BF16)
    d_ff = wd_ref.shape[0]
    for j in range(d_ff // FF_CHUNK):
        c0 = j * FF_CHUNK
        gate = jnp.dot(h, wgu_ref[:, c0:c0 + FF_CHUNK], preferred_element_type=F32)
        up = jnp.dot(h, wgu_ref[:, d_ff + c0:d_ff + c0 + FF_CHUNK], preferred_element_type=F32)
        act_ref[:, c0:c0 + FF_CHUNK] = (gate * jax.nn.sigmoid(gate) * up).astype(BF16)
    ffn = jnp.dot(act_ref[...], wd_ref[...], preferred_element_type=F32)
    x2 = x1 + g2_ref[0, 0] * ffn
    if final:
        ms = jnp.mean(x2 * x2, axis=-1, keepdims=True)
        x2 = (x2 * lax.rsqrt(ms + RMS_EPS)) * fg_ref[...]
    out_ref[...] = x2


def _post_attention(o_parts, x, mods, layer, gain, w_out, w_gu, w_down, *, is_lat, seq,
                    final_gain=None):
    t, d = x.shape
    tm = TOKEN_TILE
    per_seq = seq // tm
    group = (lambda i: 1 + i // per_seq) if is_lat else (lambda i: 0)
    row = lambda i: (i, 0)
    d_ff = w_down.shape[0]
    final = final_gain is not None
    in_specs = [pl.BlockSpec((tm, o.shape[1]), row) for o in o_parts]
    in_specs += [
        pl.BlockSpec((tm, d), row),
        _resident(w_out.shape),
        _mod_spec(layer, 2, group),
        _mod_spec(layer, 3, group),
        _mod_spec(layer, 4, group),
        _mod_spec(layer, 5, group),
        _resident((1, d)),
        _resident(w_gu.shape),
        _resident(w_down.shape),
    ]
    args = list(o_parts) + [x, w_out, mods, mods, mods, mods, gain.reshape(1, d), w_gu, w_down]
    if final:
        in_specs.append(_resident((1, d)))
        args.append(final_gain.reshape(1, d))
    return pl.pallas_call(
        functools.partial(_post_kernel, n_parts=len(o_parts), final=final),
        grid=(t // tm,),
        in_specs=in_specs,
        out_specs=pl.BlockSpec((tm, d), row),
        out_shape=jax.ShapeDtypeStruct((t, d), F32),
        scratch_shapes=[pltpu.VMEM((tm, d_ff), BF16)],
        compiler_params=_params(1),
        name=f"post_{'lat' if is_lat else 'ctx'}{'_final' if final else ''}",
    )(*args)


def _rope_tables(n):
    t = jnp.arange(n)
    row = (t // GRID_W).astype(F32)
    col = (t % GRID_W).astype(F32)
    half = HEAD_DIM // 2
    inv_freq = ROPE_THETA ** (-jnp.arange(0, half, 2, dtype=F32) / half)
    lane = jnp.arange(LANES)
    in_head = lane % HEAD_DIM
    pos = jnp.where((in_head < half)[None, :], row[:, None], col[:, None])
    ang = pos * inv_freq[in_head % (half // 2)][None, :]
    first = ((in_head % half) < half // 2)[None, :]
    cos = jnp.cos(ang)
    sin = jnp.sin(ang)
    return cos, jnp.where(first, -sin, 0.0), jnp.where(first, 0.0, sin)


def _na_bias_sources(rpb):
    h, _, nb = rpb.shape
    w = NA_WIN_W - 1
    rp = jnp.pad(rpb, ((0, 0), (1, 1), (0, 0)))
    lo, hi = rp[:, :-1], rp[:, 1:]
    z = jnp.zeros((h, 2 * NA_WIN_H, LANES // 2 - nb), F32)
    return jnp.concatenate([lo[:, :, w:], z, hi, z, lo[:, :, :w]], axis=-1)


def _state(y):
    return y.transpose(0, 3, 1, 2)


def kernel(x_prompt, x_sample, cache_a_k, cache_a_v, cache_b_k, cache_b_v, cache_c_k, cache_c_v,
           c, c_ctx, norm_gain, w_mod, b_mod, w_in_even, w_out_even, rpb_a, sink_b,
           w_in_odd, w_out_odd, q_norm_c, k_norm_c, w_gate_up, w_down, final_gain):
    batch, seq, d = x_prompt.shape
    dec_batch, dec_seq, _ = x_sample.shape
    depth = w_mod.shape[0]

    cvec = jnp.concatenate(
        [c_ctx[None, :], c, jnp.zeros((MOD_GROUPS - 1 - dec_batch, d), F32)], axis=0)
    mods = _modulation(cvec, w_mod, b_mod).reshape(depth, MOD_GROUPS, 1, 6 * d)
    rope = _rope_tables(dec_seq)

    ctx = x_prompt.reshape(batch * seq, d)
    lat = x_sample.reshape(dec_batch * dec_seq, d)
    states = {name: [] for name in ("a_k", "a_v", "b_k", "b_v", "c_k", "c_v")}

    for layer in range(depth):
        last = layer == depth - 1
        gain1, gain2 = norm_gain[layer, 0], norm_gain[layer, 1]
        if layer % 2 == 0:
            e = layer // 2
            w_in = w_in_even[e].astype(BF16)
            w_out = w_out_even[e].astype(BF16)
            qkv_c, ka, va, kb, vb = _pre_attention(
                ctx, mods, layer, gain1, w_in, odd=False, is_lat=False, seq=seq)
            (qkv_l,) = _pre_attention(
                lat, mods, layer, gain1, w_in, odd=False, is_lat=True, seq=dec_seq, rope=rope)
            states["a_k"].append(_state(ka))
            states["a_v"].append(_state(va))
            states["b_k"].append(_state(kb))
            states["b_v"].append(_state(vb))
            o_ctx = [_ctx_even_attention(qkv_c, sink_b[e], seq)]
            o_lat = [
                _lat_a_attention(qkv_l, _feature_major(cache_a_k), _feature_major(cache_a_v),
                                 e, _na_bias_sources(rpb_a[e]), dec_seq),
                _lat_b_attention(qkv_l, _feature_major(cache_b_k), _feature_major(cache_b_v),
                                 e, sink_b[e], dec_seq),
            ]
        else:
            o = layer // 2
            w_in = w_in_odd[o].astype(BF16)
            w_out = w_out_odd[o].astype(BF16)
            head_gains = (jnp.tile(q_norm_c[o], LANES // HEAD_DIM).reshape(1, LANES),
                          jnp.tile(k_norm_c[o], LANES // HEAD_DIM).reshape(1, LANES))
            qkv_c, kc, vc = _pre_attention(
                ctx, mods, layer, gain1, w_in, odd=True, is_lat=False, seq=seq,
                head_gains=head_gains)
            (qkv_l,) = _pre_attention(
                lat, mods, layer, gain1, w_in, odd=True, is_lat=True, seq=dec_seq, rope=rope,
                head_gains=head_gains)
            states["c_k"].append(_state(kc))
            states["c_v"].append(_state(vc))
            o_ctx = [_ctx_odd_attention(qkv_c, seq)]
            o_lat = [_lat_c_attention(qkv_l, _feature_major(cache_c_k),
                                      _feature_major(cache_c_v), o, dec_seq)]
        w_gu = w_gate_up[layer].astype(BF16)
        w_dn = w_down[layer].astype(BF16)
        fg = final_gain if last else None
        ctx = _post_attention(o_ctx, ctx, mods, layer, gain2, w_out, w_gu, w_dn,
                              is_lat=False, seq=seq, final_gain=fg)
        lat = _post_attention(o_lat, lat, mods, layer, gain2, w_out, w_gu, w_dn,
                              is_lat=True, seq=dec_seq, final_gain=fg)

    return (ctx.reshape(batch, seq, d), lat.reshape(dec_batch, dec_seq, d),
            jnp.stack(states["a_k"], axis=1), jnp.stack(states["a_v"], axis=1),
            jnp.stack(states["b_k"], axis=1), jnp.stack(states["b_v"], axis=1),
            jnp.stack(states["c_k"], axis=1), jnp.stack(states["c_v"], axis=1))
```

```python
import functools
import math

import jax
import jax.numpy as jnp
from jax import lax
from jax.experimental import pallas as pl
from jax.experimental.pallas import tpu as pltpu

F32 = jnp.float32
BF16 = jnp.bfloat16

D_MODEL = 1024
GRID_W = 64
HEAD_DIM = 64
A_HEADS = 8
B_HEADS = 8
B_KV_HEADS = 2
C_HEADS = 16
C_KV_HEADS = 4
NA_WIN_H = 8
NA_WIN_W = 16
B_WINDOW = 128
ROPE_THETA = 10000.0
RMS_EPS = 1e-6
NEG_INF = -1e30
QK_SCALE = 1.0 / math.sqrt(HEAD_DIM)

LANES = 128
TOKEN_TILE = 512
Q_TILE = 256
ROW_CHUNK = 32
FF_CHUNK = 256
MOD_GROUPS = 16
VMEM_LIMIT = 56 * 1024 * 1024


def _params(n_axes, vmem=VMEM_LIMIT):
    return pltpu.CompilerParams(
        dimension_semantics=("arbitrary",) * n_axes, vmem_limit_bytes=vmem)


def _resident(shape):
    nd = len(shape)
    return pl.BlockSpec(shape, lambda *_: (0,) * nd, pipeline_mode=pl.Buffered(1))


def _mod_kernel(c_ref, w_ref, b_ref, o_ref):
    c = c_ref[...]
    s = (c * jax.nn.sigmoid(c)).astype(BF16)
    o_ref[0] = jnp.dot(s, w_ref[0].astype(BF16), preferred_element_type=F32) + b_ref[0]


def _modulation(cvec, w_mod, b_mod):
    depth, d, n = w_mod.shape
    tn = 1536
    return pl.pallas_call(
        _mod_kernel,
        grid=(depth, n // tn),
        in_specs=[
            pl.BlockSpec((MOD_GROUPS, d), lambda l, j: (0, 0)),
            pl.BlockSpec((1, d, tn), lambda l, j: (l, 0, j)),
            pl.BlockSpec((1, 1, tn), lambda l, j: (l, 0, j)),
        ],
        out_specs=pl.BlockSpec((1, MOD_GROUPS, tn), lambda l, j: (l, 0, j)),
        out_shape=jax.ShapeDtypeStruct((depth, MOD_GROUPS, n), F32),
        compiler_params=_params(2),
        name="modulation",
    )(cvec, w_mod, b_mod.reshape(depth, 1, n))


def _mod_spec(layer, which, group_of_step):
    return pl.BlockSpec((1, 1, 1, D_MODEL), lambda i: (layer, group_of_step(i), 0, which))


def _adaln(x, gain, shift, scale):
    ms = jnp.mean(x * x, axis=-1, keepdims=True)
    return (x * lax.rsqrt(ms + RMS_EPS)) * gain * (1.0 + scale) + shift


def _rope(y, cos, sin_lo, sin_hi):
    outs = []
    for c in range(y.shape[1] // LANES):
        yc = y[:, c * LANES:(c + 1) * LANES]
        outs.append(yc * cos
                    + pltpu.roll(yc, LANES - 16, 1) * sin_lo
                    + pltpu.roll(yc, 16, 1) * sin_hi)
    return outs[0] if len(outs) == 1 else jnp.concatenate(outs, axis=1)


def _head_rms_norm(y, gain):
    first = lax.broadcasted_iota(jnp.int32, (1, LANES), 1) < HEAD_DIM
    outs = []
    for c in range(y.shape[1] // LANES):
        yc = y[:, c * LANES:(c + 1) * LANES]
        sq = yc * yc
        s0 = jnp.sum(jnp.where(first, sq, 0.0), axis=-1, keepdims=True)
        s1 = jnp.sum(jnp.where(first, 0.0, sq), axis=-1, keepdims=True)
        ms = jnp.where(first, s0, s1) * (1.0 / HEAD_DIM)
        outs.append(yc * lax.rsqrt(ms + RMS_EPS) * gain)
    return outs[0] if len(outs) == 1 else jnp.concatenate(outs, axis=1)


_NT = (((1,), (1,)), ((), ()))
_NN = (((1,), (0,)), ((), ()))


def _attn_scratch(m, n):
    return [pltpu.VMEM((2, m, n), F32), pltpu.VMEM((2, m, n), BF16),
            pltpu.VMEM((2, 3, m, LANES), F32)]


def _slot(scratch, step):
    return tuple(r.at[step % 2] for r in scratch)


def _put_scores(s_ref, c0, q, k, *, k_is_transposed, bias=None, valid=None):
    s = lax.dot_general(q, k, _NN if k_is_transposed else _NT, preferred_element_type=F32)
    if bias is not None:
        s = s + bias
    if valid is not None:
        s = jnp.where(valid, s, NEG_INF)
    s_ref[:q.shape[0], c0:c0 + s.shape[1]] = s


def _softmax_pv(s_ref, p_ref, stat_ref, m_rows, values, sink=None):
    n = max(c1 for _, c1, _, _ in values)
    chunks = range(0, m_rows, ROW_CHUNK)
    lanes = [slice(c, c + LANES) for c in range(0, n, LANES)]
    for r in chunks:
        rows = slice(r, r + ROW_CHUNK)
        acc = s_ref[rows, lanes[0]]
        for cols in lanes[1:]:
            acc = jnp.maximum(acc, s_ref[rows, cols])
        stat_ref[0, rows, :] = acc
    m = jnp.max(stat_ref[0, :m_rows, :], axis=-1, keepdims=True)
    if sink is not None:
        m = jnp.maximum(m, sink)
    stat_ref[1, :m_rows, :] = jnp.broadcast_to(m, (m_rows, LANES))
    for r in chunks:
        rows = slice(r, r + ROW_CHUNK)
        mb = stat_ref[1, rows, :]
        acc = None
        for cols in lanes:
            p = jnp.exp(s_ref[rows, cols] - mb)
            acc = p if acc is None else acc + p
            p_ref[rows, cols] = p.astype(BF16)
        stat_ref[2, rows, :] = acc
    denom = jnp.sum(stat_ref[2, :m_rows, :], axis=-1, keepdims=True)
    if sink is not None:
        denom = denom + jnp.exp(sink - m)
    out = None
    for c0, c1, v, v_is_transposed in values:
        pv = lax.dot_general(p_ref[:m_rows, c0:c1], v, _NT if v_is_transposed else _NN,
                             preferred_element_type=F32)
        out = pv if out is None else out + pv
    return out / denom


def _stack_group(ref, rows, col0, group):
    return jnp.concatenate(
        [ref[rows, col0 + g * HEAD_DIM: col0 + (g + 1) * HEAD_DIM] for g in range(group)], axis=0)


def _unstack_group(o, rows, group):
    return jnp.concatenate([o[g * rows:(g + 1) * rows] for g in range(group)], axis=1)


def _sink_column(sink_ref, h0, group, rows):
    return jnp.concatenate(
        [jnp.full((rows, 1), sink_ref[h0 + g], F32) for g in range(group)], axis=0)


def _store_state(ref, y):
    nb, heads, _, seq = ref.shape
    yt = y.T
    for b in range(nb):
        for h in range(heads):
            ref[b, h] = yt[h * HEAD_DIM:(h + 1) * HEAD_DIM, b * seq:(b + 1) * seq]


def _pre_even_kernel(x_ref, g_ref, sh_ref, sc_ref, w_ref, *rest, is_lat):
    if is_lat:
        cos_ref, slo_ref, shi_ref, qkv_ref = rest
    else:
        qkv_ref, ka_ref, va_ref, kb_ref, vb_ref = rest
    h = _adaln(x_ref[...], g_ref[...], sh_ref[0, 0], sc_ref[0, 0]).astype(BF16)

    def proj(c0, c1):
        return jnp.dot(h, w_ref[:, c0:c1], preferred_element_type=F32)

    na = A_HEADS * HEAD_DIM
    nb = B_HEADS * HEAD_DIM
    nkv = B_KV_HEADS * HEAD_DIM
    c = 0
    qkv_ref[:, c:c + na] = (proj(c, c + na) * QK_SCALE).astype(BF16)
    c += na
    ka = proj(c, c + na)
    qkv_ref[:, c:c + na] = ka.astype(BF16)
    c += na
    va = proj(c, c + na)
    qkv_ref[:, c:c + na] = va.astype(BF16)
    c += na
    qb = proj(c, c + nb)
    if is_lat:
        qb = _rope(qb, cos_ref[...], slo_ref[...], shi_ref[...])
    qkv_ref[:, c:c + nb] = (qb * QK_SCALE).astype(BF16)
    c += nb
    kb = proj(c, c + nkv)
    if is_lat:
        kb_out = _rope(kb, cos_ref[...], slo_ref[...], shi_ref[...])
    else:
        kb_out = kb
    qkv_ref[:, c:c + nkv] = kb_out.astype(BF16)
    c += nkv
    vb = proj(c, c + nkv)
    qkv_ref[:, c:c + nkv] = vb.astype(BF16)
    if not is_lat:
        _store_state(ka_ref, ka)
        _store_state(va_ref, va)
        _store_state(kb_ref, kb)
        _store_state(vb_ref, vb)


def _pre_odd_kernel(x_ref, g_ref, sh_ref, sc_ref, w_ref, qn_ref, kn_ref, *rest, is_lat):
    if is_lat:
        cos_ref, slo_ref, shi_ref, qkv_ref, vt_ref = rest
    else:
        qkv_ref, kc_ref, vc_ref = rest
    h = _adaln(x_ref[...], g_ref[...], sh_ref[0, 0], sc_ref[0, 0]).astype(BF16)

    def proj(c0, c1):
        return jnp.dot(h, w_ref[:, c0:c1], preferred_element_type=F32)

    nq = C_HEADS * HEAD_DIM
    nkv = C_KV_HEADS * HEAD_DIM
    q = _head_rms_norm(proj(0, nq), qn_ref[...])
    k = _head_rms_norm(proj(nq, nq + nkv), kn_ref[...])
    v = proj(nq + nkv, nq + 2 * nkv)
    if is_lat:
        q = _rope(q, cos_ref[...], slo_ref[...], shi_ref[...])
        k_out = _rope(k, cos_ref[...], slo_ref[...], shi_ref[...])
        vt = v.T
        for hh in range(C_KV_HEADS):
            vt_ref[0, hh] = vt[hh * HEAD_DIM:(hh + 1) * HEAD_DIM, :].astype(BF16)
    else:
        k_out = k
        _store_state(kc_ref, k)
        _store_state(vc_ref, v)
    qkv_ref[:, 0:nq] = (q * QK_SCALE).astype(BF16)
    qkv_ref[:, nq:nq + nkv] = k_out.astype(BF16)
    qkv_ref[:, nq + nkv:nq + 2 * nkv] = v.astype(BF16)


def _pre_attention(x, mods, layer, gain, w, *, odd, is_lat, seq, rope=None, head_gains=None):
    t, d = x.shape
    n = w.shape[1]
    tm = TOKEN_TILE
    per_seq = seq // tm
    group = (lambda i: 1 + i // per_seq) if is_lat else (lambda i: 0)
    row = lambda i: (i, 0)
    in_specs = [
        pl.BlockSpec((tm, d), row),
        _resident((1, d)),
        _mod_spec(layer, 0, group),
        _mod_spec(layer, 1, group),
        _resident((d, n)),
    ]
    args = [x, gain.reshape(1, d), mods, mods, w]
    if odd:
        in_specs += [_resident((1, LANES)), _resident((1, LANES))]
        args += list(head_gains)
    if is_lat:
        in_specs += [pl.BlockSpec((tm, LANES), lambda i: (i % per_seq, 0))] * 3
        args += list(rope)
    out_specs = [pl.BlockSpec((tm, n), row)]
    out_shape = [jax.ShapeDtypeStruct((t, n), BF16)]
    if is_lat and odd:
        out_specs.append(pl.BlockSpec((1, C_KV_HEADS, HEAD_DIM, tm),
                                      lambda i: (i // per_seq, 0, 0, i % per_seq)))
        out_shape.append(jax.ShapeDtypeStruct((t // seq, C_KV_HEADS, HEAD_DIM, seq), BF16))
    if not is_lat:
        if odd:
            widths = [C_KV_HEADS * HEAD_DIM] * 2
        else:
            widths = [A_HEADS * HEAD_DIM] * 2 + [B_KV_HEADS * HEAD_DIM] * 2
        for wd in widths:
            heads = wd // HEAD_DIM
            out_specs.append(pl.BlockSpec((tm // seq, heads, HEAD_DIM, seq), lambda i: (i, 0, 0, 0)))
            out_shape.append(jax.ShapeDtypeStruct((t // seq, heads, HEAD_DIM, seq), F32))
    body = _pre_odd_kernel if odd else _pre_even_kernel
    return pl.pallas_call(
        functools.partial(body, is_lat=is_lat),
        grid=(t // tm,),
        in_specs=in_specs,
        out_specs=out_specs,
        out_shape=out_shape,
        compiler_params=_params(1),
        name=f"pre_{'odd' if odd else 'even'}_{'lat' if is_lat else 'ctx'}",
    )(*args)


def _ctx_even_attn_kernel(sink_ref, qkv_ref, o_ref, *scratch):
    rows = slice(None)
    n = qkv_ref.shape[0]
    na = A_HEADS * HEAD_DIM
    step = 0
    for pair in range(A_HEADS // 2):
        outs = []
        for h in (2 * pair, 2 * pair + 1):
            slot = _slot(scratch, step)
            step += 1
            c = h * HEAD_DIM
            _put_scores(slot[0], 0, qkv_ref[:, c:c + HEAD_DIM],
                        qkv_ref[:, na + c:na + c + HEAD_DIM], k_is_transposed=False)
            v = qkv_ref[:, 2 * na + c:2 * na + c + HEAD_DIM]
            outs.append(_softmax_pv(*slot, n, [(0, n, v, False)]))
        o_ref[:, pair * LANES:(pair + 1) * LANES] = jnp.concatenate(outs, axis=1).astype(BF16)
    group = B_HEADS // B_KV_HEADS
    qb0 = 3 * na
    kb0 = qb0 + B_HEADS * HEAD_DIM
    vb0 = kb0 + B_KV_HEADS * HEAD_DIM
    for kv in range(B_KV_HEADS):
        slot = _slot(scratch, step)
        step += 1
        q = _stack_group(qkv_ref, rows, qb0 + kv * group * HEAD_DIM, group)
        _put_scores(slot[0], 0, q, qkv_ref[:, kb0 + kv * HEAD_DIM:kb0 + (kv + 1) * HEAD_DIM],
                    k_is_transposed=False)
        v = qkv_ref[:, vb0 + kv * HEAD_DIM:vb0 + (kv + 1) * HEAD_DIM]
        o = _softmax_pv(*slot, group * n, [(0, n, v, False)],
                        _sink_column(sink_ref, kv * group, group, n))
        c0 = na + kv * group * HEAD_DIM
        o_ref[:, c0:c0 + group * HEAD_DIM] = _unstack_group(o, n, group).astype(BF16)


def _ctx_even_attention(qkv, sink, seq):
    t, n = qkv.shape
    width = (A_HEADS + B_HEADS) * HEAD_DIM
    return pl.pallas_call(
        _ctx_even_attn_kernel,
        grid=(t // seq,),
        in_specs=[pl.BlockSpec(memory_space=pltpu.SMEM),
                  pl.BlockSpec((seq, n), lambda b: (b, 0))],
        out_specs=pl.BlockSpec((seq, width), lambda b: (b, 0)),
        out_shape=jax.ShapeDtypeStruct((t, width), BF16),
        scratch_shapes=_attn_scratch(B_HEADS // B_KV_HEADS * seq, seq),
        compiler_params=_params(1),
        name="attn_even_ctx",
    )(sink, qkv)


def _na_key_window(j):
    return min(max(Q_TILE * j - Q_TILE, 0), Q_TILE)


def _na_bias_tiles(src_ref, h):
    qc = lax.broadcasted_iota(jnp.int32, (GRID_W, LANES), 0)
    kc = lax.rem(lax.broadcasted_iota(jnp.int32, (GRID_W, LANES), 1), GRID_W)
    cs = jnp.clip(qc - NA_WIN_W // 2, 0, GRID_W - NA_WIN_W)
    col_ok = (kc >= cs) & (kc < cs + NA_WIN_W)
    tiles = []
    for i in range(2 * NA_WIN_H):
        src = jnp.broadcast_to(src_ref[h, i:i + 1, :], (GRID_W, LANES))
        tiles.append(jnp.where(col_ok, pltpu.roll(src, 0, 1, stride=1, stride_axis=0), NEG_INF))
    return tiles


def _na_bias_block(tiles, j, grid_rows):
    wh = min(NA_WIN_H, grid_rows)
    first_row = lax.broadcasted_iota(jnp.int32, (GRID_W, LANES), 1) < GRID_W
    lo_r = _na_key_window(j) // GRID_W
    strips = []
    for qr in range(j * Q_TILE // GRID_W, (j + 1) * Q_TILE // GRID_W):
        rs = min(max(qr - wh // 2, 0), grid_rows - wh)
        parts = []
        for t in range(3 * Q_TILE // LANES):
            kr = lo_r + 2 * t
            ok0 = rs <= kr < rs + wh
            ok1 = rs <= kr + 1 < rs + wh
            i = kr - qr + NA_WIN_H
            if ok0 and ok1:
                parts.append(tiles[i])
            elif ok0:
                parts.append(jnp.where(first_row, tiles[i], NEG_INF))
            elif ok1:
                parts.append(jnp.where(first_row, NEG_INF, tiles[i]))
            else:
                parts.append(jnp.full((GRID_W, LANES), NEG_INF, F32))
        strips.append(jnp.concatenate(parts, axis=1))
    return jnp.concatenate(strips, axis=0)


def _cache_head(ref, h):
    return ref[0, 0, h].astype(BF16)


def _lat_a_kernel(src_ref, q_ref, k_ref, v_ref, ck_ref, cv_ref, o_ref, *scratch):
    n = q_ref.shape[0]
    past = ck_ref.shape[4]
    win = 3 * Q_TILE
    step = 0
    for pair in range(A_HEADS // 2):
        outs = [[] for _ in range(n // Q_TILE)]
        for h in (2 * pair, 2 * pair + 1):
            cols = slice(h * HEAD_DIM, (h + 1) * HEAD_DIM)
            ck = _cache_head(ck_ref, h)
            cv = _cache_head(cv_ref, h)
            tiles = _na_bias_tiles(src_ref, h)
            for j in range(n // Q_TILE):
                slot = _slot(scratch, step)
                step += 1
                lo = _na_key_window(j)
                q = q_ref[j * Q_TILE:(j + 1) * Q_TILE, cols]
                _put_scores(slot[0], 0, q, k_ref[lo:lo + win, cols], k_is_transposed=False,
                            bias=_na_bias_block(tiles, j, n // GRID_W))
                _put_scores(slot[0], win, q, ck, k_is_transposed=True)
                outs[j].append(_softmax_pv(
                    *slot, Q_TILE,
                    [(0, win, v_ref[lo:lo + win, cols], False), (win, win + past, cv, True)]))
        for j in range(n // Q_TILE):
            o_ref[j * Q_TILE:(j + 1) * Q_TILE, pair * LANES:(pair + 1) * LANES] = (
                jnp.concatenate(outs[j], axis=1).astype(BF16))


def _feature_major(cache):
    return cache.transpose(0, 1, 3, 4, 2)


def _cache_spec(cache, layer):
    blk = (1, 1) + cache.shape[2:]
    return pl.BlockSpec(blk, lambda b: (b, layer, 0, 0, 0))


def _lat_a_attention(qkv, cache_k, cache_v, e, bias_src, seq):
    t = qkv.shape[0]
    past = cache_k.shape[4]
    na = A_HEADS * HEAD_DIM
    return pl.pallas_call(
        _lat_a_kernel,
        grid=(t // seq,),
        in_specs=[
            _resident(bias_src.shape),
            pl.BlockSpec((seq, na), lambda b: (b, 0)),
            pl.BlockSpec((seq, na), lambda b: (b, 1)),
            pl.BlockSpec((seq, na), lambda b: (b, 2)),
            _cache_spec(cache_k, e),
            _cache_spec(cache_v, e),
        ],
        out_specs=pl.BlockSpec((seq, na), lambda b: (b, 0)),
        out_shape=jax.ShapeDtypeStruct((t, na), BF16),
        scratch_shapes=_attn_scratch(Q_TILE, 3 * Q_TILE + past),
        compiler_params=_params(1),
        name="attn_even_lat_a",
    )(bias_src, qkv, qkv, qkv, cache_k, cache_v)


def _lat_b_kernel(sink_ref, q_ref, k_ref, v_ref, ck_ref, cv_ref, o_ref, *scratch):
    n = q_ref.shape[0]
    past = ck_ref.shape[4]
    group = B_HEADS // B_KV_HEADS
    win = 2 * Q_TILE
    ctx = [(_cache_head(ck_ref, kv), _cache_head(cv_ref, kv)) for kv in range(B_KV_HEADS)]
    step = 0
    for j in range(n // Q_TILE):
        lo = min(max(Q_TILE * j - B_WINDOW, 0), n - win)
        q_rows = slice(j * Q_TILE, (j + 1) * Q_TILE)
        qpos = j * Q_TILE + lax.broadcasted_iota(jnp.int32, (group * Q_TILE, 1), 0) % Q_TILE
        kpos = lo + lax.broadcasted_iota(jnp.int32, (1, win), 1)
        valid = jnp.abs(qpos - kpos) <= B_WINDOW
        for kv in range(B_KV_HEADS):
            slot = _slot(scratch, step)
            step += 1
            cols = slice(kv * HEAD_DIM, (kv + 1) * HEAD_DIM)
            q = _stack_group(q_ref, q_rows, kv * group * HEAD_DIM, group)
            _put_scores(slot[0], 0, q, k_ref[lo:lo + win, cols], k_is_transposed=False,
                        valid=valid)
            _put_scores(slot[0], win, q, ctx[kv][0], k_is_transposed=True)
            o = _softmax_pv(
                *slot, group * Q_TILE,
                [(0, win, v_ref[lo:lo + win, cols], False), (win, win + past, ctx[kv][1], True)],
                _sink_column(sink_ref, kv * group, group, Q_TILE))
            c0 = kv * group * HEAD_DIM
            o_ref[q_rows, c0:c0 + group * HEAD_DIM] = _unstack_group(o, Q_TILE, group).astype(BF16)


def _lat_b_attention(qkv, cache_k, cache_v, e, sink, seq):
    t = qkv.shape[0]
    past = cache_k.shape[4]
    na = A_HEADS * HEAD_DIM
    nb = B_HEADS * HEAD_DIM
    nkv = B_KV_HEADS * HEAD_DIM
    group = B_HEADS // B_KV_HEADS
    return pl.pallas_call(
        _lat_b_kernel,
        grid=(t // seq,),
        in_specs=[
            pl.BlockSpec(memory_space=pltpu.SMEM),
            pl.BlockSpec((seq, nb), lambda b: (b, 3 * na // nb)),
            pl.BlockSpec((seq, nkv), lambda b: (b, (3 * na + nb) // nkv)),
            pl.BlockSpec((seq, nkv), lambda b: (b, (3 * na + nb) // nkv + 1)),
            _cache_spec(cache_k, e),
            _cache_spec(cache_v, e),
        ],
        out_specs=pl.BlockSpec((seq, nb), lambda b: (b, 0)),
        out_shape=jax.ShapeDtypeStruct((t, nb), BF16),
        scratch_shapes=_attn_scratch(group * Q_TILE, 2 * Q_TILE + past),
        compiler_params=_params(1),
        name="attn_even_lat_b",
    )(sink, qkv, qkv, qkv, cache_k, cache_v)


def _ctx_odd_attn_kernel(qkv_ref, o_ref, *scratch):
    n = qkv_ref.shape[0]
    group = C_HEADS // C_KV_HEADS
    nq = C_HEADS * HEAD_DIM
    nkv = C_KV_HEADS * HEAD_DIM
    for kv in range(C_KV_HEADS):
        slot = _slot(scratch, kv)
        q = _stack_group(qkv_ref, slice(None), kv * group * HEAD_DIM, group)
        _put_scores(slot[0], 0, q, qkv_ref[:, nq + kv * HEAD_DIM:nq + (kv + 1) * HEAD_DIM],
                    k_is_transposed=False)
        v = qkv_ref[:, nq + nkv + kv * HEAD_DIM:nq + nkv + (kv + 1) * HEAD_DIM]
        o = _softmax_pv(*slot, group * n, [(0, n, v, False)])
        c0 = kv * group * HEAD_DIM
        o_ref[:, c0:c0 + group * HEAD_DIM] = _unstack_group(o, n, group).astype(BF16)


def _ctx_odd_attention(qkv, seq):
    t, n = qkv.shape
    width = C_HEADS * HEAD_DIM
    return pl.pallas_call(
        _ctx_odd_attn_kernel,
        grid=(t // seq,),
        in_specs=[pl.BlockSpec((seq, n), lambda b: (b, 0))],
        out_specs=pl.BlockSpec((seq, width), lambda b: (b, 0)),
        out_shape=jax.ShapeDtypeStruct((t, width), BF16),
        scratch_shapes=_attn_scratch(C_HEADS // C_KV_HEADS * seq, seq),
        compiler_params=_params(1),
        name="attn_odd_ctx",
    )(qkv)


def _attend_keys_on_rows(k_all, vt_all, q, rows, group, sink=None):
    s = lax.dot_general(k_all, q, _NT, preferred_element_type=F32)
    m = jnp.max(s, axis=0, keepdims=True)
    if sink is not None:
        m = jnp.maximum(m, sink)
    p = jnp.exp(s - m)
    denom = jnp.sum(p, axis=0, keepdims=True)
    if sink is not None:
        denom = denom + jnp.exp(sink - m)
    ot = jnp.dot(vt_all, p.astype(BF16), preferred_element_type=F32) / denom
    return jnp.concatenate([ot[:, g * rows:(g + 1) * rows].T for g in range(group)], axis=1)


def _lat_c_kernel(qkv_ref, vt_ref, ck_ref, cv_ref, o_ref, kall_ref, vtall_ref):
    n = qkv_ref.shape[0]
    past = ck_ref.shape[4]
    group = C_HEADS // C_KV_HEADS
    nq = C_HEADS * HEAD_DIM
    for kv in range(C_KV_HEADS):
        kall_ref[kv, 0:past, :] = ck_ref[0, 0, kv].T.astype(BF16)
        kall_ref[kv, past:past + n, :] = qkv_ref[:, nq + kv * HEAD_DIM:nq + (kv + 1) * HEAD_DIM]
        vtall_ref[kv, :, 0:past] = cv_ref[0, 0, kv].astype(BF16)
        vtall_ref[kv, :, past:past + n] = vt_ref[0, kv]

    def q_block(j, carry):
        q_rows = pl.ds(pl.multiple_of(j * Q_TILE, Q_TILE), Q_TILE)
        for kv in range(C_KV_HEADS):
            q = _stack_group(qkv_ref, q_rows, kv * group * HEAD_DIM, group)
            o = _attend_keys_on_rows(kall_ref[kv], vtall_ref[kv], q, Q_TILE, group)
            c0 = kv * group * HEAD_DIM
            o_ref[q_rows, c0:c0 + group * HEAD_DIM] = o.astype(BF16)
        return carry

    lax.fori_loop(0, n // Q_TILE, q_block, 0)


def _lat_c_attention(qkv, vt, cache_k, cache_v, o, seq):
    t, n = qkv.shape
    past = cache_k.shape[4]
    width = C_HEADS * HEAD_DIM
    return pl.pallas_call(
        _lat_c_kernel,
        grid=(t // seq,),
        in_specs=[
            pl.BlockSpec((seq, n), lambda b: (b, 0)),
            pl.BlockSpec((1,) + vt.shape[1:], lambda b: (b, 0, 0, 0)),
            _cache_spec(cache_k, o),
            _cache_spec(cache_v, o),
        ],
        out_specs=pl.BlockSpec((seq, width), lambda b: (b, 0)),
        out_shape=jax.ShapeDtypeStruct((t, width), BF16),
        scratch_shapes=[pltpu.VMEM((C_KV_HEADS, past + seq, HEAD_DIM), BF16),
                        pltpu.VMEM((C_KV_HEADS, HEAD_DIM, past + seq), BF16)],
        compiler_params=_params(1),
        name="attn_odd_lat",
    )(qkv, vt, cache_k, cache_v)


def _post_kernel(*refs, n_parts, final):
    o_refs = refs[:n_parts]
    (x_ref, wo_ref, g1_ref, sh_ref, sc_ref, g2_ref, gain_ref, wgu_ref, wd_ref) = refs[n_parts:n_parts + 9]
    rest = refs[n_parts + 9:]
    if final:
        fg_ref, out_ref, act_ref = rest
    else:
        out_ref, act_ref = rest
    mix = None
    r0 = 0
    for o_ref in o_refs:
        kk = o_ref.shape[1]
        part = jnp.dot(o_ref[...], wo_ref[r0:r0 + kk, :], preferred_element_type=F32)
        mix = part if mix is None else mix + part
        r0 += kk
    x1 = x_ref[...] + g1_ref[0, 0] * mix
    h = _adaln(x1, gain_ref[...], sh_ref[0, 0], sc_ref[0, 0]).astype(BF16)
    d_ff = wd_ref.shape[0]
    for j in range(d_ff // FF_CHUNK):
        c0 = j * FF_CHUNK
        gate = jnp.dot(h, wgu_ref[:, c0:c0 + FF_CHUNK], preferred_element_type=F32)
        up = jnp.dot(h, wgu_ref[:, d_ff + c0:d_ff + c0 + FF_CHUNK], preferred_element_type=F32)
        act_ref[:, c0:c0 + FF_CHUNK] = (gate * jax.nn.sigmoid(gate) * up).astype(BF16)
    ffn = jnp.dot(act_ref[...], wd_ref[...], preferred_element_type=F32)
    x2 = x1 + g2_ref[0, 0] * ffn
    if final:
        ms = jnp.mean(x2 * x2, axis=-1, keepdims=True)
        x2 = (x2 * lax.rsqrt(ms + RMS_EPS)) * fg_ref[...]
    out_ref[...] = x2


def _post_attention(o_parts, x, mods, layer, gain, w_out, w_gu, w_down, *, is_lat, seq,
                    final_gain=None):
    t, d = x.shape
    tm = TOKEN_TILE
    per_seq = seq // tm
    group = (lambda i: 1 + i // per_seq) if is_lat else (lambda i: 0)
    row = lambda i: (i, 0)
    d_ff = w_down.shape[0]
    final = final_gain is not None
    in_specs = [pl.BlockSpec((tm, o.shape[1]), row) for o in o_parts]
    in_specs += [
        pl.BlockSpec((tm, d), row),
        _resident(w_out.shape),
        _mod_spec(layer, 2, group),
        _mod_spec(layer, 3, group),
        _mod_spec(layer, 4, group),
        _mod_spec(layer, 5, group),
        _resident((1, d)),
        _resident(w_gu.shape),
        _resident(w_down.shape),
    ]
    args = list(o_parts) + [x, w_out, mods, mods, mods, mods, gain.reshape(1, d), w_gu, w_down]
    if final:
        in_specs.append(_resident((1, d)))
        args.append(final_gain.reshape(1, d))
    return pl.pallas_call(
        functools.partial(_post_kernel, n_parts=len(o_parts), final=final),
        grid=(t // tm,),
        in_specs=in_specs,
        out_specs=pl.BlockSpec((tm, d), row),
        out_shape=jax.ShapeDtypeStruct((t, d), F32),
        scratch_shapes=[pltpu.VMEM((tm, d_ff), BF16)],
        compiler_params=_params(1),
        name=f"post_{'lat' if is_lat else 'ctx'}{'_final' if final else ''}",
    )(*args)


def _rope_tables(n):
    t = jnp.arange(n)
    row = (t // GRID_W).astype(F32)
    col = (t % GRID_W).astype(F32)
    half = HEAD_DIM // 2
    inv_freq = ROPE_THETA ** (-jnp.arange(0, half, 2, dtype=F32) / half)
    lane = jnp.arange(LANES)
    in_head = lane % HEAD_DIM
    pos = jnp.where((in_head < half)[None, :], row[:, None], col[:, None])
    ang = pos * inv_freq[in_head % (half // 2)][None, :]
    first = ((in_head % half) < half // 2)[None, :]
    cos = jnp.cos(ang)
    sin = jnp.sin(ang)
    return cos, jnp.where(first, -sin, 0.0), jnp.where(first, 0.0, sin)


def _na_bias_sources(rpb):
    h, _, nb = rpb.shape
    w = NA_WIN_W - 1
    rp = jnp.pad(rpb, ((0, 0), (1, 1), (0, 0)))
    lo, hi = rp[:, :-1], rp[:, 1:]
    z = jnp.zeros((h, 2 * NA_WIN_H, LANES // 2 - nb), F32)
    return jnp.concatenate([lo[:, :, w:], z, hi, z, lo[:, :, :w]], axis=-1)


def _state(y):
    return y.transpose(0, 3, 1, 2)


def kernel(x_prompt, x_sample, cache_a_k, cache_a_v, cache_b_k, cache_b_v, cache_c_k, cache_c_v,
           c, c_ctx, norm_gain, w_mod, b_mod, w_in_even, w_out_even, rpb_a, sink_b,
           w_in_odd, w_out_odd, q_norm_c, k_norm_c, w_gate_up, w_down, final_gain):
    batch, seq, d = x_prompt.shape
    dec_batch, dec_seq, _ = x_sample.shape
    depth = w_mod.shape[0]

    cvec = jnp.concatenate(
        [c_ctx[None, :], c, jnp.zeros((MOD_GROUPS - 1 - dec_batch, d), F32)], axis=0)
    mods = _modulation(cvec, w_mod, b_mod).reshape(depth, MOD_GROUPS, 1, 6 * d)
    rope = _rope_tables(dec_seq)

    ctx = x_prompt.reshape(batch * seq, d)
    lat = x_sample.reshape(dec_batch * dec_seq, d)
    states = {name: [] for name in ("a_k", "a_v", "b_k", "b_v", "c_k", "c_v")}

    for layer in range(depth):
        last = layer == depth - 1
        gain1, gain2 = norm_gain[layer, 0], norm_gain[layer, 1]
        if layer % 2 == 0:
            e = layer // 2
            w_in = w_in_even[e].astype(BF16)
            w_out = w_out_even[e].astype(BF16)
            qkv_c, ka, va, kb, vb = _pre_attention(
                ctx, mods, layer, gain1, w_in, odd=False, is_lat=False, seq=seq)
            (qkv_l,) = _pre_attention(
                lat, mods, layer, gain1, w_in, odd=False, is_lat=True, seq=dec_seq, rope=rope)
            states["a_k"].append(_state(ka))
            states["a_v"].append(_state(va))
            states["b_k"].append(_state(kb))
            states["b_v"].append(_state(vb))
            o_ctx = [_ctx_even_attention(qkv_c, sink_b[e], seq)]
            o_lat = [
                _lat_a_attention(qkv_l, _feature_major(cache_a_k), _feature_major(cache_a_v),
                                 e, _na_bias_sources(rpb_a[e]), dec_seq),
                _lat_b_attention(qkv_l, _feature_major(cache_b_k), _feature_major(cache_b_v),
                                 e, sink_b[e], dec_seq),
            ]
        else:
            o = layer // 2
            w_in = w_in_odd[o].astype(BF16)
            w_out = w_out_odd[o].astype(BF16)
            head_gains = (jnp.tile(q_norm_c[o], LANES // HEAD_DIM).reshape(1, LANES),
                          jnp.tile(k_norm_c[o], LANES // HEAD_DIM).reshape(1, LANES))
            qkv_c, kc, vc = _pre_attention(
                ctx, mods, layer, gain1, w_in, odd=True, is_lat=False, seq=seq,
                head_gains=head_gains)
            qkv_l, vt_l = _pre_attention(
                lat, mods, layer, gain1, w_in, odd=True, is_lat=True, seq=dec_seq, rope=rope,
                head_gains=head_gains)
            states["c_k"].append(_state(kc))
            states["c_v"].append(_state(vc))
            o_ctx = [_ctx_odd_attention(qkv_c, seq)]
            o_lat = [_lat_c_attention(qkv_l, vt_l, _feature_major(cache_c_k),
                                      _feature_major(cache_c_v), o, dec_seq)]
        w_gu = w_gate_up[layer].astype(BF16)
        w_dn = w_down[layer].astype(BF16)
        fg = final_gain if last else None
        ctx = _post_attention(o_ctx, ctx, mods, layer, gain2, w_out, w_gu, w_dn,
                              is_lat=False, seq=seq, final_gain=fg)
        lat = _post_attention(o_lat, lat, mods, layer, gain2, w_out, w_gu, w_dn,
                              is_lat=True, seq=dec_seq, final_gain=fg)

    return (ctx.reshape(batch, seq, d), lat.reshape(dec_batch, dec_seq, d),
            jnp.stack(states["a_k"], axis=1), jnp.stack(states["a_v"], axis=1),
            jnp.stack(states["b_k"], axis=1), jnp.stack(states["b_v"], axis=1),
            jnp.stack(states["c_k"], axis=1), jnp.stack(states["c_v"], axis=1))
```

```python
import functools
import math

import jax
import jax.numpy as jnp
from jax import lax
from jax.experimental import pallas as pl
from jax.experimental.pallas import tpu as pltpu

F32 = jnp.float32
BF16 = jnp.bfloat16

D_MODEL = 1024
GRID_W = 64
HEAD_DIM = 64
A_HEADS = 8
B_HEADS = 8
B_KV_HEADS = 2
C_HEADS = 16
C_KV_HEADS = 4
NA_WIN_H = 8
NA_WIN_W = 16
B_WINDOW = 128
ROPE_THETA = 10000.0
RMS_EPS = 1e-6
NEG_INF = -1e30
LOG2E = math.log2(math.e)
QK_SCALE = LOG2E / math.sqrt(HEAD_DIM)
ONES_ROWS = 16

LANES = 128
TOKEN_TILE = 512
Q_TILE = 256
ROW_CHUNK = 32
FF_CHUNK = 256
MOD_GROUPS = 16
VMEM_LIMIT = 56 * 1024 * 1024


def _params(n_axes, vmem=VMEM_LIMIT):
    return pltpu.CompilerParams(
        dimension_semantics=("arbitrary",) * n_axes, vmem_limit_bytes=vmem)


def _resident(shape):
    nd = len(shape)
    return pl.BlockSpec(shape, lambda *_: (0,) * nd, pipeline_mode=pl.Buffered(1))


def _mod_kernel(c_ref, w_ref, b_ref, o_ref):
    c = c_ref[...]
    s = (c * jax.nn.sigmoid(c)).astype(BF16)
    o_ref[0] = jnp.dot(s, w_ref[0].astype(BF16), preferred_element_type=F32) + b_ref[0]


def _modulation(cvec, w_mod, b_mod):
    depth, d, n = w_mod.shape
    tn = 1536
    return pl.pallas_call(
        _mod_kernel,
        grid=(depth, n // tn),
        in_specs=[
            pl.BlockSpec((MOD_GROUPS, d), lambda l, j: (0, 0)),
            pl.BlockSpec((1, d, tn), lambda l, j: (l, 0, j)),
            pl.BlockSpec((1, 1, tn), lambda l, j: (l, 0, j)),
        ],
        out_specs=pl.BlockSpec((1, MOD_GROUPS, tn), lambda l, j: (l, 0, j)),
        out_shape=jax.ShapeDtypeStruct((depth, MOD_GROUPS, n), F32),
        compiler_params=_params(2),
        name="modulation",
    )(cvec, w_mod, b_mod.reshape(depth, 1, n))


def _mod_spec(layer, which, group_of_step):
    return pl.BlockSpec((1, 1, 1, D_MODEL), lambda i: (layer, group_of_step(i), 0, which))


def _adaln(x, gain, shift, scale):
    ms = jnp.mean(x * x, axis=-1, keepdims=True)
    return (x * lax.rsqrt(ms + RMS_EPS)) * gain * (1.0 + scale) + shift


def _rope(y, cos, sin_lo, sin_hi):
    outs = []
    for c in range(y.shape[1] // LANES):
        yc = y[:, c * LANES:(c + 1) * LANES]
        outs.append(yc * cos
                    + pltpu.roll(yc, LANES - 16, 1) * sin_lo
                    + pltpu.roll(yc, 16, 1) * sin_hi)
    return outs[0] if len(outs) == 1 else jnp.concatenate(outs, axis=1)


def _head_rms_norm(y, gain):
    first = lax.broadcasted_iota(jnp.int32, (1, LANES), 1) < HEAD_DIM
    outs = []
    for c in range(y.shape[1] // LANES):
        yc = y[:, c * LANES:(c + 1) * LANES]
        sq = yc * yc
        s0 = jnp.sum(jnp.where(first, sq, 0.0), axis=-1, keepdims=True)
        s1 = jnp.sum(jnp.where(first, 0.0, sq), axis=-1, keepdims=True)
        ms = jnp.where(first, s0, s1) * (1.0 / HEAD_DIM)
        outs.append(yc * lax.rsqrt(ms + RMS_EPS) * gain)
    return outs[0] if len(outs) == 1 else jnp.concatenate(outs, axis=1)


_NT = (((1,), (1,)), ((), ()))
_NN = (((1,), (0,)), ((), ()))


def _attn_scratch(m, n):
    return [pltpu.VMEM((2, m, n), F32), pltpu.VMEM((2, m, n), BF16),
            pltpu.VMEM((2, 3, m, LANES), F32)]


def _slot(scratch, step):
    return tuple(r.at[step % 2] for r in scratch)


def _put_scores(s_ref, c0, q, k, *, k_is_transposed, bias=None, valid=None):
    s = lax.dot_general(q, k, _NN if k_is_transposed else _NT, preferred_element_type=F32)
    if bias is not None:
        s = s + bias
    if valid is not None:
        s = jnp.where(valid, s, NEG_INF)
    s_ref[:q.shape[0], c0:c0 + s.shape[1]] = s


def _softmax_pv(s_ref, p_ref, stat_ref, m_rows, values, sink=None):
    n = max(c1 for _, c1, _, _ in values)
    chunks = range(0, m_rows, ROW_CHUNK)
    lanes = [slice(c, c + LANES) for c in range(0, n, LANES)]
    for r in chunks:
        rows = slice(r, r + ROW_CHUNK)
        acc = s_ref[rows, lanes[0]]
        for cols in lanes[1:]:
            acc = jnp.maximum(acc, s_ref[rows, cols])
        stat_ref[0, rows, :] = acc
    m = jnp.max(stat_ref[0, :m_rows, :], axis=-1, keepdims=True)
    if sink is not None:
        m = jnp.maximum(m, sink)
    stat_ref[1, :m_rows, :] = jnp.broadcast_to(m, (m_rows, LANES))
    for r in chunks:
        rows = slice(r, r + ROW_CHUNK)
        mb = stat_ref[1, rows, :]
        acc = None
        for cols in lanes:
            p = jnp.exp2(s_ref[rows, cols] - mb)
            acc = p if acc is None else acc + p
            p_ref[rows, cols] = p.astype(BF16)
        stat_ref[2, rows, :] = acc
    denom = jnp.sum(stat_ref[2, :m_rows, :], axis=-1, keepdims=True)
    if sink is not None:
        denom = denom + jnp.exp2(sink - m)
    out = None
    for c0, c1, v, v_is_transposed in values:
        pv = lax.dot_general(p_ref[:m_rows, c0:c1], v, _NT if v_is_transposed else _NN,
                             preferred_element_type=F32)
        out = pv if out is None else out + pv
    return out / denom


def _stack_group(ref, rows, col0, group):
    return jnp.concatenate(
        [ref[rows, col0 + g * HEAD_DIM: col0 + (g + 1) * HEAD_DIM] for g in range(group)], axis=0)


def _unstack_group(o, rows, group):
    return jnp.concatenate([o[g * rows:(g + 1) * rows] for g in range(group)], axis=1)


def _sink_column(sink_ref, h0, group, rows):
    return jnp.concatenate(
        [jnp.full((rows, 1), sink_ref[h0 + g] * LOG2E, F32) for g in range(group)], axis=0)


def _store_state(ref, y):
    nb, heads, _, seq = ref.shape
    yt = y.T
    for b in range(nb):
        for h in range(heads):
            ref[b, h] = yt[h * HEAD_DIM:(h + 1) * HEAD_DIM, b * seq:(b + 1) * seq]


def _pre_even_kernel(x_ref, g_ref, sh_ref, sc_ref, w_ref, *rest, is_lat):
    if is_lat:
        cos_ref, slo_ref, shi_ref, qkv_ref = rest
    else:
        qkv_ref, ka_ref, va_ref, kb_ref, vb_ref = rest
    h = _adaln(x_ref[...], g_ref[...], sh_ref[0, 0], sc_ref[0, 0]).astype(BF16)

    def proj(c0, c1):
        return jnp.dot(h, w_ref[:, c0:c1], preferred_element_type=F32)

    na = A_HEADS * HEAD_DIM
    nb = B_HEADS * HEAD_DIM
    nkv = B_KV_HEADS * HEAD_DIM
    c = 0
    qkv_ref[:, c:c + na] = (proj(c, c + na) * QK_SCALE).astype(BF16)
    c += na
    ka = proj(c, c + na)
    qkv_ref[:, c:c + na] = ka.astype(BF16)
    c += na
    va = proj(c, c + na)
    qkv_ref[:, c:c + na] = va.astype(BF16)
    c += na
    qb = proj(c, c + nb)
    if is_lat:
        qb = _rope(qb, cos_ref[...], slo_ref[...], shi_ref[...])
    qkv_ref[:, c:c + nb] = (qb * QK_SCALE).astype(BF16)
    c += nb
    kb = proj(c, c + nkv)
    if is_lat:
        kb_out = _rope(kb, cos_ref[...], slo_ref[...], shi_ref[...])
    else:
        kb_out = kb
    qkv_ref[:, c:c + nkv] = kb_out.astype(BF16)
    c += nkv
    vb = proj(c, c + nkv)
    qkv_ref[:, c:c + nkv] = vb.astype(BF16)
    if not is_lat:
        _store_state(ka_ref, ka)
        _store_state(va_ref, va)
        _store_state(kb_ref, kb)
        _store_state(vb_ref, vb)


def _pre_odd_kernel(x_ref, g_ref, sh_ref, sc_ref, w_ref, qn_ref, kn_ref, *rest, is_lat):
    if is_lat:
        cos_ref, slo_ref, shi_ref, qkv_ref, vt_ref = rest
    else:
        qkv_ref, kc_ref, vc_ref = rest
    h = _adaln(x_ref[...], g_ref[...], sh_ref[0, 0], sc_ref[0, 0]).astype(BF16)

    def proj(c0, c1):
        return jnp.dot(h, w_ref[:, c0:c1], preferred_element_type=F32)

    nq = C_HEADS * HEAD_DIM
    nkv = C_KV_HEADS * HEAD_DIM
    q = _head_rms_norm(proj(0, nq), qn_ref[...])
    k = _head_rms_norm(proj(nq, nq + nkv), kn_ref[...])
    v = proj(nq + nkv, nq + 2 * nkv)
    if is_lat:
        q = _rope(q, cos_ref[...], slo_ref[...], shi_ref[...])
        k_out = _rope(k, cos_ref[...], slo_ref[...], shi_ref[...])
        vt = v.T
        for hh in range(C_KV_HEADS):
            vt_ref[0, hh] = vt[hh * HEAD_DIM:(hh + 1) * HEAD_DIM, :].astype(BF16)
    else:
        k_out = k
        _store_state(kc_ref, k)
        _store_state(vc_ref, v)
    qkv_ref[:, 0:nq] = (q * QK_SCALE).astype(BF16)
    qkv_ref[:, nq:nq + nkv] = k_out.astype(BF16)
    qkv_ref[:, nq + nkv:nq + 2 * nkv] = v.astype(BF16)


def _pre_attention(x, mods, layer, gain, w, *, odd, is_lat, seq, rope=None, head_gains=None):
    t, d = x.shape
    n = w.shape[1]
    tm = TOKEN_TILE
    per_seq = seq // tm
    group = (lambda i: 1 + i // per_seq) if is_lat else (lambda i: 0)
    row = lambda i: (i, 0)
    in_specs = [
        pl.BlockSpec((tm, d), row),
        _resident((1, d)),
        _mod_spec(layer, 0, group),
        _mod_spec(layer, 1, group),
        _resident((d, n)),
    ]
    args = [x, gain.reshape(1, d), mods, mods, w]
    if odd:
        in_specs += [_resident((1, LANES)), _resident((1, LANES))]
        args += list(head_gains)
    if is_lat:
        in_specs += [pl.BlockSpec((tm, LANES), lambda i: (i % per_seq, 0))] * 3
        args += list(rope)
    out_specs = [pl.BlockSpec((tm, n), row)]
    out_shape = [jax.ShapeDtypeStruct((t, n), BF16)]
    if is_lat and odd:
        out_specs.append(pl.BlockSpec((1, C_KV_HEADS, HEAD_DIM, tm),
                                      lambda i: (i // per_seq, 0, 0, i % per_seq)))
        out_shape.append(jax.ShapeDtypeStruct((t // seq, C_KV_HEADS, HEAD_DIM, seq), BF16))
    if not is_lat:
        if odd:
            widths = [C_KV_HEADS * HEAD_DIM] * 2
        else:
            widths = [A_HEADS * HEAD_DIM] * 2 + [B_KV_HEADS * HEAD_DIM] * 2
        for wd in widths:
            heads = wd // HEAD_DIM
            out_specs.append(pl.BlockSpec((tm // seq, heads, HEAD_DIM, seq), lambda i: (i, 0, 0, 0)))
            out_shape.append(jax.ShapeDtypeStruct((t // seq, heads, HEAD_DIM, seq), F32))
    body = _pre_odd_kernel if odd else _pre_even_kernel
    return pl.pallas_call(
        functools.partial(body, is_lat=is_lat),
        grid=(t // tm,),
        in_specs=in_specs,
        out_specs=out_specs,
        out_shape=out_shape,
        compiler_params=_params(1),
        name=f"pre_{'odd' if odd else 'even'}_{'lat' if is_lat else 'ctx'}",
    )(*args)


def _sink_row(sink_ref, h0, group, rows):
    return jnp.concatenate(
        [jnp.full((1, rows), sink_ref[h0 + g] * LOG2E, F32) for g in range(group)], axis=1)


def _ctx_even_attn_kernel(sink_ref, qkv_ref, vat_ref, vbt_ref, o_ref):
    rows = slice(None)
    n = qkv_ref.shape[0]
    na = A_HEADS * HEAD_DIM
    for pair in range(A_HEADS // 2):
        outs = []
        for h in (2 * pair, 2 * pair + 1):
            c = h * HEAD_DIM
            outs.append(_attend_keys_on_rows(
                qkv_ref[:, na + c:na + c + HEAD_DIM], _with_ones_rows(vat_ref[0, h]),
                qkv_ref[:, c:c + HEAD_DIM], n, 1))
        o_ref[:, pair * LANES:(pair + 1) * LANES] = jnp.concatenate(outs, axis=1).astype(BF16)
    group = B_HEADS // B_KV_HEADS
    qb0 = 3 * na
    kb0 = qb0 + B_HEADS * HEAD_DIM
    for kv in range(B_KV_HEADS):
        q = _stack_group(qkv_ref, rows, qb0 + kv * group * HEAD_DIM, group)
        k = qkv_ref[:, kb0 + kv * HEAD_DIM:kb0 + (kv + 1) * HEAD_DIM]
        o = _attend_keys_on_rows(k, _with_ones_rows(vbt_ref[0, kv]), q, n, group,
                                 _sink_row(sink_ref, kv * group, group, n))
        c0 = na + kv * group * HEAD_DIM
        o_ref[:, c0:c0 + group * HEAD_DIM] = o.astype(BF16)


def _ctx_even_attention(qkv, vat, vbt, sink, seq):
    t, n = qkv.shape
    width = (A_HEADS + B_HEADS) * HEAD_DIM
    return pl.pallas_call(
        _ctx_even_attn_kernel,
        grid=(t // seq,),
        in_specs=[pl.BlockSpec(memory_space=pltpu.SMEM),
                  pl.BlockSpec((seq, n), lambda b: (b, 0)),
                  pl.BlockSpec((1,) + vat.shape[1:], lambda b: (b, 0, 0, 0)),
                  pl.BlockSpec((1,) + vbt.shape[1:], lambda b: (b, 0, 0, 0))],
        out_specs=pl.BlockSpec((seq, width), lambda b: (b, 0)),
        out_shape=jax.ShapeDtypeStruct((t, width), BF16),
        compiler_params=_params(1),
        name="attn_even_ctx",
    )(sink, qkv, vat, vbt)


def _na_key_window(j):
    return min(max(Q_TILE * j - Q_TILE, 0), Q_TILE)


def _na_bias_tiles(src_ref, h):
    qc = lax.broadcasted_iota(jnp.int32, (GRID_W, LANES), 0)
    kc = lax.rem(lax.broadcasted_iota(jnp.int32, (GRID_W, LANES), 1), GRID_W)
    cs = jnp.clip(qc - NA_WIN_W // 2, 0, GRID_W - NA_WIN_W)
    col_ok = (kc >= cs) & (kc < cs + NA_WIN_W)
    tiles = []
    for i in range(2 * NA_WIN_H):
        src = jnp.broadcast_to(src_ref[h, i:i + 1, :], (GRID_W, LANES))
        tiles.append(jnp.where(col_ok, pltpu.roll(src, 0, 1, stride=1, stride_axis=0), NEG_INF))
    return tiles


def _na_bias_block(tiles, j, grid_rows):
    wh = min(NA_WIN_H, grid_rows)
    first_row = lax.broadcasted_iota(jnp.int32, (GRID_W, LANES), 1) < GRID_W
    lo_r = _na_key_window(j) // GRID_W
    strips = []
    for qr in range(j * Q_TILE // GRID_W, (j + 1) * Q_TILE // GRID_W):
        rs = min(max(qr - wh // 2, 0), grid_rows - wh)
        parts = []
        for t in range(3 * Q_TILE // LANES):
            kr = lo_r + 2 * t
            ok0 = rs <= kr < rs + wh
            ok1 = rs <= kr + 1 < rs + wh
            i = kr - qr + NA_WIN_H
            if ok0 and ok1:
                parts.append(tiles[i])
            elif ok0:
                parts.append(jnp.where(first_row, tiles[i], NEG_INF))
            elif ok1:
                parts.append(jnp.where(first_row, NEG_INF, tiles[i]))
            else:
                parts.append(jnp.full((GRID_W, LANES), NEG_INF, F32))
        strips.append(jnp.concatenate(parts, axis=1))
    return jnp.concatenate(strips, axis=0)


def _cache_head(ref, h):
    return ref[0, 0, h].astype(BF16)


def _lat_a_kernel(src_ref, q_ref, k_ref, v_ref, ck_ref, cv_ref, o_ref, *scratch):
    n = q_ref.shape[0]
    past = ck_ref.shape[4]
    win = 3 * Q_TILE
    step = 0
    for pair in range(A_HEADS // 2):
        outs = [[] for _ in range(n // Q_TILE)]
        for h in (2 * pair, 2 * pair + 1):
            cols = slice(h * HEAD_DIM, (h + 1) * HEAD_DIM)
            ck = _cache_head(ck_ref, h)
            cv = _cache_head(cv_ref, h)
            tiles = _na_bias_tiles(src_ref, h)
            for j in range(n // Q_TILE):
                slot = _slot(scratch, step)
                step += 1
                lo = _na_key_window(j)
                q = q_ref[j * Q_TILE:(j + 1) * Q_TILE, cols]
                _put_scores(slot[0], 0, q, k_ref[lo:lo + win, cols], k_is_transposed=False,
                            bias=_na_bias_block(tiles, j, n // GRID_W))
                _put_scores(slot[0], win, q, ck, k_is_transposed=True)
                outs[j].append(_softmax_pv(
                    *slot, Q_TILE,
                    [(0, win, v_ref[lo:lo + win, cols], False), (win, win + past, cv, True)]))
        for j in range(n // Q_TILE):
            o_ref[j * Q_TILE:(j + 1) * Q_TILE, pair * LANES:(pair + 1) * LANES] = (
                jnp.concatenate(outs[j], axis=1).astype(BF16))


def _feature_major(cache):
    return cache.transpose(0, 1, 3, 4, 2)


def _cache_spec(cache, layer):
    blk = (1, 1) + cache.shape[2:]
    return pl.BlockSpec(blk, lambda b: (b, layer, 0, 0, 0))


def _lat_a_attention(qkv, cache_k, cache_v, e, bias_src, seq):
    t = qkv.shape[0]
    past = cache_k.shape[4]
    na = A_HEADS * HEAD_DIM
    return pl.pallas_call(
        _lat_a_kernel,
        grid=(t // seq,),
        in_specs=[
            _resident(bias_src.shape),
            pl.BlockSpec((seq, na), lambda b: (b, 0)),
            pl.BlockSpec((seq, na), lambda b: (b, 1)),
            pl.BlockSpec((seq, na), lambda b: (b, 2)),
            _cache_spec(cache_k, e),
            _cache_spec(cache_v, e),
        ],
        out_specs=pl.BlockSpec((seq, na), lambda b: (b, 0)),
        out_shape=jax.ShapeDtypeStruct((t, na), BF16),
        scratch_shapes=_attn_scratch(Q_TILE, 3 * Q_TILE + past),
        compiler_params=_params(1),
        name="attn_even_lat_a",
    )(bias_src, qkv, qkv, qkv, cache_k, cache_v)


def _lat_b_kernel(sink_ref, q_ref, k_ref, v_ref, ck_ref, cv_ref, o_ref, *scratch):
    n = q_ref.shape[0]
    past = ck_ref.shape[4]
    group = B_HEADS // B_KV_HEADS
    win = 2 * Q_TILE
    ctx = [(_cache_head(ck_ref, kv), _cache_head(cv_ref, kv)) for kv in range(B_KV_HEADS)]
    step = 0
    for j in range(n // Q_TILE):
        lo = min(max(Q_TILE * j - B_WINDOW, 0), n - win)
        q_rows = slice(j * Q_TILE, (j + 1) * Q_TILE)
        qpos = j * Q_TILE + lax.broadcasted_iota(jnp.int32, (group * Q_TILE, 1), 0) % Q_TILE
        kpos = lo + lax.broadcasted_iota(jnp.int32, (1, win), 1)
        valid = jnp.abs(qpos - kpos) <= B_WINDOW
        for kv in range(B_KV_HEADS):
            slot = _slot(scratch, step)
            step += 1
            cols = slice(kv * HEAD_DIM, (kv + 1) * HEAD_DIM)
            q = _stack_group(q_ref, q_rows, kv * group * HEAD_DIM, group)
            _put_scores(slot[0], 0, q, k_ref[lo:lo + win, cols], k_is_transposed=False,
                        valid=valid)
            _put_scores(slot[0], win, q, ctx[kv][0], k_is_transposed=True)
            o = _softmax_pv(
                *slot, group * Q_TILE,
                [(0, win, v_ref[lo:lo + win, cols], False), (win, win + past, ctx[kv][1], True)],
                _sink_column(sink_ref, kv * group, group, Q_TILE))
            c0 = kv * group * HEAD_DIM
            o_ref[q_rows, c0:c0 + group * HEAD_DIM] = _unstack_group(o, Q_TILE, group).astype(BF16)


def _lat_b_attention(qkv, cache_k, cache_v, e, sink, seq):
    t = qkv.shape[0]
    past = cache_k.shape[4]
    na = A_HEADS * HEAD_DIM
    nb = B_HEADS * HEAD_DIM
    nkv = B_KV_HEADS * HEAD_DIM
    group = B_HEADS // B_KV_HEADS
    return pl.pallas_call(
        _lat_b_kernel,
        grid=(t // seq,),
        in_specs=[
            pl.BlockSpec(memory_space=pltpu.SMEM),
            pl.BlockSpec((seq, nb), lambda b: (b, 3 * na // nb)),
            pl.BlockSpec((seq, nkv), lambda b: (b, (3 * na + nb) // nkv)),
            pl.BlockSpec((seq, nkv), lambda b: (b, (3 * na + nb) // nkv + 1)),
            _cache_spec(cache_k, e),
            _cache_spec(cache_v, e),
        ],
        out_specs=pl.BlockSpec((seq, nb), lambda b: (b, 0)),
        out_shape=jax.ShapeDtypeStruct((t, nb), BF16),
        scratch_shapes=_attn_scratch(group * Q_TILE, 2 * Q_TILE + past),
        compiler_params=_params(1),
        name="attn_even_lat_b",
    )(sink, qkv, qkv, qkv, cache_k, cache_v)


def _with_ones_rows(vt):
    return jnp.concatenate([vt.astype(BF16), jnp.ones((ONES_ROWS, vt.shape[1]), BF16)], axis=0)


def _ctx_odd_attn_kernel(qkv_ref, vt_ref, o_ref):
    n = qkv_ref.shape[0]
    group = C_HEADS // C_KV_HEADS
    nq = C_HEADS * HEAD_DIM
    for kv in range(C_KV_HEADS):
        q = _stack_group(qkv_ref, slice(None), kv * group * HEAD_DIM, group)
        k = qkv_ref[:, nq + kv * HEAD_DIM:nq + (kv + 1) * HEAD_DIM]
        o = _attend_keys_on_rows(k, _with_ones_rows(vt_ref[0, kv]), q, n, group)
        c0 = kv * group * HEAD_DIM
        o_ref[:, c0:c0 + group * HEAD_DIM] = o.astype(BF16)


def _ctx_odd_attention(qkv, vt, seq):
    t, n = qkv.shape
    width = C_HEADS * HEAD_DIM
    return pl.pallas_call(
        _ctx_odd_attn_kernel,
        grid=(t // seq,),
        in_specs=[pl.BlockSpec((seq, n), lambda b: (b, 0)),
                  pl.BlockSpec((1,) + vt.shape[1:], lambda b: (b, 0, 0, 0))],
        out_specs=pl.BlockSpec((seq, width), lambda b: (b, 0)),
        out_shape=jax.ShapeDtypeStruct((t, width), BF16),
        compiler_params=_params(1),
        name="attn_odd_ctx",
    )(qkv, vt)


def _attend_keys_on_rows(k_all, vt_ones, q, rows, group, sink=None):
    s = lax.dot_general(k_all, q, _NT, preferred_element_type=F32)
    m = jnp.max(s, axis=0, keepdims=True)
    if sink is not None:
        m = jnp.maximum(m, sink)
    p = jnp.exp2(s - m).astype(BF16)
    ot = jnp.dot(vt_ones, p, preferred_element_type=F32)
    denom = ot[HEAD_DIM:HEAD_DIM + 1, :]
    if sink is not None:
        denom = denom + jnp.exp2(sink - m)
    ot = ot[:HEAD_DIM, :] / denom
    return jnp.concatenate([ot[:, g * rows:(g + 1) * rows].T for g in range(group)], axis=1)


def _lat_c_kernel(qkv_ref, vt_ref, ck_ref, cv_ref, o_ref, kall_ref, vtall_ref):
    n = qkv_ref.shape[0]
    past = ck_ref.shape[4]
    group = C_HEADS // C_KV_HEADS
    nq = C_HEADS * HEAD_DIM
    for kv in range(C_KV_HEADS):
        kall_ref[kv, 0:past, :] = ck_ref[0, 0, kv].T.astype(BF16)
        kall_ref[kv, past:past + n, :] = qkv_ref[:, nq + kv * HEAD_DIM:nq + (kv + 1) * HEAD_DIM]
        vtall_ref[kv, 0:HEAD_DIM, 0:past] = cv_ref[0, 0, kv].astype(BF16)
        vtall_ref[kv, 0:HEAD_DIM, past:past + n] = vt_ref[0, kv]
        vtall_ref[kv, HEAD_DIM:, :] = jnp.ones((ONES_ROWS, past + n), BF16)

    def q_block(j, carry):
        q_rows = pl.ds(pl.multiple_of(j * Q_TILE, Q_TILE), Q_TILE)
        for kv in range(C_KV_HEADS):
            q = _stack_group(qkv_ref, q_rows, kv * group * HEAD_DIM, group)
            o = _attend_keys_on_rows(kall_ref[kv], vtall_ref[kv], q, Q_TILE, group)
            c0 = kv * group * HEAD_DIM
            o_ref[q_rows, c0:c0 + group * HEAD_DIM] = o.astype(BF16)
        return carry

    lax.fori_loop(0, n // Q_TILE, q_block, 0)


def _lat_c_attention(qkv, vt, cache_k, cache_v, o, seq):
    t, n = qkv.shape
    past = cache_k.shape[4]
    width = C_HEADS * HEAD_DIM
    return pl.pallas_call(
        _lat_c_kernel,
        grid=(t // seq,),
        in_specs=[
            pl.BlockSpec((seq, n), lambda b: (b, 0)),
            pl.BlockSpec((1,) + vt.shape[1:], lambda b: (b, 0, 0, 0)),
            _cache_spec(cache_k, o),
            _cache_spec(cache_v, o),
        ],
        out_specs=pl.BlockSpec((seq, width), lambda b: (b, 0)),
        out_shape=jax.ShapeDtypeStruct((t, width), BF16),
        scratch_shapes=[pltpu.VMEM((C_KV_HEADS, past + seq, HEAD_DIM), BF16),
                        pltpu.VMEM((C_KV_HEADS, HEAD_DIM + ONES_ROWS, past + seq), BF16)],
        compiler_params=_params(1),
        name="attn_odd_lat",
    )(qkv, vt, cache_k, cache_v)


def _post_kernel(*refs, n_parts, final):
    o_refs = refs[:n_parts]
    (x_ref, wo_ref, g1_ref, sh_ref, sc_ref, g2_ref, gain_ref, wgu_ref, wd_ref) = refs[n_parts:n_parts + 9]
    rest = refs[n_parts + 9:]
    if final:
        fg_ref, out_ref, act_ref = rest
    else:
        out_ref, act_ref = rest
    mix = None
    r0 = 0
    for o_ref in o_refs:
        kk = o_ref.shape[1]
        part = jnp.dot(o_ref[...], wo_ref[r0:r0 + kk, :], preferred_element_type=F32)
        mix = part if mix is None else mix + part
        r0 += kk
    x1 = x_ref[...] + g1_ref[0, 0] * mix
    h = _adaln(x1, gain_ref[...], sh_ref[0, 0], sc_ref[0, 0]).astype(BF16)
    d_ff = wd_ref.shape[0]
    for j in range(d_ff // FF_CHUNK):
        c0 = j * FF_CHUNK
        gate = jnp.dot(h, wgu_ref[:, c0:c0 + FF_CHUNK], preferred_element_type=F32)
        up = jnp.dot(h, wgu_ref[:, d_ff + c0:d_ff + c0 + FF_CHUNK], preferred_element_type=F32)
        act_ref[:, c0:c0 + FF_CHUNK] = (gate * jax.nn.sigmoid(gate) * up).astype(BF16)
    ffn = jnp.dot(act_ref[...], wd_ref[...], preferred_element_type=F32)
    x2 = x1 + g2_ref[0, 0] * ffn
    if final:
        ms = jnp.mean(x2 * x2, axis=-1, keepdims=True)
        x2 = (x2 * lax.rsqrt(ms + RMS_EPS)) * fg_ref[...]
    out_ref[...] = x2


def _post_attention(o_parts, x, mods, layer, gain, w_out, w_gu, w_down, *, is_lat, seq,
                    final_gain=None):
    t, d = x.shape
    tm = TOKEN_TILE
    per_seq = seq // tm
    group = (lambda i: 1 + i // per_seq) if is_lat else (lambda i: 0)
    row = lambda i: (i, 0)
    d_ff = w_down.shape[0]
    final = final_gain is not None
    in_specs = [pl.BlockSpec((tm, o.shape[1]), row) for o in o_parts]
    in_specs += [
        pl.BlockSpec((tm, d), row),
        _resident(w_out.shape),
        _mod_spec(layer, 2, group),
        _mod_spec(layer, 3, group),
        _mod_spec(layer, 4, group),
        _mod_spec(layer, 5, group),
        _resident((1, d)),
        _resident(w_gu.shape),
        _resident(w_down.shape),
    ]
    args = list(o_parts) + [x, w_out, mods, mods, mods, mods, gain.reshape(1, d), w_gu, w_down]
    if final:
        in_specs.append(_resident((1, d)))
        args.append(final_gain.reshape(1, d))
    return pl.pallas_call(
        functools.partial(_post_kernel, n_parts=len(o_parts), final=final),
        grid=(t // tm,),
        in_specs=in_specs,
        out_specs=pl.BlockSpec((tm, d), row),
        out_shape=jax.ShapeDtypeStruct((t, d), F32),
        scratch_shapes=[pltpu.VMEM((tm, d_ff), BF16)],
        compiler_params=_params(1),
        name=f"post_{'lat' if is_lat else 'ctx'}{'_final' if final else ''}",
    )(*args)


def _rope_tables(n):
    t = jnp.arange(n)
    row = (t // GRID_W).astype(F32)
    col = (t % GRID_W).astype(F32)
    half = HEAD_DIM // 2
    inv_freq = ROPE_THETA ** (-jnp.arange(0, half, 2, dtype=F32) / half)
    lane = jnp.arange(LANES)
    in_head = lane % HEAD_DIM
    pos = jnp.where((in_head < half)[None, :], row[:, None], col[:, None])
    ang = pos * inv_freq[in_head % (half // 2)][None, :]
    first = ((in_head % half) < half // 2)[None, :]
    cos = jnp.cos(ang)
    sin = jnp.sin(ang)
    return cos, jnp.where(first, -sin, 0.0), jnp.where(first, 0.0, sin)


def _na_bias_sources(rpb):
    h, _, nb = rpb.shape
    w = NA_WIN_W - 1
    rp = jnp.pad(rpb * LOG2E, ((0, 0), (1, 1), (0, 0)))
    lo, hi = rp[:, :-1], rp[:, 1:]
    z = jnp.zeros((h, 2 * NA_WIN_H, LANES // 2 - nb), F32)
    return jnp.concatenate([lo[:, :, w:], z, hi, z, lo[:, :, :w]], axis=-1)


def _state(y):
    return y.transpose(0, 3, 1, 2)


def kernel(x_prompt, x_sample, cache_a_k, cache_a_v, cache_b_k, cache_b_v, cache_c_k, cache_c_v,
           c, c_ctx, norm_gain, w_mod, b_mod, w_in_even, w_out_even, rpb_a, sink_b,
           w_in_odd, w_out_odd, q_norm_c, k_norm_c, w_gate_up, w_down, final_gain):
    batch, seq, d = x_prompt.shape
    dec_batch, dec_seq, _ = x_sample.shape
    depth = w_mod.shape[0]

    cvec = jnp.concatenate(
        [c_ctx[None, :], c, jnp.zeros((MOD_GROUPS - 1 - dec_batch, d), F32)], axis=0)
    mods = _modulation(cvec, w_mod, b_mod).reshape(depth, MOD_GROUPS, 1, 6 * d)
    rope = _rope_tables(dec_seq)

    ctx = x_prompt.reshape(batch * seq, d)
    lat = x_sample.reshape(dec_batch * dec_seq, d)
    states = {name: [] for name in ("a_k", "a_v", "b_k", "b_v", "c_k", "c_v")}

    for layer in range(depth):
        last = layer == depth - 1
        gain1, gain2 = norm_gain[layer, 0], norm_gain[layer, 1]
        if layer % 2 == 0:
            e = layer // 2
            w_in = w_in_even[e].astype(BF16)
            w_out = w_out_even[e].astype(BF16)
            qkv_c, ka, va, kb, vb = _pre_attention(
                ctx, mods, layer, gain1, w_in, odd=False, is_lat=False, seq=seq)
            (qkv_l,) = _pre_attention(
                lat, mods, layer, gain1, w_in, odd=False, is_lat=True, seq=dec_seq, rope=rope)
            states["a_k"].append(_state(ka))
            states["a_v"].append(_state(va))
            states["b_k"].append(_state(kb))
            states["b_v"].append(_state(vb))
            o_ctx = [_ctx_even_attention(qkv_c, va, vb, sink_b[e], seq)]
            o_lat = [
                _lat_a_attention(qkv_l, _feature_major(cache_a_k), _feature_major(cache_a_v),
                                 e, _na_bias_sources(rpb_a[e]), dec_seq),
                _lat_b_attention(qkv_l, _feature_major(cache_b_k), _feature_major(cache_b_v),
                                 e, sink_b[e], dec_seq),
            ]
        else:
            o = layer // 2
            w_in = w_in_odd[o].astype(BF16)
            w_out = w_out_odd[o].astype(BF16)
            head_gains = (jnp.tile(q_norm_c[o], LANES // HEAD_DIM).reshape(1, LANES),
                          jnp.tile(k_norm_c[o], LANES // HEAD_DIM).reshape(1, LANES))
            qkv_c, kc, vc = _pre_attention(
                ctx, mods, layer, gain1, w_in, odd=True, is_lat=False, seq=seq,
                head_gains=head_gains)
            qkv_l, vt_l = _pre_attention(
                lat, mods, layer, gain1, w_in, odd=True, is_lat=True, seq=dec_seq, rope=rope,
                head_gains=head_gains)
            states["c_k"].append(_state(kc))
            states["c_v"].append(_state(vc))
            o_ctx = [_ctx_odd_attention(qkv_c, vc, seq)]
            o_lat = [_lat_c_attention(qkv_l, vt_l, _feature_major(cache_c_k),
                                      _feature_major(cache_c_v), o, dec_seq)]
        w_gu = w_gate_up[layer].astype(BF16)
        w_dn = w_down[layer].astype(BF16)
        fg = final_gain if last else None
        ctx = _post_attention(o_ctx, ctx, mods, layer, gain2, w_out, w_gu, w_dn,
                              is_lat=False, seq=seq, final_gain=fg)
        lat = _post_attention(o_lat, lat, mods, layer, gain2, w_out, w_gu, w_dn,
                              is_lat=True, seq=dec_seq, final_gain=fg)

    return (ctx.reshape(batch, seq, d), lat.reshape(dec_batch, dec_seq, d),
            jnp.stack(states["a_k"], axis=1), jnp.stack(states["a_v"], axis=1),
            jnp.stack(states["b_k"], axis=1), jnp.stack(states["b_v"], axis=1),
            jnp.stack(states["c_k"], axis=1), jnp.stack(states["c_v"], axis=1))
```

```python
import functools
import math

import jax
import jax.numpy as jnp
from jax import lax
from jax.experimental import pallas as pl
from jax.experimental.pallas import tpu as pltpu

F32 = jnp.float32
BF16 = jnp.bfloat16

D_MODEL = 1024
GRID_W = 64
HEAD_DIM = 64
A_HEADS = 8
B_HEADS = 8
B_KV_HEADS = 2
C_HEADS = 16
C_KV_HEADS = 4
NA_WIN_H = 8
NA_WIN_W = 16
B_WINDOW = 128
ROPE_THETA = 10000.0
RMS_EPS = 1e-6
NEG_INF = -1e30
LOG2E = math.log2(math.e)
QK_SCALE = LOG2E / math.sqrt(HEAD_DIM)

LANES = 128
TOKEN_TILE = 512
Q_TILE = 256
ONES_ROWS = 16
FF_CHUNK = 256
MOD_GROUPS = 16
VMEM_LIMIT = 56 * 1024 * 1024


def _params(n_axes, vmem=VMEM_LIMIT):
    return pltpu.CompilerParams(
        dimension_semantics=("arbitrary",) * n_axes, vmem_limit_bytes=vmem)


def _resident(shape):
    nd = len(shape)
    return pl.BlockSpec(shape, lambda *_: (0,) * nd, pipeline_mode=pl.Buffered(1))


def _resident_layer(stacked, layer):
    return pl.BlockSpec((None,) + stacked.shape[1:], lambda *_: (layer, 0, 0),
                        pipeline_mode=pl.Buffered(1))


def _mod_kernel(c_ref, w_ref, b_ref, o_ref):
    c = c_ref[...]
    s = (c * jax.nn.sigmoid(c)).astype(BF16)
    o_ref[0] = jnp.dot(s, w_ref[0].astype(BF16), preferred_element_type=F32) + b_ref[0]


def _modulation(cvec, w_mod, b_mod):
    depth, d, n = w_mod.shape
    tn = 1536
    return pl.pallas_call(
        _mod_kernel,
        grid=(depth, n // tn),
        in_specs=[
            pl.BlockSpec((MOD_GROUPS, d), lambda l, j: (0, 0)),
            pl.BlockSpec((1, d, tn), lambda l, j: (l, 0, j)),
            pl.BlockSpec((1, 1, tn), lambda l, j: (l, 0, j)),
        ],
        out_specs=pl.BlockSpec((1, MOD_GROUPS, tn), lambda l, j: (l, 0, j)),
        out_shape=jax.ShapeDtypeStruct((depth, MOD_GROUPS, n), F32),
        compiler_params=_params(2),
        name="modulation",
    )(cvec, w_mod, b_mod.reshape(depth, 1, n))


def _mod_spec(layer, which, group_of_step):
    return pl.BlockSpec((1, 1, 1, D_MODEL), lambda i: (layer, group_of_step(i), 0, which))


def _adaln(x, gain, shift, scale):
    ms = jnp.mean(x * x, axis=-1, keepdims=True)
    return (x * lax.rsqrt(ms + RMS_EPS)) * gain * (1.0 + scale) + shift


def _rope(y, cos, sin_lo, sin_hi):
    outs = []
    for c in range(y.shape[1] // LANES):
        yc = y[:, c * LANES:(c + 1) * LANES]
        outs.append(yc * cos
                    + pltpu.roll(yc, LANES - 16, 1) * sin_lo
                    + pltpu.roll(yc, 16, 1) * sin_hi)
    return outs[0] if len(outs) == 1 else jnp.concatenate(outs, axis=1)


def _head_rms_norm(y, gain):
    first = lax.broadcasted_iota(jnp.int32, (1, LANES), 1) < HEAD_DIM
    outs = []
    for c in range(y.shape[1] // LANES):
        yc = y[:, c * LANES:(c + 1) * LANES]
        sq = yc * yc
        s0 = jnp.sum(jnp.where(first, sq, 0.0), axis=-1, keepdims=True)
        s1 = jnp.sum(jnp.where(first, 0.0, sq), axis=-1, keepdims=True)
        ms = jnp.where(first, s0, s1) * (1.0 / HEAD_DIM)
        outs.append(yc * lax.rsqrt(ms + RMS_EPS) * gain)
    return outs[0] if len(outs) == 1 else jnp.concatenate(outs, axis=1)


_NT = (((1,), (1,)), ((), ()))


def _with_ones_rows(vt):
    return jnp.concatenate([vt.astype(BF16), jnp.ones((ONES_ROWS, vt.shape[1]), BF16)], axis=0)


def _attend(segments, q, rows, group, sink=None):
    scores = []
    for k, _, bias, valid in segments:
        s = lax.dot_general(k, q, _NT, preferred_element_type=F32)
        if bias is not None:
            s = s + bias
        if valid is not None:
            s = jnp.where(valid, s, NEG_INF)
        scores.append(s)
    m = functools.reduce(jnp.maximum, [jnp.max(s, axis=0, keepdims=True) for s in scores])
    if sink is not None:
        m = jnp.maximum(m, sink)
    ot = None
    for s, (_, vt_ones, _, _) in zip(scores, segments):
        part = jnp.dot(vt_ones, jnp.exp2(s - m).astype(BF16), preferred_element_type=F32)
        ot = part if ot is None else ot + part
    denom = ot[HEAD_DIM:HEAD_DIM + 1, :]
    if sink is not None:
        denom = denom + jnp.exp2(sink - m)
    ot = ot[:HEAD_DIM, :] / denom
    return jnp.concatenate([ot[:, g * rows:(g + 1) * rows].T for g in range(group)], axis=1)


def _stack_group(ref, rows, col0, group):
    return jnp.concatenate(
        [ref[rows, col0 + g * HEAD_DIM: col0 + (g + 1) * HEAD_DIM] for g in range(group)], axis=0)


def _sink_row(sink_ref, h0, group, rows):
    return jnp.concatenate(
        [jnp.full((1, rows), sink_ref[h0 + g] * LOG2E, F32) for g in range(group)], axis=1)


def _store_feature_major(ref, y):
    nb, heads, _, seq = ref.shape
    yt = y.T
    for b in range(nb):
        for h in range(heads):
            ref[b, h] = yt[h * HEAD_DIM:(h + 1) * HEAD_DIM, b * seq:(b + 1) * seq].astype(ref.dtype)


def _pre_even_kernel(x_ref, g_ref, sh_ref, sc_ref, w_ref, *rest, is_lat):
    if is_lat:
        cos_ref, slo_ref, shi_ref, qk_ref, va_ref, vb_ref = rest
    else:
        qk_ref, ka_ref, va_ref, kb_ref, vb_ref = rest
    h = _adaln(x_ref[...], g_ref[...], sh_ref[0, 0], sc_ref[0, 0]).astype(BF16)

    def proj(c0, c1):
        return jnp.dot(h, w_ref[:, c0:c1], preferred_element_type=F32)

    na = A_HEADS * HEAD_DIM
    nb = B_HEADS * HEAD_DIM
    nkv = B_KV_HEADS * HEAD_DIM
    qk_ref[:, 0:na] = (proj(0, na) * QK_SCALE).astype(BF16)
    ka = proj(na, 2 * na)
    qk_ref[:, na:2 * na] = ka.astype(BF16)
    va = proj(2 * na, 3 * na)
    qb = proj(3 * na, 3 * na + nb)
    if is_lat:
        qb = _rope(qb, cos_ref[...], slo_ref[...], shi_ref[...])
    qk_ref[:, 2 * na:2 * na + nb] = (qb * QK_SCALE).astype(BF16)
    kb = proj(3 * na + nb, 3 * na + nb + nkv)
    if is_lat:
        kb_out = _rope(kb, cos_ref[...], slo_ref[...], shi_ref[...])
    else:
        kb_out = kb
    qk_ref[:, 2 * na + nb:2 * na + nb + nkv] = kb_out.astype(BF16)
    vb = proj(3 * na + nb + nkv, 3 * na + nb + 2 * nkv)
    _store_feature_major(va_ref, va)
    _store_feature_major(vb_ref, vb)
    if not is_lat:
        _store_feature_major(ka_ref, ka)
        _store_feature_major(kb_ref, kb)


def _pre_odd_kernel(x_ref, g_ref, sh_ref, sc_ref, w_ref, qn_ref, kn_ref, *rest, is_lat):
    if is_lat:
        cos_ref, slo_ref, shi_ref, qk_ref, vc_ref = rest
    else:
        qk_ref, kc_ref, vc_ref = rest
    h = _adaln(x_ref[...], g_ref[...], sh_ref[0, 0], sc_ref[0, 0]).astype(BF16)

    def proj(c0, c1):
        return jnp.dot(h, w_ref[:, c0:c1], preferred_element_type=F32)

    nq = C_HEADS * HEAD_DIM
    nkv = C_KV_HEADS * HEAD_DIM
    q = _head_rms_norm(proj(0, nq), qn_ref[...])
    k = _head_rms_norm(proj(nq, nq + nkv), kn_ref[...])
    v = proj(nq + nkv, nq + 2 * nkv)
    if is_lat:
        q = _rope(q, cos_ref[...], slo_ref[...], shi_ref[...])
        k_out = _rope(k, cos_ref[...], slo_ref[...], shi_ref[...])
    else:
        k_out = k
        _store_feature_major(kc_ref, k)
    _store_feature_major(vc_ref, v)
    qk_ref[:, 0:nq] = (q * QK_SCALE).astype(BF16)
    qk_ref[:, nq:nq + nkv] = k_out.astype(BF16)


def _pre_attention(x, mods, layer, gain, w, sub, *, odd, is_lat, seq, rope=None,
                   head_gains=None):
    t, d = x.shape
    tm = TOKEN_TILE
    per_seq = max(seq // tm, 1)
    per_tile = max(tm // seq, 1)
    group = (lambda i: 1 + i // per_seq) if is_lat else (lambda i: 0)
    row = lambda i: (i, 0)
    in_specs = [
        pl.BlockSpec((tm, d), row),
        _resident((1, d)),
        _mod_spec(layer, 0, group),
        _mod_spec(layer, 1, group),
        _resident_layer(w, sub),
    ]
    args = [x, gain.reshape(1, d), mods, mods, w]
    if odd:
        in_specs += [_resident((1, LANES)), _resident((1, LANES))]
        args += list(head_gains)
    if is_lat:
        in_specs += [pl.BlockSpec((tm, LANES), lambda i: (i % per_seq, 0))] * 3
        args += list(rope)
    if odd:
        n_qk = (C_HEADS + C_KV_HEADS) * HEAD_DIM
        lat_heads, ctx_heads = [C_KV_HEADS], [C_KV_HEADS, C_KV_HEADS]
    else:
        n_qk = (2 * A_HEADS + B_HEADS + B_KV_HEADS) * HEAD_DIM
        lat_heads, ctx_heads = [A_HEADS, B_KV_HEADS], [A_HEADS, A_HEADS, B_KV_HEADS, B_KV_HEADS]
    out_specs = [pl.BlockSpec((tm, n_qk), row)]
    out_shape = [jax.ShapeDtypeStruct((t, n_qk), BF16)]
    for heads in (lat_heads if is_lat else ctx_heads):
        blk = (per_tile, heads, HEAD_DIM, min(tm, seq))
        out_specs.append(pl.BlockSpec(blk, lambda i: (i // per_seq, 0, 0, i % per_seq)))
        out_shape.append(jax.ShapeDtypeStruct((t // seq, heads, HEAD_DIM, seq),
                                              BF16 if is_lat else F32))
    body = _pre_odd_kernel if odd else _pre_even_kernel
    return pl.pallas_call(
        functools.partial(body, is_lat=is_lat),
        grid=(t // tm,),
        in_specs=in_specs,
        out_specs=out_specs,
        out_shape=out_shape,
        compiler_params=_params(1),
        name=f"pre_{'odd' if odd else 'even'}_{'lat' if is_lat else 'ctx'}",
    )(*args)


def _batch_spec(arr):
    nd = arr.ndim
    return pl.BlockSpec((1,) + arr.shape[1:], lambda b: (b,) + (0,) * (nd - 1))


def _cache_spec(cache, layer):
    blk = (1, 1) + cache.shape[2:]
    return pl.BlockSpec(blk, lambda b: (b, layer, 0, 0, 0))


def _feature_major(cache):
    return cache.transpose(0, 1, 3, 4, 2)


def _cache_keys(ref, h):
    return ref[0, 0, h].T.astype(BF16)


def _ctx_even_attn_kernel(sink_ref, qk_ref, vat_ref, vbt_ref, o_ref):
    rows = slice(None)
    n = qk_ref.shape[0]
    na = A_HEADS * HEAD_DIM
    for pair in range(A_HEADS // 2):
        outs = []
        for h in (2 * pair, 2 * pair + 1):
            c = h * HEAD_DIM
            seg = (qk_ref[:, na + c:na + c + HEAD_DIM], _with_ones_rows(vat_ref[0, h]), None, None)
            outs.append(_attend([seg], qk_ref[:, c:c + HEAD_DIM], n, 1))
        o_ref[:, pair * LANES:(pair + 1) * LANES] = jnp.concatenate(outs, axis=1).astype(BF16)
    group = B_HEADS // B_KV_HEADS
    qb0 = 2 * na
    kb0 = qb0 + B_HEADS * HEAD_DIM
    for kv in range(B_KV_HEADS):
        q = _stack_group(qk_ref, rows, qb0 + kv * group * HEAD_DIM, group)
        seg = (qk_ref[:, kb0 + kv * HEAD_DIM:kb0 + (kv + 1) * HEAD_DIM],
               _with_ones_rows(vbt_ref[0, kv]), None, None)
        o = _attend([seg], q, n, group, _sink_row(sink_ref, kv * group, group, n))
        c0 = na + kv * group * HEAD_DIM
        o_ref[:, c0:c0 + group * HEAD_DIM] = o.astype(BF16)


def _ctx_even_attention(qk, vat, vbt, sink, seq):
    t, n = qk.shape
    width = (A_HEADS + B_HEADS) * HEAD_DIM
    return pl.pallas_call(
        _ctx_even_attn_kernel,
        grid=(t // seq,),
        in_specs=[pl.BlockSpec(memory_space=pltpu.SMEM),
                  pl.BlockSpec((seq, n), lambda b: (b, 0)),
                  _batch_spec(vat), _batch_spec(vbt)],
        out_specs=pl.BlockSpec((seq, width), lambda b: (b, 0)),
        out_shape=jax.ShapeDtypeStruct((t, width), BF16),
        compiler_params=_params(1),
        name="attn_even_ctx",
    )(sink, qk, vat, vbt)


def _na_key_window(j):
    return min(max(Q_TILE * j - Q_TILE, 0), Q_TILE)


def _na_bias_tiles(src_ref, h):
    kc = lax.broadcasted_iota(jnp.int32, (GRID_W, LANES), 0)
    qc = lax.rem(lax.broadcasted_iota(jnp.int32, (GRID_W, LANES), 1), GRID_W)
    cs = jnp.clip(qc - NA_WIN_W // 2, 0, GRID_W - NA_WIN_W)
    col_ok = (kc >= cs) & (kc < cs + NA_WIN_W)
    tiles = []
    for i in range(2 * NA_WIN_H):
        src = jnp.broadcast_to(src_ref[h, i:i + 1, :], (GRID_W, LANES))
        tiles.append(jnp.where(col_ok, pltpu.roll(src, 0, 1, stride=1, stride_axis=0), NEG_INF))
    return tiles


def _na_bias_block(tiles, j, grid_rows):
    wh = min(NA_WIN_H, grid_rows)
    first_row = lax.broadcasted_iota(jnp.int32, (GRID_W, LANES), 1) < GRID_W
    lo_r = _na_key_window(j) // GRID_W
    q_rows = range(j * Q_TILE // GRID_W, (j + 1) * Q_TILE // GRID_W, 2)

    def row_start(qr):
        return min(max(qr - wh // 2, 0), grid_rows - wh)

    strips = []
    for kr in range(lo_r, lo_r + 3 * Q_TILE // GRID_W):
        parts = []
        for qr in q_rows:
            ok0 = row_start(qr) <= kr < row_start(qr) + wh
            ok1 = row_start(qr + 1) <= kr < row_start(qr + 1) + wh
            i = kr - qr + NA_WIN_H - 1
            if ok0 and ok1:
                parts.append(tiles[i])
            elif ok0:
                parts.append(jnp.where(first_row, tiles[i], NEG_INF))
            elif ok1:
                parts.append(jnp.where(first_row, NEG_INF, tiles[i]))
            else:
                parts.append(jnp.full((GRID_W, LANES), NEG_INF, F32))
        strips.append(jnp.concatenate(parts, axis=1))
    return jnp.concatenate(strips, axis=0)


def _lat_a_kernel(src_ref, q_ref, k_ref, vt_ref, ck_ref, cv_ref, o_ref):
    n = q_ref.shape[0]
    win = 3 * Q_TILE
    for pair in range(A_HEADS // 2):
        outs = [[] for _ in range(n // Q_TILE)]
        for h in (2 * pair, 2 * pair + 1):
            cols = slice(h * HEAD_DIM, (h + 1) * HEAD_DIM)
            ctx = (_cache_keys(ck_ref, h), _with_ones_rows(cv_ref[0, 0, h]), None, None)
            tiles = _na_bias_tiles(src_ref, h)
            for j in range(n // Q_TILE):
                lo = _na_key_window(j)
                local = (k_ref[lo:lo + win, cols], _with_ones_rows(vt_ref[0, h, :, lo:lo + win]),
                         _na_bias_block(tiles, j, n // GRID_W), None)
                outs[j].append(
                    _attend([local, ctx], q_ref[j * Q_TILE:(j + 1) * Q_TILE, cols], Q_TILE, 1))
        for j in range(n // Q_TILE):
            o_ref[j * Q_TILE:(j + 1) * Q_TILE, pair * LANES:(pair + 1) * LANES] = (
                jnp.concatenate(outs[j], axis=1).astype(BF16))


def _lat_a_attention(qk, vt, cache_k, cache_v, e, bias_src, seq):
    t = qk.shape[0]
    na = A_HEADS * HEAD_DIM
    return pl.pallas_call(
        _lat_a_kernel,
        grid=(t // seq,),
        in_specs=[
            _resident(bias_src.shape),
            pl.BlockSpec((seq, na), lambda b: (b, 0)),
            pl.BlockSpec((seq, na), lambda b: (b, 1)),
            _batch_spec(vt),
            _cache_spec(cache_k, e),
            _cache_spec(cache_v, e),
        ],
        out_specs=pl.BlockSpec((seq, na), lambda b: (b, 0)),
        out_shape=jax.ShapeDtypeStruct((t, na), BF16),
        compiler_params=_params(1),
        name="attn_even_lat_a",
    )(bias_src, qk, qk, vt, cache_k, cache_v)


def _lat_b_kernel(sink_ref, q_ref, k_ref, vt_ref, ck_ref, cv_ref, o_ref):
    n = q_ref.shape[0]
    group = B_HEADS // B_KV_HEADS
    win = 2 * Q_TILE
    ctx = [(_cache_keys(ck_ref, kv), _with_ones_rows(cv_ref[0, 0, kv]), None, None)
           for kv in range(B_KV_HEADS)]
    for j in range(n // Q_TILE):
        lo = min(max(Q_TILE * j - B_WINDOW, 0), n - win)
        q_rows = slice(j * Q_TILE, (j + 1) * Q_TILE)
        kpos = lo + lax.broadcasted_iota(jnp.int32, (win, 1), 0)
        qpos = j * Q_TILE + lax.broadcasted_iota(jnp.int32, (1, group * Q_TILE), 1) % Q_TILE
        valid = jnp.abs(qpos - kpos) <= B_WINDOW
        for kv in range(B_KV_HEADS):
            q = _stack_group(q_ref, q_rows, kv * group * HEAD_DIM, group)
            local = (k_ref[lo:lo + win, kv * HEAD_DIM:(kv + 1) * HEAD_DIM],
                     _with_ones_rows(vt_ref[0, kv, :, lo:lo + win]), None, valid)
            o = _attend([local, ctx[kv]], q, Q_TILE, group,
                        _sink_row(sink_ref, kv * group, group, Q_TILE))
            c0 = kv * group * HEAD_DIM
            o_ref[q_rows, c0:c0 + group * HEAD_DIM] = o.astype(BF16)


def _lat_b_attention(qk, vt, cache_k, cache_v, e, sink, seq):
    t = qk.shape[0]
    na = A_HEADS * HEAD_DIM
    nb = B_HEADS * HEAD_DIM
    nkv = B_KV_HEADS * HEAD_DIM
    return pl.pallas_call(
        _lat_b_kernel,
        grid=(t // seq,),
        in_specs=[
            pl.BlockSpec(memory_space=pltpu.SMEM),
            pl.BlockSpec((seq, nb), lambda b: (b, 2 * na // nb)),
            pl.BlockSpec((seq, nkv), lambda b: (b, (2 * na + nb) // nkv)),
            _batch_spec(vt),
            _cache_spec(cache_k, e),
            _cache_spec(cache_v, e),
        ],
        out_specs=pl.BlockSpec((seq, nb), lambda b: (b, 0)),
        out_shape=jax.ShapeDtypeStruct((t, nb), BF16),
        compiler_params=_params(1),
        name="attn_even_lat_b",
    )(sink, qk, qk, vt, cache_k, cache_v)


def _ctx_odd_attn_kernel(qk_ref, vt_ref, o_ref):
    n = qk_ref.shape[0]
    group = C_HEADS // C_KV_HEADS
    nq = C_HEADS * HEAD_DIM
    for kv in range(C_KV_HEADS):
        q = _stack_group(qk_ref, slice(None), kv * group * HEAD_DIM, group)
        seg = (qk_ref[:, nq + kv * HEAD_DIM:nq + (kv + 1) * HEAD_DIM],
               _with_ones_rows(vt_ref[0, kv]), None, None)
        c0 = kv * group * HEAD_DIM
        o_ref[:, c0:c0 + group * HEAD_DIM] = _attend([seg], q, n, group).astype(BF16)


def _ctx_odd_attention(qk, vt, seq):
    t, n = qk.shape
    width = C_HEADS * HEAD_DIM
    return pl.pallas_call(
        _ctx_odd_attn_kernel,
        grid=(t // seq,),
        in_specs=[pl.BlockSpec((seq, n), lambda b: (b, 0)), _batch_spec(vt)],
        out_specs=pl.BlockSpec((seq, width), lambda b: (b, 0)),
        out_shape=jax.ShapeDtypeStruct((t, width), BF16),
        compiler_params=_params(1),
        name="attn_odd_ctx",
    )(qk, vt)


def _lat_c_kernel(qk_ref, vt_ref, ck_ref, cv_ref, o_ref, kall_ref, vtall_ref):
    n = qk_ref.shape[0]
    past = ck_ref.shape[4]
    group = C_HEADS // C_KV_HEADS
    nq = C_HEADS * HEAD_DIM
    for kv in range(C_KV_HEADS):
        kall_ref[kv, 0:past, :] = _cache_keys(ck_ref, kv)
        kall_ref[kv, past:past + n, :] = qk_ref[:, nq + kv * HEAD_DIM:nq + (kv + 1) * HEAD_DIM]
        vtall_ref[kv, 0:HEAD_DIM, 0:past] = cv_ref[0, 0, kv].astype(BF16)
        vtall_ref[kv, 0:HEAD_DIM, past:past + n] = vt_ref[0, kv]
        vtall_ref[kv, HEAD_DIM:, :] = jnp.ones((ONES_ROWS, past + n), BF16)

    def q_block(j, carry):
        q_rows = pl.ds(pl.multiple_of(j * Q_TILE, Q_TILE), Q_TILE)
        for kv in range(C_KV_HEADS):
            q = _stack_group(qk_ref, q_rows, kv * group * HEAD_DIM, group)
            o = _attend([(kall_ref[kv], vtall_ref[kv], None, None)], q, Q_TILE, group)
            c0 = kv * group * HEAD_DIM
            o_ref[q_rows, c0:c0 + group * HEAD_DIM] = o.astype(BF16)
        return carry

    lax.fori_loop(0, n // Q_TILE, q_block, 0)


def _lat_c_attention(qk, vt, cache_k, cache_v, o, seq):
    t, n = qk.shape
    past = cache_k.shape[4]
    width = C_HEADS * HEAD_DIM
    return pl.pallas_call(
        _lat_c_kernel,
        grid=(t // seq,),
        in_specs=[
            pl.BlockSpec((seq, n), lambda b: (b, 0)),
            _batch_spec(vt),
            _cache_spec(cache_k, o),
            _cache_spec(cache_v, o),
        ],
        out_specs=pl.BlockSpec((seq, width), lambda b: (b, 0)),
        out_shape=jax.ShapeDtypeStruct((t, width), BF16),
        scratch_shapes=[pltpu.VMEM((C_KV_HEADS, past + seq, HEAD_DIM), BF16),
                        pltpu.VMEM((C_KV_HEADS, HEAD_DIM + ONES_ROWS, past + seq), BF16)],
        compiler_params=_params(1),
        name="attn_odd_lat",
    )(qk, vt, cache_k, cache_v)


def _post_kernel(*refs, n_parts, final):
    o_refs = refs[:n_parts]
    (x_ref, wo_ref, g1_ref, sh_ref, sc_ref, g2_ref, gain_ref, wgu_ref, wd_ref) = refs[n_parts:n_parts + 9]
    rest = refs[n_parts + 9:]
    if final:
        fg_ref, out_ref, act_ref = rest
    else:
        out_ref, act_ref = rest
    mix = None
    r0 = 0
    for o_ref in o_refs:
        kk = o_ref.shape[1]
        part = jnp.dot(o_ref[...], wo_ref[r0:r0 + kk, :], preferred_element_type=F32)
        mix = part if mix is None else mix + part
        r0 += kk
    x1 = x_ref[...] + g1_ref[0, 0] * mix
    h = _adaln(x1, gain_ref[...], sh_ref[0, 0], sc_ref[0, 0]).astype(BF16)
    d_ff = wd_ref.shape[0]
    for j in range(d_ff // FF_CHUNK):
        c0 = j * FF_CHUNK
        gate = jnp.dot(h, wgu_ref[:, c0:c0 + FF_CHUNK], preferred_element_type=F32)
        up = jnp.dot(h, wgu_ref[:, d_ff + c0:d_ff + c0 + FF_CHUNK], preferred_element_type=F32)
        act_ref[:, c0:c0 + FF_CHUNK] = (gate * jax.nn.sigmoid(gate) * up).astype(BF16)
    ffn = jnp.dot(act_ref[...], wd_ref[...], preferred_element_type=F32)
    x2 = x1 + g2_ref[0, 0] * ffn
    if final:
        ms = jnp.mean(x2 * x2, axis=-1, keepdims=True)
        x2 = (x2 * lax.rsqrt(ms + RMS_EPS)) * fg_ref[...]
    out_ref[...] = x2


def _post_attention(o_parts, x, mods, layer, gain, w_out, sub, w_gu, w_down, *, is_lat, seq,
                    final_gain=None):
    t, d = x.shape
    tm = TOKEN_TILE
    per_seq = max(seq // tm, 1)
    group = (lambda i: 1 + i // per_seq) if is_lat else (lambda i: 0)
    row = lambda i: (i, 0)
    d_ff = w_down.shape[1]
    final = final_gain is not None
    in_specs = [pl.BlockSpec((tm, o.shape[1]), row) for o in o_parts]
    in_specs += [
        pl.BlockSpec((tm, d), row),
        _resident_layer(w_out, sub),
        _mod_spec(layer, 2, group),
        _mod_spec(layer, 3, group),
        _mod_spec(layer, 4, group),
        _mod_spec(layer, 5, group),
        _resident((1, d)),
        _resident_layer(w_gu, layer),
        _resident_layer(w_down, layer),
    ]
    args = list(o_parts) + [x, w_out, mods, mods, mods, mods, gain.reshape(1, d), w_gu, w_down]
    if final:
        in_specs.append(_resident((1, d)))
        args.append(final_gain.reshape(1, d))
    return pl.pallas_call(
        functools.partial(_post_kernel, n_parts=len(o_parts), final=final),
        grid=(t // tm,),
        in_specs=in_specs,
        out_specs=pl.BlockSpec((tm, d), row),
        out_shape=jax.ShapeDtypeStruct((t, d), F32),
        scratch_shapes=[pltpu.VMEM((tm, d_ff), BF16)],
        compiler_params=_params(1),
        name=f"post_{'lat' if is_lat else 'ctx'}{'_final' if final else ''}",
    )(*args)


def _rope_tables(n):
    t = jnp.arange(n)
    row = (t // GRID_W).astype(F32)
    col = (t % GRID_W).astype(F32)
    half = HEAD_DIM // 2
    inv_freq = ROPE_THETA ** (-jnp.arange(0, half, 2, dtype=F32) / half)
    lane = jnp.arange(LANES)
    in_head = lane % HEAD_DIM
    pos = jnp.where((in_head < half)[None, :], row[:, None], col[:, None])
    ang = pos * inv_freq[in_head % (half // 2)][None, :]
    first = ((in_head % half) < half // 2)[None, :]
    cos = jnp.cos(ang)
    sin = jnp.sin(ang)
    return cos, jnp.where(first, -sin, 0.0), jnp.where(first, 0.0, sin)


def _na_bias_sources(rpb):
    h, _, nb = rpb.shape
    w = NA_WIN_W - 1
    rp = jnp.pad(rpb[:, :, ::-1] * LOG2E, ((0, 0), (1, 1), (0, 0)))
    this, prev = rp[:, 1:], rp[:, :-1]
    z = jnp.zeros((h, 2 * NA_WIN_H, LANES // 2 - nb), F32)
    return jnp.concatenate([this[:, :, w:], z, prev, z, this[:, :, :w]], axis=-1)


def _state(y):
    return y.transpose(0, 3, 1, 2)


def kernel(x_prompt, x_sample, cache_a_k, cache_a_v, cache_b_k, cache_b_v, cache_c_k, cache_c_v,
           c, c_ctx, norm_gain, w_mod, b_mod, w_in_even, w_out_even, rpb_a, sink_b,
           w_in_odd, w_out_odd, q_norm_c, k_norm_c, w_gate_up, w_down, final_gain):
    batch, seq, d = x_prompt.shape
    dec_batch, dec_seq, _ = x_sample.shape
    depth = w_mod.shape[0]

    cvec = jnp.concatenate(
        [c_ctx[None, :], c, jnp.zeros((MOD_GROUPS - 1 - dec_batch, d), F32)], axis=0)
    mods = _modulation(cvec, w_mod, b_mod).reshape(depth, MOD_GROUPS, 1, 6 * d)
    rope = _rope_tables(dec_seq)
    w_in = {False: w_in_even.astype(BF16), True: w_in_odd.astype(BF16)}
    w_out = {False: w_out_even.astype(BF16), True: w_out_odd.astype(BF16)}
    w_gu = w_gate_up.astype(BF16)
    w_dn = w_down.astype(BF16)

    ctx = x_prompt.reshape(batch * seq, d)
    lat = x_sample.reshape(dec_batch * dec_seq, d)
    states = {name: [] for name in ("a_k", "a_v", "b_k", "b_v", "c_k", "c_v")}

    for layer in range(depth):
        odd = layer % 2 == 1
        sub = layer // 2
        gain1, gain2 = norm_gain[layer, 0], norm_gain[layer, 1]
        pre = functools.partial(_pre_attention, mods=mods, layer=layer, gain=gain1, w=w_in[odd],
                                sub=sub, odd=odd)
        if not odd:
            qk_c, ka, va, kb, vb = pre(ctx, is_lat=False, seq=seq)
            qk_l, vat_l, vbt_l = pre(lat, is_lat=True, seq=dec_seq, rope=rope)
            for name, y in (("a_k", ka), ("a_v", va), ("b_k", kb), ("b_v", vb)):
                states[name].append(_state(y))
            o_ctx = [_ctx_even_attention(qk_c, va, vb, sink_b[sub], seq)]
            o_lat = [
                _lat_a_attention(qk_l, vat_l, _feature_major(cache_a_k), _feature_major(cache_a_v),
                                 sub, _na_bias_sources(rpb_a[sub]), dec_seq),
                _lat_b_attention(qk_l, vbt_l, _feature_major(cache_b_k), _feature_major(cache_b_v),
                                 sub, sink_b[sub], dec_seq),
            ]
        else:
            head_gains = (jnp.tile(q_norm_c[sub], LANES // HEAD_DIM).reshape(1, LANES),
                          jnp.tile(k_norm_c[sub], LANES // HEAD_DIM).reshape(1, LANES))
            qk_c, kc, vc = pre(ctx, is_lat=False, seq=seq, head_gains=head_gains)
            qk_l, vct_l = pre(lat, is_lat=True, seq=dec_seq, rope=rope, head_gains=head_gains)
            states["c_k"].append(_state(kc))
            states["c_v"].append(_state(vc))
            o_ctx = [_ctx_odd_attention(qk_c, vc, seq)]
            o_lat = [_lat_c_attention(qk_l, vct_l, _feature_major(cache_c_k),
                                      _feature_major(cache_c_v), sub, dec_seq)]
        fg = final_gain if layer == depth - 1 else None
        post = functools.partial(_post_attention, mods=mods, layer=layer, gain=gain2,
                                 w_out=w_out[odd], sub=sub, w_gu=w_gu, w_down=w_dn, final_gain=fg)
        ctx = post(o_ctx, ctx, is_lat=False, seq=seq)
        lat = post(o_lat, lat, is_lat=True, seq=dec_seq)

    return (ctx.reshape(batch, seq, d), lat.reshape(dec_batch, dec_seq, d),
            jnp.stack(states["a_k"], axis=1), jnp.stack(states["a_v"], axis=1),
            jnp.stack(states["b_k"], axis=1), jnp.stack(states["b_v"], axis=1),
            jnp.stack(states["c_k"], axis=1), jnp.stack(states["c_v"], axis=1))
```

```python
import functools
import math

import jax
import jax.numpy as jnp
import numpy as np
from jax import lax
from jax.experimental import pallas as pl
from jax.experimental.pallas import tpu as pltpu

F32 = jnp.float32
BF16 = jnp.bfloat16

D_MODEL = 1024
GRID_W = 64
HEAD_DIM = 64
A_HEADS = 8
B_HEADS = 8
B_KV_HEADS = 2
C_HEADS = 16
C_KV_HEADS = 4
NA_WIN_H = 8
NA_WIN_W = 16
B_WINDOW = 128
ROPE_THETA = 10000.0
RMS_EPS = 1e-6
NEG_INF = -1e30
LOG2E = math.log2(math.e)
QK_SCALE = LOG2E / math.sqrt(HEAD_DIM)

LANES = 128
TOKEN_TILE = 512
Q_TILE = 256
ONES_ROWS = 16
KEY_CHUNK = 256
FF_CHUNK = 256
MOD_GROUPS = 16
VMEM_LIMIT = 56 * 1024 * 1024


def _params(n_axes, vmem=VMEM_LIMIT):
    return pltpu.CompilerParams(
        dimension_semantics=("arbitrary",) * n_axes, vmem_limit_bytes=vmem)


def _resident(shape):
    nd = len(shape)
    return pl.BlockSpec(shape, lambda *_: (0,) * nd, pipeline_mode=pl.Buffered(1))


def _resident_layer(stacked, layer):
    return pl.BlockSpec((None,) + stacked.shape[1:], lambda *_: (layer, 0, 0),
                        pipeline_mode=pl.Buffered(1))


def _mod_kernel(c_ref, w_ref, b_ref, o_ref):
    c = c_ref[...]
    s = (c * jax.nn.sigmoid(c)).astype(BF16)
    o_ref[0] = jnp.dot(s, w_ref[0].astype(BF16), preferred_element_type=F32) + b_ref[0]


def _modulation(cvec, w_mod, b_mod):
    depth, d, n = w_mod.shape
    tn = 1536
    return pl.pallas_call(
        _mod_kernel,
        grid=(depth, n // tn),
        in_specs=[
            pl.BlockSpec((MOD_GROUPS, d), lambda l, j: (0, 0)),
            pl.BlockSpec((1, d, tn), lambda l, j: (l, 0, j)),
            pl.BlockSpec((1, 1, tn), lambda l, j: (l, 0, j)),
        ],
        out_specs=pl.BlockSpec((1, MOD_GROUPS, tn), lambda l, j: (l, 0, j)),
        out_shape=jax.ShapeDtypeStruct((depth, MOD_GROUPS, n), F32),
        compiler_params=_params(2),
        name="modulation",
    )(cvec, w_mod, b_mod.reshape(depth, 1, n))


def _mod_spec(layer, which, group_of_step):
    return pl.BlockSpec((1, 1, 1, D_MODEL), lambda i: (layer, group_of_step(i), 0, which))


def _adaln(x, gain, shift, scale):
    ms = jnp.mean(x * x, axis=-1, keepdims=True)
    return (x * lax.rsqrt(ms + RMS_EPS)) * gain * (1.0 + scale) + shift


def _rope(y, cos, sin_lo, sin_hi):
    outs = []
    for c in range(y.shape[1] // LANES):
        yc = y[:, c * LANES:(c + 1) * LANES]
        outs.append(yc * cos
                    + pltpu.roll(yc, LANES - 16, 1) * sin_lo
                    + pltpu.roll(yc, 16, 1) * sin_hi)
    return outs[0] if len(outs) == 1 else jnp.concatenate(outs, axis=1)


def _head_rms_norm(y, gain):
    first = lax.broadcasted_iota(jnp.int32, (1, LANES), 1) < HEAD_DIM
    outs = []
    for c in range(y.shape[1] // LANES):
        yc = y[:, c * LANES:(c + 1) * LANES]
        sq = yc * yc
        s0 = jnp.sum(jnp.where(first, sq, 0.0), axis=-1, keepdims=True)
        s1 = jnp.sum(jnp.where(first, 0.0, sq), axis=-1, keepdims=True)
        ms = jnp.where(first, s0, s1) * (1.0 / HEAD_DIM)
        outs.append(yc * lax.rsqrt(ms + RMS_EPS) * gain)
    return outs[0] if len(outs) == 1 else jnp.concatenate(outs, axis=1)


_NT = (((1,), (1,)), ((), ()))


def _with_ones_rows(vt):
    return jnp.concatenate([vt.astype(BF16), jnp.ones((ONES_ROWS, vt.shape[1]), BF16)], axis=0)


def _attend(segments, q, rows, group, sink=None):
    scores = []
    for k, _, bias, valid in segments:
        s = _scores(k, q)
        if bias is not None:
            s = s + bias
        if valid is not None:
            s = jnp.where(valid, s, NEG_INF)
        scores.append(s)
    return _softmax_values(scores, [seg[1] for seg in segments], rows, group, sink)


def _scores(k, q):
    return lax.dot_general(k, q, _NT, preferred_element_type=F32)


def _softmax_values(scores, vts, rows, group, sink=None):
    m = functools.reduce(jnp.maximum, [jnp.max(s, axis=0, keepdims=True) for s in scores])
    if sink is not None:
        m = jnp.maximum(m, sink)
    ot = None
    for s, vt_ones in zip(scores, vts):
        part = jnp.dot(vt_ones, jnp.exp2(s - m).astype(BF16), preferred_element_type=F32)
        ot = part if ot is None else ot + part
    denom = ot[HEAD_DIM:HEAD_DIM + 1, :]
    if sink is not None:
        denom = denom + jnp.exp2(sink - m)
    ot = ot[:HEAD_DIM, :] / denom
    return jnp.concatenate([ot[:, g * rows:(g + 1) * rows].T for g in range(group)], axis=1)


def _stack_group(ref, rows, col0, group):
    return jnp.concatenate(
        [ref[rows, col0 + g * HEAD_DIM: col0 + (g + 1) * HEAD_DIM] for g in range(group)], axis=0)


def _sink_row(sink_ref, h0, group, rows):
    return jnp.concatenate(
        [jnp.full((1, rows), sink_ref[h0 + g] * LOG2E, F32) for g in range(group)], axis=1)


def _store_feature_major(ref, y):
    nb, heads, _, seq = ref.shape
    yt = y.T
    for b in range(nb):
        for h in range(heads):
            ref[b, h] = yt[h * HEAD_DIM:(h + 1) * HEAD_DIM, b * seq:(b + 1) * seq].astype(ref.dtype)


def _pre_even_kernel(x_ref, g_ref, sh_ref, sc_ref, w_ref, *rest, is_lat):
    if is_lat:
        cos_ref, slo_ref, shi_ref, qk_ref, va_ref, vb_ref = rest
    else:
        qk_ref, ka_ref, va_ref, kb_ref, vb_ref = rest
    h = _adaln(x_ref[...], g_ref[...], sh_ref[0, 0], sc_ref[0, 0]).astype(BF16)

    def proj(c0, c1):
        return jnp.dot(h, w_ref[:, c0:c1], preferred_element_type=F32)

    na = A_HEADS * HEAD_DIM
    nb = B_HEADS * HEAD_DIM
    nkv = B_KV_HEADS * HEAD_DIM
    qk_ref[:, 0:na] = (proj(0, na) * QK_SCALE).astype(BF16)
    ka = proj(na, 2 * na)
    qk_ref[:, na:2 * na] = ka.astype(BF16)
    va = proj(2 * na, 3 * na)
    qb = proj(3 * na, 3 * na + nb)
    if is_lat:
        qb = _rope(qb, cos_ref[...], slo_ref[...], shi_ref[...])
    qk_ref[:, 2 * na:2 * na + nb] = (qb * QK_SCALE).astype(BF16)
    kvb = proj(3 * na + nb, 3 * na + nb + 2 * nkv)
    kb, vb = kvb[:, :nkv], kvb[:, nkv:]
    if is_lat:
        kb_out = _rope(kb, cos_ref[...], slo_ref[...], shi_ref[...])
    else:
        kb_out = kb
    qk_ref[:, 2 * na + nb:2 * na + nb + nkv] = kb_out.astype(BF16)
    _store_feature_major(va_ref, va)
    _store_feature_major(vb_ref, vb)
    if not is_lat:
        _store_feature_major(ka_ref, ka)
        _store_feature_major(kb_ref, kb)


def _pre_odd_kernel(x_ref, g_ref, sh_ref, sc_ref, w_ref, qn_ref, kn_ref, *rest, is_lat):
    if is_lat:
        cos_ref, slo_ref, shi_ref, qk_ref, vc_ref = rest
    else:
        qk_ref, kc_ref, vc_ref = rest
    h = _adaln(x_ref[...], g_ref[...], sh_ref[0, 0], sc_ref[0, 0]).astype(BF16)

    def proj(c0, c1):
        return jnp.dot(h, w_ref[:, c0:c1], preferred_element_type=F32)

    nq = C_HEADS * HEAD_DIM
    nkv = C_KV_HEADS * HEAD_DIM
    q = _head_rms_norm(proj(0, nq), qn_ref[...])
    k = _head_rms_norm(proj(nq, nq + nkv), kn_ref[...])
    v = proj(nq + nkv, nq + 2 * nkv)
    if is_lat:
        q = _rope(q, cos_ref[...], slo_ref[...], shi_ref[...])
        k_out = _rope(k, cos_ref[...], slo_ref[...], shi_ref[...])
    else:
        k_out = k
        _store_feature_major(kc_ref, k)
    _store_feature_major(vc_ref, v)
    qk_ref[:, 0:nq] = (q * QK_SCALE).astype(BF16)
    qk_ref[:, nq:nq + nkv] = k_out.astype(BF16)


def _pre_attention(x, mods, layer, gain, w, sub, *, odd, is_lat, seq, rope=None,
                   head_gains=None):
    t, d = x.shape
    tm = TOKEN_TILE
    per_seq = max(seq // tm, 1)
    per_tile = max(tm // seq, 1)
    group = (lambda i: 1 + i // per_seq) if is_lat else (lambda i: 0)
    row = lambda i: (i, 0)
    in_specs = [
        pl.BlockSpec((tm, d), row),
        _resident((1, d)),
        _mod_spec(layer, 0, group),
        _mod_spec(layer, 1, group),
        _resident_layer(w, sub),
    ]
    args = [x, gain.reshape(1, d), mods, mods, w]
    if odd:
        in_specs += [_resident(a.shape) for a in head_gains]
        args += list(head_gains)
    if is_lat:
        in_specs += [pl.BlockSpec((tm, LANES), lambda i: (i % per_seq, 0))] * 3
        args += list(rope)
    if odd:
        n_qk = (C_HEADS + C_KV_HEADS) * HEAD_DIM
        lat_heads, ctx_heads = [C_KV_HEADS], [C_KV_HEADS, C_KV_HEADS]
    else:
        n_qk = (2 * A_HEADS + B_HEADS + B_KV_HEADS) * HEAD_DIM
        lat_heads, ctx_heads = [A_HEADS, B_KV_HEADS], [A_HEADS, A_HEADS, B_KV_HEADS, B_KV_HEADS]
    out_specs = [pl.BlockSpec((tm, n_qk), row)]
    out_shape = [jax.ShapeDtypeStruct((t, n_qk), BF16)]
    for heads in (lat_heads if is_lat else ctx_heads):
        blk = (per_tile, heads, HEAD_DIM, min(tm, seq))
        out_specs.append(pl.BlockSpec(blk, lambda i: (i // per_seq, 0, 0, i % per_seq)))
        out_shape.append(jax.ShapeDtypeStruct((t // seq, heads, HEAD_DIM, seq),
                                              BF16 if is_lat else F32))
    body = _pre_odd_kernel if odd else _pre_even_kernel
    return pl.pallas_call(
        functools.partial(body, is_lat=is_lat),
        grid=(t // tm,),
        in_specs=in_specs,
        out_specs=out_specs,
        out_shape=out_shape,
        compiler_params=_params(1),
        name=f"pre_{'odd' if odd else 'even'}_{'lat' if is_lat else 'ctx'}",
    )(*args)


def _batch_spec(arr):
    nd = arr.ndim
    return pl.BlockSpec((1,) + arr.shape[1:], lambda b: (b,) + (0,) * (nd - 1))


def _cache_spec(cache, layer):
    blk = (1, 1) + cache.shape[2:]
    return pl.BlockSpec(blk, lambda b: (b, layer, 0, 0, 0))


def _feature_major(cache):
    return cache.transpose(0, 1, 3, 4, 2)


def _cache_keys(ref, h):
    return ref[0, 0, h].T.astype(BF16)


def _ctx_even_attn_kernel(sink_ref, qk_ref, vat_ref, vbt_ref, o_ref):
    rows = slice(None)
    n = qk_ref.shape[0]
    na = A_HEADS * HEAD_DIM
    for pair in range(A_HEADS // 2):
        outs = []
        for h in (2 * pair, 2 * pair + 1):
            c = h * HEAD_DIM
            seg = (qk_ref[:, na + c:na + c + HEAD_DIM], _with_ones_rows(vat_ref[0, h]), None, None)
            outs.append(_attend([seg], qk_ref[:, c:c + HEAD_DIM], n, 1))
        o_ref[:, pair * LANES:(pair + 1) * LANES] = jnp.concatenate(outs, axis=1).astype(BF16)
    group = B_HEADS // B_KV_HEADS
    qb0 = 2 * na
    kb0 = qb0 + B_HEADS * HEAD_DIM
    for kv in range(B_KV_HEADS):
        q = _stack_group(qk_ref, rows, qb0 + kv * group * HEAD_DIM, group)
        seg = (qk_ref[:, kb0 + kv * HEAD_DIM:kb0 + (kv + 1) * HEAD_DIM],
               _with_ones_rows(vbt_ref[0, kv]), None, None)
        o = _attend([seg], q, n, group, _sink_row(sink_ref, kv * group, group, n))
        c0 = na + kv * group * HEAD_DIM
        o_ref[:, c0:c0 + group * HEAD_DIM] = o.astype(BF16)


def _ctx_even_attention(qk, vat, vbt, sink, seq):
    t, n = qk.shape
    width = (A_HEADS + B_HEADS) * HEAD_DIM
    return pl.pallas_call(
        _ctx_even_attn_kernel,
        grid=(t // seq,),
        in_specs=[pl.BlockSpec(memory_space=pltpu.SMEM),
                  pl.BlockSpec((seq, n), lambda b: (b, 0)),
                  _batch_spec(vat), _batch_spec(vbt)],
        out_specs=pl.BlockSpec((seq, width), lambda b: (b, 0)),
        out_shape=jax.ShapeDtypeStruct((t, width), BF16),
        compiler_params=_params(1),
        name="attn_even_ctx",
    )(sink, qk, vat, vbt)


def _na_key_window(j):
    return min(max(Q_TILE * j - Q_TILE, 0), Q_TILE)


def _na_bias_tiles(src_ref, h):
    kc = lax.broadcasted_iota(jnp.int32, (GRID_W, LANES), 0)
    qc = lax.rem(lax.broadcasted_iota(jnp.int32, (GRID_W, LANES), 1), GRID_W)
    cs = jnp.clip(qc - NA_WIN_W // 2, 0, GRID_W - NA_WIN_W)
    col_ok = (kc >= cs) & (kc < cs + NA_WIN_W)
    tiles = []
    for i in range(2 * NA_WIN_H):
        src = jnp.broadcast_to(src_ref[h, i:i + 1, :], (GRID_W, LANES))
        tiles.append(jnp.where(col_ok, pltpu.roll(src, 0, 1, stride=1, stride_axis=0), NEG_INF))
    return tiles


def _na_bias_block(tiles, j, grid_rows):
    wh = min(NA_WIN_H, grid_rows)
    first_row = lax.broadcasted_iota(jnp.int32, (GRID_W, LANES), 1) < GRID_W
    lo_r = _na_key_window(j) // GRID_W
    q_rows = range(j * Q_TILE // GRID_W, (j + 1) * Q_TILE // GRID_W, 2)

    def row_start(qr):
        return min(max(qr - wh // 2, 0), grid_rows - wh)

    strips = []
    for kr in range(lo_r, lo_r + 3 * Q_TILE // GRID_W):
        parts = []
        for qr in q_rows:
            ok0 = row_start(qr) <= kr < row_start(qr) + wh
            ok1 = row_start(qr + 1) <= kr < row_start(qr + 1) + wh
            i = kr - qr + NA_WIN_H - 1
            if ok0 and ok1:
                parts.append(tiles[i])
            elif ok0:
                parts.append(jnp.where(first_row, tiles[i], NEG_INF))
            elif ok1:
                parts.append(jnp.where(first_row, NEG_INF, tiles[i]))
            else:
                parts.append(jnp.full((GRID_W, LANES), NEG_INF, F32))
        strips.append(jnp.concatenate(parts, axis=1))
    return jnp.concatenate(strips, axis=0)


def _lat_a_kernel(src_ref, q_ref, k_ref, vt_ref, ck_ref, cv_ref, o_ref):
    n = q_ref.shape[0]
    win = 3 * Q_TILE
    for pair in range(A_HEADS // 2):
        outs = [[] for _ in range(n // Q_TILE)]
        for h in (2 * pair, 2 * pair + 1):
            cols = slice(h * HEAD_DIM, (h + 1) * HEAD_DIM)
            ctx = (_cache_keys(ck_ref, h), _with_ones_rows(cv_ref[0, 0, h]), None, None)
            tiles = _na_bias_tiles(src_ref, h)
            for j in range(n // Q_TILE):
                lo = _na_key_window(j)
                local = (k_ref[lo:lo + win, cols], _with_ones_rows(vt_ref[0, h, :, lo:lo + win]),
                         _na_bias_block(tiles, j, n // GRID_W), None)
                outs[j].append(
                    _attend([local, ctx], q_ref[j * Q_TILE:(j + 1) * Q_TILE, cols], Q_TILE, 1))
        for j in range(n // Q_TILE):
            o_ref[j * Q_TILE:(j + 1) * Q_TILE, pair * LANES:(pair + 1) * LANES] = (
                jnp.concatenate(outs[j], axis=1).astype(BF16))


def _lat_a_attention(qk, vt, cache_k, cache_v, e, bias_src, seq):
    t = qk.shape[0]
    na = A_HEADS * HEAD_DIM
    return pl.pallas_call(
        _lat_a_kernel,
        grid=(t // seq,),
        in_specs=[
            _resident(bias_src.shape),
            pl.BlockSpec((seq, na), lambda b: (b, 0)),
            pl.BlockSpec((seq, na), lambda b: (b, 1)),
            _batch_spec(vt),
            _cache_spec(cache_k, e),
            _cache_spec(cache_v, e),
        ],
        out_specs=pl.BlockSpec((seq, na), lambda b: (b, 0)),
        out_shape=jax.ShapeDtypeStruct((t, na), BF16),
        compiler_params=_params(1),
        name="attn_even_lat_a",
    )(bias_src, qk, qk, vt, cache_k, cache_v)


def _lat_b_kernel(sink_ref, q_ref, k_ref, vt_ref, ck_ref, cv_ref, o_ref):
    n = q_ref.shape[0]
    group = B_HEADS // B_KV_HEADS
    win = 2 * Q_TILE
    ctx = [(_cache_keys(ck_ref, kv), _with_ones_rows(cv_ref[0, 0, kv]), None, None)
           for kv in range(B_KV_HEADS)]
    for j in range(n // Q_TILE):
        lo = min(max(Q_TILE * j - B_WINDOW, 0), n - win)
        q_rows = slice(j * Q_TILE, (j + 1) * Q_TILE)
        kpos = lo + lax.broadcasted_iota(jnp.int32, (win, 1), 0)
        qpos = j * Q_TILE + lax.broadcasted_iota(jnp.int32, (1, group * Q_TILE), 1) % Q_TILE
        valid = jnp.abs(qpos - kpos) <= B_WINDOW
        for kv in range(B_KV_HEADS):
            q = _stack_group(q_ref, q_rows, kv * group * HEAD_DIM, group)
            local = (k_ref[lo:lo + win, kv * HEAD_DIM:(kv + 1) * HEAD_DIM],
                     _with_ones_rows(vt_ref[0, kv, :, lo:lo + win]), None, valid)
            o = _attend([local, ctx[kv]], q, Q_TILE, group,
                        _sink_row(sink_ref, kv * group, group, Q_TILE))
            c0 = kv * group * HEAD_DIM
            o_ref[q_rows, c0:c0 + group * HEAD_DIM] = o.astype(BF16)


def _lat_b_attention(qk, vt, cache_k, cache_v, e, sink, seq):
    t = qk.shape[0]
    na = A_HEADS * HEAD_DIM
    nb = B_HEADS * HEAD_DIM
    nkv = B_KV_HEADS * HEAD_DIM
    return pl.pallas_call(
        _lat_b_kernel,
        grid=(t // seq,),
        in_specs=[
            pl.BlockSpec(memory_space=pltpu.SMEM),
            pl.BlockSpec((seq, nb), lambda b: (b, 2 * na // nb)),
            pl.BlockSpec((seq, nkv), lambda b: (b, (2 * na + nb) // nkv)),
            _batch_spec(vt),
            _cache_spec(cache_k, e),
            _cache_spec(cache_v, e),
        ],
        out_specs=pl.BlockSpec((seq, nb), lambda b: (b, 0)),
        out_shape=jax.ShapeDtypeStruct((t, nb), BF16),
        compiler_params=_params(1),
        name="attn_even_lat_b",
    )(sink, qk, qk, vt, cache_k, cache_v)


def _ctx_odd_attn_kernel(qk_ref, vt_ref, o_ref):
    n = qk_ref.shape[0]
    group = C_HEADS // C_KV_HEADS
    nq = C_HEADS * HEAD_DIM
    for kv in range(C_KV_HEADS):
        q = _stack_group(qk_ref, slice(None), kv * group * HEAD_DIM, group)
        seg = (qk_ref[:, nq + kv * HEAD_DIM:nq + (kv + 1) * HEAD_DIM],
               _with_ones_rows(vt_ref[0, kv]), None, None)
        c0 = kv * group * HEAD_DIM
        o_ref[:, c0:c0 + group * HEAD_DIM] = _attend([seg], q, n, group).astype(BF16)


def _ctx_odd_attention(qk, vt, seq):
    t, n = qk.shape
    width = C_HEADS * HEAD_DIM
    return pl.pallas_call(
        _ctx_odd_attn_kernel,
        grid=(t // seq,),
        in_specs=[pl.BlockSpec((seq, n), lambda b: (b, 0)), _batch_spec(vt)],
        out_specs=pl.BlockSpec((seq, width), lambda b: (b, 0)),
        out_shape=jax.ShapeDtypeStruct((t, width), BF16),
        compiler_params=_params(1),
        name="attn_odd_ctx",
    )(qk, vt)


def _lat_c_kernel(qk_ref, vt_ref, ck_ref, cv_ref, o_ref, kall_ref, vtall_ref, s0_ref, s1_ref,
                  m_ref):
    n = qk_ref.shape[0]
    past = ck_ref.shape[4]
    group = C_HEADS // C_KV_HEADS
    nq = C_HEADS * HEAD_DIM
    n_blocks = n // Q_TILE
    for kv in range(C_KV_HEADS):
        kall_ref[kv, 0:past, :] = _cache_keys(ck_ref, kv)
        kall_ref[kv, past:past + n, :] = qk_ref[:, nq + kv * HEAD_DIM:nq + (kv + 1) * HEAD_DIM]
        vtall_ref[kv, 0:HEAD_DIM, 0:past] = cv_ref[0, 0, kv].astype(BF16)
        vtall_ref[kv, 0:HEAD_DIM, past:past + n] = vt_ref[0, kv]
        vtall_ref[kv, HEAD_DIM:, :] = jnp.ones((ONES_ROWS, past + n), BF16)

    def rows_of(j):
        if isinstance(j, int):
            return slice(j * Q_TILE, (j + 1) * Q_TILE)
        return pl.ds(pl.multiple_of(j * Q_TILE, Q_TILE), Q_TILE)

    chunks = range(0, past + n, KEY_CHUNK)

    slots = ((s0_ref, m_ref.at[0]), (s1_ref, m_ref.at[1]))

    def item(cur, nxt, kv, j_next, kv_next):
        s_cur, m_cur = cur
        s_nxt, m_nxt = nxt
        q_next = _stack_group(qk_ref, rows_of(j_next), kv_next * group * HEAD_DIM, group)
        m = None if s_cur is None else m_cur[...]
        m_next = None
        ot = None
        for c0 in chunks:
            s = _scores(kall_ref[kv_next, c0:c0 + KEY_CHUNK, :], q_next)
            s_nxt[c0:c0 + KEY_CHUNK, :] = s
            m_c = jnp.max(s, axis=0, keepdims=True)
            m_next = m_c if m_next is None else jnp.maximum(m_next, m_c)
            if s_cur is not None:
                p = jnp.exp2(s_cur[c0:c0 + KEY_CHUNK, :] - m).astype(BF16)
                part = jnp.dot(vtall_ref[kv, :, c0:c0 + KEY_CHUNK], p, preferred_element_type=F32)
                ot = part if ot is None else ot + part
        m_nxt[...] = m_next
        if s_cur is None:
            return None
        ot = ot[:HEAD_DIM, :] / ot[HEAD_DIM:HEAD_DIM + 1, :]
        return jnp.concatenate(
            [ot[:, g * Q_TILE:(g + 1) * Q_TILE].T for g in range(group)], axis=1)

    item((None, None), slots[0], None, 0, 0)

    def q_block(j, carry):
        for kv in range(C_KV_HEADS):
            last = kv + 1 == C_KV_HEADS
            o = item(slots[kv % 2], slots[(kv + 1) % 2], kv,
                     lax.rem(j + 1, n_blocks) if last else j, 0 if last else kv + 1)
            c0 = kv * group * HEAD_DIM
            o_ref[rows_of(j), c0:c0 + group * HEAD_DIM] = o.astype(BF16)
        return carry

    lax.fori_loop(0, n_blocks, q_block, 0)


def _lat_c_attention(qk, vt, cache_k, cache_v, o, seq):
    t, n = qk.shape
    past = cache_k.shape[4]
    width = C_HEADS * HEAD_DIM
    return pl.pallas_call(
        _lat_c_kernel,
        grid=(t // seq,),
        in_specs=[
            pl.BlockSpec((seq, n), lambda b: (b, 0)),
            _batch_spec(vt),
            _cache_spec(cache_k, o),
            _cache_spec(cache_v, o),
        ],
        out_specs=pl.BlockSpec((seq, width), lambda b: (b, 0)),
        out_shape=jax.ShapeDtypeStruct((t, width), BF16),
        scratch_shapes=[pltpu.VMEM((C_KV_HEADS, past + seq, HEAD_DIM), BF16),
                        pltpu.VMEM((C_KV_HEADS, HEAD_DIM + ONES_ROWS, past + seq), BF16),
                        pltpu.VMEM((past + seq, C_HEADS // C_KV_HEADS * Q_TILE), F32),
                        pltpu.VMEM((past + seq, C_HEADS // C_KV_HEADS * Q_TILE), F32),
                        pltpu.VMEM((2, 1, C_HEADS // C_KV_HEADS * Q_TILE), F32)],
        compiler_params=_params(1),
        name="attn_odd_lat",
    )(qk, vt, cache_k, cache_v)


def _post_kernel(*refs, n_parts, final):
    o_refs = refs[:n_parts]
    (x_ref, wo_ref, g1_ref, sh_ref, sc_ref, g2_ref, gain_ref, wgu_ref, wd_ref) = refs[n_parts:n_parts + 9]
    rest = refs[n_parts + 9:]
    if final:
        fg_ref, out_ref, act_ref = rest
    else:
        out_ref, act_ref = rest
    mix = None
    r0 = 0
    for o_ref in o_refs:
        kk = o_ref.shape[1]
        part = jnp.dot(o_ref[...], wo_ref[r0:r0 + kk, :], preferred_element_type=F32)
        mix = part if mix is None else mix + part
        r0 += kk
    x1 = x_ref[...] + g1_ref[0, 0] * mix
    h = _adaln(x1, gain_ref[...], sh_ref[0, 0], sc_ref[0, 0]).astype(BF16)
    d_ff = wd_ref.shape[0]
    for j in range(d_ff // FF_CHUNK):
        c0 = j * FF_CHUNK
        gate = jnp.dot(h, wgu_ref[:, c0:c0 + FF_CHUNK], preferred_element_type=F32)
        up = jnp.dot(h, wgu_ref[:, d_ff + c0:d_ff + c0 + FF_CHUNK], preferred_element_type=F32)
        act_ref[:, c0:c0 + FF_CHUNK] = (gate * jax.nn.sigmoid(gate) * up).astype(BF16)
    ffn = jnp.dot(act_ref[...], wd_ref[...], preferred_element_type=F32)
    x2 = x1 + g2_ref[0, 0] * ffn
    if final:
        ms = jnp.mean(x2 * x2, axis=-1, keepdims=True)
        x2 = (x2 * lax.rsqrt(ms + RMS_EPS)) * fg_ref[...]
    out_ref[...] = x2


def _post_attention(o_parts, x, mods, layer, gain, w_out, sub, w_gu, w_down, *, is_lat, seq,
                    final_gain=None):
    t, d = x.shape
    tm = TOKEN_TILE
    per_seq = max(seq // tm, 1)
    group = (lambda i: 1 + i // per_seq) if is_lat else (lambda i: 0)
    row = lambda i: (i, 0)
    d_ff = w_down.shape[1]
    final = final_gain is not None
    in_specs = [pl.BlockSpec((tm, o.shape[1]), row) for o in o_parts]
    in_specs += [
        pl.BlockSpec((tm, d), row),
        _resident_layer(w_out, sub),
        _mod_spec(layer, 2, group),
        _mod_spec(layer, 3, group),
        _mod_spec(layer, 4, group),
        _mod_spec(layer, 5, group),
        _resident((1, d)),
        _resident_layer(w_gu, layer),
        _resident_layer(w_down, layer),
    ]
    args = list(o_parts) + [x, w_out, mods, mods, mods, mods, gain.reshape(1, d), w_gu, w_down]
    if final:
        in_specs.append(_resident((1, d)))
        args.append(final_gain.reshape(1, d))
    return pl.pallas_call(
        functools.partial(_post_kernel, n_parts=len(o_parts), final=final),
        grid=(t // tm,),
        in_specs=in_specs,
        out_specs=pl.BlockSpec((tm, d), row),
        out_shape=jax.ShapeDtypeStruct((t, d), F32),
        scratch_shapes=[pltpu.VMEM((tm, d_ff), BF16)],
        compiler_params=_params(1),
        name=f"post_{'lat' if is_lat else 'ctx'}{'_final' if final else ''}",
    )(*args)


def _rope_tables(n):
    t = np.arange(n)
    row = (t // GRID_W).astype(np.float32)
    col = (t % GRID_W).astype(np.float32)
    half = HEAD_DIM // 2
    inv_freq = np.float32(ROPE_THETA) ** (-np.arange(0, half, 2, dtype=np.float32) / np.float32(half))
    lane = np.arange(LANES)
    in_head = lane % HEAD_DIM
    pos = np.where((in_head < half)[None, :], row[:, None], col[:, None])
    ang = (pos * inv_freq[in_head % (half // 2)][None, :]).astype(np.float32)
    first = ((in_head % half) < half // 2)[None, :]
    cos = np.cos(ang)
    sin = np.sin(ang)
    zero = np.float32(0.0)
    return tuple(jnp.asarray(a, F32) for a in
                 (cos, np.where(first, -sin, zero), np.where(first, zero, sin)))


def _na_bias_sources(rpb):
    h, _, nb = rpb.shape
    w = NA_WIN_W - 1
    rp = jnp.pad(rpb[:, :, ::-1] * LOG2E, ((0, 0), (1, 1), (0, 0)))
    this, prev = rp[:, 1:], rp[:, :-1]
    z = jnp.zeros((h, 2 * NA_WIN_H, LANES // 2 - nb), F32)
    return jnp.concatenate([this[:, :, w:], z, prev, z, this[:, :, :w]], axis=-1)


def _state(y):
    return y.transpose(0, 3, 1, 2)


def kernel(x_prompt, x_sample, cache_a_k, cache_a_v, cache_b_k, cache_b_v, cache_c_k, cache_c_v,
           c, c_ctx, norm_gain, w_mod, b_mod, w_in_even, w_out_even, rpb_a, sink_b,
           w_in_odd, w_out_odd, q_norm_c, k_norm_c, w_gate_up, w_down, final_gain):
    batch, seq, d = x_prompt.shape
    dec_batch, dec_seq, _ = x_sample.shape
    depth = w_mod.shape[0]

    cvec = jnp.concatenate(
        [c_ctx[None, :], c, jnp.zeros((MOD_GROUPS - 1 - dec_batch, d), F32)], axis=0)
    mods = _modulation(cvec, w_mod, b_mod).reshape(depth, MOD_GROUPS, 1, 6 * d)
    rope = _rope_tables(dec_seq)
    w_in = {False: w_in_even.astype(BF16), True: w_in_odd.astype(BF16)}
    w_out = {False: w_out_even.astype(BF16), True: w_out_odd.astype(BF16)}
    w_gu = w_gate_up.astype(BF16)
    w_dn = w_down.astype(BF16)

    ctx = x_prompt.reshape(batch * seq, d)
    lat = x_sample.reshape(dec_batch * dec_seq, d)
    states = {name: [] for name in ("a_k", "a_v", "b_k", "b_v", "c_k", "c_v")}

    for layer in range(depth):
        odd = layer % 2 == 1
        sub = layer // 2
        gain1, gain2 = norm_gain[layer, 0], norm_gain[layer, 1]
        pre = functools.partial(_pre_attention, mods=mods, layer=layer, gain=gain1, w=w_in[odd],
                                sub=sub, odd=odd)
        if not odd:
            qk_c, ka, va, kb, vb = pre(ctx, is_lat=False, seq=seq)
            qk_l, vat_l, vbt_l = pre(lat, is_lat=True, seq=dec_seq, rope=rope)
            for name, y in (("a_k", ka), ("a_v", va), ("b_k", kb), ("b_v", vb)):
                states[name].append(_state(y))
            o_ctx = [_ctx_even_attention(qk_c, va, vb, sink_b[sub], seq)]
            o_lat = [
                _lat_a_attention(qk_l, vat_l, _feature_major(cache_a_k), _feature_major(cache_a_v),
                                 sub, _na_bias_sources(rpb_a[sub]), dec_seq),
                _lat_b_attention(qk_l, vbt_l, _feature_major(cache_b_k), _feature_major(cache_b_v),
                                 sub, sink_b[sub], dec_seq),
            ]
        else:
            per = LANES // HEAD_DIM
            head_gains = (jnp.tile(q_norm_c[sub], per).reshape(1, LANES),
                          jnp.tile(k_norm_c[sub], per).reshape(1, LANES))
            qk_c, kc, vc = pre(ctx, is_lat=False, seq=seq, head_gains=head_gains)
            qk_l, vct_l = pre(lat, is_lat=True, seq=dec_seq, rope=rope, head_gains=head_gains)
            states["c_k"].append(_state(kc))
            states["c_v"].append(_state(vc))
            o_ctx = [_ctx_odd_attention(qk_c, vc, seq)]
            o_lat = [_lat_c_attention(qk_l, vct_l, _feature_major(cache_c_k),
                                      _feature_major(cache_c_v), sub, dec_seq)]
        fg = final_gain if layer == depth - 1 else None
        post = functools.partial(_post_attention, mods=mods, layer=layer, gain=gain2,
                                 w_out=w_out[odd], sub=sub, w_gu=w_gu, w_down=w_dn, final_gain=fg)
        ctx = post(o_ctx, ctx, is_lat=False, seq=seq)
        lat = post(o_lat, lat, is_lat=True, seq=dec_seq)

    return (ctx.reshape(batch, seq, d), lat.reshape(dec_batch, dec_seq, d),
            jnp.stack(states["a_k"], axis=1), jnp.stack(states["a_v"], axis=1),
            jnp.stack(states["b_k"], axis=1), jnp.stack(states["b_v"], axis=1),
            jnp.stack(states["c_k"], axis=1), jnp.stack(states["c_v"], axis=1))
```

```python
import functools
import math

import jax
import jax.numpy as jnp
import numpy as np
from jax import lax
from jax.experimental import pallas as pl
from jax.experimental.pallas import tpu as pltpu

F32 = jnp.float32
BF16 = jnp.bfloat16

D_MODEL = 1024
GRID_W = 64
HEAD_DIM = 64
A_HEADS = 8
B_HEADS = 8
B_KV_HEADS = 2
C_HEADS = 16
C_KV_HEADS = 4
NA_WIN_H = 8
NA_WIN_W = 16
B_WINDOW = 128
ROPE_THETA = 10000.0
RMS_EPS = 1e-6
NEG_INF = -1e30
LOG2E = math.log2(math.e)
QK_SCALE = LOG2E / math.sqrt(HEAD_DIM)

LANES = 128
TOKEN_TILE = 512
PRE_TILE = 1024
Q_TILE = 256
CTX_BATCHES = 2
ONES_ROWS = 16
KEY_CHUNK = 256
FF_CHUNK = 256
MOD_GROUPS = 16
VMEM_LIMIT = 56 * 1024 * 1024


def _params(n_axes, vmem=VMEM_LIMIT):
    return pltpu.CompilerParams(
        dimension_semantics=("arbitrary",) * n_axes, vmem_limit_bytes=vmem)


def _resident(shape):
    nd = len(shape)
    return pl.BlockSpec(shape, lambda *_: (0,) * nd, pipeline_mode=pl.Buffered(1))


def _resident_layer(stacked, layer):
    return pl.BlockSpec((None,) + stacked.shape[1:], lambda *_: (layer, 0, 0),
                        pipeline_mode=pl.Buffered(1))


def _mod_kernel(c_ref, w_ref, b_ref, o_ref):
    c = c_ref[...]
    s = (c * jax.nn.sigmoid(c)).astype(BF16)
    o_ref[0] = jnp.dot(s, w_ref[0].astype(BF16), preferred_element_type=F32) + b_ref[0]


def _modulation(cvec, w_mod, b_mod):
    depth, d, n = w_mod.shape
    tn = 1536
    return pl.pallas_call(
        _mod_kernel,
        grid=(depth, n // tn),
        in_specs=[
            pl.BlockSpec((MOD_GROUPS, d), lambda l, j: (0, 0)),
            pl.BlockSpec((1, d, tn), lambda l, j: (l, 0, j)),
            pl.BlockSpec((1, 1, tn), lambda l, j: (l, 0, j)),
        ],
        out_specs=pl.BlockSpec((1, MOD_GROUPS, tn), lambda l, j: (l, 0, j)),
        out_shape=jax.ShapeDtypeStruct((depth, MOD_GROUPS, n), F32),
        compiler_params=_params(2),
        name="modulation",
    )(cvec, w_mod, b_mod.reshape(depth, 1, n))


def _mod_spec(layer, which, group_of_step):
    return pl.BlockSpec((1, 1, 1, D_MODEL), lambda i: (layer, group_of_step(i), 0, which))


def _adaln(x, gain, shift, scale):
    ms = jnp.mean(x * x, axis=-1, keepdims=True)
    return (x * lax.rsqrt(ms + RMS_EPS)) * gain * (1.0 + scale) + shift


def _rope(y, cos, sin_lo, sin_hi):
    outs = []
    for c in range(y.shape[1] // LANES):
        yc = y[:, c * LANES:(c + 1) * LANES]
        outs.append(yc * cos
                    + pltpu.roll(yc, LANES - 16, 1) * sin_lo
                    + pltpu.roll(yc, 16, 1) * sin_hi)
    return outs[0] if len(outs) == 1 else jnp.concatenate(outs, axis=1)


def _head_rms_norm(y, gain):
    first = lax.broadcasted_iota(jnp.int32, (1, LANES), 1) < HEAD_DIM
    outs = []
    for c in range(y.shape[1] // LANES):
        yc = y[:, c * LANES:(c + 1) * LANES]
        sq = yc * yc
        s0 = jnp.sum(jnp.where(first, sq, 0.0), axis=-1, keepdims=True)
        s1 = jnp.sum(jnp.where(first, 0.0, sq), axis=-1, keepdims=True)
        ms = jnp.where(first, s0, s1) * (1.0 / HEAD_DIM)
        outs.append(yc * lax.rsqrt(ms + RMS_EPS) * gain)
    return outs[0] if len(outs) == 1 else jnp.concatenate(outs, axis=1)


_NT = (((1,), (1,)), ((), ()))


def _with_ones_rows(vt):
    return jnp.concatenate([vt.astype(BF16), jnp.ones((ONES_ROWS, vt.shape[1]), BF16)], axis=0)


def _attend(segments, q, rows, group, sink=None):
    scores = []
    for k, _, bias, valid in segments:
        s = _scores(k, q)
        if bias is not None:
            s = s + bias
        if valid is not None:
            s = jnp.where(valid, s, NEG_INF)
        scores.append(s)
    return _softmax_values(scores, [seg[1] for seg in segments], rows, group, sink)


def _scores(k, q):
    return lax.dot_general(k, q, _NT, preferred_element_type=F32)


def _softmax_values(scores, vts, rows, group, sink=None):
    m = functools.reduce(jnp.maximum, [jnp.max(s, axis=0, keepdims=True) for s in scores])
    if sink is not None:
        m = jnp.maximum(m, sink)
    ot = None
    for s, vt_ones in zip(scores, vts):
        part = jnp.dot(vt_ones, jnp.exp2(s - m).astype(BF16), preferred_element_type=F32)
        ot = part if ot is None else ot + part
    denom = ot[HEAD_DIM:HEAD_DIM + 1, :]
    if sink is not None:
        denom = denom + jnp.exp2(sink - m)
    ot = ot[:HEAD_DIM, :] / denom
    return jnp.concatenate([ot[:, g * rows:(g + 1) * rows].T for g in range(group)], axis=1)


def _stack_group(ref, rows, col0, group):
    return jnp.concatenate(
        [ref[rows, col0 + g * HEAD_DIM: col0 + (g + 1) * HEAD_DIM] for g in range(group)], axis=0)


def _sink_row(sink_ref, h0, group, rows):
    return jnp.concatenate(
        [jnp.full((1, rows), sink_ref[h0 + g] * LOG2E, F32) for g in range(group)], axis=1)


def _store_feature_major(ref, y):
    nb, heads, _, seq = ref.shape
    yt = y.T
    for b in range(nb):
        for h in range(heads):
            ref[b, h] = yt[h * HEAD_DIM:(h + 1) * HEAD_DIM, b * seq:(b + 1) * seq].astype(ref.dtype)


def _pre_even_kernel(x_ref, g_ref, sh_ref, sc_ref, w_ref, *rest, is_lat):
    if is_lat:
        cos_ref, slo_ref, shi_ref, qk_ref, va_ref, vb_ref = rest
    else:
        qk_ref, ka_ref, va_ref, kb_ref, vb_ref = rest
    h = _adaln(x_ref[...], g_ref[...], sh_ref[0, 0], sc_ref[0, 0]).astype(BF16)

    def proj(c0, c1):
        return jnp.dot(h, w_ref[:, c0:c1], preferred_element_type=F32)

    na = A_HEADS * HEAD_DIM
    nb = B_HEADS * HEAD_DIM
    nkv = B_KV_HEADS * HEAD_DIM
    qk_ref[:, 0:na] = (proj(0, na) * QK_SCALE).astype(BF16)
    ka = proj(na, 2 * na)
    qk_ref[:, na:2 * na] = ka.astype(BF16)
    va = proj(2 * na, 3 * na)
    qb = proj(3 * na, 3 * na + nb)
    if is_lat:
        qb = _rope(qb, cos_ref[...], slo_ref[...], shi_ref[...])
    qk_ref[:, 2 * na:2 * na + nb] = (qb * QK_SCALE).astype(BF16)
    kvb = proj(3 * na + nb, 3 * na + nb + 2 * nkv)
    kb, vb = kvb[:, :nkv], kvb[:, nkv:]
    if is_lat:
        kb_out = _rope(kb, cos_ref[...], slo_ref[...], shi_ref[...])
    else:
        kb_out = kb
    qk_ref[:, 2 * na + nb:2 * na + nb + nkv] = kb_out.astype(BF16)
    _store_feature_major(va_ref, va)
    _store_feature_major(vb_ref, vb)
    if not is_lat:
        _store_feature_major(ka_ref, ka)
        _store_feature_major(kb_ref, kb)


def _pre_odd_kernel(x_ref, g_ref, sh_ref, sc_ref, w_ref, qn_ref, kn_ref, *rest, is_lat):
    if is_lat:
        cos_ref, slo_ref, shi_ref, qk_ref, vc_ref = rest
    else:
        qk_ref, kc_ref, vc_ref = rest
    h = _adaln(x_ref[...], g_ref[...], sh_ref[0, 0], sc_ref[0, 0]).astype(BF16)

    def proj(c0, c1):
        return jnp.dot(h, w_ref[:, c0:c1], preferred_element_type=F32)

    nq = C_HEADS * HEAD_DIM
    nkv = C_KV_HEADS * HEAD_DIM
    q = _head_rms_norm(proj(0, nq), qn_ref[...])
    k = _head_rms_norm(proj(nq, nq + nkv), kn_ref[...])
    v = proj(nq + nkv, nq + 2 * nkv)
    if is_lat:
        q = _rope(q, cos_ref[...], slo_ref[...], shi_ref[...])
        k_out = _rope(k, cos_ref[...], slo_ref[...], shi_ref[...])
    else:
        k_out = k
        _store_feature_major(kc_ref, k)
    _store_feature_major(vc_ref, v)
    qk_ref[:, 0:nq] = (q * QK_SCALE).astype(BF16)
    qk_ref[:, nq:nq + nkv] = k_out.astype(BF16)


def _pre_attention(x, mods, layer, gain, w, sub, *, odd, is_lat, seq, rope=None,
                   head_gains=None):
    t, d = x.shape
    tm = PRE_TILE
    per_seq = max(seq // tm, 1)
    per_tile = max(tm // seq, 1)
    group = (lambda i: 1 + i // per_seq) if is_lat else (lambda i: 0)
    row = lambda i: (i, 0)
    in_specs = [
        pl.BlockSpec((tm, d), row),
        _resident((1, d)),
        _mod_spec(layer, 0, group),
        _mod_spec(layer, 1, group),
        _resident_layer(w, sub),
    ]
    args = [x, gain.reshape(1, d), mods, mods, w]
    if odd:
        in_specs += [_resident(a.shape) for a in head_gains]
        args += list(head_gains)
    if is_lat:
        in_specs += [pl.BlockSpec((tm, LANES), lambda i: (i % per_seq, 0))] * 3
        args += list(rope)
    if odd:
        n_qk = (C_HEADS + C_KV_HEADS) * HEAD_DIM
        lat_heads, ctx_heads = [C_KV_HEADS], [C_KV_HEADS, C_KV_HEADS]
    else:
        n_qk = (2 * A_HEADS + B_HEADS + B_KV_HEADS) * HEAD_DIM
        lat_heads, ctx_heads = [A_HEADS, B_KV_HEADS], [A_HEADS, A_HEADS, B_KV_HEADS, B_KV_HEADS]
    out_specs = [pl.BlockSpec((tm, n_qk), row)]
    out_shape = [jax.ShapeDtypeStruct((t, n_qk), BF16)]
    for heads in (lat_heads if is_lat else ctx_heads):
        blk = (per_tile, heads, HEAD_DIM, min(tm, seq))
        out_specs.append(pl.BlockSpec(blk, lambda i: (i // per_seq, 0, 0, i % per_seq)))
        out_shape.append(jax.ShapeDtypeStruct((t // seq, heads, HEAD_DIM, seq),
                                              BF16 if is_lat else F32))
    body = _pre_odd_kernel if odd else _pre_even_kernel
    return pl.pallas_call(
        functools.partial(body, is_lat=is_lat),
        grid=(t // tm,),
        in_specs=in_specs,
        out_specs=out_specs,
        out_shape=out_shape,
        compiler_params=_params(1),
        name=f"pre_{'odd' if odd else 'even'}_{'lat' if is_lat else 'ctx'}",
    )(*args)


def _batch_spec(arr, nb=1):
    nd = arr.ndim
    return pl.BlockSpec((nb,) + arr.shape[1:], lambda b: (b,) + (0,) * (nd - 1))


def _cache_spec(cache, layer):
    blk = (1, 1) + cache.shape[2:]
    return pl.BlockSpec(blk, lambda b: (b, layer, 0, 0, 0))


def _feature_major(cache):
    return cache.transpose(0, 1, 3, 4, 2)


def _cache_keys(ref, h):
    return ref[0, 0, h].T.astype(BF16)


def _ctx_even_attn_kernel(sink_ref, qk_ref, vat_ref, vbt_ref, o_ref):
    nb = vat_ref.shape[0]
    n = qk_ref.shape[0] // nb
    na = A_HEADS * HEAD_DIM
    group = B_HEADS // B_KV_HEADS
    qb0 = 2 * na
    kb0 = qb0 + B_HEADS * HEAD_DIM
    for b in range(nb):
        rows = slice(b * n, (b + 1) * n)
        for pair in range(A_HEADS // 2):
            outs = []
            for h in (2 * pair, 2 * pair + 1):
                c = h * HEAD_DIM
                seg = (qk_ref[rows, na + c:na + c + HEAD_DIM], _with_ones_rows(vat_ref[b, h]),
                       None, None)
                outs.append(_attend([seg], qk_ref[rows, c:c + HEAD_DIM], n, 1))
            o_ref[rows, pair * LANES:(pair + 1) * LANES] = (
                jnp.concatenate(outs, axis=1).astype(BF16))
        for kv in range(B_KV_HEADS):
            q = _stack_group(qk_ref, rows, qb0 + kv * group * HEAD_DIM, group)
            seg = (qk_ref[rows, kb0 + kv * HEAD_DIM:kb0 + (kv + 1) * HEAD_DIM],
                   _with_ones_rows(vbt_ref[b, kv]), None, None)
            o = _attend([seg], q, n, group, _sink_row(sink_ref, kv * group, group, n))
            c0 = na + kv * group * HEAD_DIM
            o_ref[rows, c0:c0 + group * HEAD_DIM] = o.astype(BF16)


def _ctx_even_attention(qk, vat, vbt, sink, seq):
    t, n = qk.shape
    width = (A_HEADS + B_HEADS) * HEAD_DIM
    nb = CTX_BATCHES
    return pl.pallas_call(
        _ctx_even_attn_kernel,
        grid=(t // (nb * seq),),
        in_specs=[pl.BlockSpec(memory_space=pltpu.SMEM),
                  pl.BlockSpec((nb * seq, n), lambda b: (b, 0)),
                  _batch_spec(vat, nb), _batch_spec(vbt, nb)],
        out_specs=pl.BlockSpec((nb * seq, width), lambda b: (b, 0)),
        out_shape=jax.ShapeDtypeStruct((t, width), BF16),
        compiler_params=_params(1),
        name="attn_even_ctx",
    )(sink, qk, vat, vbt)


def _na_key_window(j):
    return min(max(Q_TILE * j - Q_TILE, 0), Q_TILE)


def _na_bias_tiles(src_ref, h):
    kc = lax.broadcasted_iota(jnp.int32, (GRID_W, LANES), 0)
    qc = lax.rem(lax.broadcasted_iota(jnp.int32, (GRID_W, LANES), 1), GRID_W)
    cs = jnp.clip(qc - NA_WIN_W // 2, 0, GRID_W - NA_WIN_W)
    col_ok = (kc >= cs) & (kc < cs + NA_WIN_W)
    tiles = []
    for i in range(2 * NA_WIN_H):
        src = jnp.broadcast_to(src_ref[h, i:i + 1, :], (GRID_W, LANES))
        tiles.append(jnp.where(col_ok, pltpu.roll(src, 0, 1, stride=1, stride_axis=0), NEG_INF))
    return tiles


def _na_bias_block(tiles, j, grid_rows):
    wh = min(NA_WIN_H, grid_rows)
    first_row = lax.broadcasted_iota(jnp.int32, (GRID_W, LANES), 1) < GRID_W
    lo_r = _na_key_window(j) // GRID_W
    q_rows = range(j * Q_TILE // GRID_W, (j + 1) * Q_TILE // GRID_W, 2)

    def row_start(qr):
        return min(max(qr - wh // 2, 0), grid_rows - wh)

    strips = []
    for kr in range(lo_r, lo_r + 3 * Q_TILE // GRID_W):
        parts = []
        for qr in q_rows:
            ok0 = row_start(qr) <= kr < row_start(qr) + wh
            ok1 = row_start(qr + 1) <= kr < row_start(qr + 1) + wh
            i = kr - qr + NA_WIN_H - 1
            if ok0 and ok1:
                parts.append(tiles[i])
            elif ok0:
                parts.append(jnp.where(first_row, tiles[i], NEG_INF))
            elif ok1:
                parts.append(jnp.where(first_row, NEG_INF, tiles[i]))
            else:
                parts.append(jnp.full((GRID_W, LANES), NEG_INF, F32))
        strips.append(jnp.concatenate(parts, axis=1))
    return jnp.concatenate(strips, axis=0)


def _lat_a_kernel(src_ref, q_ref, k_ref, vt_ref, ck_ref, cv_ref, o_ref):
    n = q_ref.shape[0]
    win = 3 * Q_TILE
    for pair in range(A_HEADS // 2):
        outs = [[] for _ in range(n // Q_TILE)]
        for h in (2 * pair, 2 * pair + 1):
            cols = slice(h * HEAD_DIM, (h + 1) * HEAD_DIM)
            ctx = (_cache_keys(ck_ref, h), _with_ones_rows(cv_ref[0, 0, h]), None, None)
            tiles = _na_bias_tiles(src_ref, h)
            for j in range(n // Q_TILE):
                lo = _na_key_window(j)
                local = (k_ref[lo:lo + win, cols], _with_ones_rows(vt_ref[0, h, :, lo:lo + win]),
                         _na_bias_block(tiles, j, n // GRID_W), None)
                outs[j].append(
                    _attend([local, ctx], q_ref[j * Q_TILE:(j + 1) * Q_TILE, cols], Q_TILE, 1))
        for j in range(n // Q_TILE):
            o_ref[j * Q_TILE:(j + 1) * Q_TILE, pair * LANES:(pair + 1) * LANES] = (
                jnp.concatenate(outs[j], axis=1).astype(BF16))


def _lat_a_attention(qk, vt, cache_k, cache_v, e, bias_src, seq):
    t = qk.shape[0]
    na = A_HEADS * HEAD_DIM
    return pl.pallas_call(
        _lat_a_kernel,
        grid=(t // seq,),
        in_specs=[
            _resident(bias_src.shape),
            pl.BlockSpec((seq, na), lambda b: (b, 0)),
            pl.BlockSpec((seq, na), lambda b: (b, 1)),
            _batch_spec(vt),
            _cache_spec(cache_k, e),
            _cache_spec(cache_v, e),
        ],
        out_specs=pl.BlockSpec((seq, na), lambda b: (b, 0)),
        out_shape=jax.ShapeDtypeStruct((t, na), BF16),
        compiler_params=_params(1),
        name="attn_even_lat_a",
    )(bias_src, qk, qk, vt, cache_k, cache_v)


def _lat_b_kernel(sink_ref, q_ref, k_ref, vt_ref, ck_ref, cv_ref, o_ref):
    n = q_ref.shape[0]
    group = B_HEADS // B_KV_HEADS
    win = 2 * Q_TILE
    ctx = [(_cache_keys(ck_ref, kv), _with_ones_rows(cv_ref[0, 0, kv]), None, None)
           for kv in range(B_KV_HEADS)]
    for j in range(n // Q_TILE):
        lo = min(max(Q_TILE * j - B_WINDOW, 0), n - win)
        q_rows = slice(j * Q_TILE, (j + 1) * Q_TILE)
        kpos = lo + lax.broadcasted_iota(jnp.int32, (win, 1), 0)
        qpos = j * Q_TILE + lax.broadcasted_iota(jnp.int32, (1, group * Q_TILE), 1) % Q_TILE
        valid = jnp.abs(qpos - kpos) <= B_WINDOW
        for kv in range(B_KV_HEADS):
            q = _stack_group(q_ref, q_rows, kv * group * HEAD_DIM, group)
            local = (k_ref[lo:lo + win, kv * HEAD_DIM:(kv + 1) * HEAD_DIM],
                     _with_ones_rows(vt_ref[0, kv, :, lo:lo + win]), None, valid)
            o = _attend([local, ctx[kv]], q, Q_TILE, group,
                        _sink_row(sink_ref, kv * group, group, Q_TILE))
            c0 = kv * group * HEAD_DIM
            o_ref[q_rows, c0:c0 + group * HEAD_DIM] = o.astype(BF16)


def _lat_b_attention(qk, vt, cache_k, cache_v, e, sink, seq):
    t = qk.shape[0]
    na = A_HEADS * HEAD_DIM
    nb = B_HEADS * HEAD_DIM
    nkv = B_KV_HEADS * HEAD_DIM
    return pl.pallas_call(
        _lat_b_kernel,
        grid=(t // seq,),
        in_specs=[
            pl.BlockSpec(memory_space=pltpu.SMEM),
            pl.BlockSpec((seq, nb), lambda b: (b, 2 * na // nb)),
            pl.BlockSpec((seq, nkv), lambda b: (b, (2 * na + nb) // nkv)),
            _batch_spec(vt),
            _cache_spec(cache_k, e),
            _cache_spec(cache_v, e),
        ],
        out_specs=pl.BlockSpec((seq, nb), lambda b: (b, 0)),
        out_shape=jax.ShapeDtypeStruct((t, nb), BF16),
        compiler_params=_params(1),
        name="attn_even_lat_b",
    )(sink, qk, qk, vt, cache_k, cache_v)


def _ctx_odd_attn_kernel(qk_ref, vt_ref, o_ref):
    nb = vt_ref.shape[0]
    n = qk_ref.shape[0] // nb
    group = C_HEADS // C_KV_HEADS
    nq = C_HEADS * HEAD_DIM
    for b in range(nb):
        rows = slice(b * n, (b + 1) * n)
        for kv in range(C_KV_HEADS):
            q = _stack_group(qk_ref, rows, kv * group * HEAD_DIM, group)
            seg = (qk_ref[rows, nq + kv * HEAD_DIM:nq + (kv + 1) * HEAD_DIM],
                   _with_ones_rows(vt_ref[b, kv]), None, None)
            c0 = kv * group * HEAD_DIM
            o_ref[rows, c0:c0 + group * HEAD_DIM] = _attend([seg], q, n, group).astype(BF16)


def _ctx_odd_attention(qk, vt, seq):
    t, n = qk.shape
    width = C_HEADS * HEAD_DIM
    nb = CTX_BATCHES
    return pl.pallas_call(
        _ctx_odd_attn_kernel,
        grid=(t // (nb * seq),),
        in_specs=[pl.BlockSpec((nb * seq, n), lambda b: (b, 0)), _batch_spec(vt, nb)],
        out_specs=pl.BlockSpec((nb * seq, width), lambda b: (b, 0)),
        out_shape=jax.ShapeDtypeStruct((t, width), BF16),
        compiler_params=_params(1),
        name="attn_odd_ctx",
    )(qk, vt)


def _lat_c_kernel(qk_ref, vt_ref, ck_ref, cv_ref, o_ref, kall_ref, vtall_ref, s0_ref, s1_ref,
                  m_ref):
    n = qk_ref.shape[0]
    past = ck_ref.shape[4]
    group = C_HEADS // C_KV_HEADS
    nq = C_HEADS * HEAD_DIM
    n_blocks = n // Q_TILE
    for kv in range(C_KV_HEADS):
        kall_ref[kv, 0:past, :] = _cache_keys(ck_ref, kv)
        kall_ref[kv, past:past + n, :] = qk_ref[:, nq + kv * HEAD_DIM:nq + (kv + 1) * HEAD_DIM]
        vtall_ref[kv, 0:HEAD_DIM, 0:past] = cv_ref[0, 0, kv].astype(BF16)
        vtall_ref[kv, 0:HEAD_DIM, past:past + n] = vt_ref[0, kv]
        vtall_ref[kv, HEAD_DIM:, :] = jnp.ones((ONES_ROWS, past + n), BF16)

    def rows_of(j):
        if isinstance(j, int):
            return slice(j * Q_TILE, (j + 1) * Q_TILE)
        return pl.ds(pl.multiple_of(j * Q_TILE, Q_TILE), Q_TILE)

    chunks = range(0, past + n, KEY_CHUNK)

    slots = ((s0_ref, m_ref.at[0]), (s1_ref, m_ref.at[1]))

    def item(cur, nxt, kv, j_next, kv_next):
        s_cur, m_cur = cur
        s_nxt, m_nxt = nxt
        q_next = _stack_group(qk_ref, rows_of(j_next), kv_next * group * HEAD_DIM, group)
        m = None if s_cur is None else m_cur[...]
        m_next = None
        ot = None
        for c0 in chunks:
            s = _scores(kall_ref[kv_next, c0:c0 + KEY_CHUNK, :], q_next)
            s_nxt[c0:c0 + KEY_CHUNK, :] = s
            m_c = jnp.max(s, axis=0, keepdims=True)
            m_next = m_c if m_next is None else jnp.maximum(m_next, m_c)
            if s_cur is not None:
                p = jnp.exp2(s_cur[c0:c0 + KEY_CHUNK, :] - m).astype(BF16)
                part = jnp.dot(vtall_ref[kv, :, c0:c0 + KEY_CHUNK], p, preferred_element_type=F32)
                ot = part if ot is None else ot + part
        m_nxt[...] = m_next
        if s_cur is None:
            return None
        ot = ot[:HEAD_DIM, :] / ot[HEAD_DIM:HEAD_DIM + 1, :]
        return jnp.concatenate(
            [ot[:, g * Q_TILE:(g + 1) * Q_TILE].T for g in range(group)], axis=1)

    item((None, None), slots[0], None, 0, 0)

    def q_block(j, carry):
        for kv in range(C_KV_HEADS):
            last = kv + 1 == C_KV_HEADS
            o = item(slots[kv % 2], slots[(kv + 1) % 2], kv,
                     lax.rem(j + 1, n_blocks) if last else j, 0 if last else kv + 1)
            c0 = kv * group * HEAD_DIM
            o_ref[rows_of(j), c0:c0 + group * HEAD_DIM] = o.astype(BF16)
        return carry

    lax.fori_loop(0, n_blocks, q_block, 0)


def _lat_c_attention(qk, vt, cache_k, cache_v, o, seq):
    t, n = qk.shape
    past = cache_k.shape[4]
    width = C_HEADS * HEAD_DIM
    return pl.pallas_call(
        _lat_c_kernel,
        grid=(t // seq,),
        in_specs=[
            pl.BlockSpec((seq, n), lambda b: (b, 0)),
            _batch_spec(vt),
            _cache_spec(cache_k, o),
            _cache_spec(cache_v, o),
        ],
        out_specs=pl.BlockSpec((seq, width), lambda b: (b, 0)),
        out_shape=jax.ShapeDtypeStruct((t, width), BF16),
        scratch_shapes=[pltpu.VMEM((C_KV_HEADS, past + seq, HEAD_DIM), BF16),
                        pltpu.VMEM((C_KV_HEADS, HEAD_DIM + ONES_ROWS, past + seq), BF16),
                        pltpu.VMEM((past + seq, C_HEADS // C_KV_HEADS * Q_TILE), F32),
                        pltpu.VMEM((past + seq, C_HEADS // C_KV_HEADS * Q_TILE), F32),
                        pltpu.VMEM((2, 1, C_HEADS // C_KV_HEADS * Q_TILE), F32)],
        compiler_params=_params(1),
        name="attn_odd_lat",
    )(qk, vt, cache_k, cache_v)


def _post_kernel(*refs, n_parts, final):
    o_refs = refs[:n_parts]
    (x_ref, wo_ref, g1_ref, sh_ref, sc_ref, g2_ref, gain_ref, wgu_ref, wd_ref) = refs[n_parts:n_parts + 9]
    rest = refs[n_parts + 9:]
    if final:
        fg_ref, out_ref, act_ref = rest
    else:
        out_ref, act_ref = rest
    mix = None
    r0 = 0
    for o_ref in o_refs:
        kk = o_ref.shape[1]
        part = jnp.dot(o_ref[...], wo_ref[r0:r0 + kk, :], preferred_element_type=F32)
        mix = part if mix is None else mix + part
        r0 += kk
    x1 = x_ref[...] + g1_ref[0, 0] * mix
    h = _adaln(x1, gain_ref[...], sh_ref[0, 0], sc_ref[0, 0]).astype(BF16)
    d_ff = wd_ref.shape[0]
    for j in range(d_ff // FF_CHUNK):
        c0 = j * FF_CHUNK
        gate = jnp.dot(h, wgu_ref[:, c0:c0 + FF_CHUNK], preferred_element_type=F32)
        up = jnp.dot(h, wgu_ref[:, d_ff + c0:d_ff + c0 + FF_CHUNK], preferred_element_type=F32)
        act_ref[:, c0:c0 + FF_CHUNK] = (gate * jax.nn.sigmoid(gate) * up).astype(BF16)
    ffn = jnp.dot(act_ref[...], wd_ref[...], preferred_element_type=F32)
    x2 = x1 + g2_ref[0, 0] * ffn
    if final:
        ms = jnp.mean(x2 * x2, axis=-1, keepdims=True)
        x2 = (x2 * lax.rsqrt(ms + RMS_EPS)) * fg_ref[...]
    out_ref[...] = x2


def _post_attention(o_parts, x, mods, layer, gain, w_out, sub, w_gu, w_down, *, is_lat, seq,
                    final_gain=None):
    t, d = x.shape
    tm = TOKEN_TILE
    per_seq = max(seq // tm, 1)
    group = (lambda i: 1 + i // per_seq) if is_lat else (lambda i: 0)
    row = lambda i: (i, 0)
    d_ff = w_down.shape[1]
    final = final_gain is not None
    in_specs = [pl.BlockSpec((tm, o.shape[1]), row) for o in o_parts]
    in_specs += [
        pl.BlockSpec((tm, d), row),
        _resident_layer(w_out, sub),
        _mod_spec(layer, 2, group),
        _mod_spec(layer, 3, group),
        _mod_spec(layer, 4, group),
        _mod_spec(layer, 5, group),
        _resident((1, d)),
        _resident_layer(w_gu, layer),
        _resident_layer(w_down, layer),
    ]
    args = list(o_parts) + [x, w_out, mods, mods, mods, mods, gain.reshape(1, d), w_gu, w_down]
    if final:
        in_specs.append(_resident((1, d)))
        args.append(final_gain.reshape(1, d))
    return pl.pallas_call(
        functools.partial(_post_kernel, n_parts=len(o_parts), final=final),
        grid=(t // tm,),
        in_specs=in_specs,
        out_specs=pl.BlockSpec((tm, d), row),
        out_shape=jax.ShapeDtypeStruct((t, d), F32),
        scratch_shapes=[pltpu.VMEM((tm, d_ff), BF16)],
        compiler_params=_params(1),
        name=f"post_{'lat' if is_lat else 'ctx'}{'_final' if final else ''}",
    )(*args)


def _rope_tables(n):
    t = np.arange(n)
    row = (t // GRID_W).astype(np.float32)
    col = (t % GRID_W).astype(np.float32)
    half = HEAD_DIM // 2
    inv_freq = np.float32(ROPE_THETA) ** (-np.arange(0, half, 2, dtype=np.float32) / np.float32(half))
    lane = np.arange(LANES)
    in_head = lane % HEAD_DIM
    pos = np.where((in_head < half)[None, :], row[:, None], col[:, None])
    ang = (pos * inv_freq[in_head % (half // 2)][None, :]).astype(np.float32)
    first = ((in_head % half) < half // 2)[None, :]
    cos = np.cos(ang)
    sin = np.sin(ang)
    zero = np.float32(0.0)
    return tuple(jnp.asarray(a, F32) for a in
                 (cos, np.where(first, -sin, zero), np.where(first, zero, sin)))


def _na_bias_sources(rpb):
    h, _, nb = rpb.shape
    w = NA_WIN_W - 1
    rp = jnp.pad(rpb[:, :, ::-1] * LOG2E, ((0, 0), (1, 1), (0, 0)))
    this, prev = rp[:, 1:], rp[:, :-1]
    z = jnp.zeros((h, 2 * NA_WIN_H, LANES // 2 - nb), F32)
    return jnp.concatenate([this[:, :, w:], z, prev, z, this[:, :, :w]], axis=-1)


def _state(y):
    return y.transpose(0, 3, 1, 2)


def kernel(x_prompt, x_sample, cache_a_k, cache_a_v, cache_b_k, cache_b_v, cache_c_k, cache_c_v,
           c, c_ctx, norm_gain, w_mod, b_mod, w_in_even, w_out_even, rpb_a, sink_b,
           w_in_odd, w_out_odd, q_norm_c, k_norm_c, w_gate_up, w_down, final_gain):
    batch, seq, d = x_prompt.shape
    dec_batch, dec_seq, _ = x_sample.shape
    depth = w_mod.shape[0]

    cvec = jnp.concatenate(
        [c_ctx[None, :], c, jnp.zeros((MOD_GROUPS - 1 - dec_batch, d), F32)], axis=0)
    mods = _modulation(cvec, w_mod, b_mod).reshape(depth, MOD_GROUPS, 1, 6 * d)
    rope = _rope_tables(dec_seq)
    w_in = {False: w_in_even.astype(BF16), True: w_in_odd.astype(BF16)}
    w_out = {False: w_out_even.astype(BF16), True: w_out_odd.astype(BF16)}
    w_gu = w_gate_up.astype(BF16)
    w_dn = w_down.astype(BF16)

    ctx = x_prompt.reshape(batch * seq, d)
    lat = x_sample.reshape(dec_batch * dec_seq, d)
    states = {name: [] for name in ("a_k", "a_v", "b_k", "b_v", "c_k", "c_v")}

    for layer in range(depth):
        odd = layer % 2 == 1
        sub = layer // 2
        gain1, gain2 = norm_gain[layer, 0], norm_gain[layer, 1]
        pre = functools.partial(_pre_attention, mods=mods, layer=layer, gain=gain1, w=w_in[odd],
                                sub=sub, odd=odd)
        if not odd:
            qk_c, ka, va, kb, vb = pre(ctx, is_lat=False, seq=seq)
            qk_l, vat_l, vbt_l = pre(lat, is_lat=True, seq=dec_seq, rope=rope)
            for name, y in (("a_k", ka), ("a_v", va), ("b_k", kb), ("b_v", vb)):
                states[name].append(_state(y))
            o_ctx = [_ctx_even_attention(qk_c, va, vb, sink_b[sub], seq)]
            o_lat = [
                _lat_a_attention(qk_l, vat_l, _feature_major(cache_a_k), _feature_major(cache_a_v),
                                 sub, _na_bias_sources(rpb_a[sub]), dec_seq),
                _lat_b_attention(qk_l, vbt_l, _feature_major(cache_b_k), _feature_major(cache_b_v),
                                 sub, sink_b[sub], dec_seq),
            ]
        else:
            per = LANES // HEAD_DIM
            head_gains = (jnp.tile(q_norm_c[sub], per).reshape(1, LANES),
                          jnp.tile(k_norm_c[sub], per).reshape(1, LANES))
            qk_c, kc, vc = pre(ctx, is_lat=False, seq=seq, head_gains=head_gains)
            qk_l, vct_l = pre(lat, is_lat=True, seq=dec_seq, rope=rope, head_gains=head_gains)
            states["c_k"].append(_state(kc))
            states["c_v"].append(_state(vc))
            o_ctx = [_ctx_odd_attention(qk_c, vc, seq)]
            o_lat = [_lat_c_attention(qk_l, vct_l, _feature_major(cache_c_k),
                                      _feature_major(cache_c_v), sub, dec_seq)]
        fg = final_gain if layer == depth - 1 else None
        post = functools.partial(_post_attention, mods=mods, layer=layer, gain=gain2,
                                 w_out=w_out[odd], sub=sub, w_gu=w_gu, w_down=w_dn, final_gain=fg)
        ctx = post(o_ctx, ctx, is_lat=False, seq=seq)
        lat = post(o_lat, lat, is_lat=True, seq=dec_seq)

    return (ctx.reshape(batch, seq, d), lat.reshape(dec_batch, dec_seq, d),
            jnp.stack(states["a_k"], axis=1), jnp.stack(states["a_v"], axis=1),
            jnp.stack(states["b_k"], axis=1), jnp.stack(states["b_v"], axis=1),
            jnp.stack(states["c_k"], axis=1), jnp.stack(states["c_v"], axis=1))
```

```python
import functools
import math

import jax
import jax.numpy as jnp
import numpy as np
from jax import lax
from jax.experimental import pallas as pl
from jax.experimental.pallas import tpu as pltpu

F32 = jnp.float32
BF16 = jnp.bfloat16

D_MODEL = 1024
GRID_W = 64
HEAD_DIM = 64
A_HEADS = 8
B_HEADS = 8
B_KV_HEADS = 2
C_HEADS = 16
C_KV_HEADS = 4
NA_WIN_H = 8
NA_WIN_W = 16
B_WINDOW = 128
ROPE_THETA = 10000.0
RMS_EPS = 1e-6
NEG_INF = -1e30
LOG2E = math.log2(math.e)
QK_SCALE = LOG2E / math.sqrt(HEAD_DIM)

LANES = 128
TOKEN_TILE = 512
Q_TILE = 256
CTX_BATCHES = 2
ONES_ROWS = 16
KEY_CHUNK = 256
FF_CHUNK = 256
MOD_GROUPS = 16
VMEM_LIMIT = 56 * 1024 * 1024


def _params(n_axes, vmem=VMEM_LIMIT):
    return pltpu.CompilerParams(
        dimension_semantics=("arbitrary",) * n_axes, vmem_limit_bytes=vmem)


def _resident(shape):
    nd = len(shape)
    return pl.BlockSpec(shape, lambda *_: (0,) * nd, pipeline_mode=pl.Buffered(1))


def _resident_layer(stacked, layer):
    return pl.BlockSpec((None,) + stacked.shape[1:], lambda *_: (layer, 0, 0),
                        pipeline_mode=pl.Buffered(1))


def _mod_kernel(c_ref, w_ref, b_ref, o_ref):
    c = c_ref[...]
    s = (c * jax.nn.sigmoid(c)).astype(BF16)
    o_ref[0] = jnp.dot(s, w_ref[0].astype(BF16), preferred_element_type=F32) + b_ref[0]


def _modulation(cvec, w_mod, b_mod):
    depth, d, n = w_mod.shape
    tn = 1536
    return pl.pallas_call(
        _mod_kernel,
        grid=(depth, n // tn),
        in_specs=[
            pl.BlockSpec((MOD_GROUPS, d), lambda l, j: (0, 0)),
            pl.BlockSpec((1, d, tn), lambda l, j: (l, 0, j)),
            pl.BlockSpec((1, 1, tn), lambda l, j: (l, 0, j)),
        ],
        out_specs=pl.BlockSpec((1, MOD_GROUPS, tn), lambda l, j: (l, 0, j)),
        out_shape=jax.ShapeDtypeStruct((depth, MOD_GROUPS, n), F32),
        compiler_params=_params(2),
        name="modulation",
    )(cvec, w_mod, b_mod.reshape(depth, 1, n))


def _mod_spec(layer, which, group_of_step):
    return pl.BlockSpec((1, 1, 1, D_MODEL), lambda i: (layer, group_of_step(i), 0, which))


def _adaln(x, gain, shift, scale):
    ms = jnp.mean(x * x, axis=-1, keepdims=True)
    return (x * lax.rsqrt(ms + RMS_EPS)) * gain * (1.0 + scale) + shift


def _rope(y, cos, sin_lo, sin_hi):
    outs = []
    for c in range(y.shape[1] // LANES):
        yc = y[:, c * LANES:(c + 1) * LANES]
        outs.append(yc * cos
                    + pltpu.roll(yc, LANES - 16, 1) * sin_lo
                    + pltpu.roll(yc, 16, 1) * sin_hi)
    return outs[0] if len(outs) == 1 else jnp.concatenate(outs, axis=1)


def _head_rms_norm(y, gain):
    first = lax.broadcasted_iota(jnp.int32, (1, LANES), 1) < HEAD_DIM
    outs = []
    for c in range(y.shape[1] // LANES):
        yc = y[:, c * LANES:(c + 1) * LANES]
        sq = yc * yc
        s0 = jnp.sum(jnp.where(first, sq, 0.0), axis=-1, keepdims=True)
        s1 = jnp.sum(jnp.where(first, 0.0, sq), axis=-1, keepdims=True)
        ms = jnp.where(first, s0, s1) * (1.0 / HEAD_DIM)
        outs.append(yc * lax.rsqrt(ms + RMS_EPS) * gain)
    return outs[0] if len(outs) == 1 else jnp.concatenate(outs, axis=1)


_NT = (((1,), (1,)), ((), ()))


def _with_ones_rows(vt):
    return jnp.concatenate([vt.astype(BF16), jnp.ones((ONES_ROWS, vt.shape[1]), BF16)], axis=0)


def _attend(segments, q, rows, group, sink=None):
    scores = []
    for k, _, bias, valid in segments:
        s = _scores(k, q)
        if bias is not None:
            s = s + bias
        if valid is not None:
            s = jnp.where(valid, s, NEG_INF)
        scores.append(s)
    return _softmax_values(scores, [seg[1] for seg in segments], rows, group, sink)


def _scores(k, q):
    return lax.dot_general(k, q, _NT, preferred_element_type=F32)


def _softmax_values(scores, vts, rows, group, sink=None):
    m = functools.reduce(jnp.maximum, [jnp.max(s, axis=0, keepdims=True) for s in scores])
    if sink is not None:
        m = jnp.maximum(m, sink)
    ot = None
    for s, vt_ones in zip(scores, vts):
        part = jnp.dot(vt_ones, jnp.exp2(s - m).astype(BF16), preferred_element_type=F32)
        ot = part if ot is None else ot + part
    denom = ot[HEAD_DIM:HEAD_DIM + 1, :]
    if sink is not None:
        denom = denom + jnp.exp2(sink - m)
    ot = ot[:HEAD_DIM, :] / denom
    return jnp.concatenate([ot[:, g * rows:(g + 1) * rows].T for g in range(group)], axis=1)


def _pipelined_item(cur, nxt, n_keys, *, q_next, keys_next, vt_cur, bias_next=None):
    s_cur, m_cur = cur
    s_nxt, m_nxt = nxt
    m = None if s_cur is None else m_cur[...]
    m_next = None
    ot = None
    for c0 in range(0, n_keys, KEY_CHUNK):
        keys = slice(c0, c0 + KEY_CHUNK)
        s = _scores(keys_next(c0), q_next)
        bias = None if bias_next is None else bias_next(c0)
        if bias is not None:
            s = s + bias
        s_nxt[keys, :] = s
        m_c = jnp.max(s, axis=0, keepdims=True)
        m_next = m_c if m_next is None else jnp.maximum(m_next, m_c)
        if s_cur is not None:
            p = jnp.exp2(s_cur[keys, :] - m).astype(BF16)
            part = jnp.dot(vt_cur(c0), p, preferred_element_type=F32)
            ot = part if ot is None else ot + part
    m_nxt[...] = m_next
    if s_cur is None:
        return None
    return ot[:HEAD_DIM, :] / ot[HEAD_DIM:HEAD_DIM + 1, :]


def _score_slots(n_keys, m):
    return [pltpu.VMEM((n_keys, m), F32), pltpu.VMEM((n_keys, m), F32),
            pltpu.VMEM((2, 1, m), F32)]


def _stack_group(ref, rows, col0, group):
    return jnp.concatenate(
        [ref[rows, col0 + g * HEAD_DIM: col0 + (g + 1) * HEAD_DIM] for g in range(group)], axis=0)


def _sink_row(sink_ref, h0, group, rows):
    return jnp.concatenate(
        [jnp.full((1, rows), sink_ref[h0 + g] * LOG2E, F32) for g in range(group)], axis=1)


def _store_feature_major(ref, y):
    nb, heads, _, seq = ref.shape
    yt = y.T
    for b in range(nb):
        for h in range(heads):
            ref[b, h] = yt[h * HEAD_DIM:(h + 1) * HEAD_DIM, b * seq:(b + 1) * seq].astype(ref.dtype)


def _pre_even_kernel(x_ref, g_ref, sh_ref, sc_ref, w_ref, *rest, is_lat):
    if is_lat:
        cos_ref, slo_ref, shi_ref, qk_ref, va_ref, vb_ref = rest
    else:
        qk_ref, ka_ref, va_ref, kb_ref, vb_ref = rest
    h = _adaln(x_ref[...], g_ref[...], sh_ref[0, 0], sc_ref[0, 0]).astype(BF16)

    def proj(c0, c1):
        return jnp.dot(h, w_ref[:, c0:c1], preferred_element_type=F32)

    na = A_HEADS * HEAD_DIM
    nb = B_HEADS * HEAD_DIM
    nkv = B_KV_HEADS * HEAD_DIM
    qk_ref[:, 0:na] = (proj(0, na) * QK_SCALE).astype(BF16)
    ka = proj(na, 2 * na)
    qk_ref[:, na:2 * na] = ka.astype(BF16)
    va = proj(2 * na, 3 * na)
    qb = proj(3 * na, 3 * na + nb)
    if is_lat:
        qb = _rope(qb, cos_ref[...], slo_ref[...], shi_ref[...])
    qk_ref[:, 2 * na:2 * na + nb] = (qb * QK_SCALE).astype(BF16)
    kvb = proj(3 * na + nb, 3 * na + nb + 2 * nkv)
    kb, vb = kvb[:, :nkv], kvb[:, nkv:]
    if is_lat:
        kb_out = _rope(kb, cos_ref[...], slo_ref[...], shi_ref[...])
    else:
        kb_out = kb
    qk_ref[:, 2 * na + nb:2 * na + nb + nkv] = kb_out.astype(BF16)
    _store_feature_major(va_ref, va)
    _store_feature_major(vb_ref, vb)
    if not is_lat:
        _store_feature_major(ka_ref, ka)
        _store_feature_major(kb_ref, kb)


def _pre_odd_kernel(x_ref, g_ref, sh_ref, sc_ref, w_ref, qn_ref, kn_ref, *rest, is_lat):
    if is_lat:
        cos_ref, slo_ref, shi_ref, qk_ref, vc_ref = rest
    else:
        qk_ref, kc_ref, vc_ref = rest
    h = _adaln(x_ref[...], g_ref[...], sh_ref[0, 0], sc_ref[0, 0]).astype(BF16)

    def proj(c0, c1):
        return jnp.dot(h, w_ref[:, c0:c1], preferred_element_type=F32)

    nq = C_HEADS * HEAD_DIM
    nkv = C_KV_HEADS * HEAD_DIM
    q = _head_rms_norm(proj(0, nq), qn_ref[...])
    k = _head_rms_norm(proj(nq, nq + nkv), kn_ref[...])
    v = proj(nq + nkv, nq + 2 * nkv)
    if is_lat:
        q = _rope(q, cos_ref[...], slo_ref[...], shi_ref[...])
        k_out = _rope(k, cos_ref[...], slo_ref[...], shi_ref[...])
    else:
        k_out = k
        _store_feature_major(kc_ref, k)
    _store_feature_major(vc_ref, v)
    qk_ref[:, 0:nq] = (q * QK_SCALE).astype(BF16)
    qk_ref[:, nq:nq + nkv] = k_out.astype(BF16)


def _pre_attention(x, mods, layer, gain, w, sub, *, odd, is_lat, seq, rope=None,
                   head_gains=None):
    t, d = x.shape
    tm = TOKEN_TILE
    per_seq = max(seq // tm, 1)
    per_tile = max(tm // seq, 1)
    group = (lambda i: 1 + i // per_seq) if is_lat else (lambda i: 0)
    row = lambda i: (i, 0)
    in_specs = [
        pl.BlockSpec((tm, d), row),
        _resident((1, d)),
        _mod_spec(layer, 0, group),
        _mod_spec(layer, 1, group),
        _resident_layer(w, sub),
    ]
    args = [x, gain.reshape(1, d), mods, mods, w]
    if odd:
        in_specs += [_resident(a.shape) for a in head_gains]
        args += list(head_gains)
    if is_lat:
        in_specs += [pl.BlockSpec((tm, LANES), lambda i: (i % per_seq, 0))] * 3
        args += list(rope)
    if odd:
        n_qk = (C_HEADS + C_KV_HEADS) * HEAD_DIM
        lat_heads, ctx_heads = [C_KV_HEADS], [C_KV_HEADS, C_KV_HEADS]
    else:
        n_qk = (2 * A_HEADS + B_HEADS + B_KV_HEADS) * HEAD_DIM
        lat_heads, ctx_heads = [A_HEADS, B_KV_HEADS], [A_HEADS, A_HEADS, B_KV_HEADS, B_KV_HEADS]
    out_specs = [pl.BlockSpec((tm, n_qk), row)]
    out_shape = [jax.ShapeDtypeStruct((t, n_qk), BF16)]
    for heads in (lat_heads if is_lat else ctx_heads):
        blk = (per_tile, heads, HEAD_DIM, min(tm, seq))
        out_specs.append(pl.BlockSpec(blk, lambda i: (i // per_seq, 0, 0, i % per_seq)))
        out_shape.append(jax.ShapeDtypeStruct((t // seq, heads, HEAD_DIM, seq),
                                              BF16 if is_lat else F32))
    body = _pre_odd_kernel if odd else _pre_even_kernel
    return pl.pallas_call(
        functools.partial(body, is_lat=is_lat),
        grid=(t // tm,),
        in_specs=in_specs,
        out_specs=out_specs,
        out_shape=out_shape,
        compiler_params=_params(1),
        name=f"pre_{'odd' if odd else 'even'}_{'lat' if is_lat else 'ctx'}",
    )(*args)


def _batch_spec(arr, nb=1):
    nd = arr.ndim
    return pl.BlockSpec((nb,) + arr.shape[1:], lambda b: (b,) + (0,) * (nd - 1))


def _cache_spec(cache, layer):
    blk = (1, 1) + cache.shape[2:]
    return pl.BlockSpec(blk, lambda b: (b, layer, 0, 0, 0))


def _feature_major(cache):
    return cache.transpose(0, 1, 3, 4, 2)


def _cache_keys(ref, h):
    return ref[0, 0, h].T.astype(BF16)


def _ctx_even_attn_kernel(sink_ref, qk_ref, vat_ref, vbt_ref, o_ref):
    nb = vat_ref.shape[0]
    n = qk_ref.shape[0] // nb
    na = A_HEADS * HEAD_DIM
    group = B_HEADS // B_KV_HEADS
    qb0 = 2 * na
    kb0 = qb0 + B_HEADS * HEAD_DIM
    for b in range(nb):
        rows = slice(b * n, (b + 1) * n)
        for pair in range(A_HEADS // 2):
            outs = []
            for h in (2 * pair, 2 * pair + 1):
                c = h * HEAD_DIM
                seg = (qk_ref[rows, na + c:na + c + HEAD_DIM], _with_ones_rows(vat_ref[b, h]),
                       None, None)
                outs.append(_attend([seg], qk_ref[rows, c:c + HEAD_DIM], n, 1))
            o_ref[rows, pair * LANES:(pair + 1) * LANES] = (
                jnp.concatenate(outs, axis=1).astype(BF16))
        for kv in range(B_KV_HEADS):
            q = _stack_group(qk_ref, rows, qb0 + kv * group * HEAD_DIM, group)
            seg = (qk_ref[rows, kb0 + kv * HEAD_DIM:kb0 + (kv + 1) * HEAD_DIM],
                   _with_ones_rows(vbt_ref[b, kv]), None, None)
            o = _attend([seg], q, n, group, _sink_row(sink_ref, kv * group, group, n))
            c0 = na + kv * group * HEAD_DIM
            o_ref[rows, c0:c0 + group * HEAD_DIM] = o.astype(BF16)


def _ctx_even_attention(qk, vat, vbt, sink, seq):
    t, n = qk.shape
    width = (A_HEADS + B_HEADS) * HEAD_DIM
    nb = CTX_BATCHES
    return pl.pallas_call(
        _ctx_even_attn_kernel,
        grid=(t // (nb * seq),),
        in_specs=[pl.BlockSpec(memory_space=pltpu.SMEM),
                  pl.BlockSpec((nb * seq, n), lambda b: (b, 0)),
                  _batch_spec(vat, nb), _batch_spec(vbt, nb)],
        out_specs=pl.BlockSpec((nb * seq, width), lambda b: (b, 0)),
        out_shape=jax.ShapeDtypeStruct((t, width), BF16),
        compiler_params=_params(1),
        name="attn_even_ctx",
    )(sink, qk, vat, vbt)


def _na_bias_tiles(src_ref, h):
    kc = lax.broadcasted_iota(jnp.int32, (GRID_W, LANES), 0)
    qc = lax.rem(lax.broadcasted_iota(jnp.int32, (GRID_W, LANES), 1), GRID_W)
    cs = jnp.clip(qc - NA_WIN_W // 2, 0, GRID_W - NA_WIN_W)
    col_ok = (kc >= cs) & (kc < cs + NA_WIN_W)
    tiles = []
    for i in range(2 * NA_WIN_H):
        src = jnp.broadcast_to(src_ref[h, i:i + 1, :], (GRID_W, LANES))
        tiles.append(jnp.where(col_ok, pltpu.roll(src, 0, 1, stride=1, stride_axis=0), NEG_INF))
    return tiles


def _na_bias_block(tiles, key_rows, grid_rows):
    wh = min(NA_WIN_H, grid_rows)
    first_row = lax.broadcasted_iota(jnp.int32, (GRID_W, LANES), 1) < GRID_W

    def row_start(qr):
        return min(max(qr - wh // 2, 0), grid_rows - wh)

    strips = []
    for kr in key_rows:
        parts = []
        for qr in range(0, grid_rows, 2):
            ok0 = row_start(qr) <= kr < row_start(qr) + wh
            ok1 = row_start(qr + 1) <= kr < row_start(qr + 1) + wh
            i = kr - qr + NA_WIN_H - 1
            if ok0 and ok1:
                parts.append(tiles[i])
            elif ok0:
                parts.append(jnp.where(first_row, tiles[i], NEG_INF))
            elif ok1:
                parts.append(jnp.where(first_row, NEG_INF, tiles[i]))
            else:
                parts.append(jnp.full((GRID_W, LANES), NEG_INF, F32))
        strips.append(jnp.concatenate(parts, axis=1))
    return jnp.concatenate(strips, axis=0)


def _lat_a_kernel(src_ref, q_ref, k_ref, vt_ref, ck_ref, cv_ref, o_ref,
                  qs_ref, kall_ref, vtall_ref, os_ref, s0_ref, s1_ref, m_ref):
    n = q_ref.shape[0]
    past = ck_ref.shape[4]
    grid_rows = n // GRID_W
    for h in range(A_HEADS):
        cols = slice(h * HEAD_DIM, (h + 1) * HEAD_DIM)
        qs_ref[h] = q_ref[:, cols]
        kall_ref[h, 0:n, :] = k_ref[:, cols]
        kall_ref[h, n:n + past, :] = _cache_keys(ck_ref, h)
        vtall_ref[h, 0:HEAD_DIM, 0:n] = vt_ref[0, h]
        vtall_ref[h, 0:HEAD_DIM, n:n + past] = cv_ref[0, 0, h].astype(BF16)
        vtall_ref[h, HEAD_DIM:, :] = jnp.ones((ONES_ROWS, n + past), BF16)

    slots = ((s0_ref, m_ref.at[0]), (s1_ref, m_ref.at[1]))

    def step(cur, nxt, h, h_next):
        tiles = _na_bias_tiles(src_ref, h_next)

        def bias(c0):
            if c0 >= n:
                return None
            return _na_bias_block(tiles, range(c0 // GRID_W, (c0 + KEY_CHUNK) // GRID_W), grid_rows)

        return _pipelined_item(
            cur, nxt, n + past, q_next=qs_ref[h_next],
            keys_next=lambda c0: kall_ref[h_next, c0:c0 + KEY_CHUNK, :],
            vt_cur=lambda c0: vtall_ref[h, :, c0:c0 + KEY_CHUNK], bias_next=bias)

    step((None, None), slots[0], None, 0)

    def head_pair(p, carry):
        h0 = 2 * p
        os_ref[h0] = step(slots[0], slots[1], h0, h0 + 1).T.astype(BF16)
        os_ref[h0 + 1] = step(slots[1], slots[0], h0 + 1, lax.rem(h0 + 2, A_HEADS)).T.astype(BF16)
        return carry

    lax.fori_loop(0, A_HEADS // 2, head_pair, 0)
    for pair in range(A_HEADS // 2):
        o_ref[:, pair * LANES:(pair + 1) * LANES] = jnp.concatenate(
            [os_ref[2 * pair], os_ref[2 * pair + 1]], axis=1)


def _lat_a_attention(qk, vt, cache_k, cache_v, e, bias_src, seq):
    t = qk.shape[0]
    past = cache_k.shape[4]
    na = A_HEADS * HEAD_DIM
    return pl.pallas_call(
        _lat_a_kernel,
        grid=(t // seq,),
        in_specs=[
            _resident(bias_src.shape),
            pl.BlockSpec((seq, na), lambda b: (b, 0)),
            pl.BlockSpec((seq, na), lambda b: (b, 1)),
            _batch_spec(vt),
            _cache_spec(cache_k, e),
            _cache_spec(cache_v, e),
        ],
        out_specs=pl.BlockSpec((seq, na), lambda b: (b, 0)),
        out_shape=jax.ShapeDtypeStruct((t, na), BF16),
        scratch_shapes=[pltpu.VMEM((A_HEADS, seq, HEAD_DIM), BF16),
                        pltpu.VMEM((A_HEADS, seq + past, HEAD_DIM), BF16),
                        pltpu.VMEM((A_HEADS, HEAD_DIM + ONES_ROWS, seq + past), BF16),
                        pltpu.VMEM((A_HEADS, seq, HEAD_DIM), BF16)]
                       + _score_slots(seq + past, seq),
        compiler_params=_params(1),
        name="attn_even_lat_a",
    )(bias_src, qk, qk, vt, cache_k, cache_v)


def _lat_b_kernel(sink_ref, q_ref, k_ref, vt_ref, ck_ref, cv_ref, o_ref):
    n = q_ref.shape[0]
    group = B_HEADS // B_KV_HEADS
    win = 2 * Q_TILE
    ctx = [(_cache_keys(ck_ref, kv), _with_ones_rows(cv_ref[0, 0, kv]), None, None)
           for kv in range(B_KV_HEADS)]
    for j in range(n // Q_TILE):
        lo = min(max(Q_TILE * j - B_WINDOW, 0), n - win)
        q_rows = slice(j * Q_TILE, (j + 1) * Q_TILE)
        kpos = lo + lax.broadcasted_iota(jnp.int32, (win, 1), 0)
        qpos = j * Q_TILE + lax.broadcasted_iota(jnp.int32, (1, group * Q_TILE), 1) % Q_TILE
        valid = jnp.abs(qpos - kpos) <= B_WINDOW
        for kv in range(B_KV_HEADS):
            q = _stack_group(q_ref, q_rows, kv * group * HEAD_DIM, group)
            local = (k_ref[lo:lo + win, kv * HEAD_DIM:(kv + 1) * HEAD_DIM],
                     _with_ones_rows(vt_ref[0, kv, :, lo:lo + win]), None, valid)
            o = _attend([local, ctx[kv]], q, Q_TILE, group,
                        _sink_row(sink_ref, kv * group, group, Q_TILE))
            c0 = kv * group * HEAD_DIM
            o_ref[q_rows, c0:c0 + group * HEAD_DIM] = o.astype(BF16)


def _lat_b_attention(qk, vt, cache_k, cache_v, e, sink, seq):
    t = qk.shape[0]
    na = A_HEADS * HEAD_DIM
    nb = B_HEADS * HEAD_DIM
    nkv = B_KV_HEADS * HEAD_DIM
    return pl.pallas_call(
        _lat_b_kernel,
        grid=(t // seq,),
        in_specs=[
            pl.BlockSpec(memory_space=pltpu.SMEM),
            pl.BlockSpec((seq, nb), lambda b: (b, 2 * na // nb)),
            pl.BlockSpec((seq, nkv), lambda b: (b, (2 * na + nb) // nkv)),
            _batch_spec(vt),
            _cache_spec(cache_k, e),
            _cache_spec(cache_v, e),
        ],
        out_specs=pl.BlockSpec((seq, nb), lambda b: (b, 0)),
        out_shape=jax.ShapeDtypeStruct((t, nb), BF16),
        compiler_params=_params(1),
        name="attn_even_lat_b",
    )(sink, qk, qk, vt, cache_k, cache_v)


def _ctx_odd_attn_kernel(qk_ref, vt_ref, o_ref):
    nb = vt_ref.shape[0]
    n = qk_ref.shape[0] // nb
    group = C_HEADS // C_KV_HEADS
    nq = C_HEADS * HEAD_DIM
    for b in range(nb):
        rows = slice(b * n, (b + 1) * n)
        for kv in range(C_KV_HEADS):
            q = _stack_group(qk_ref, rows, kv * group * HEAD_DIM, group)
            seg = (qk_ref[rows, nq + kv * HEAD_DIM:nq + (kv + 1) * HEAD_DIM],
                   _with_ones_rows(vt_ref[b, kv]), None, None)
            c0 = kv * group * HEAD_DIM
            o_ref[rows, c0:c0 + group * HEAD_DIM] = _attend([seg], q, n, group).astype(BF16)


def _ctx_odd_attention(qk, vt, seq):
    t, n = qk.shape
    width = C_HEADS * HEAD_DIM
    nb = CTX_BATCHES
    return pl.pallas_call(
        _ctx_odd_attn_kernel,
        grid=(t // (nb * seq),),
        in_specs=[pl.BlockSpec((nb * seq, n), lambda b: (b, 0)), _batch_spec(vt, nb)],
        out_specs=pl.BlockSpec((nb * seq, width), lambda b: (b, 0)),
        out_shape=jax.ShapeDtypeStruct((t, width), BF16),
        compiler_params=_params(1),
        name="attn_odd_ctx",
    )(qk, vt)


def _lat_c_kernel(qk_ref, vt_ref, ck_ref, cv_ref, o_ref, kall_ref, vtall_ref, s0_ref, s1_ref,
                  m_ref):
    n = qk_ref.shape[0]
    past = ck_ref.shape[4]
    group = C_HEADS // C_KV_HEADS
    nq = C_HEADS * HEAD_DIM
    n_blocks = n // Q_TILE
    for kv in range(C_KV_HEADS):
        kall_ref[kv, 0:past, :] = _cache_keys(ck_ref, kv)
        kall_ref[kv, past:past + n, :] = qk_ref[:, nq + kv * HEAD_DIM:nq + (kv + 1) * HEAD_DIM]
        vtall_ref[kv, 0:HEAD_DIM, 0:past] = cv_ref[0, 0, kv].astype(BF16)
        vtall_ref[kv, 0:HEAD_DIM, past:past + n] = vt_ref[0, kv]
        vtall_ref[kv, HEAD_DIM:, :] = jnp.ones((ONES_ROWS, past + n), BF16)

    def rows_of(j):
        if isinstance(j, int):
            return slice(j * Q_TILE, (j + 1) * Q_TILE)
        return pl.ds(pl.multiple_of(j * Q_TILE, Q_TILE), Q_TILE)

    slots = ((s0_ref, m_ref.at[0]), (s1_ref, m_ref.at[1]))

    def item(cur, nxt, kv, j_next, kv_next):
        ot = _pipelined_item(
            cur, nxt, past + n,
            q_next=_stack_group(qk_ref, rows_of(j_next), kv_next * group * HEAD_DIM, group),
            keys_next=lambda c0: kall_ref[kv_next, c0:c0 + KEY_CHUNK, :],
            vt_cur=lambda c0: vtall_ref[kv, :, c0:c0 + KEY_CHUNK])
        if ot is None:
            return None
        return jnp.concatenate(
            [ot[:, g * Q_TILE:(g + 1) * Q_TILE].T for g in range(group)], axis=1)

    item((None, None), slots[0], None, 0, 0)

    def q_block(j, carry):
        for kv in range(C_KV_HEADS):
            last = kv + 1 == C_KV_HEADS
            o = item(slots[kv % 2], slots[(kv + 1) % 2], kv,
                     lax.rem(j + 1, n_blocks) if last else j, 0 if last else kv + 1)
            c0 = kv * group * HEAD_DIM
            o_ref[rows_of(j), c0:c0 + group * HEAD_DIM] = o.astype(BF16)
        return carry

    lax.fori_loop(0, n_blocks, q_block, 0)


def _lat_c_attention(qk, vt, cache_k, cache_v, o, seq):
    t, n = qk.shape
    past = cache_k.shape[4]
    width = C_HEADS * HEAD_DIM
    return pl.pallas_call(
        _lat_c_kernel,
        grid=(t // seq,),
        in_specs=[
            pl.BlockSpec((seq, n), lambda b: (b, 0)),
            _batch_spec(vt),
            _cache_spec(cache_k, o),
            _cache_spec(cache_v, o),
        ],
        out_specs=pl.BlockSpec((seq, width), lambda b: (b, 0)),
        out_shape=jax.ShapeDtypeStruct((t, width), BF16),
        scratch_shapes=[pltpu.VMEM((C_KV_HEADS, past + seq, HEAD_DIM), BF16),
                        pltpu.VMEM((C_KV_HEADS, HEAD_DIM + ONES_ROWS, past + seq), BF16)]
                       + _score_slots(past + seq, C_HEADS // C_KV_HEADS * Q_TILE),
        compiler_params=_params(1),
        name="attn_odd_lat",
    )(qk, vt, cache_k, cache_v)


def _post_kernel(*refs, n_parts, final):
    o_refs = refs[:n_parts]
    (x_ref, wo_ref, g1_ref, sh_ref, sc_ref, g2_ref, gain_ref, wgu_ref, wd_ref) = refs[n_parts:n_parts + 9]
    rest = refs[n_parts + 9:]
    if final:
        fg_ref, out_ref, act_ref = rest
    else:
        out_ref, act_ref = rest
    mix = None
    r0 = 0
    for o_ref in o_refs:
        kk = o_ref.shape[1]
        part = jnp.dot(o_ref[...], wo_ref[r0:r0 + kk, :], preferred_element_type=F32)
        mix = part if mix is None else mix + part
        r0 += kk
    x1 = x_ref[...] + g1_ref[0, 0] * mix
    h = _adaln(x1, gain_ref[...], sh_ref[0, 0], sc_ref[0, 0]).astype(BF16)
    d_ff = wd_ref.shape[0]
    for j in range(d_ff // FF_CHUNK):
        c0 = j * FF_CHUNK
        gate = jnp.dot(h, wgu_ref[:, c0:c0 + FF_CHUNK], preferred_element_type=F32)
        up = jnp.dot(h, wgu_ref[:, d_ff + c0:d_ff + c0 + FF_CHUNK], preferred_element_type=F32)
        act_ref[:, c0:c0 + FF_CHUNK] = (gate * jax.nn.sigmoid(gate) * up).astype(BF16)
    ffn = jnp.dot(act_ref[...], wd_ref[...], preferred_element_type=F32)
    x2 = x1 + g2_ref[0, 0] * ffn
    if final:
        ms = jnp.mean(x2 * x2, axis=-1, keepdims=True)
        x2 = (x2 * lax.rsqrt(ms + RMS_EPS)) * fg_ref[...]
    out_ref[...] = x2


def _post_attention(o_parts, x, mods, layer, gain, w_out, sub, w_gu, w_down, *, is_lat, seq,
                    final_gain=None):
    t, d = x.shape
    tm = TOKEN_TILE
    per_seq = max(seq // tm, 1)
    group = (lambda i: 1 + i // per_seq) if is_lat else (lambda i: 0)
    row = lambda i: (i, 0)
    d_ff = w_down.shape[1]
    final = final_gain is not None
    in_specs = [pl.BlockSpec((tm, o.shape[1]), row) for o in o_parts]
    in_specs += [
        pl.BlockSpec((tm, d), row),
        _resident_layer(w_out, sub),
        _mod_spec(layer, 2, group),
        _mod_spec(layer, 3, group),
        _mod_spec(layer, 4, group),
        _mod_spec(layer, 5, group),
        _resident((1, d)),
        _resident_layer(w_gu, layer),
        _resident_layer(w_down, layer),
    ]
    args = list(o_parts) + [x, w_out, mods, mods, mods, mods, gain.reshape(1, d), w_gu, w_down]
    if final:
        in_specs.append(_resident((1, d)))
        args.append(final_gain.reshape(1, d))
    return pl.pallas_call(
        functools.partial(_post_kernel, n_parts=len(o_parts), final=final),
        grid=(t // tm,),
        in_specs=in_specs,
        out_specs=pl.BlockSpec((tm, d), row),
        out_shape=jax.ShapeDtypeStruct((t, d), F32),
        scratch_shapes=[pltpu.VMEM((tm, d_ff), BF16)],
        compiler_params=_params(1),
        name=f"post_{'lat' if is_lat else 'ctx'}{'_final' if final else ''}",
    )(*args)


def _rope_tables(n):
    t = np.arange(n)
    row = (t // GRID_W).astype(np.float32)
    col = (t % GRID_W).astype(np.float32)
    half = HEAD_DIM // 2
    inv_freq = np.float32(ROPE_THETA) ** (-np.arange(0, half, 2, dtype=np.float32) / np.float32(half))
    lane = np.arange(LANES)
    in_head = lane % HEAD_DIM
    pos = np.where((in_head < half)[None, :], row[:, None], col[:, None])
    ang = (pos * inv_freq[in_head % (half // 2)][None, :]).astype(np.float32)
    first = ((in_head % half) < half // 2)[None, :]
    cos = np.cos(ang)
    sin = np.sin(ang)
    zero = np.float32(0.0)
    return tuple(jnp.asarray(a, F32) for a in
                 (cos, np.where(first, -sin, zero), np.where(first, zero, sin)))


def _na_bias_sources(rpb):
    h, _, nb = rpb.shape
    w = NA_WIN_W - 1
    rp = jnp.pad(rpb[:, :, ::-1] * LOG2E, ((0, 0), (1, 1), (0, 0)))
    this, prev = rp[:, 1:], rp[:, :-1]
    z = jnp.zeros((h, 2 * NA_WIN_H, LANES // 2 - nb), F32)
    return jnp.concatenate([this[:, :, w:], z, prev, z, this[:, :, :w]], axis=-1)


def _state(y):
    return y.transpose(0, 3, 1, 2)


def kernel(x_prompt, x_sample, cache_a_k, cache_a_v, cache_b_k, cache_b_v, cache_c_k, cache_c_v,
           c, c_ctx, norm_gain, w_mod, b_mod, w_in_even, w_out_even, rpb_a, sink_b,
           w_in_odd, w_out_odd, q_norm_c, k_norm_c, w_gate_up, w_down, final_gain):
    batch, seq, d = x_prompt.shape
    dec_batch, dec_seq, _ = x_sample.shape
    depth = w_mod.shape[0]

    cvec = jnp.concatenate(
        [c_ctx[None, :], c, jnp.zeros((MOD_GROUPS - 1 - dec_batch, d), F32)], axis=0)
    mods = _modulation(cvec, w_mod, b_mod).reshape(depth, MOD_GROUPS, 1, 6 * d)
    rope = _rope_tables(dec_seq)
    w_in = {False: w_in_even.astype(BF16), True: w_in_odd.astype(BF16)}
    w_out = {False: w_out_even.astype(BF16), True: w_out_odd.astype(BF16)}
    w_gu = w_gate_up.astype(BF16)
    w_dn = w_down.astype(BF16)

    ctx = x_prompt.reshape(batch * seq, d)
    lat = x_sample.reshape(dec_batch * dec_seq, d)
    states = {name: [] for name in ("a_k", "a_v", "b_k", "b_v", "c_k", "c_v")}

    for layer in range(depth):
        odd = layer % 2 == 1
        sub = layer // 2
        gain1, gain2 = norm_gain[layer, 0], norm_gain[layer, 1]
        pre = functools.partial(_pre_attention, mods=mods, layer=layer, gain=gain1, w=w_in[odd],
                                sub=sub, odd=odd)
        if not odd:
            qk_c, ka, va, kb, vb = pre(ctx, is_lat=False, seq=seq)
            qk_l, vat_l, vbt_l = pre(lat, is_lat=True, seq=dec_seq, rope=rope)
            for name, y in (("a_k", ka), ("a_v", va), ("b_k", kb), ("b_v", vb)):
                states[name].append(_state(y))
            o_ctx = [_ctx_even_attention(qk_c, va, vb, sink_b[sub], seq)]
            o_lat = [
                _lat_a_attention(qk_l, vat_l, _feature_major(cache_a_k), _feature_major(cache_a_v),
                                 sub, _na_bias_sources(rpb_a[sub]), dec_seq),
                _lat_b_attention(qk_l, vbt_l, _feature_major(cache_b_k), _feature_major(cache_b_v),
                                 sub, sink_b[sub], dec_seq),
            ]
        else:
            per = LANES // HEAD_DIM
            head_gains = (jnp.tile(q_norm_c[sub], per).reshape(1, LANES),
                          jnp.tile(k_norm_c[sub], per).reshape(1, LANES))
            qk_c, kc, vc = pre(ctx, is_lat=False, seq=seq, head_gains=head_gains)
            qk_l, vct_l = pre(lat, is_lat=True, seq=dec_seq, rope=rope, head_gains=head_gains)
            states["c_k"].append(_state(kc))
            states["c_v"].append(_state(vc))
            o_ctx = [_ctx_odd_attention(qk_c, vc, seq)]
            o_lat = [_lat_c_attention(qk_l, vct_l, _feature_major(cache_c_k),
                                      _feature_major(cache_c_v), sub, dec_seq)]
        fg = final_gain if layer == depth - 1 else None
        post = functools.partial(_post_attention, mods=mods, layer=layer, gain=gain2,
                                 w_out=w_out[odd], sub=sub, w_gu=w_gu, w_down=w_dn, final_gain=fg)
        ctx = post(o_ctx, ctx, is_lat=False, seq=seq)
        lat = post(o_lat, lat, is_lat=True, seq=dec_seq)

    return (ctx.reshape(batch, seq, d), lat.reshape(dec_batch, dec_seq, d),
            jnp.stack(states["a_k"], axis=1), jnp.stack(states["a_v"], axis=1),
            jnp.stack(states["b_k"], axis=1), jnp.stack(states["b_v"], axis=1),
            jnp.stack(states["c_k"], axis=1), jnp.stack(states["c_v"], axis=1))
```

```python
import functools
import math

import jax
import jax.numpy as jnp
import numpy as np
from jax import lax
from jax.experimental import pallas as pl
from jax.experimental.pallas import tpu as pltpu

F32 = jnp.float32
BF16 = jnp.bfloat16

D_MODEL = 1024
GRID_W = 64
HEAD_DIM = 64
A_HEADS = 8
B_HEADS = 8
B_KV_HEADS = 2
C_HEADS = 16
C_KV_HEADS = 4
NA_WIN_H = 8
NA_WIN_W = 16
B_WINDOW = 128
ROPE_THETA = 10000.0
RMS_EPS = 1e-6
NEG_INF = -1e30
LOG2E = math.log2(math.e)
QK_SCALE = LOG2E / math.sqrt(HEAD_DIM)

LANES = 128
TOKEN_TILE = 512
Q_TILE = 256
CTX_BATCHES = 2
ONES_ROWS = 16
KEY_CHUNK = 256
FF_CHUNK = 256
MOD_GROUPS = 16
VMEM_LIMIT = 56 * 1024 * 1024


def _params(n_axes, vmem=VMEM_LIMIT):
    return pltpu.CompilerParams(
        dimension_semantics=("arbitrary",) * n_axes, vmem_limit_bytes=vmem)


def _resident(shape):
    nd = len(shape)
    return pl.BlockSpec(shape, lambda *_: (0,) * nd, pipeline_mode=pl.Buffered(1))


def _resident_layer(stacked, layer):
    return pl.BlockSpec((None,) + stacked.shape[1:], lambda *_: (layer, 0, 0),
                        pipeline_mode=pl.Buffered(1))


def _mod_kernel(c_ref, w_ref, b_ref, o_ref):
    c = c_ref[...]
    s = (c * jax.nn.sigmoid(c)).astype(BF16)
    o_ref[0] = jnp.dot(s, w_ref[0].astype(BF16), preferred_element_type=F32) + b_ref[0]


def _modulation(cvec, w_mod, b_mod):
    depth, d, n = w_mod.shape
    tn = 1536
    return pl.pallas_call(
        _mod_kernel,
        grid=(depth, n // tn),
        in_specs=[
            pl.BlockSpec((MOD_GROUPS, d), lambda l, j: (0, 0)),
            pl.BlockSpec((1, d, tn), lambda l, j: (l, 0, j)),
            pl.BlockSpec((1, 1, tn), lambda l, j: (l, 0, j)),
        ],
        out_specs=pl.BlockSpec((1, MOD_GROUPS, tn), lambda l, j: (l, 0, j)),
        out_shape=jax.ShapeDtypeStruct((depth, MOD_GROUPS, n), F32),
        compiler_params=_params(2),
        name="modulation",
    )(cvec, w_mod, b_mod.reshape(depth, 1, n))


def _mod_spec(layer, which, group_of_step):
    return pl.BlockSpec((1, 1, 1, D_MODEL), lambda i: (layer, group_of_step(i), 0, which))


def _adaln(x, gain, shift, scale):
    ms = jnp.mean(x * x, axis=-1, keepdims=True)
    return (x * lax.rsqrt(ms + RMS_EPS)) * gain * (1.0 + scale) + shift


def _rope(y, cos, sin_lo, sin_hi):
    outs = []
    for c in range(y.shape[1] // LANES):
        yc = y[:, c * LANES:(c + 1) * LANES]
        outs.append(yc * cos
                    + pltpu.roll(yc, LANES - 16, 1) * sin_lo
                    + pltpu.roll(yc, 16, 1) * sin_hi)
    return outs[0] if len(outs) == 1 else jnp.concatenate(outs, axis=1)


def _head_rms_norm(y, gain):
    first = lax.broadcasted_iota(jnp.int32, (1, LANES), 1) < HEAD_DIM
    outs = []
    for c in range(y.shape[1] // LANES):
        yc = y[:, c * LANES:(c + 1) * LANES]
        sq = yc * yc
        s0 = jnp.sum(jnp.where(first, sq, 0.0), axis=-1, keepdims=True)
        s1 = jnp.sum(jnp.where(first, 0.0, sq), axis=-1, keepdims=True)
        ms = jnp.where(first, s0, s1) * (1.0 / HEAD_DIM)
        outs.append(yc * lax.rsqrt(ms + RMS_EPS) * gain)
    return outs[0] if len(outs) == 1 else jnp.concatenate(outs, axis=1)


_NT = (((1,), (1,)), ((), ()))


def _with_ones_rows(vt):
    return jnp.concatenate([vt.astype(BF16), jnp.ones((ONES_ROWS, vt.shape[1]), BF16)], axis=0)


def _attend(segments, q, rows, group, sink=None):
    scores = []
    for k, _, bias, valid in segments:
        s = _scores(k, q)
        if bias is not None:
            s = s + bias
        if valid is not None:
            s = jnp.where(valid, s, NEG_INF)
        scores.append(s)
    return _softmax_values(scores, [seg[1] for seg in segments], rows, group, sink)


def _scores(k, q):
    return lax.dot_general(k, q, _NT, preferred_element_type=F32)


def _softmax_values(scores, vts, rows, group, sink=None):
    m = functools.reduce(jnp.maximum, [jnp.max(s, axis=0, keepdims=True) for s in scores])
    if sink is not None:
        m = jnp.maximum(m, sink)
    ot = None
    for s, vt_ones in zip(scores, vts):
        part = jnp.dot(vt_ones, jnp.exp2(s - m).astype(BF16), preferred_element_type=F32)
        ot = part if ot is None else ot + part
    denom = ot[HEAD_DIM:HEAD_DIM + 1, :]
    if sink is not None:
        denom = denom + jnp.exp2(sink - m)
    ot = ot[:HEAD_DIM, :] / denom
    return jnp.concatenate([ot[:, g * rows:(g + 1) * rows].T for g in range(group)], axis=1)


def _pipelined_item(cur, nxt, n_keys, *, q_next, keys_next, vt_cur, bias_next=None,
                    cols_of=None):
    s_cur, m_cur = cur
    s_nxt, m_nxt = nxt
    width = s_nxt.shape[1]
    if cols_of is None:
        cols_of = lambda c0: (0, width)
    m = None if s_cur is None else m_cur[...]
    m_next = [None] * (width // Q_TILE)
    ot = [None] * (width // Q_TILE)
    for c0 in range(0, n_keys, KEY_CHUNK):
        keys = slice(c0, c0 + KEY_CHUNK)
        lo, hi = cols_of(c0)
        blocks = [(g, slice(g * Q_TILE - lo, (g + 1) * Q_TILE - lo))
                  for g in range(lo // Q_TILE, hi // Q_TILE)]
        s = _scores(keys_next(c0), q_next[lo:hi])
        bias = None if bias_next is None else bias_next(c0)
        if bias is not None:
            s = s + bias
        s_nxt[keys, lo:hi] = s
        m_c = jnp.max(s, axis=0, keepdims=True)
        for g, cols in blocks:
            m_next[g] = m_c[:, cols] if m_next[g] is None else jnp.maximum(m_next[g], m_c[:, cols])
        if s_cur is not None:
            p = jnp.exp2(s_cur[keys, lo:hi] - m[:, lo:hi]).astype(BF16)
            part = jnp.dot(vt_cur(c0), p, preferred_element_type=F32)
            for g, cols in blocks:
                ot[g] = part[:, cols] if ot[g] is None else ot[g] + part[:, cols]
    m_nxt[...] = jnp.concatenate(m_next, axis=1)
    if s_cur is None:
        return None
    ot = jnp.concatenate(ot, axis=1)
    return ot[:HEAD_DIM, :] / ot[HEAD_DIM:HEAD_DIM + 1, :]


def _score_slots(n_keys, m):
    return [pltpu.VMEM((n_keys, m), F32), pltpu.VMEM((n_keys, m), F32),
            pltpu.VMEM((2, 1, m), F32)]


def _stack_group(ref, rows, col0, group):
    return jnp.concatenate(
        [ref[rows, col0 + g * HEAD_DIM: col0 + (g + 1) * HEAD_DIM] for g in range(group)], axis=0)


def _sink_row(sink_ref, h0, group, rows):
    return jnp.concatenate(
        [jnp.full((1, rows), sink_ref[h0 + g] * LOG2E, F32) for g in range(group)], axis=1)


def _store_feature_major(ref, y):
    nb, heads, _, seq = ref.shape
    yt = y.T
    for b in range(nb):
        for h in range(heads):
            ref[b, h] = yt[h * HEAD_DIM:(h + 1) * HEAD_DIM, b * seq:(b + 1) * seq].astype(ref.dtype)


def _pre_even_kernel(x_ref, g_ref, sh_ref, sc_ref, w_ref, *rest, is_lat):
    if is_lat:
        cos_ref, slo_ref, shi_ref, qk_ref, va_ref, vb_ref = rest
    else:
        qk_ref, ka_ref, va_ref, kb_ref, vb_ref = rest
    h = _adaln(x_ref[...], g_ref[...], sh_ref[0, 0], sc_ref[0, 0]).astype(BF16)

    def proj(c0, c1):
        return jnp.dot(h, w_ref[:, c0:c1], preferred_element_type=F32)

    na = A_HEADS * HEAD_DIM
    nb = B_HEADS * HEAD_DIM
    nkv = B_KV_HEADS * HEAD_DIM
    qk_ref[:, 0:na] = (proj(0, na) * QK_SCALE).astype(BF16)
    ka = proj(na, 2 * na)
    qk_ref[:, na:2 * na] = ka.astype(BF16)
    va = proj(2 * na, 3 * na)
    qb = proj(3 * na, 3 * na + nb)
    if is_lat:
        qb = _rope(qb, cos_ref[...], slo_ref[...], shi_ref[...])
    qk_ref[:, 2 * na:2 * na + nb] = (qb * QK_SCALE).astype(BF16)
    kvb = proj(3 * na + nb, 3 * na + nb + 2 * nkv)
    kb, vb = kvb[:, :nkv], kvb[:, nkv:]
    if is_lat:
        kb_out = _rope(kb, cos_ref[...], slo_ref[...], shi_ref[...])
    else:
        kb_out = kb
    qk_ref[:, 2 * na + nb:2 * na + nb + nkv] = kb_out.astype(BF16)
    _store_feature_major(va_ref, va)
    _store_feature_major(vb_ref, vb)
    if not is_lat:
        _store_feature_major(ka_ref, ka)
        _store_feature_major(kb_ref, kb)


def _pre_odd_kernel(x_ref, g_ref, sh_ref, sc_ref, w_ref, qn_ref, kn_ref, *rest, is_lat):
    if is_lat:
        cos_ref, slo_ref, shi_ref, qk_ref, vc_ref = rest
    else:
        qk_ref, kc_ref, vc_ref = rest
    h = _adaln(x_ref[...], g_ref[...], sh_ref[0, 0], sc_ref[0, 0]).astype(BF16)

    def proj(c0, c1):
        return jnp.dot(h, w_ref[:, c0:c1], preferred_element_type=F32)

    nq = C_HEADS * HEAD_DIM
    nkv = C_KV_HEADS * HEAD_DIM
    q = _head_rms_norm(proj(0, nq), qn_ref[...])
    k = _head_rms_norm(proj(nq, nq + nkv), kn_ref[...])
    v = proj(nq + nkv, nq + 2 * nkv)
    if is_lat:
        q = _rope(q, cos_ref[...], slo_ref[...], shi_ref[...])
        k_out = _rope(k, cos_ref[...], slo_ref[...], shi_ref[...])
    else:
        k_out = k
        _store_feature_major(kc_ref, k)
    _store_feature_major(vc_ref, v)
    qk_ref[:, 0:nq] = (q * QK_SCALE).astype(BF16)
    qk_ref[:, nq:nq + nkv] = k_out.astype(BF16)


def _pre_attention(x, mods, layer, gain, w, sub, *, odd, is_lat, seq, rope=None,
                   head_gains=None):
    t, d = x.shape
    tm = TOKEN_TILE
    per_seq = max(seq // tm, 1)
    per_tile = max(tm // seq, 1)
    group = (lambda i: 1 + i // per_seq) if is_lat else (lambda i: 0)
    row = lambda i: (i, 0)
    in_specs = [
        pl.BlockSpec((tm, d), row),
        _resident((1, d)),
        _mod_spec(layer, 0, group),
        _mod_spec(layer, 1, group),
        _resident_layer(w, sub),
    ]
    args = [x, gain.reshape(1, d), mods, mods, w]
    if odd:
        in_specs += [_resident(a.shape) for a in head_gains]
        args += list(head_gains)
    if is_lat:
        in_specs += [pl.BlockSpec((tm, LANES), lambda i: (i % per_seq, 0))] * 3
        args += list(rope)
    if odd:
        n_qk = (C_HEADS + C_KV_HEADS) * HEAD_DIM
        lat_heads, ctx_heads = [C_KV_HEADS], [C_KV_HEADS, C_KV_HEADS]
    else:
        n_qk = (2 * A_HEADS + B_HEADS + B_KV_HEADS) * HEAD_DIM
        lat_heads, ctx_heads = [A_HEADS, B_KV_HEADS], [A_HEADS, A_HEADS, B_KV_HEADS, B_KV_HEADS]
    out_specs = [pl.BlockSpec((tm, n_qk), row)]
    out_shape = [jax.ShapeDtypeStruct((t, n_qk), BF16)]
    for heads in (lat_heads if is_lat else ctx_heads):
        blk = (per_tile, heads, HEAD_DIM, min(tm, seq))
        out_specs.append(pl.BlockSpec(blk, lambda i: (i // per_seq, 0, 0, i % per_seq)))
        out_shape.append(jax.ShapeDtypeStruct((t // seq, heads, HEAD_DIM, seq),
                                              BF16 if is_lat else F32))
    body = _pre_odd_kernel if odd else _pre_even_kernel
    return pl.pallas_call(
        functools.partial(body, is_lat=is_lat),
        grid=(t // tm,),
        in_specs=in_specs,
        out_specs=out_specs,
        out_shape=out_shape,
        compiler_params=_params(1),
        name=f"pre_{'odd' if odd else 'even'}_{'lat' if is_lat else 'ctx'}",
    )(*args)


def _batch_spec(arr, nb=1):
    nd = arr.ndim
    return pl.BlockSpec((nb,) + arr.shape[1:], lambda b: (b,) + (0,) * (nd - 1))


def _cache_spec(cache, layer):
    blk = (1, 1) + cache.shape[2:]
    return pl.BlockSpec(blk, lambda b: (b, layer, 0, 0, 0))


def _feature_major(cache):
    return cache.transpose(0, 1, 3, 4, 2)


def _cache_keys(ref, h):
    return ref[0, 0, h].T.astype(BF16)


def _ctx_even_attn_kernel(sink_ref, qk_ref, vat_ref, vbt_ref, o_ref):
    nb = vat_ref.shape[0]
    n = qk_ref.shape[0] // nb
    na = A_HEADS * HEAD_DIM
    group = B_HEADS // B_KV_HEADS
    qb0 = 2 * na
    kb0 = qb0 + B_HEADS * HEAD_DIM
    for b in range(nb):
        rows = slice(b * n, (b + 1) * n)
        for pair in range(A_HEADS // 2):
            outs = []
            for h in (2 * pair, 2 * pair + 1):
                c = h * HEAD_DIM
                seg = (qk_ref[rows, na + c:na + c + HEAD_DIM], _with_ones_rows(vat_ref[b, h]),
                       None, None)
                outs.append(_attend([seg], qk_ref[rows, c:c + HEAD_DIM], n, 1))
            o_ref[rows, pair * LANES:(pair + 1) * LANES] = (
                jnp.concatenate(outs, axis=1).astype(BF16))
        for kv in range(B_KV_HEADS):
            q = _stack_group(qk_ref, rows, qb0 + kv * group * HEAD_DIM, group)
            seg = (qk_ref[rows, kb0 + kv * HEAD_DIM:kb0 + (kv + 1) * HEAD_DIM],
                   _with_ones_rows(vbt_ref[b, kv]), None, None)
            o = _attend([seg], q, n, group, _sink_row(sink_ref, kv * group, group, n))
            c0 = na + kv * group * HEAD_DIM
            o_ref[rows, c0:c0 + group * HEAD_DIM] = o.astype(BF16)


def _ctx_even_attention(qk, vat, vbt, sink, seq):
    t, n = qk.shape
    width = (A_HEADS + B_HEADS) * HEAD_DIM
    nb = CTX_BATCHES
    return pl.pallas_call(
        _ctx_even_attn_kernel,
        grid=(t // (nb * seq),),
        in_specs=[pl.BlockSpec(memory_space=pltpu.SMEM),
                  pl.BlockSpec((nb * seq, n), lambda b: (b, 0)),
                  _batch_spec(vat, nb), _batch_spec(vbt, nb)],
        out_specs=pl.BlockSpec((nb * seq, width), lambda b: (b, 0)),
        out_shape=jax.ShapeDtypeStruct((t, width), BF16),
        compiler_params=_params(1),
        name="attn_even_ctx",
    )(sink, qk, vat, vbt)


def _na_bias_tiles(src_ref, h):
    kc = lax.broadcasted_iota(jnp.int32, (GRID_W, LANES), 0)
    qc = lax.rem(lax.broadcasted_iota(jnp.int32, (GRID_W, LANES), 1), GRID_W)
    cs = jnp.clip(qc - NA_WIN_W // 2, 0, GRID_W - NA_WIN_W)
    col_ok = (kc >= cs) & (kc < cs + NA_WIN_W)
    tiles = []
    for i in range(2 * NA_WIN_H):
        src = jnp.broadcast_to(src_ref[h, i:i + 1, :], (GRID_W, LANES))
        tiles.append(jnp.where(col_ok, pltpu.roll(src, 0, 1, stride=1, stride_axis=0), NEG_INF))
    return tiles


def _na_row_window(qr, grid_rows):
    wh = min(NA_WIN_H, grid_rows)
    start = min(max(qr - wh // 2, 0), grid_rows - wh)
    return start, start + wh


def _na_reachable_query_rows(key_rows, grid_rows):
    per_block = Q_TILE // GRID_W
    hit = [qr for qr in range(grid_rows)
           if any(_na_row_window(qr, grid_rows)[0] <= kr < _na_row_window(qr, grid_rows)[1]
                  for kr in key_rows)]
    return min(hit) // per_block * per_block, (max(hit) // per_block + 1) * per_block


def _na_bias_block(tiles, key_rows, q_rows, grid_rows):
    first_row = lax.broadcasted_iota(jnp.int32, (GRID_W, LANES), 1) < GRID_W

    def in_window(kr, qr):
        start, stop = _na_row_window(qr, grid_rows)
        return start <= kr < stop

    strips = []
    for kr in key_rows:
        parts = []
        for qr in range(q_rows.start, q_rows.stop, 2):
            ok0 = in_window(kr, qr)
            ok1 = in_window(kr, qr + 1)
            i = kr - qr + NA_WIN_H - 1
            if ok0 and ok1:
                parts.append(tiles[i])
            elif ok0:
                parts.append(jnp.where(first_row, tiles[i], NEG_INF))
            elif ok1:
                parts.append(jnp.where(first_row, NEG_INF, tiles[i]))
            else:
                parts.append(jnp.full((GRID_W, LANES), NEG_INF, F32))
        strips.append(jnp.concatenate(parts, axis=1))
    return jnp.concatenate(strips, axis=0)


def _lat_a_kernel(src_ref, q_ref, k_ref, vt_ref, ck_ref, cv_ref, o_ref,
                  qs_ref, kall_ref, vtall_ref, os_ref, s0_ref, s1_ref, m_ref):
    n = q_ref.shape[0]
    past = ck_ref.shape[4]
    grid_rows = n // GRID_W
    for h in range(A_HEADS):
        cols = slice(h * HEAD_DIM, (h + 1) * HEAD_DIM)
        qs_ref[h] = q_ref[:, cols]
        kall_ref[h, 0:n, :] = k_ref[:, cols]
        kall_ref[h, n:n + past, :] = _cache_keys(ck_ref, h)
        vtall_ref[h, 0:HEAD_DIM, 0:n] = vt_ref[0, h]
        vtall_ref[h, 0:HEAD_DIM, n:n + past] = cv_ref[0, 0, h].astype(BF16)
        vtall_ref[h, HEAD_DIM:, :] = jnp.ones((ONES_ROWS, n + past), BF16)

    slots = ((s0_ref, m_ref.at[0]), (s1_ref, m_ref.at[1]))

    def key_rows(c0):
        return range(c0 // GRID_W, (c0 + KEY_CHUNK) // GRID_W)

    def query_rows(c0):
        if c0 >= n:
            return range(0, grid_rows)
        return range(*_na_reachable_query_rows(key_rows(c0), grid_rows))

    def step(cur, nxt, h, h_next):
        tiles = _na_bias_tiles(src_ref, h_next)

        def bias(c0):
            if c0 >= n:
                return None
            return _na_bias_block(tiles, key_rows(c0), query_rows(c0), grid_rows)

        return _pipelined_item(
            cur, nxt, n + past, q_next=qs_ref[h_next],
            keys_next=lambda c0: kall_ref[h_next, c0:c0 + KEY_CHUNK, :],
            vt_cur=lambda c0: vtall_ref[h, :, c0:c0 + KEY_CHUNK], bias_next=bias,
            cols_of=lambda c0: (query_rows(c0).start * GRID_W, query_rows(c0).stop * GRID_W))

    step((None, None), slots[0], None, 0)

    def head_pair(p, carry):
        h0 = 2 * p
        os_ref[h0] = step(slots[0], slots[1], h0, h0 + 1).T.astype(BF16)
        os_ref[h0 + 1] = step(slots[1], slots[0], h0 + 1, lax.rem(h0 + 2, A_HEADS)).T.astype(BF16)
        return carry

    lax.fori_loop(0, A_HEADS // 2, head_pair, 0)
    for pair in range(A_HEADS // 2):
        o_ref[:, pair * LANES:(pair + 1) * LANES] = jnp.concatenate(
            [os_ref[2 * pair], os_ref[2 * pair + 1]], axis=1)


def _lat_a_attention(qk, vt, cache_k, cache_v, e, bias_src, seq):
    t = qk.shape[0]
    past = cache_k.shape[4]
    na = A_HEADS * HEAD_DIM
    return pl.pallas_call(
        _lat_a_kernel,
        grid=(t // seq,),
        in_specs=[
            _resident(bias_src.shape),
            pl.BlockSpec((seq, na), lambda b: (b, 0)),
            pl.BlockSpec((seq, na), lambda b: (b, 1)),
            _batch_spec(vt),
            _cache_spec(cache_k, e),
            _cache_spec(cache_v, e),
        ],
        out_specs=pl.BlockSpec((seq, na), lambda b: (b, 0)),
        out_shape=jax.ShapeDtypeStruct((t, na), BF16),
        scratch_shapes=[pltpu.VMEM((A_HEADS, seq, HEAD_DIM), BF16),
                        pltpu.VMEM((A_HEADS, seq + past, HEAD_DIM), BF16),
                        pltpu.VMEM((A_HEADS, HEAD_DIM + ONES_ROWS, seq + past), BF16),
                        pltpu.VMEM((A_HEADS, seq, HEAD_DIM), BF16)]
                       + _score_slots(seq + past, seq),
        compiler_params=_params(1),
        name="attn_even_lat_a",
    )(bias_src, qk, qk, vt, cache_k, cache_v)


def _lat_b_kernel(sink_ref, q_ref, k_ref, vt_ref, ck_ref, cv_ref, o_ref):
    n = q_ref.shape[0]
    group = B_HEADS // B_KV_HEADS
    win = 2 * Q_TILE
    ctx = [(_cache_keys(ck_ref, kv), _with_ones_rows(cv_ref[0, 0, kv]), None, None)
           for kv in range(B_KV_HEADS)]
    for j in range(n // Q_TILE):
        lo = min(max(Q_TILE * j - B_WINDOW, 0), n - win)
        q_rows = slice(j * Q_TILE, (j + 1) * Q_TILE)
        kpos = lo + lax.broadcasted_iota(jnp.int32, (win, 1), 0)
        qpos = j * Q_TILE + lax.broadcasted_iota(jnp.int32, (1, group * Q_TILE), 1) % Q_TILE
        valid = jnp.abs(qpos - kpos) <= B_WINDOW
        for kv in range(B_KV_HEADS):
            q = _stack_group(q_ref, q_rows, kv * group * HEAD_DIM, group)
            local = (k_ref[lo:lo + win, kv * HEAD_DIM:(kv + 1) * HEAD_DIM],
                     _with_ones_rows(vt_ref[0, kv, :, lo:lo + win]), None, valid)
            o = _attend([local, ctx[kv]], q, Q_TILE, group,
                        _sink_row(sink_ref, kv * group, group, Q_TILE))
            c0 = kv * group * HEAD_DIM
            o_ref[q_rows, c0:c0 + group * HEAD_DIM] = o.astype(BF16)


def _lat_b_attention(qk, vt, cache_k, cache_v, e, sink, seq):
    t = qk.shape[0]
    na = A_HEADS * HEAD_DIM
    nb = B_HEADS * HEAD_DIM
    nkv = B_KV_HEADS * HEAD_DIM
    return pl.pallas_call(
        _lat_b_kernel,
        grid=(t // seq,),
        in_specs=[
            pl.BlockSpec(memory_space=pltpu.SMEM),
            pl.BlockSpec((seq, nb), lambda b: (b, 2 * na // nb)),
            pl.BlockSpec((seq, nkv), lambda b: (b, (2 * na + nb) // nkv)),
            _batch_spec(vt),
            _cache_spec(cache_k, e),
            _cache_spec(cache_v, e),
        ],
        out_specs=pl.BlockSpec((seq, nb), lambda b: (b, 0)),
        out_shape=jax.ShapeDtypeStruct((t, nb), BF16),
        compiler_params=_params(1),
        name="attn_even_lat_b",
    )(sink, qk, qk, vt, cache_k, cache_v)


def _ctx_odd_attn_kernel(qk_ref, vt_ref, o_ref):
    nb = vt_ref.shape[0]
    n = qk_ref.shape[0] // nb
    group = C_HEADS // C_KV_HEADS
    nq = C_HEADS * HEAD_DIM
    for b in range(nb):
        rows = slice(b * n, (b + 1) * n)
        for kv in range(C_KV_HEADS):
            q = _stack_group(qk_ref, rows, kv * group * HEAD_DIM, group)
            seg = (qk_ref[rows, nq + kv * HEAD_DIM:nq + (kv + 1) * HEAD_DIM],
                   _with_ones_rows(vt_ref[b, kv]), None, None)
            c0 = kv * group * HEAD_DIM
            o_ref[rows, c0:c0 + group * HEAD_DIM] = _attend([seg], q, n, group).astype(BF16)


def _ctx_odd_attention(qk, vt, seq):
    t, n = qk.shape
    width = C_HEADS * HEAD_DIM
    nb = CTX_BATCHES
    return pl.pallas_call(
        _ctx_odd_attn_kernel,
        grid=(t // (nb * seq),),
        in_specs=[pl.BlockSpec((nb * seq, n), lambda b: (b, 0)), _batch_spec(vt, nb)],
        out_specs=pl.BlockSpec((nb * seq, width), lambda b: (b, 0)),
        out_shape=jax.ShapeDtypeStruct((t, width), BF16),
        compiler_params=_params(1),
        name="attn_odd_ctx",
    )(qk, vt)


def _lat_c_kernel(qk_ref, vt_ref, ck_ref, cv_ref, o_ref, kall_ref, vtall_ref, s0_ref, s1_ref,
                  m_ref):
    n = qk_ref.shape[0]
    past = ck_ref.shape[4]
    group = C_HEADS // C_KV_HEADS
    nq = C_HEADS * HEAD_DIM
    n_blocks = n // Q_TILE
    for kv in range(C_KV_HEADS):
        kall_ref[kv, 0:past, :] = _cache_keys(ck_ref, kv)
        kall_ref[kv, past:past + n, :] = qk_ref[:, nq + kv * HEAD_DIM:nq + (kv + 1) * HEAD_DIM]
        vtall_ref[kv, 0:HEAD_DIM, 0:past] = cv_ref[0, 0, kv].astype(BF16)
        vtall_ref[kv, 0:HEAD_DIM, past:past + n] = vt_ref[0, kv]
        vtall_ref[kv, HEAD_DIM:, :] = jnp.ones((ONES_ROWS, past + n), BF16)

    def rows_of(j):
        if isinstance(j, int):
            return slice(j * Q_TILE, (j + 1) * Q_TILE)
        return pl.ds(pl.multiple_of(j * Q_TILE, Q_TILE), Q_TILE)

    slots = ((s0_ref, m_ref.at[0]), (s1_ref, m_ref.at[1]))

    def item(cur, nxt, kv, j_next, kv_next):
        ot = _pipelined_item(
            cur, nxt, past + n,
            q_next=_stack_group(qk_ref, rows_of(j_next), kv_next * group * HEAD_DIM, group),
            keys_next=lambda c0: kall_ref[kv_next, c0:c0 + KEY_CHUNK, :],
            vt_cur=lambda c0: vtall_ref[kv, :, c0:c0 + KEY_CHUNK])
        if ot is None:
            return None
        return jnp.concatenate(
            [ot[:, g * Q_TILE:(g + 1) * Q_TILE].T for g in range(group)], axis=1)

    item((None, None), slots[0], None, 0, 0)

    def q_block(j, carry):
        for kv in range(C_KV_HEADS):
            last = kv + 1 == C_KV_HEADS
            o = item(slots[kv % 2], slots[(kv + 1) % 2], kv,
                     lax.rem(j + 1, n_blocks) if last else j, 0 if last else kv + 1)
            c0 = kv * group * HEAD_DIM
            o_ref[rows_of(j), c0:c0 + group * HEAD_DIM] = o.astype(BF16)
        return carry

    lax.fori_loop(0, n_blocks, q_block, 0)


def _lat_c_attention(qk, vt, cache_k, cache_v, o, seq):
    t, n = qk.shape
    past = cache_k.shape[4]
    width = C_HEADS * HEAD_DIM
    return pl.pallas_call(
        _lat_c_kernel,
        grid=(t // seq,),
        in_specs=[
            pl.BlockSpec((seq, n), lambda b: (b, 0)),
            _batch_spec(vt),
            _cache_spec(cache_k, o),
            _cache_spec(cache_v, o),
        ],
        out_specs=pl.BlockSpec((seq, width), lambda b: (b, 0)),
        out_shape=jax.ShapeDtypeStruct((t, width), BF16),
        scratch_shapes=[pltpu.VMEM((C_KV_HEADS, past + seq, HEAD_DIM), BF16),
                        pltpu.VMEM((C_KV_HEADS, HEAD_DIM + ONES_ROWS, past + seq), BF16)]
                       + _score_slots(past + seq, C_HEADS // C_KV_HEADS * Q_TILE),
        compiler_params=_params(1),
        name="attn_odd_lat",
    )(qk, vt, cache_k, cache_v)


def _post_kernel(*refs, n_parts, final):
    o_refs = refs[:n_parts]
    (x_ref, wo_ref, g1_ref, sh_ref, sc_ref, g2_ref, gain_ref, wgu_ref, wd_ref) = refs[n_parts:n_parts + 9]
    rest = refs[n_parts + 9:]
    if final:
        fg_ref, out_ref, act_ref = rest
    else:
        out_ref, act_ref = rest
    mix = None
    r0 = 0
    for o_ref in o_refs:
        kk = o_ref.shape[1]
        part = jnp.dot(o_ref[...], wo_ref[r0:r0 + kk, :], preferred_element_type=F32)
        mix = part if mix is None else mix + part
        r0 += kk
    x1 = x_ref[...] + g1_ref[0, 0] * mix
    h = _adaln(x1, gain_ref[...], sh_ref[0, 0], sc_ref[0, 0]).astype(BF16)
    d_ff = wd_ref.shape[0]
    for j in range(d_ff // FF_CHUNK):
        c0 = j * FF_CHUNK
        gate = jnp.dot(h, wgu_ref[:, c0:c0 + FF_CHUNK], preferred_element_type=F32)
        up = jnp.dot(h, wgu_ref[:, d_ff + c0:d_ff + c0 + FF_CHUNK], preferred_element_type=F32)
        act_ref[:, c0:c0 + FF_CHUNK] = (gate * jax.nn.sigmoid(gate) * up).astype(BF16)
    ffn = jnp.dot(act_ref[...], wd_ref[...], preferred_element_type=F32)
    x2 = x1 + g2_ref[0, 0] * ffn
    if final:
        ms = jnp.mean(x2 * x2, axis=-1, keepdims=True)
        x2 = (x2 * lax.rsqrt(ms + RMS_EPS)) * fg_ref[...]
    out_ref[...] = x2


def _post_attention(o_parts, x, mods, layer, gain, w_out, sub, w_gu, w_down, *, is_lat, seq,
                    final_gain=None):
    t, d = x.shape
    tm = TOKEN_TILE
    per_seq = max(seq // tm, 1)
    group = (lambda i: 1 + i // per_seq) if is_lat else (lambda i: 0)
    row = lambda i: (i, 0)
    d_ff = w_down.shape[1]
    final = final_gain is not None
    in_specs = [pl.BlockSpec((tm, o.shape[1]), row) for o in o_parts]
    in_specs += [
        pl.BlockSpec((tm, d), row),
        _resident_layer(w_out, sub),
        _mod_spec(layer, 2, group),
        _mod_spec(layer, 3, group),
        _mod_spec(layer, 4, group),
        _mod_spec(layer, 5, group),
        _resident((1, d)),
        _resident_layer(w_gu, layer),
        _resident_layer(w_down, layer),
    ]
    args = list(o_parts) + [x, w_out, mods, mods, mods, mods, gain.reshape(1, d), w_gu, w_down]
    if final:
        in_specs.append(_resident((1, d)))
        args.append(final_gain.reshape(1, d))
    return pl.pallas_call(
        functools.partial(_post_kernel, n_parts=len(o_parts), final=final),
        grid=(t // tm,),
        in_specs=in_specs,
        out_specs=pl.BlockSpec((tm, d), row),
        out_shape=jax.ShapeDtypeStruct((t, d), F32),
        scratch_shapes=[pltpu.VMEM((tm, d_ff), BF16)],
        compiler_params=_params(1),
        name=f"post_{'lat' if is_lat else 'ctx'}{'_final' if final else ''}",
    )(*args)


def _rope_tables(n):
    t = np.arange(n)
    row = (t // GRID_W).astype(np.float32)
    col = (t % GRID_W).astype(np.float32)
    half = HEAD_DIM // 2
    inv_freq = np.float32(ROPE_THETA) ** (-np.arange(0, half, 2, dtype=np.float32) / np.float32(half))
    lane = np.arange(LANES)
    in_head = lane % HEAD_DIM
    pos = np.where((in_head < half)[None, :], row[:, None], col[:, None])
    ang = (pos * inv_freq[in_head % (half // 2)][None, :]).astype(np.float32)
    first = ((in_head % half) < half // 2)[None, :]
    cos = np.cos(ang)
    sin = np.sin(ang)
    zero = np.float32(0.0)
    return tuple(jnp.asarray(a, F32) for a in
                 (cos, np.where(first, -sin, zero), np.where(first, zero, sin)))


def _na_bias_sources(rpb):
    h, _, nb = rpb.shape
    w = NA_WIN_W - 1
    rp = jnp.pad(rpb[:, :, ::-1] * LOG2E, ((0, 0), (1, 1), (0, 0)))
    this, prev = rp[:, 1:], rp[:, :-1]
    z = jnp.zeros((h, 2 * NA_WIN_H, LANES // 2 - nb), F32)
    return jnp.concatenate([this[:, :, w:], z, prev, z, this[:, :, :w]], axis=-1)


def _state(y):
    return y.transpose(0, 3, 1, 2)


def kernel(x_prompt, x_sample, cache_a_k, cache_a_v, cache_b_k, cache_b_v, cache_c_k, cache_c_v,
           c, c_ctx, norm_gain, w_mod, b_mod, w_in_even, w_out_even, rpb_a, sink_b,
           w_in_odd, w_out_odd, q_norm_c, k_norm_c, w_gate_up, w_down, final_gain):
    batch, seq, d = x_prompt.shape
    dec_batch, dec_seq, _ = x_sample.shape
    depth = w_mod.shape[0]

    cvec = jnp.concatenate(
        [c_ctx[None, :], c, jnp.zeros((MOD_GROUPS - 1 - dec_batch, d), F32)], axis=0)
    mods = _modulation(cvec, w_mod, b_mod).reshape(depth, MOD_GROUPS, 1, 6 * d)
    rope = _rope_tables(dec_seq)
    w_in = {False: w_in_even.astype(BF16), True: w_in_odd.astype(BF16)}
    w_out = {False: w_out_even.astype(BF16), True: w_out_odd.astype(BF16)}
    w_gu = w_gate_up.astype(BF16)
    w_dn = w_down.astype(BF16)

    ctx = x_prompt.reshape(batch * seq, d)
    lat = x_sample.reshape(dec_batch * dec_seq, d)
    states = {name: [] for name in ("a_k", "a_v", "b_k", "b_v", "c_k", "c_v")}

    for layer in range(depth):
        odd = layer % 2 == 1
        sub = layer // 2
        gain1, gain2 = norm_gain[layer, 0], norm_gain[layer, 1]
        pre = functools.partial(_pre_attention, mods=mods, layer=layer, gain=gain1, w=w_in[odd],
                                sub=sub, odd=odd)
        if not odd:
            qk_c, ka, va, kb, vb = pre(ctx, is_lat=False, seq=seq)
            qk_l, vat_l, vbt_l = pre(lat, is_lat=True, seq=dec_seq, rope=rope)
            for name, y in (("a_k", ka), ("a_v", va), ("b_k", kb), ("b_v", vb)):
                states[name].append(_state(y))
            o_ctx = [_ctx_even_attention(qk_c, va, vb, sink_b[sub], seq)]
            o_lat = [
                _lat_a_attention(qk_l, vat_l, _feature_major(cache_a_k), _feature_major(cache_a_v),
                                 sub, _na_bias_sources(rpb_a[sub]), dec_seq),
                _lat_b_attention(qk_l, vbt_l, _feature_major(cache_b_k), _feature_major(cache_b_v),
                                 sub, sink_b[sub], dec_seq),
            ]
        else:
            per = LANES // HEAD_DIM
            head_gains = (jnp.tile(q_norm_c[sub], per).reshape(1, LANES),
                          jnp.tile(k_norm_c[sub], per).reshape(1, LANES))
            qk_c, kc, vc = pre(ctx, is_lat=False, seq=seq, head_gains=head_gains)
            qk_l, vct_l = pre(lat, is_lat=True, seq=dec_seq, rope=rope, head_gains=head_gains)
            states["c_k"].append(_state(kc))
            states["c_v"].append(_state(vc))
            o_ctx = [_ctx_odd_attention(qk_c, vc, seq)]
            o_lat = [_lat_c_attention(qk_l, vct_l, _feature_major(cache_c_k),
                                      _feature_major(cache_c_v), sub, dec_seq)]
        fg = final_gain if layer == depth - 1 else None
        post = functools.partial(_post_attention, mods=mods, layer=layer, gain=gain2,
                                 w_out=w_out[odd], sub=sub, w_gu=w_gu, w_down=w_dn, final_gain=fg)
        ctx = post(o_ctx, ctx, is_lat=False, seq=seq)
        lat = post(o_lat, lat, is_lat=True, seq=dec_seq)

    return (ctx.reshape(batch, seq, d), lat.reshape(dec_batch, dec_seq, d),
            jnp.stack(states["a_k"], axis=1), jnp.stack(states["a_v"], axis=1),
            jnp.stack(states["b_k"], axis=1), jnp.stack(states["b_v"], axis=1),
            jnp.stack(states["c_k"], axis=1), jnp.stack(states["c_v"], axis=1))
```

```python
import functools
import math

import jax
import jax.numpy as jnp
import numpy as np
from jax import lax
from jax.experimental import pallas as pl
from jax.experimental.pallas import tpu as pltpu

F32 = jnp.float32
BF16 = jnp.bfloat16

D_MODEL = 1024
GRID_W = 64
HEAD_DIM = 64
A_HEADS = 8
B_HEADS = 8
B_KV_HEADS = 2
C_HEADS = 16
C_KV_HEADS = 4
NA_WIN_H = 8
NA_WIN_W = 16
B_WINDOW = 128
ROPE_THETA = 10000.0
RMS_EPS = 1e-6
NEG_INF = -1e30
LOG2E = math.log2(math.e)
QK_SCALE = LOG2E / math.sqrt(HEAD_DIM)

LANES = 128
TOKEN_TILE = 512
Q_TILE = 256
CTX_BATCHES = 8
ONES_ROWS = 16
KEY_CHUNK = 256
FF_CHUNK = 256
MOD_GROUPS = 16
VMEM_LIMIT = 56 * 1024 * 1024


def _params(n_axes, vmem=VMEM_LIMIT):
    return pltpu.CompilerParams(
        dimension_semantics=("arbitrary",) * n_axes, vmem_limit_bytes=vmem)


def _resident(shape):
    nd = len(shape)
    return pl.BlockSpec(shape, lambda *_: (0,) * nd, pipeline_mode=pl.Buffered(1))


def _resident_layer(stacked, layer):
    return pl.BlockSpec((None,) + stacked.shape[1:], lambda *_: (layer, 0, 0),
                        pipeline_mode=pl.Buffered(1))


def _mod_kernel(c_ref, w_ref, b_ref, o_ref):
    c = c_ref[...]
    s = (c * jax.nn.sigmoid(c)).astype(BF16)
    o_ref[0] = jnp.dot(s, w_ref[0].astype(BF16), preferred_element_type=F32) + b_ref[0]


def _modulation(cvec, w_mod, b_mod):
    depth, d, n = w_mod.shape
    tn = 1536
    return pl.pallas_call(
        _mod_kernel,
        grid=(depth, n // tn),
        in_specs=[
            pl.BlockSpec((MOD_GROUPS, d), lambda l, j: (0, 0)),
            pl.BlockSpec((1, d, tn), lambda l, j: (l, 0, j)),
            pl.BlockSpec((1, 1, tn), lambda l, j: (l, 0, j)),
        ],
        out_specs=pl.BlockSpec((1, MOD_GROUPS, tn), lambda l, j: (l, 0, j)),
        out_shape=jax.ShapeDtypeStruct((depth, MOD_GROUPS, n), F32),
        compiler_params=_params(2),
        name="modulation",
    )(cvec, w_mod, b_mod.reshape(depth, 1, n))


def _mod_spec(layer, which, group_of_step):
    return pl.BlockSpec((1, 1, 1, D_MODEL), lambda i: (layer, group_of_step(i), 0, which))


def _adaln(x, gain, shift, scale):
    ms = jnp.mean(x * x, axis=-1, keepdims=True)
    return (x * lax.rsqrt(ms + RMS_EPS)) * gain * (1.0 + scale) + shift


def _rope(y, cos, sin_lo, sin_hi):
    outs = []
    for c in range(y.shape[1] // LANES):
        yc = y[:, c * LANES:(c + 1) * LANES]
        outs.append(yc * cos
                    + pltpu.roll(yc, LANES - 16, 1) * sin_lo
                    + pltpu.roll(yc, 16, 1) * sin_hi)
    return outs[0] if len(outs) == 1 else jnp.concatenate(outs, axis=1)


def _head_rms_norm(y, gain):
    first = lax.broadcasted_iota(jnp.int32, (1, LANES), 1) < HEAD_DIM
    outs = []
    for c in range(y.shape[1] // LANES):
        yc = y[:, c * LANES:(c + 1) * LANES]
        sq = yc * yc
        s0 = jnp.sum(jnp.where(first, sq, 0.0), axis=-1, keepdims=True)
        s1 = jnp.sum(jnp.where(first, 0.0, sq), axis=-1, keepdims=True)
        ms = jnp.where(first, s0, s1) * (1.0 / HEAD_DIM)
        outs.append(yc * lax.rsqrt(ms + RMS_EPS) * gain)
    return outs[0] if len(outs) == 1 else jnp.concatenate(outs, axis=1)


_NT = (((1,), (1,)), ((), ()))


def _with_ones_rows(vt):
    return jnp.concatenate([vt.astype(BF16), jnp.ones((ONES_ROWS, vt.shape[1]), BF16)], axis=0)


def _attend(segments, q, rows, group, sink=None):
    scores = []
    for k, _, bias, valid in segments:
        s = _scores(k, q)
        if bias is not None:
            s = s + bias
        if valid is not None:
            s = jnp.where(valid, s, NEG_INF)
        scores.append(s)
    return _softmax_values(scores, [seg[1] for seg in segments], rows, group, sink)


def _scores(k, q):
    return lax.dot_general(k, q, _NT, preferred_element_type=F32)


def _softmax_values(scores, vts, rows, group, sink=None):
    m = functools.reduce(jnp.maximum, [jnp.max(s, axis=0, keepdims=True) for s in scores])
    if sink is not None:
        m = jnp.maximum(m, sink)
    ot = None
    for s, vt_ones in zip(scores, vts):
        part = jnp.dot(vt_ones, jnp.exp2(s - m).astype(BF16), preferred_element_type=F32)
        ot = part if ot is None else ot + part
    denom = ot[HEAD_DIM:HEAD_DIM + 1, :]
    if sink is not None:
        denom = denom + jnp.exp2(sink - m)
    ot = ot[:HEAD_DIM, :] / denom
    return jnp.concatenate([ot[:, g * rows:(g + 1) * rows].T for g in range(group)], axis=1)


def _pipelined_item(cur, nxt, n_keys, *, q_next, keys_next, vt_cur, bias_next=None,
                    cols_of=None, sink_cur=None, sink_next=None):
    s_cur, m_cur = cur
    s_nxt, m_nxt = nxt
    width = s_nxt.shape[1]
    if cols_of is None:
        cols_of = lambda c0: (0, width)
    m = None if s_cur is None else m_cur[...]
    m_next = [None] * (width // Q_TILE)
    ot = [None] * (width // Q_TILE)
    for c0 in range(0, n_keys, KEY_CHUNK):
        keys = slice(c0, c0 + KEY_CHUNK)
        lo, hi = cols_of(c0)
        blocks = [(g, slice(g * Q_TILE - lo, (g + 1) * Q_TILE - lo))
                  for g in range(lo // Q_TILE, hi // Q_TILE)]
        s = _scores(keys_next(c0), q_next[lo:hi])
        bias = None if bias_next is None else bias_next(c0)
        if bias is not None:
            s = s + bias
        s_nxt[keys, lo:hi] = s
        m_c = jnp.max(s, axis=0, keepdims=True)
        for g, cols in blocks:
            m_next[g] = m_c[:, cols] if m_next[g] is None else jnp.maximum(m_next[g], m_c[:, cols])
        if s_cur is not None:
            p = jnp.exp2(s_cur[keys, lo:hi] - m[:, lo:hi]).astype(BF16)
            part = jnp.dot(vt_cur(c0), p, preferred_element_type=F32)
            for g, cols in blocks:
                ot[g] = part[:, cols] if ot[g] is None else ot[g] + part[:, cols]
    m_next = jnp.concatenate(m_next, axis=1)
    if sink_next is not None:
        m_next = jnp.maximum(m_next, sink_next)
    m_nxt[...] = m_next
    if s_cur is None:
        return None
    ot = jnp.concatenate(ot, axis=1)
    denom = ot[HEAD_DIM:HEAD_DIM + 1, :]
    if sink_cur is not None:
        denom = denom + jnp.exp2(sink_cur - m)
    return ot[:HEAD_DIM, :] / denom


def _score_slots(n_keys, m):
    return [pltpu.VMEM((n_keys, m), F32), pltpu.VMEM((n_keys, m), F32),
            pltpu.VMEM((2, 1, m), F32)]


def _stack_group(ref, rows, col0, group):
    return jnp.concatenate(
        [ref[rows, col0 + g * HEAD_DIM: col0 + (g + 1) * HEAD_DIM] for g in range(group)], axis=0)


def _sink_row(sink_ref, h0, group, rows):
    return jnp.concatenate(
        [jnp.full((1, rows), sink_ref[h0 + g] * LOG2E, F32) for g in range(group)], axis=1)


def _store_feature_major(ref, y):
    nb, heads, _, seq = ref.shape
    yt = y.T
    for b in range(nb):
        for h in range(heads):
            ref[b, h] = yt[h * HEAD_DIM:(h + 1) * HEAD_DIM, b * seq:(b + 1) * seq].astype(ref.dtype)


def _pre_even_kernel(x_ref, g_ref, sh_ref, sc_ref, w_ref, *rest, is_lat):
    if is_lat:
        cos_ref, slo_ref, shi_ref, qk_ref, va_ref, vb_ref = rest
    else:
        qk_ref, ka_ref, va_ref, kb_ref, vb_ref = rest
    h = _adaln(x_ref[...], g_ref[...], sh_ref[0, 0], sc_ref[0, 0]).astype(BF16)

    def proj(c0, c1):
        return jnp.dot(h, w_ref[:, c0:c1], preferred_element_type=F32)

    na = A_HEADS * HEAD_DIM
    nb = B_HEADS * HEAD_DIM
    nkv = B_KV_HEADS * HEAD_DIM
    qk_ref[:, 0:na] = (proj(0, na) * QK_SCALE).astype(BF16)
    ka = proj(na, 2 * na)
    qk_ref[:, na:2 * na] = ka.astype(BF16)
    va = proj(2 * na, 3 * na)
    qb = proj(3 * na, 3 * na + nb)
    if is_lat:
        qb = _rope(qb, cos_ref[...], slo_ref[...], shi_ref[...])
    qk_ref[:, 2 * na:2 * na + nb] = (qb * QK_SCALE).astype(BF16)
    kvb = proj(3 * na + nb, 3 * na + nb + 2 * nkv)
    kb, vb = kvb[:, :nkv], kvb[:, nkv:]
    if is_lat:
        kb_out = _rope(kb, cos_ref[...], slo_ref[...], shi_ref[...])
    else:
        kb_out = kb
    qk_ref[:, 2 * na + nb:2 * na + nb + nkv] = kb_out.astype(BF16)
    _store_feature_major(va_ref, va)
    _store_feature_major(vb_ref, vb)
    if not is_lat:
        _store_feature_major(ka_ref, ka)
        _store_feature_major(kb_ref, kb)


def _pre_odd_kernel(x_ref, g_ref, sh_ref, sc_ref, w_ref, qn_ref, kn_ref, *rest, is_lat):
    if is_lat:
        cos_ref, slo_ref, shi_ref, qk_ref, vc_ref = rest
    else:
        qk_ref, kc_ref, vc_ref = rest
    h = _adaln(x_ref[...], g_ref[...], sh_ref[0, 0], sc_ref[0, 0]).astype(BF16)

    def proj(c0, c1):
        return jnp.dot(h, w_ref[:, c0:c1], preferred_element_type=F32)

    nq = C_HEADS * HEAD_DIM
    nkv = C_KV_HEADS * HEAD_DIM
    q = _head_rms_norm(proj(0, nq), qn_ref[...])
    k = _head_rms_norm(proj(nq, nq + nkv), kn_ref[...])
    v = proj(nq + nkv, nq + 2 * nkv)
    if is_lat:
        q = _rope(q, cos_ref[...], slo_ref[...], shi_ref[...])
        k_out = _rope(k, cos_ref[...], slo_ref[...], shi_ref[...])
    else:
        k_out = k
        _store_feature_major(kc_ref, k)
    _store_feature_major(vc_ref, v)
    qk_ref[:, 0:nq] = (q * QK_SCALE).astype(BF16)
    qk_ref[:, nq:nq + nkv] = k_out.astype(BF16)


def _pre_attention(x, mods, layer, gain, w, sub, *, odd, is_lat, seq, rope=None,
                   head_gains=None):
    t, d = x.shape
    tm = TOKEN_TILE
    per_seq = max(seq // tm, 1)
    per_tile = max(tm // seq, 1)
    group = (lambda i: 1 + i // per_seq) if is_lat else (lambda i: 0)
    row = lambda i: (i, 0)
    in_specs = [
        pl.BlockSpec((tm, d), row),
        _resident((1, d)),
        _mod_spec(layer, 0, group),
        _mod_spec(layer, 1, group),
        _resident_layer(w, sub),
    ]
    args = [x, gain.reshape(1, d), mods, mods, w]
    if odd:
        in_specs += [_resident(a.shape) for a in head_gains]
        args += list(head_gains)
    if is_lat:
        in_specs += [pl.BlockSpec((tm, LANES), lambda i: (i % per_seq, 0))] * 3
        args += list(rope)
    if odd:
        n_qk = (C_HEADS + C_KV_HEADS) * HEAD_DIM
        lat_heads, ctx_heads = [C_KV_HEADS], [C_KV_HEADS, C_KV_HEADS]
    else:
        n_qk = (2 * A_HEADS + B_HEADS + B_KV_HEADS) * HEAD_DIM
        lat_heads, ctx_heads = [A_HEADS, B_KV_HEADS], [A_HEADS, A_HEADS, B_KV_HEADS, B_KV_HEADS]
    out_specs = [pl.BlockSpec((tm, n_qk), row)]
    out_shape = [jax.ShapeDtypeStruct((t, n_qk), BF16)]
    for heads in (lat_heads if is_lat else ctx_heads):
        blk = (per_tile, heads, HEAD_DIM, min(tm, seq))
        out_specs.append(pl.BlockSpec(blk, lambda i: (i // per_seq, 0, 0, i % per_seq)))
        out_shape.append(jax.ShapeDtypeStruct((t // seq, heads, HEAD_DIM, seq),
                                              BF16 if is_lat else F32))
    body = _pre_odd_kernel if odd else _pre_even_kernel
    return pl.pallas_call(
        functools.partial(body, is_lat=is_lat),
        grid=(t // tm,),
        in_specs=in_specs,
        out_specs=out_specs,
        out_shape=out_shape,
        compiler_params=_params(1),
        name=f"pre_{'odd' if odd else 'even'}_{'lat' if is_lat else 'ctx'}",
    )(*args)


def _batch_spec(arr, nb=1):
    nd = arr.ndim
    return pl.BlockSpec((nb,) + arr.shape[1:], lambda b: (b,) + (0,) * (nd - 1))


def _cache_spec(cache, layer):
    blk = (1, 1) + cache.shape[2:]
    return pl.BlockSpec(blk, lambda b: (b, layer, 0, 0, 0))


def _feature_major(cache):
    return cache.transpose(0, 1, 3, 4, 2)


def _cache_keys(ref, h):
    return ref[0, 0, h].T.astype(BF16)


def _ctx_even_attn_kernel(sink_ref, qk_ref, vat_ref, vbt_ref, o_ref,
                          sa0_ref, sa1_ref, ma_ref, sb0_ref, sb1_ref, mb_ref):
    nb = vat_ref.shape[0]
    n = qk_ref.shape[0] // nb
    na = A_HEADS * HEAD_DIM
    group = B_HEADS // B_KV_HEADS
    qb0 = 2 * na
    kb0 = qb0 + B_HEADS * HEAD_DIM

    def rows_of(b, c0=0, size=n):
        if isinstance(b, int):
            return slice(b * n + c0, b * n + c0 + size)
        return pl.ds(pl.multiple_of(b * n + c0, size), size)

    def following(b, i, count):
        return (lax.rem(b + 1, nb), 0) if i + 1 == count else (b, i + 1)

    slots_a = ((sa0_ref, ma_ref.at[0]), (sa1_ref, ma_ref.at[1]))

    def item_a(cur, nxt, b, h, b_next, h_next):
        c = h_next * HEAD_DIM
        return _pipelined_item(
            cur, nxt, n, q_next=qk_ref[rows_of(b_next), c:c + HEAD_DIM],
            keys_next=lambda c0: qk_ref[rows_of(b_next, c0, KEY_CHUNK), na + c:na + c + HEAD_DIM],
            vt_cur=lambda c0: _with_ones_rows(vat_ref[b, h, :, c0:c0 + KEY_CHUNK]))

    item_a((None, None), slots_a[0], None, None, 0, 0)

    def sequence_a(b, carry):
        for pair in range(A_HEADS // 2):
            outs = []
            for h in (2 * pair, 2 * pair + 1):
                ot = item_a(slots_a[h % 2], slots_a[(h + 1) % 2], b, h, *following(b, h, A_HEADS))
                outs.append(ot.T)
            o_ref[rows_of(b), pair * LANES:(pair + 1) * LANES] = (
                jnp.concatenate(outs, axis=1).astype(BF16))
        return carry

    lax.fori_loop(0, nb, sequence_a, 0)

    slots_b = ((sb0_ref, mb_ref.at[0]), (sb1_ref, mb_ref.at[1]))
    sinks = [_sink_row(sink_ref, kv * group, group, n) for kv in range(B_KV_HEADS)]

    def item_b(cur, nxt, b, kv, b_next, kv_next):
        kcols = slice(kb0 + kv_next * HEAD_DIM, kb0 + (kv_next + 1) * HEAD_DIM)
        ot = _pipelined_item(
            cur, nxt, n,
            q_next=_stack_group(qk_ref, rows_of(b_next), qb0 + kv_next * group * HEAD_DIM, group),
            keys_next=lambda c0: qk_ref[rows_of(b_next, c0, KEY_CHUNK), kcols],
            vt_cur=lambda c0: _with_ones_rows(vbt_ref[b, kv, :, c0:c0 + KEY_CHUNK]),
            sink_cur=None if kv is None else sinks[kv], sink_next=sinks[kv_next])
        if ot is None:
            return None
        return jnp.concatenate([ot[:, g * n:(g + 1) * n].T for g in range(group)], axis=1)

    item_b((None, None), slots_b[0], None, None, 0, 0)

    def sequence_b(b, carry):
        for kv in range(B_KV_HEADS):
            o = item_b(slots_b[kv % 2], slots_b[(kv + 1) % 2], b, kv,
                       *following(b, kv, B_KV_HEADS))
            c0 = na + kv * group * HEAD_DIM
            o_ref[rows_of(b), c0:c0 + group * HEAD_DIM] = o.astype(BF16)
        return carry

    lax.fori_loop(0, nb, sequence_b, 0)


def _ctx_even_attention(qk, vat, vbt, sink, seq):
    t, n = qk.shape
    width = (A_HEADS + B_HEADS) * HEAD_DIM
    nb = CTX_BATCHES
    return pl.pallas_call(
        _ctx_even_attn_kernel,
        grid=(t // (nb * seq),),
        in_specs=[pl.BlockSpec(memory_space=pltpu.SMEM),
                  pl.BlockSpec((nb * seq, n), lambda b: (b, 0)),
                  _batch_spec(vat, nb), _batch_spec(vbt, nb)],
        out_specs=pl.BlockSpec((nb * seq, width), lambda b: (b, 0)),
        out_shape=jax.ShapeDtypeStruct((t, width), BF16),
        scratch_shapes=(_score_slots(seq, seq)
                        + _score_slots(seq, B_HEADS // B_KV_HEADS * seq)),
        compiler_params=_params(1),
        name="attn_even_ctx",
    )(sink, qk, vat, vbt)


def _na_bias_tiles(src_ref, h):
    kc = lax.broadcasted_iota(jnp.int32, (GRID_W, LANES), 0)
    qc = lax.rem(lax.broadcasted_iota(jnp.int32, (GRID_W, LANES), 1), GRID_W)
    cs = jnp.clip(qc - NA_WIN_W // 2, 0, GRID_W - NA_WIN_W)
    col_ok = (kc >= cs) & (kc < cs + NA_WIN_W)
    tiles = []
    for i in range(2 * NA_WIN_H):
        src = jnp.broadcast_to(src_ref[h, i:i + 1, :], (GRID_W, LANES))
        tiles.append(jnp.where(col_ok, pltpu.roll(src, 0, 1, stride=1, stride_axis=0), NEG_INF))
    return tiles


def _na_row_window(qr, grid_rows):
    wh = min(NA_WIN_H, grid_rows)
    start = min(max(qr - wh // 2, 0), grid_rows - wh)
    return start, start + wh


def _na_reachable_query_rows(key_rows, grid_rows):
    per_block = Q_TILE // GRID_W
    hit = [qr for qr in range(grid_rows)
           if any(_na_row_window(qr, grid_rows)[0] <= kr < _na_row_window(qr, grid_rows)[1]
                  for kr in key_rows)]
    return min(hit) // per_block * per_block, (max(hit) // per_block + 1) * per_block


def _na_bias_block(tiles, key_rows, q_rows, grid_rows):
    first_row = lax.broadcasted_iota(jnp.int32, (GRID_W, LANES), 1) < GRID_W

    def in_window(kr, qr):
        start, stop = _na_row_window(qr, grid_rows)
        return start <= kr < stop

    strips = []
    for kr in key_rows:
        parts = []
        for qr in range(q_rows.start, q_rows.stop, 2):
            ok0 = in_window(kr, qr)
            ok1 = in_window(kr, qr + 1)
            i = kr - qr + NA_WIN_H - 1
            if ok0 and ok1:
                parts.append(tiles[i])
            elif ok0:
                parts.append(jnp.where(first_row, tiles[i], NEG_INF))
            elif ok1:
                parts.append(jnp.where(first_row, NEG_INF, tiles[i]))
            else:
                parts.append(jnp.full((GRID_W, LANES), NEG_INF, F32))
        strips.append(jnp.concatenate(parts, axis=1))
    return jnp.concatenate(strips, axis=0)


def _lat_a_kernel(src_ref, q_ref, k_ref, vt_ref, ck_ref, cv_ref, o_ref,
                  qs_ref, kall_ref, vtall_ref, os_ref, s0_ref, s1_ref, m_ref):
    n = q_ref.shape[0]
    past = ck_ref.shape[4]
    grid_rows = n // GRID_W
    for h in range(A_HEADS):
        cols = slice(h * HEAD_DIM, (h + 1) * HEAD_DIM)
        qs_ref[h] = q_ref[:, cols]
        kall_ref[h, 0:n, :] = k_ref[:, cols]
        kall_ref[h, n:n + past, :] = _cache_keys(ck_ref, h)
        vtall_ref[h, 0:HEAD_DIM, 0:n] = vt_ref[0, h]
        vtall_ref[h, 0:HEAD_DIM, n:n + past] = cv_ref[0, 0, h].astype(BF16)
        vtall_ref[h, HEAD_DIM:, :] = jnp.ones((ONES_ROWS, n + past), BF16)

    slots = ((s0_ref, m_ref.at[0]), (s1_ref, m_ref.at[1]))

    def key_rows(c0):
        return range(c0 // GRID_W, (c0 + KEY_CHUNK) // GRID_W)

    def query_rows(c0):
        if c0 >= n:
            return range(0, grid_rows)
        return range(*_na_reachable_query_rows(key_rows(c0), grid_rows))

    def step(cur, nxt, h, h_next):
        tiles = _na_bias_tiles(src_ref, h_next)

        def bias(c0):
            if c0 >= n:
                return None
            return _na_bias_block(tiles, key_rows(c0), query_rows(c0), grid_rows)

        return _pipelined_item(
            cur, nxt, n + past, q_next=qs_ref[h_next],
            keys_next=lambda c0: kall_ref[h_next, c0:c0 + KEY_CHUNK, :],
            vt_cur=lambda c0: vtall_ref[h, :, c0:c0 + KEY_CHUNK], bias_next=bias,
            cols_of=lambda c0: (query_rows(c0).start * GRID_W, query_rows(c0).stop * GRID_W))

    step((None, None), slots[0], None, 0)

    def head_pair(p, carry):
        h0 = 2 * p
        os_ref[h0] = step(slots[0], slots[1], h0, h0 + 1).T.astype(BF16)
        os_ref[h0 + 1] = step(slots[1], slots[0], h0 + 1, lax.rem(h0 + 2, A_HEADS)).T.astype(BF16)
        return carry

    lax.fori_loop(0, A_HEADS // 2, head_pair, 0)
    for pair in range(A_HEADS // 2):
        o_ref[:, pair * LANES:(pair + 1) * LANES] = jnp.concatenate(
            [os_ref[2 * pair], os_ref[2 * pair + 1]], axis=1)


def _lat_a_attention(qk, vt, cache_k, cache_v, e, bias_src, seq):
    t = qk.shape[0]
    past = cache_k.shape[4]
    na = A_HEADS * HEAD_DIM
    return pl.pallas_call(
        _lat_a_kernel,
        grid=(t // seq,),
        in_specs=[
            _resident(bias_src.shape),
            pl.BlockSpec((seq, na), lambda b: (b, 0)),
            pl.BlockSpec((seq, na), lambda b: (b, 1)),
            _batch_spec(vt),
            _cache_spec(cache_k, e),
            _cache_spec(cache_v, e),
        ],
        out_specs=pl.BlockSpec((seq, na), lambda b: (b, 0)),
        out_shape=jax.ShapeDtypeStruct((t, na), BF16),
        scratch_shapes=[pltpu.VMEM((A_HEADS, seq, HEAD_DIM), BF16),
                        pltpu.VMEM((A_HEADS, seq + past, HEAD_DIM), BF16),
                        pltpu.VMEM((A_HEADS, HEAD_DIM + ONES_ROWS, seq + past), BF16),
                        pltpu.VMEM((A_HEADS, seq, HEAD_DIM), BF16)]
                       + _score_slots(seq + past, seq),
        compiler_params=_params(1),
        name="attn_even_lat_a",
    )(bias_src, qk, qk, vt, cache_k, cache_v)


def _lat_b_kernel(sink_ref, q_ref, k_ref, vt_ref, ck_ref, cv_ref, o_ref):
    n = q_ref.shape[0]
    group = B_HEADS // B_KV_HEADS
    win = 2 * Q_TILE
    ctx = [(_cache_keys(ck_ref, kv), _with_ones_rows(cv_ref[0, 0, kv]), None, None)
           for kv in range(B_KV_HEADS)]
    for j in range(n // Q_TILE):
        lo = min(max(Q_TILE * j - B_WINDOW, 0), n - win)
        q_rows = slice(j * Q_TILE, (j + 1) * Q_TILE)
        kpos = lo + lax.broadcasted_iota(jnp.int32, (win, 1), 0)
        qpos = j * Q_TILE + lax.broadcasted_iota(jnp.int32, (1, group * Q_TILE), 1) % Q_TILE
        valid = jnp.abs(qpos - kpos) <= B_WINDOW
        for kv in range(B_KV_HEADS):
            q = _stack_group(q_ref, q_rows, kv * group * HEAD_DIM, group)
            local = (k_ref[lo:lo + win, kv * HEAD_DIM:(kv + 1) * HEAD_DIM],
                     _with_ones_rows(vt_ref[0, kv, :, lo:lo + win]), None, valid)
            o = _attend([local, ctx[kv]], q, Q_TILE, group,
                        _sink_row(sink_ref, kv * group, group, Q_TILE))
            c0 = kv * group * HEAD_DIM
            o_ref[q_rows, c0:c0 + group * HEAD_DIM] = o.astype(BF16)


def _lat_b_attention(qk, vt, cache_k, cache_v, e, sink, seq):
    t = qk.shape[0]
    na = A_HEADS * HEAD_DIM
    nb = B_HEADS * HEAD_DIM
    nkv = B_KV_HEADS * HEAD_DIM
    return pl.pallas_call(
        _lat_b_kernel,
        grid=(t // seq,),
        in_specs=[
            pl.BlockSpec(memory_space=pltpu.SMEM),
            pl.BlockSpec((seq, nb), lambda b: (b, 2 * na // nb)),
            pl.BlockSpec((seq, nkv), lambda b: (b, (2 * na + nb) // nkv)),
            _batch_spec(vt),
            _cache_spec(cache_k, e),
            _cache_spec(cache_v, e),
        ],
        out_specs=pl.BlockSpec((seq, nb), lambda b: (b, 0)),
        out_shape=jax.ShapeDtypeStruct((t, nb), BF16),
        compiler_params=_params(1),
        name="attn_even_lat_b",
    )(sink, qk, qk, vt, cache_k, cache_v)


def _ctx_odd_attn_kernel(qk_ref, vt_ref, o_ref, s0_ref, s1_ref, m_ref):
    nb = vt_ref.shape[0]
    n = qk_ref.shape[0] // nb
    group = C_HEADS // C_KV_HEADS
    nq = C_HEADS * HEAD_DIM

    def rows_of(b, c0=0, size=n):
        if isinstance(b, int):
            return slice(b * n + c0, b * n + c0 + size)
        return pl.ds(pl.multiple_of(b * n + c0, size), size)

    slots = ((s0_ref, m_ref.at[0]), (s1_ref, m_ref.at[1]))

    def item(cur, nxt, b, kv, b_next, kv_next):
        kcols = slice(nq + kv_next * HEAD_DIM, nq + (kv_next + 1) * HEAD_DIM)
        ot = _pipelined_item(
            cur, nxt, n,
            q_next=_stack_group(qk_ref, rows_of(b_next), kv_next * group * HEAD_DIM, group),
            keys_next=lambda c0: qk_ref[rows_of(b_next, c0, KEY_CHUNK), kcols],
            vt_cur=lambda c0: _with_ones_rows(vt_ref[b, kv, :, c0:c0 + KEY_CHUNK]))
        if ot is None:
            return None
        return jnp.concatenate([ot[:, g * n:(g + 1) * n].T for g in range(group)], axis=1)

    item((None, None), slots[0], None, None, 0, 0)

    def sequence(b, carry):
        for kv in range(C_KV_HEADS):
            last = kv + 1 == C_KV_HEADS
            o = item(slots[kv % 2], slots[(kv + 1) % 2], b, kv,
                     lax.rem(b + 1, nb) if last else b, 0 if last else kv + 1)
            c0 = kv * group * HEAD_DIM
            o_ref[rows_of(b), c0:c0 + group * HEAD_DIM] = o.astype(BF16)
        return carry

    lax.fori_loop(0, nb, sequence, 0)


def _ctx_odd_attention(qk, vt, seq):
    t, n = qk.shape
    width = C_HEADS * HEAD_DIM
    nb = CTX_BATCHES
    return pl.pallas_call(
        _ctx_odd_attn_kernel,
        grid=(t // (nb * seq),),
        in_specs=[pl.BlockSpec((nb * seq, n), lambda b: (b, 0)), _batch_spec(vt, nb)],
        out_specs=pl.BlockSpec((nb * seq, width), lambda b: (b, 0)),
        out_shape=jax.ShapeDtypeStruct((t, width), BF16),
        scratch_shapes=_score_slots(seq, C_HEADS // C_KV_HEADS * seq),
        compiler_params=_params(1),
        name="attn_odd_ctx",
    )(qk, vt)


def _lat_c_kernel(qk_ref, vt_ref, ck_ref, cv_ref, o_ref, kall_ref, vtall_ref, s0_ref, s1_ref,
                  m_ref):
    n = qk_ref.shape[0]
    past = ck_ref.shape[4]
    group = C_HEADS // C_KV_HEADS
    nq = C_HEADS * HEAD_DIM
    n_blocks = n // Q_TILE
    for kv in range(C_KV_HEADS):
        kall_ref[kv, 0:past, :] = _cache_keys(ck_ref, kv)
        kall_ref[kv, past:past + n, :] = qk_ref[:, nq + kv * HEAD_DIM:nq + (kv + 1) * HEAD_DIM]
        vtall_ref[kv, 0:HEAD_DIM, 0:past] = cv_ref[0, 0, kv].astype(BF16)
        vtall_ref[kv, 0:HEAD_DIM, past:past + n] = vt_ref[0, kv]
        vtall_ref[kv, HEAD_DIM:, :] = jnp.ones((ONES_ROWS, past + n), BF16)

    def rows_of(j):
        if isinstance(j, int):
            return slice(j * Q_TILE, (j + 1) * Q_TILE)
        return pl.ds(pl.multiple_of(j * Q_TILE, Q_TILE), Q_TILE)

    slots = ((s0_ref, m_ref.at[0]), (s1_ref, m_ref.at[1]))

    def item(cur, nxt, kv, j_next, kv_next):
        ot = _pipelined_item(
            cur, nxt, past + n,
            q_next=_stack_group(qk_ref, rows_of(j_next), kv_next * group * HEAD_DIM, group),
            keys_next=lambda c0: kall_ref[kv_next, c0:c0 + KEY_CHUNK, :],
            vt_cur=lambda c0: vtall_ref[kv, :, c0:c0 + KEY_CHUNK])
        if ot is None:
            return None
        return jnp.concatenate(
            [ot[:, g * Q_TILE:(g + 1) * Q_TILE].T for g in range(group)], axis=1)

    item((None, None), slots[0], None, 0, 0)

    def q_block(j, carry):
        for kv in range(C_KV_HEADS):
            last = kv + 1 == C_KV_HEADS
            o = item(slots[kv % 2], slots[(kv + 1) % 2], kv,
                     lax.rem(j + 1, n_blocks) if last else j, 0 if last else kv + 1)
            c0 = kv * group * HEAD_DIM
            o_ref[rows_of(j), c0:c0 + group * HEAD_DIM] = o.astype(BF16)
        return carry

    lax.fori_loop(0, n_blocks, q_block, 0)


def _lat_c_attention(qk, vt, cache_k, cache_v, o, seq):
    t, n = qk.shape
    past = cache_k.shape[4]
    width = C_HEADS * HEAD_DIM
    return pl.pallas_call(
        _lat_c_kernel,
        grid=(t // seq,),
        in_specs=[
            pl.BlockSpec((seq, n), lambda b: (b, 0)),
            _batch_spec(vt),
            _cache_spec(cache_k, o),
            _cache_spec(cache_v, o),
        ],
        out_specs=pl.BlockSpec((seq, width), lambda b: (b, 0)),
        out_shape=jax.ShapeDtypeStruct((t, width), BF16),
        scratch_shapes=[pltpu.VMEM((C_KV_HEADS, past + seq, HEAD_DIM), BF16),
                        pltpu.VMEM((C_KV_HEADS, HEAD_DIM + ONES_ROWS, past + seq), BF16)]
                       + _score_slots(past + seq, C_HEADS // C_KV_HEADS * Q_TILE),
        compiler_params=_params(1),
        name="attn_odd_lat",
    )(qk, vt, cache_k, cache_v)


def _post_kernel(*refs, n_parts, final):
    o_refs = refs[:n_parts]
    (x_ref, wo_ref, g1_ref, sh_ref, sc_ref, g2_ref, gain_ref, wgu_ref, wd_ref) = refs[n_parts:n_parts + 9]
    rest = refs[n_parts + 9:]
    if final:
        fg_ref, out_ref, act_ref = rest
    else:
        out_ref, act_ref = rest
    mix = None
    r0 = 0
    for o_ref in o_refs:
        kk = o_ref.shape[1]
        part = jnp.dot(o_ref[...], wo_ref[r0:r0 + kk, :], preferred_element_type=F32)
        mix = part if mix is None else mix + part
        r0 += kk
    x1 = x_ref[...] + g1_ref[0, 0] * mix
    h = _adaln(x1, gain_ref[...], sh_ref[0, 0], sc_ref[0, 0]).astype(BF16)
    d_ff = wd_ref.shape[0]
    for j in range(d_ff // FF_CHUNK):
        c0 = j * FF_CHUNK
        gate = jnp.dot(h, wgu_ref[:, c0:c0 + FF_CHUNK], preferred_element_type=F32)
        up = jnp.dot(h, wgu_ref[:, d_ff + c0:d_ff + c0 + FF_CHUNK], preferred_element_type=F32)
        act_ref[:, c0:c0 + FF_CHUNK] = (gate * jax.nn.sigmoid(gate) * up).astype(BF16)
    ffn = jnp.dot(act_ref[...], wd_ref[...], preferred_element_type=F32)
    x2 = x1 + g2_ref[0, 0] * ffn
    if final:
        ms = jnp.mean(x2 * x2, axis=-1, keepdims=True)
        x2 = (x2 * lax.rsqrt(ms + RMS_EPS)) * fg_ref[...]
    out_ref[...] = x2


def _post_attention(o_parts, x, mods, layer, gain, w_out, sub, w_gu, w_down, *, is_lat, seq,
                    final_gain=None):
    t, d = x.shape
    tm = TOKEN_TILE
    per_seq = max(seq // tm, 1)
    group = (lambda i: 1 + i // per_seq) if is_lat else (lambda i: 0)
    row = lambda i: (i, 0)
    d_ff = w_down.shape[1]
    final = final_gain is not None
    in_specs = [pl.BlockSpec((tm, o.shape[1]), row) for o in o_parts]
    in_specs += [
        pl.BlockSpec((tm, d), row),
        _resident_layer(w_out, sub),
        _mod_spec(layer, 2, group),
        _mod_spec(layer, 3, group),
        _mod_spec(layer, 4, group),
        _mod_spec(layer, 5, group),
        _resident((1, d)),
        _resident_layer(w_gu, layer),
        _resident_layer(w_down, layer),
    ]
    args = list(o_parts) + [x, w_out, mods, mods, mods, mods, gain.reshape(1, d), w_gu, w_down]
    if final:
        in_specs.append(_resident((1, d)))
        args.append(final_gain.reshape(1, d))
    return pl.pallas_call(
        functools.partial(_post_kernel, n_parts=len(o_parts), final=final),
        grid=(t // tm,),
        in_specs=in_specs,
        out_specs=pl.BlockSpec((tm, d), row),
        out_shape=jax.ShapeDtypeStruct((t, d), F32),
        scratch_shapes=[pltpu.VMEM((tm, d_ff), BF16)],
        compiler_params=_params(1),
        name=f"post_{'lat' if is_lat else 'ctx'}{'_final' if final else ''}",
    )(*args)


def _rope_tables(n):
    t = np.arange(n)
    row = (t // GRID_W).astype(np.float32)
    col = (t % GRID_W).astype(np.float32)
    half = HEAD_DIM // 2
    inv_freq = np.float32(ROPE_THETA) ** (-np.arange(0, half, 2, dtype=np.float32) / np.float32(half))
    lane = np.arange(LANES)
    in_head = lane % HEAD_DIM
    pos = np.where((in_head < half)[None, :], row[:, None], col[:, None])
    ang = (pos * inv_freq[in_head % (half // 2)][None, :]).astype(np.float32)
    first = ((in_head % half) < half // 2)[None, :]
    cos = np.cos(ang)
    sin = np.sin(ang)
    zero = np.float32(0.0)
    return tuple(jnp.asarray(a, F32) for a in
                 (cos, np.where(first, -sin, zero), np.where(first, zero, sin)))


def _na_bias_sources(rpb):
    h, _, nb = rpb.shape
    w = NA_WIN_W - 1
    rp = jnp.pad(rpb[:, :, ::-1] * LOG2E, ((0, 0), (1, 1), (0, 0)))
    this, prev = rp[:, 1:], rp[:, :-1]
    z = jnp.zeros((h, 2 * NA_WIN_H, LANES // 2 - nb), F32)
    return jnp.concatenate([this[:, :, w:], z, prev, z, this[:, :, :w]], axis=-1)


def _state(y):
    return y.transpose(0, 3, 1, 2)


def kernel(x_prompt, x_sample, cache_a_k, cache_a_v, cache_b_k, cache_b_v, cache_c_k, cache_c_v,
           c, c_ctx, norm_gain, w_mod, b_mod, w_in_even, w_out_even, rpb_a, sink_b,
           w_in_odd, w_out_odd, q_norm_c, k_norm_c, w_gate_up, w_down, final_gain):
    batch, seq, d = x_prompt.shape
    dec_batch, dec_seq, _ = x_sample.shape
    depth = w_mod.shape[0]

    cvec = jnp.concatenate(
        [c_ctx[None, :], c, jnp.zeros((MOD_GROUPS - 1 - dec_batch, d), F32)], axis=0)
    mods = _modulation(cvec, w_mod, b_mod).reshape(depth, MOD_GROUPS, 1, 6 * d)
    rope = _rope_tables(dec_seq)
    w_in = {False: w_in_even.astype(BF16), True: w_in_odd.astype(BF16)}
    w_out = {False: w_out_even.astype(BF16), True: w_out_odd.astype(BF16)}
    w_gu = w_gate_up.astype(BF16)
    w_dn = w_down.astype(BF16)

    ctx = x_prompt.reshape(batch * seq, d)
    lat = x_sample.reshape(dec_batch * dec_seq, d)
    states = {name: [] for name in ("a_k", "a_v", "b_k", "b_v", "c_k", "c_v")}

    for layer in range(depth):
        odd = layer % 2 == 1
        sub = layer // 2
        gain1, gain2 = norm_gain[layer, 0], norm_gain[layer, 1]
        pre = functools.partial(_pre_attention, mods=mods, layer=layer, gain=gain1, w=w_in[odd],
                                sub=sub, odd=odd)
        if not odd:
            qk_c, ka, va, kb, vb = pre(ctx, is_lat=False, seq=seq)
            qk_l, vat_l, vbt_l = pre(lat, is_lat=True, seq=dec_seq, rope=rope)
            for name, y in (("a_k", ka), ("a_v", va), ("b_k", kb), ("b_v", vb)):
                states[name].append(_state(y))
            o_ctx = [_ctx_even_attention(qk_c, va, vb, sink_b[sub], seq)]
            o_lat = [
                _lat_a_attention(qk_l, vat_l, _feature_major(cache_a_k), _feature_major(cache_a_v),
                                 sub, _na_bias_sources(rpb_a[sub]), dec_seq),
                _lat_b_attention(qk_l, vbt_l, _feature_major(cache_b_k), _feature_major(cache_b_v),
                                 sub, sink_b[sub], dec_seq),
            ]
        else:
            per = LANES // HEAD_DIM
            head_gains = (jnp.tile(q_norm_c[sub], per).reshape(1, LANES),
                          jnp.tile(k_norm_c[sub], per).reshape(1, LANES))
            qk_c, kc, vc = pre(ctx, is_lat=False, seq=seq, head_gains=head_gains)
            qk_l, vct_l = pre(lat, is_lat=True, seq=dec_seq, rope=rope, head_gains=head_gains)
            states["c_k"].append(_state(kc))
            states["c_v"].append(_state(vc))
            o_ctx = [_ctx_odd_attention(qk_c, vc, seq)]
            o_lat = [_lat_c_attention(qk_l, vct_l, _feature_major(cache_c_k),
                                      _feature_major(cache_c_v), sub, dec_seq)]
        fg = final_gain if layer == depth - 1 else None
        post = functools.partial(_post_attention, mods=mods, layer=layer, gain=gain2,
                                 w_out=w_out[odd], sub=sub, w_gu=w_gu, w_down=w_dn, final_gain=fg)
        ctx = post(o_ctx, ctx, is_lat=False, seq=seq)
        lat = post(o_lat, lat, is_lat=True, seq=dec_seq)

    return (ctx.reshape(batch, seq, d), lat.reshape(dec_batch, dec_seq, d),
            jnp.stack(states["a_k"], axis=1), jnp.stack(states["a_v"], axis=1),
            jnp.stack(states["b_k"], axis=1), jnp.stack(states["b_v"], axis=1),
            jnp.stack(states["c_k"], axis=1), jnp.stack(states["c_v"], axis=1))
```

```python
import functools
import math

import jax
import jax.numpy as jnp
import numpy as np
from jax import lax
from jax.experimental import pallas as pl
from jax.experimental.pallas import tpu as pltpu

F32 = jnp.float32
BF16 = jnp.bfloat16

D_MODEL = 1024
GRID_W = 64
HEAD_DIM = 64
A_HEADS = 8
B_HEADS = 8
B_KV_HEADS = 2
C_HEADS = 16
C_KV_HEADS = 4
NA_WIN_H = 8
NA_WIN_W = 16
B_WINDOW = 128
ROPE_THETA = 10000.0
RMS_EPS = 1e-6
NEG_INF = -1e30
LOG2E = math.log2(math.e)
QK_SCALE = LOG2E / math.sqrt(HEAD_DIM)

LANES = 128
TOKEN_TILE = 512
Q_TILE = 256
CTX_BATCHES = 8
ONES_ROWS = 16
KEY_CHUNK = 256
FF_CHUNK = 256
MOD_GROUPS = 16
VMEM_LIMIT = 56 * 1024 * 1024


def _params(n_axes, vmem=VMEM_LIMIT):
    return pltpu.CompilerParams(
        dimension_semantics=("arbitrary",) * n_axes, vmem_limit_bytes=vmem)


def _resident(shape):
    nd = len(shape)
    return pl.BlockSpec(shape, lambda *_: (0,) * nd, pipeline_mode=pl.Buffered(1))


def _resident_layer(stacked, layer):
    return pl.BlockSpec((None,) + stacked.shape[1:], lambda *_: (layer, 0, 0),
                        pipeline_mode=pl.Buffered(1))


def _mod_kernel(c_ref, w_ref, b_ref, o_ref):
    c = c_ref[...]
    s = (c * jax.nn.sigmoid(c)).astype(BF16)
    o_ref[0] = jnp.dot(s, w_ref[0].astype(BF16), preferred_element_type=F32) + b_ref[0]


def _modulation(cvec, w_mod, b_mod):
    depth, d, n = w_mod.shape
    tn = 1536
    return pl.pallas_call(
        _mod_kernel,
        grid=(depth, n // tn),
        in_specs=[
            pl.BlockSpec((MOD_GROUPS, d), lambda l, j: (0, 0)),
            pl.BlockSpec((1, d, tn), lambda l, j: (l, 0, j)),
            pl.BlockSpec((1, 1, tn), lambda l, j: (l, 0, j)),
        ],
        out_specs=pl.BlockSpec((1, MOD_GROUPS, tn), lambda l, j: (l, 0, j)),
        out_shape=jax.ShapeDtypeStruct((depth, MOD_GROUPS, n), F32),
        compiler_params=_params(2),
        name="modulation",
    )(cvec, w_mod, b_mod.reshape(depth, 1, n))


def _mod_spec(layer, which, group_of_step):
    return pl.BlockSpec((1, 1, 1, D_MODEL), lambda i: (layer, group_of_step(i), 0, which))


def _adaln(x, gain, shift, scale):
    ms = jnp.mean(x * x, axis=-1, keepdims=True)
    return (x * lax.rsqrt(ms + RMS_EPS)) * gain * (1.0 + scale) + shift


def _rope(y, cos, sin_lo, sin_hi):
    outs = []
    for c in range(y.shape[1] // LANES):
        yc = y[:, c * LANES:(c + 1) * LANES]
        outs.append(yc * cos
                    + pltpu.roll(yc, LANES - 16, 1) * sin_lo
                    + pltpu.roll(yc, 16, 1) * sin_hi)
    return outs[0] if len(outs) == 1 else jnp.concatenate(outs, axis=1)


def _head_rms_norm(y, gain):
    first = lax.broadcasted_iota(jnp.int32, (1, LANES), 1) < HEAD_DIM
    outs = []
    for c in range(y.shape[1] // LANES):
        yc = y[:, c * LANES:(c + 1) * LANES]
        sq = yc * yc
        s0 = jnp.sum(jnp.where(first, sq, 0.0), axis=-1, keepdims=True)
        s1 = jnp.sum(jnp.where(first, 0.0, sq), axis=-1, keepdims=True)
        ms = jnp.where(first, s0, s1) * (1.0 / HEAD_DIM)
        outs.append(yc * lax.rsqrt(ms + RMS_EPS) * gain)
    return outs[0] if len(outs) == 1 else jnp.concatenate(outs, axis=1)


_NT = (((1,), (1,)), ((), ()))


def _with_ones_rows(vt):
    return jnp.concatenate([vt.astype(BF16), jnp.ones((ONES_ROWS, vt.shape[1]), BF16)], axis=0)


def _attend(segments, q, rows, group, sink=None):
    scores = []
    for k, _, bias, valid in segments:
        s = _scores(k, q)
        if bias is not None:
            s = s + bias
        if valid is not None:
            s = jnp.where(valid, s, NEG_INF)
        scores.append(s)
    return _softmax_values(scores, [seg[1] for seg in segments], rows, group, sink)


def _scores(k, q):
    return lax.dot_general(k, q, _NT, preferred_element_type=F32)


def _softmax_values(scores, vts, rows, group, sink=None):
    m = functools.reduce(jnp.maximum, [jnp.max(s, axis=0, keepdims=True) for s in scores])
    if sink is not None:
        m = jnp.maximum(m, sink)
    ot = None
    for s, vt_ones in zip(scores, vts):
        part = jnp.dot(vt_ones, jnp.exp2(s - m).astype(BF16), preferred_element_type=F32)
        ot = part if ot is None else ot + part
    denom = ot[HEAD_DIM:HEAD_DIM + 1, :]
    if sink is not None:
        denom = denom + jnp.exp2(sink - m)
    ot = ot[:HEAD_DIM, :] / denom
    return jnp.concatenate([ot[:, g * rows:(g + 1) * rows].T for g in range(group)], axis=1)


def _pipelined_item(cur, nxt, n_keys, *, q_next, keys_next, vt_cur, bias_next=None,
                    cols_of=None, sink_cur=None, sink_next=None):
    s_cur, m_cur = cur
    s_nxt, m_nxt = nxt
    width = s_nxt.shape[1]
    if cols_of is None:
        cols_of = lambda c0: (0, width)
    m = None if s_cur is None else m_cur[...]
    m_next = [None] * (width // Q_TILE)
    ot = [None] * (width // Q_TILE)
    for c0 in range(0, n_keys, KEY_CHUNK):
        keys = slice(c0, c0 + KEY_CHUNK)
        lo, hi = cols_of(c0)
        blocks = [(g, slice(g * Q_TILE - lo, (g + 1) * Q_TILE - lo))
                  for g in range(lo // Q_TILE, hi // Q_TILE)]
        s = _scores(keys_next(c0), q_next[lo:hi])
        bias = None if bias_next is None else bias_next(c0)
        if bias is not None:
            s = s + bias
        s_nxt[keys, lo:hi] = s
        m_c = jnp.max(s, axis=0, keepdims=True)
        for g, cols in blocks:
            m_next[g] = m_c[:, cols] if m_next[g] is None else jnp.maximum(m_next[g], m_c[:, cols])
        if s_cur is not None:
            p = jnp.exp2(s_cur[keys, lo:hi] - m[:, lo:hi]).astype(BF16)
            part = jnp.dot(vt_cur(c0), p, preferred_element_type=F32)
            for g, cols in blocks:
                ot[g] = part[:, cols] if ot[g] is None else ot[g] + part[:, cols]
    m_next = jnp.concatenate(m_next, axis=1)
    if sink_next is not None:
        m_next = jnp.maximum(m_next, sink_next)
    m_nxt[...] = m_next
    if s_cur is None:
        return None
    ot = jnp.concatenate(ot, axis=1)
    denom = ot[HEAD_DIM:HEAD_DIM + 1, :]
    if sink_cur is not None:
        denom = denom + jnp.exp2(sink_cur - m)
    return ot[:HEAD_DIM, :] / denom


def _score_slots(n_keys, m):
    return [pltpu.VMEM((n_keys, m), F32), pltpu.VMEM((n_keys, m), F32),
            pltpu.VMEM((2, 1, m), F32)]


def _stack_group(ref, rows, col0, group):
    return jnp.concatenate(
        [ref[rows, col0 + g * HEAD_DIM: col0 + (g + 1) * HEAD_DIM] for g in range(group)], axis=0)


def _sink_row(sink_ref, h0, group, rows):
    return jnp.concatenate(
        [jnp.full((1, rows), sink_ref[h0 + g] * LOG2E, F32) for g in range(group)], axis=1)


def _store_feature_major(ref, y):
    nb, heads, _, seq = ref.shape
    yt = y.T
    for b in range(nb):
        for h in range(heads):
            ref[b, h] = yt[h * HEAD_DIM:(h + 1) * HEAD_DIM, b * seq:(b + 1) * seq].astype(ref.dtype)


def _pre_even_kernel(x_ref, g_ref, sh_ref, sc_ref, w_ref, *rest, is_lat):
    if is_lat:
        cos_ref, slo_ref, shi_ref, qk_ref, va_ref, vb_ref = rest
    else:
        qk_ref, ka_ref, va_ref, kb_ref, vb_ref = rest
    h = _adaln(x_ref[...], g_ref[...], sh_ref[0, 0], sc_ref[0, 0]).astype(BF16)

    def proj(c0, c1):
        return jnp.dot(h, w_ref[:, c0:c1], preferred_element_type=F32)

    na = A_HEADS * HEAD_DIM
    nb = B_HEADS * HEAD_DIM
    nkv = B_KV_HEADS * HEAD_DIM
    qk_ref[:, 0:na] = (proj(0, na) * QK_SCALE).astype(BF16)
    ka = proj(na, 2 * na)
    qk_ref[:, na:2 * na] = ka.astype(BF16)
    va = proj(2 * na, 3 * na)
    qb = proj(3 * na, 3 * na + nb)
    if is_lat:
        qb = _rope(qb, cos_ref[...], slo_ref[...], shi_ref[...])
    qk_ref[:, 2 * na:2 * na + nb] = (qb * QK_SCALE).astype(BF16)
    kvb = proj(3 * na + nb, 3 * na + nb + 2 * nkv)
    kb, vb = kvb[:, :nkv], kvb[:, nkv:]
    if is_lat:
        kb_out = _rope(kb, cos_ref[...], slo_ref[...], shi_ref[...])
    else:
        kb_out = kb
    qk_ref[:, 2 * na + nb:2 * na + nb + nkv] = kb_out.astype(BF16)
    _store_feature_major(va_ref, va)
    _store_feature_major(vb_ref, vb)
    if not is_lat:
        _store_feature_major(ka_ref, ka)
        _store_feature_major(kb_ref, kb)


def _pre_odd_kernel(x_ref, g_ref, sh_ref, sc_ref, w_ref, qn_ref, kn_ref, *rest, is_lat):
    if is_lat:
        cos_ref, slo_ref, shi_ref, qk_ref, vc_ref = rest
    else:
        qk_ref, kc_ref, vc_ref = rest
    h = _adaln(x_ref[...], g_ref[...], sh_ref[0, 0], sc_ref[0, 0]).astype(BF16)

    def proj(c0, c1):
        return jnp.dot(h, w_ref[:, c0:c1], preferred_element_type=F32)

    nq = C_HEADS * HEAD_DIM
    nkv = C_KV_HEADS * HEAD_DIM
    q = _head_rms_norm(proj(0, nq), qn_ref[...])
    k = _head_rms_norm(proj(nq, nq + nkv), kn_ref[...])
    v = proj(nq + nkv, nq + 2 * nkv)
    if is_lat:
        q = _rope(q, cos_ref[...], slo_ref[...], shi_ref[...])
        k_out = _rope(k, cos_ref[...], slo_ref[...], shi_ref[...])
    else:
        k_out = k
        _store_feature_major(kc_ref, k)
    _store_feature_major(vc_ref, v)
    qk_ref[:, 0:nq] = (q * QK_SCALE).astype(BF16)
    qk_ref[:, nq:nq + nkv] = k_out.astype(BF16)


def _pre_attention(x, mods, layer, gain, w, sub, *, odd, is_lat, seq, rope=None,
                   head_gains=None):
    t, d = x.shape
    tm = TOKEN_TILE
    per_seq = max(seq // tm, 1)
    per_tile = max(tm // seq, 1)
    group = (lambda i: 1 + i // per_seq) if is_lat else (lambda i: 0)
    row = lambda i: (i, 0)
    in_specs = [
        pl.BlockSpec((tm, d), row),
        _resident((1, d)),
        _mod_spec(layer, 0, group),
        _mod_spec(layer, 1, group),
        _resident_layer(w, sub),
    ]
    args = [x, gain.reshape(1, d), mods, mods, w]
    if odd:
        in_specs += [_resident(a.shape) for a in head_gains]
        args += list(head_gains)
    if is_lat:
        in_specs += [pl.BlockSpec((tm, LANES), lambda i: (i % per_seq, 0))] * 3
        args += list(rope)
    if odd:
        n_qk = (C_HEADS + C_KV_HEADS) * HEAD_DIM
        lat_heads, ctx_heads = [C_KV_HEADS], [C_KV_HEADS, C_KV_HEADS]
    else:
        n_qk = (2 * A_HEADS + B_HEADS + B_KV_HEADS) * HEAD_DIM
        lat_heads, ctx_heads = [A_HEADS, B_KV_HEADS], [A_HEADS, A_HEADS, B_KV_HEADS, B_KV_HEADS]
    out_specs = [pl.BlockSpec((tm, n_qk), row)]
    out_shape = [jax.ShapeDtypeStruct((t, n_qk), BF16)]
    for heads in (lat_heads if is_lat else ctx_heads):
        blk = (per_tile, heads, HEAD_DIM, min(tm, seq))
        out_specs.append(pl.BlockSpec(blk, lambda i: (i // per_seq, 0, 0, i % per_seq)))
        out_shape.append(jax.ShapeDtypeStruct((t // seq, heads, HEAD_DIM, seq),
                                              BF16 if is_lat else F32))
    body = _pre_odd_kernel if odd else _pre_even_kernel
    return pl.pallas_call(
        functools.partial(body, is_lat=is_lat),
        grid=(t // tm,),
        in_specs=in_specs,
        out_specs=out_specs,
        out_shape=out_shape,
        compiler_params=_params(1),
        name=f"pre_{'odd' if odd else 'even'}_{'lat' if is_lat else 'ctx'}",
    )(*args)


def _batch_spec(arr, nb=1):
    nd = arr.ndim
    return pl.BlockSpec((nb,) + arr.shape[1:], lambda b: (b,) + (0,) * (nd - 1))


def _cache_spec(cache, layer):
    blk = (1, 1) + cache.shape[2:]
    return pl.BlockSpec(blk, lambda b: (b, layer, 0, 0, 0))


def _feature_major(cache):
    return cache.transpose(0, 1, 3, 4, 2)


def _cache_keys(ref, h):
    return ref[0, 0, h].T.astype(BF16)


def _ctx_even_attn_kernel(sink_ref, qk_ref, vat_ref, vbt_ref, o_ref,
                          sa0_ref, sa1_ref, ma_ref, sb0_ref, sb1_ref, mb_ref):
    nb = vat_ref.shape[0]
    n = qk_ref.shape[0] // nb
    na = A_HEADS * HEAD_DIM
    group = B_HEADS // B_KV_HEADS
    qb0 = 2 * na
    kb0 = qb0 + B_HEADS * HEAD_DIM

    def rows_of(b, c0=0, size=n):
        if isinstance(b, int):
            return slice(b * n + c0, b * n + c0 + size)
        return pl.ds(pl.multiple_of(b * n + c0, size), size)

    def following(b, i, count):
        return (lax.rem(b + 1, nb), 0) if i + 1 == count else (b, i + 1)

    slots_a = ((sa0_ref, ma_ref.at[0]), (sa1_ref, ma_ref.at[1]))

    def item_a(cur, nxt, b, h, b_next, h_next):
        c = h_next * HEAD_DIM
        return _pipelined_item(
            cur, nxt, n, q_next=qk_ref[rows_of(b_next), c:c + HEAD_DIM],
            keys_next=lambda c0: qk_ref[rows_of(b_next, c0, KEY_CHUNK), na + c:na + c + HEAD_DIM],
            vt_cur=lambda c0: _with_ones_rows(vat_ref[b, h, :, c0:c0 + KEY_CHUNK]))

    item_a((None, None), slots_a[0], None, None, 0, 0)

    def sequence_a(b, carry):
        for pair in range(A_HEADS // 2):
            outs = []
            for h in (2 * pair, 2 * pair + 1):
                ot = item_a(slots_a[h % 2], slots_a[(h + 1) % 2], b, h, *following(b, h, A_HEADS))
                outs.append(ot.T)
            o_ref[rows_of(b), pair * LANES:(pair + 1) * LANES] = (
                jnp.concatenate(outs, axis=1).astype(BF16))
        return carry

    lax.fori_loop(0, nb, sequence_a, 0)

    slots_b = ((sb0_ref, mb_ref.at[0]), (sb1_ref, mb_ref.at[1]))
    sinks = [_sink_row(sink_ref, kv * group, group, n) for kv in range(B_KV_HEADS)]

    def item_b(cur, nxt, b, kv, b_next, kv_next):
        kcols = slice(kb0 + kv_next * HEAD_DIM, kb0 + (kv_next + 1) * HEAD_DIM)
        ot = _pipelined_item(
            cur, nxt, n,
            q_next=_stack_group(qk_ref, rows_of(b_next), qb0 + kv_next * group * HEAD_DIM, group),
            keys_next=lambda c0: qk_ref[rows_of(b_next, c0, KEY_CHUNK), kcols],
            vt_cur=lambda c0: _with_ones_rows(vbt_ref[b, kv, :, c0:c0 + KEY_CHUNK]),
            sink_cur=None if kv is None else sinks[kv], sink_next=sinks[kv_next])
        if ot is None:
            return None
        return jnp.concatenate([ot[:, g * n:(g + 1) * n].T for g in range(group)], axis=1)

    item_b((None, None), slots_b[0], None, None, 0, 0)

    def sequence_b(b, carry):
        for kv in range(B_KV_HEADS):
            o = item_b(slots_b[kv % 2], slots_b[(kv + 1) % 2], b, kv,
                       *following(b, kv, B_KV_HEADS))
            c0 = na + kv * group * HEAD_DIM
            o_ref[rows_of(b), c0:c0 + group * HEAD_DIM] = o.astype(BF16)
        return carry

    lax.fori_loop(0, nb, sequence_b, 0)


def _ctx_even_attention(qk, vat, vbt, sink, seq):
    t, n = qk.shape
    width = (A_HEADS + B_HEADS) * HEAD_DIM
    nb = CTX_BATCHES
    return pl.pallas_call(
        _ctx_even_attn_kernel,
        grid=(t // (nb * seq),),
        in_specs=[pl.BlockSpec(memory_space=pltpu.SMEM),
                  pl.BlockSpec((nb * seq, n), lambda b: (b, 0)),
                  _batch_spec(vat, nb), _batch_spec(vbt, nb)],
        out_specs=pl.BlockSpec((nb * seq, width), lambda b: (b, 0)),
        out_shape=jax.ShapeDtypeStruct((t, width), BF16),
        scratch_shapes=(_score_slots(seq, seq)
                        + _score_slots(seq, B_HEADS // B_KV_HEADS * seq)),
        compiler_params=_params(1),
        name="attn_even_ctx",
    )(sink, qk, vat, vbt)


def _na_bias_tiles(src_ref, h):
    kc = lax.broadcasted_iota(jnp.int32, (GRID_W, LANES), 0)
    qc = lax.rem(lax.broadcasted_iota(jnp.int32, (GRID_W, LANES), 1), GRID_W)
    cs = jnp.clip(qc - NA_WIN_W // 2, 0, GRID_W - NA_WIN_W)
    col_ok = (kc >= cs) & (kc < cs + NA_WIN_W)
    tiles = []
    for i in range(2 * NA_WIN_H):
        src = jnp.broadcast_to(src_ref[h, i:i + 1, :], (GRID_W, LANES))
        tiles.append(jnp.where(col_ok, pltpu.roll(src, 0, 1, stride=1, stride_axis=0), NEG_INF))
    return tiles


def _na_row_window(qr, grid_rows):
    wh = min(NA_WIN_H, grid_rows)
    start = min(max(qr - wh // 2, 0), grid_rows - wh)
    return start, start + wh


def _na_reachable_query_rows(key_rows, grid_rows):
    per_block = Q_TILE // GRID_W
    hit = [qr for qr in range(grid_rows)
           if any(_na_row_window(qr, grid_rows)[0] <= kr < _na_row_window(qr, grid_rows)[1]
                  for kr in key_rows)]
    return min(hit) // per_block * per_block, (max(hit) // per_block + 1) * per_block


def _na_bias_block(tiles, key_rows, q_rows, grid_rows):
    first_row = lax.broadcasted_iota(jnp.int32, (GRID_W, LANES), 1) < GRID_W

    def in_window(kr, qr):
        start, stop = _na_row_window(qr, grid_rows)
        return start <= kr < stop

    strips = []
    for kr in key_rows:
        parts = []
        for qr in range(q_rows.start, q_rows.stop, 2):
            ok0 = in_window(kr, qr)
            ok1 = in_window(kr, qr + 1)
            i = kr - qr + NA_WIN_H - 1
            if ok0 and ok1:
                parts.append(tiles[i])
            elif ok0:
                parts.append(jnp.where(first_row, tiles[i], NEG_INF))
            elif ok1:
                parts.append(jnp.where(first_row, NEG_INF, tiles[i]))
            else:
                parts.append(jnp.full((GRID_W, LANES), NEG_INF, F32))
        strips.append(jnp.concatenate(parts, axis=1))
    return jnp.concatenate(strips, axis=0)


def _lat_a_kernel(src_ref, q_ref, k_ref, vt_ref, ck_ref, cv_ref, o_ref,
                  qs_ref, kall_ref, vtall_ref, os_ref, s0_ref, s1_ref, m_ref):
    n = q_ref.shape[0]
    past = ck_ref.shape[4]
    grid_rows = n // GRID_W
    for h in range(A_HEADS):
        cols = slice(h * HEAD_DIM, (h + 1) * HEAD_DIM)
        qs_ref[h] = q_ref[:, cols]
        kall_ref[h, 0:n, :] = k_ref[:, cols]
        kall_ref[h, n:n + past, :] = _cache_keys(ck_ref, h)
        vtall_ref[h, 0:HEAD_DIM, 0:n] = vt_ref[0, h]
        vtall_ref[h, 0:HEAD_DIM, n:n + past] = cv_ref[0, 0, h].astype(BF16)
        vtall_ref[h, HEAD_DIM:, :] = jnp.ones((ONES_ROWS, n + past), BF16)

    slots = ((s0_ref, m_ref.at[0]), (s1_ref, m_ref.at[1]))

    def key_rows(c0):
        return range(c0 // GRID_W, (c0 + KEY_CHUNK) // GRID_W)

    def query_rows(c0):
        if c0 >= n:
            return range(0, grid_rows)
        return range(*_na_reachable_query_rows(key_rows(c0), grid_rows))

    def step(cur, nxt, h, h_next):
        tiles = _na_bias_tiles(src_ref, h_next)

        def bias(c0):
            if c0 >= n:
                return None
            return _na_bias_block(tiles, key_rows(c0), query_rows(c0), grid_rows)

        return _pipelined_item(
            cur, nxt, n + past, q_next=qs_ref[h_next],
            keys_next=lambda c0: kall_ref[h_next, c0:c0 + KEY_CHUNK, :],
            vt_cur=lambda c0: vtall_ref[h, :, c0:c0 + KEY_CHUNK], bias_next=bias,
            cols_of=lambda c0: (query_rows(c0).start * GRID_W, query_rows(c0).stop * GRID_W))

    step((None, None), slots[0], None, 0)

    def head_pair(p, carry):
        h0 = 2 * p
        os_ref[h0] = step(slots[0], slots[1], h0, h0 + 1).T.astype(BF16)
        os_ref[h0 + 1] = step(slots[1], slots[0], h0 + 1, lax.rem(h0 + 2, A_HEADS)).T.astype(BF16)
        return carry

    lax.fori_loop(0, A_HEADS // 2, head_pair, 0)
    for pair in range(A_HEADS // 2):
        o_ref[:, pair * LANES:(pair + 1) * LANES] = jnp.concatenate(
            [os_ref[2 * pair], os_ref[2 * pair + 1]], axis=1)


def _lat_a_attention(qk, vt, cache_k, cache_v, e, bias_src, seq):
    t = qk.shape[0]
    past = cache_k.shape[4]
    na = A_HEADS * HEAD_DIM
    return pl.pallas_call(
        _lat_a_kernel,
        grid=(t // seq,),
        in_specs=[
            _resident(bias_src.shape),
            pl.BlockSpec((seq, na), lambda b: (b, 0)),
            pl.BlockSpec((seq, na), lambda b: (b, 1)),
            _batch_spec(vt),
            _cache_spec(cache_k, e),
            _cache_spec(cache_v, e),
        ],
        out_specs=pl.BlockSpec((seq, na), lambda b: (b, 0)),
        out_shape=jax.ShapeDtypeStruct((t, na), BF16),
        scratch_shapes=[pltpu.VMEM((A_HEADS, seq, HEAD_DIM), BF16),
                        pltpu.VMEM((A_HEADS, seq + past, HEAD_DIM), BF16),
                        pltpu.VMEM((A_HEADS, HEAD_DIM + ONES_ROWS, seq + past), BF16),
                        pltpu.VMEM((A_HEADS, seq, HEAD_DIM), BF16)]
                       + _score_slots(seq + past, seq),
        compiler_params=_params(1),
        name="attn_even_lat_a",
    )(bias_src, qk, qk, vt, cache_k, cache_v)


def _band_window_start(j, n):
    return min(max(Q_TILE * j - B_WINDOW, 0), n - 2 * Q_TILE)


def _band_bias(n, group):
    j = np.arange(n // Q_TILE)[:, None, None]
    lo = np.clip(Q_TILE * j - B_WINDOW, 0, n - 2 * Q_TILE)
    kpos = lo + np.arange(2 * Q_TILE)[None, :, None]
    qpos = Q_TILE * j + (np.arange(group * Q_TILE) % Q_TILE)[None, None, :]
    return jnp.asarray(np.where(np.abs(qpos - kpos) <= B_WINDOW, 0.0, NEG_INF), F32)


def _lat_b_kernel(sink_ref, band_ref, q_ref, k_ref, vt_ref, ck_ref, cv_ref, o_ref,
                  ckeys_ref, cvt_ref, vtw_ref, s0_ref, s1_ref, m_ref):
    n = q_ref.shape[0]
    past = ck_ref.shape[4]
    group = B_HEADS // B_KV_HEADS
    win = 2 * Q_TILE
    n_blocks = n // Q_TILE
    for kv in range(B_KV_HEADS):
        ckeys_ref[kv] = _cache_keys(ck_ref, kv)
        cvt_ref[kv] = _with_ones_rows(cv_ref[0, 0, kv])
        for j in range(n_blocks):
            lo = _band_window_start(j, n)
            vtw_ref[j, kv] = _with_ones_rows(vt_ref[0, kv, :, lo:lo + win])
    sinks = [_sink_row(sink_ref, kv * group, group, Q_TILE) for kv in range(B_KV_HEADS)]

    def rows_of(j):
        if isinstance(j, int):
            return slice(j * Q_TILE, (j + 1) * Q_TILE)
        return pl.ds(pl.multiple_of(j * Q_TILE, Q_TILE), Q_TILE)

    def window_rows(j, c0):
        if isinstance(j, int):
            lo = _band_window_start(j, n) + c0
            return slice(lo, lo + KEY_CHUNK)
        lo = jnp.clip(Q_TILE * j - B_WINDOW, 0, n - win) + c0
        return pl.ds(pl.multiple_of(lo, B_WINDOW), KEY_CHUNK)

    slots = ((s0_ref, m_ref.at[0]), (s1_ref, m_ref.at[1]))

    def item(cur, nxt, j, kv, j_next, kv_next):
        kcols = slice(kv_next * HEAD_DIM, (kv_next + 1) * HEAD_DIM)

        def keys_next(c0):
            if c0 < win:
                return k_ref[window_rows(j_next, c0), kcols]
            return ckeys_ref[kv_next, c0 - win:c0 - win + KEY_CHUNK, :]

        def vt_cur(c0):
            if c0 < win:
                return vtw_ref[j, kv, :, c0:c0 + KEY_CHUNK]
            return cvt_ref[kv, :, c0 - win:c0 - win + KEY_CHUNK]

        ot = _pipelined_item(
            cur, nxt, win + past,
            q_next=_stack_group(q_ref, rows_of(j_next), kv_next * group * HEAD_DIM, group),
            keys_next=keys_next, vt_cur=vt_cur,
            bias_next=lambda c0: band_ref[j_next, c0:c0 + KEY_CHUNK, :] if c0 < win else None,
            sink_cur=None if kv is None else sinks[kv], sink_next=sinks[kv_next])
        if ot is None:
            return None
        return jnp.concatenate(
            [ot[:, g * Q_TILE:(g + 1) * Q_TILE].T for g in range(group)], axis=1)

    item((None, None), slots[0], None, None, 0, 0)

    def q_block(j, carry):
        for kv in range(B_KV_HEADS):
            last = kv + 1 == B_KV_HEADS
            o = item(slots[kv % 2], slots[(kv + 1) % 2], j, kv,
                     lax.rem(j + 1, n_blocks) if last else j, 0 if last else kv + 1)
            c0 = kv * group * HEAD_DIM
            o_ref[rows_of(j), c0:c0 + group * HEAD_DIM] = o.astype(BF16)
        return carry

    lax.fori_loop(0, n_blocks, q_block, 0)


def _lat_b_attention(qk, vt, cache_k, cache_v, e, sink, seq):
    t = qk.shape[0]
    past = cache_k.shape[4]
    na = A_HEADS * HEAD_DIM
    nb = B_HEADS * HEAD_DIM
    nkv = B_KV_HEADS * HEAD_DIM
    group = B_HEADS // B_KV_HEADS
    band = _band_bias(seq, group)
    return pl.pallas_call(
        _lat_b_kernel,
        grid=(t // seq,),
        in_specs=[
            pl.BlockSpec(memory_space=pltpu.SMEM),
            _resident(band.shape),
            pl.BlockSpec((seq, nb), lambda b: (b, 2 * na // nb)),
            pl.BlockSpec((seq, nkv), lambda b: (b, (2 * na + nb) // nkv)),
            _batch_spec(vt),
            _cache_spec(cache_k, e),
            _cache_spec(cache_v, e),
        ],
        out_specs=pl.BlockSpec((seq, nb), lambda b: (b, 0)),
        out_shape=jax.ShapeDtypeStruct((t, nb), BF16),
        scratch_shapes=[pltpu.VMEM((B_KV_HEADS, past, HEAD_DIM), BF16),
                        pltpu.VMEM((B_KV_HEADS, HEAD_DIM + ONES_ROWS, past), BF16),
                        pltpu.VMEM((seq // Q_TILE, B_KV_HEADS, HEAD_DIM + ONES_ROWS, 2 * Q_TILE),
                                   BF16)]
                       + _score_slots(2 * Q_TILE + past, group * Q_TILE),
        compiler_params=_params(1),
        name="attn_even_lat_b",
    )(sink, band, qk, qk, vt, cache_k, cache_v)


def _ctx_odd_attn_kernel(qk_ref, vt_ref, o_ref, s0_ref, s1_ref, m_ref):
    nb = vt_ref.shape[0]
    n = qk_ref.shape[0] // nb
    group = C_HEADS // C_KV_HEADS
    nq = C_HEADS * HEAD_DIM

    def rows_of(b, c0=0, size=n):
        if isinstance(b, int):
            return slice(b * n + c0, b * n + c0 + size)
        return pl.ds(pl.multiple_of(b * n + c0, size), size)

    slots = ((s0_ref, m_ref.at[0]), (s1_ref, m_ref.at[1]))

    def item(cur, nxt, b, kv, b_next, kv_next):
        kcols = slice(nq + kv_next * HEAD_DIM, nq + (kv_next + 1) * HEAD_DIM)
        ot = _pipelined_item(
            cur, nxt, n,
            q_next=_stack_group(qk_ref, rows_of(b_next), kv_next * group * HEAD_DIM, group),
            keys_next=lambda c0: qk_ref[rows_of(b_next, c0, KEY_CHUNK), kcols],
            vt_cur=lambda c0: _with_ones_rows(vt_ref[b, kv, :, c0:c0 + KEY_CHUNK]))
        if ot is None:
            return None
        return jnp.concatenate([ot[:, g * n:(g + 1) * n].T for g in range(group)], axis=1)

    item((None, None), slots[0], None, None, 0, 0)

    def sequence(b, carry):
        for kv in range(C_KV_HEADS):
            last = kv + 1 == C_KV_HEADS
            o = item(slots[kv % 2], slots[(kv + 1) % 2], b, kv,
                     lax.rem(b + 1, nb) if last else b, 0 if last else kv + 1)
            c0 = kv * group * HEAD_DIM
            o_ref[rows_of(b), c0:c0 + group * HEAD_DIM] = o.astype(BF16)
        return carry

    lax.fori_loop(0, nb, sequence, 0)


def _ctx_odd_attention(qk, vt, seq):
    t, n = qk.shape
    width = C_HEADS * HEAD_DIM
    nb = CTX_BATCHES
    return pl.pallas_call(
        _ctx_odd_attn_kernel,
        grid=(t // (nb * seq),),
        in_specs=[pl.BlockSpec((nb * seq, n), lambda b: (b, 0)), _batch_spec(vt, nb)],
        out_specs=pl.BlockSpec((nb * seq, width), lambda b: (b, 0)),
        out_shape=jax.ShapeDtypeStruct((t, width), BF16),
        scratch_shapes=_score_slots(seq, C_HEADS // C_KV_HEADS * seq),
        compiler_params=_params(1),
        name="attn_odd_ctx",
    )(qk, vt)


def _lat_c_kernel(qk_ref, vt_ref, ck_ref, cv_ref, o_ref, kall_ref, vtall_ref, s0_ref, s1_ref,
                  m_ref):
    n = qk_ref.shape[0]
    past = ck_ref.shape[4]
    group = C_HEADS // C_KV_HEADS
    nq = C_HEADS * HEAD_DIM
    n_blocks = n // Q_TILE
    for kv in range(C_KV_HEADS):
        kall_ref[kv, 0:past, :] = _cache_keys(ck_ref, kv)
        kall_ref[kv, past:past + n, :] = qk_ref[:, nq + kv * HEAD_DIM:nq + (kv + 1) * HEAD_DIM]
        vtall_ref[kv, 0:HEAD_DIM, 0:past] = cv_ref[0, 0, kv].astype(BF16)
        vtall_ref[kv, 0:HEAD_DIM, past:past + n] = vt_ref[0, kv]
        vtall_ref[kv, HEAD_DIM:, :] = jnp.ones((ONES_ROWS, past + n), BF16)

    def rows_of(j):
        if isinstance(j, int):
            return slice(j * Q_TILE, (j + 1) * Q_TILE)
        return pl.ds(pl.multiple_of(j * Q_TILE, Q_TILE), Q_TILE)

    slots = ((s0_ref, m_ref.at[0]), (s1_ref, m_ref.at[1]))

    def item(cur, nxt, kv, j_next, kv_next):
        ot = _pipelined_item(
            cur, nxt, past + n,
            q_next=_stack_group(qk_ref, rows_of(j_next), kv_next * group * HEAD_DIM, group),
            keys_next=lambda c0: kall_ref[kv_next, c0:c0 + KEY_CHUNK, :],
            vt_cur=lambda c0: vtall_ref[kv, :, c0:c0 + KEY_CHUNK])
        if ot is None:
            return None
        return jnp.concatenate(
            [ot[:, g * Q_TILE:(g + 1) * Q_TILE].T for g in range(group)], axis=1)

    item((None, None), slots[0], None, 0, 0)

    def q_block(j, carry):
        for kv in range(C_KV_HEADS):
            last = kv + 1 == C_KV_HEADS
            o = item(slots[kv % 2], slots[(kv + 1) % 2], kv,
                     lax.rem(j + 1, n_blocks) if last else j, 0 if last else kv + 1)
            c0 = kv * group * HEAD_DIM
            o_ref[rows_of(j), c0:c0 + group * HEAD_DIM] = o.astype(BF16)
        return carry

    lax.fori_loop(0, n_blocks, q_block, 0)


def _lat_c_attention(qk, vt, cache_k, cache_v, o, seq):
    t, n = qk.shape
    past = cache_k.shape[4]
    width = C_HEADS * HEAD_DIM
    return pl.pallas_call(
        _lat_c_kernel,
        grid=(t // seq,),
        in_specs=[
            pl.BlockSpec((seq, n), lambda b: (b, 0)),
            _batch_spec(vt),
            _cache_spec(cache_k, o),
            _cache_spec(cache_v, o),
        ],
        out_specs=pl.BlockSpec((seq, width), lambda b: (b, 0)),
        out_shape=jax.ShapeDtypeStruct((t, width), BF16),
        scratch_shapes=[pltpu.VMEM((C_KV_HEADS, past + seq, HEAD_DIM), BF16),
                        pltpu.VMEM((C_KV_HEADS, HEAD_DIM + ONES_ROWS, past + seq), BF16)]
                       + _score_slots(past + seq, C_HEADS // C_KV_HEADS * Q_TILE),
        compiler_params=_params(1),
        name="attn_odd_lat",
    )(qk, vt, cache_k, cache_v)


def _post_kernel(*refs, n_parts, final):
    o_refs = refs[:n_parts]
    (x_ref, wo_ref, g1_ref, sh_ref, sc_ref, g2_ref, gain_ref, wgu_ref, wd_ref) = refs[n_parts:n_parts + 9]
    rest = refs[n_parts + 9:]
    if final:
        fg_ref, out_ref, act_ref = rest
    else:
        out_ref, act_ref = rest
    mix = None
    r0 = 0
    for o_ref in o_refs:
        kk = o_ref.shape[1]
        part = jnp.dot(o_ref[...], wo_ref[r0:r0 + kk, :], preferred_element_type=F32)
        mix = part if mix is None else mix + part
        r0 += kk
    x1 = x_ref[...] + g1_ref[0, 0] * mix
    h = _adaln(x1, gain_ref[...], sh_ref[0, 0], sc_ref[0, 0]).astype(BF16)
    d_ff = wd_ref.shape[0]
    for j in range(d_ff // FF_CHUNK):
        c0 = j * FF_CHUNK
        gate = jnp.dot(h, wgu_ref[:, c0:c0 + FF_CHUNK], preferred_element_type=F32)
        up = jnp.dot(h, wgu_ref[:, d_ff + c0:d_ff + c0 + FF_CHUNK], preferred_element_type=F32)
        act_ref[:, c0:c0 + FF_CHUNK] = (gate * jax.nn.sigmoid(gate) * up).astype(BF16)
    ffn = jnp.dot(act_ref[...], wd_ref[...], preferred_element_type=F32)
    x2 = x1 + g2_ref[0, 0] * ffn
    if final:
        ms = jnp.mean(x2 * x2, axis=-1, keepdims=True)
        x2 = (x2 * lax.rsqrt(ms + RMS_EPS)) * fg_ref[...]
    out_ref[...] = x2


def _post_attention(o_parts, x, mods, layer, gain, w_out, sub, w_gu, w_down, *, is_lat, seq,
                    final_gain=None):
    t, d = x.shape
    tm = TOKEN_TILE
    per_seq = max(seq // tm, 1)
    group = (lambda i: 1 + i // per_seq) if is_lat else (lambda i: 0)
    row = lambda i: (i, 0)
    d_ff = w_down.shape[1]
    final = final_gain is not None
    in_specs = [pl.BlockSpec((tm, o.shape[1]), row) for o in o_parts]
    in_specs += [
        pl.BlockSpec((tm, d), row),
        _resident_layer(w_out, sub),
        _mod_spec(layer, 2, group),
        _mod_spec(layer, 3, group),
        _mod_spec(layer, 4, group),
        _mod_spec(layer, 5, group),
        _resident((1, d)),
        _resident_layer(w_gu, layer),
        _resident_layer(w_down, layer),
    ]
    args = list(o_parts) + [x, w_out, mods, mods, mods, mods, gain.reshape(1, d), w_gu, w_down]
    if final:
        in_specs.append(_resident((1, d)))
        args.append(final_gain.reshape(1, d))
    return pl.pallas_call(
        functools.partial(_post_kernel, n_parts=len(o_parts), final=final),
        grid=(t // tm,),
        in_specs=in_specs,
        out_specs=pl.BlockSpec((tm, d), row),
        out_shape=jax.ShapeDtypeStruct((t, d), F32),
        scratch_shapes=[pltpu.VMEM((tm, d_ff), BF16)],
        compiler_params=_params(1),
        name=f"post_{'lat' if is_lat else 'ctx'}{'_final' if final else ''}",
    )(*args)


def _rope_tables(n):
    t = np.arange(n)
    row = (t // GRID_W).astype(np.float32)
    col = (t % GRID_W).astype(np.float32)
    half = HEAD_DIM // 2
    inv_freq = np.float32(ROPE_THETA) ** (-np.arange(0, half, 2, dtype=np.float32) / np.float32(half))
    lane = np.arange(LANES)
    in_head = lane % HEAD_DIM
    pos = np.where((in_head < half)[None, :], row[:, None], col[:, None])
    ang = (pos * inv_freq[in_head % (half // 2)][None, :]).astype(np.float32)
    first = ((in_head % half) < half // 2)[None, :]
    cos = np.cos(ang)
    sin = np.sin(ang)
    zero = np.float32(0.0)
    return tuple(jnp.asarray(a, F32) for a in
                 (cos, np.where(first, -sin, zero), np.where(first, zero, sin)))


def _na_bias_sources(rpb):
    h, _, nb = rpb.shape
    w = NA_WIN_W - 1
    rp = jnp.pad(rpb[:, :, ::-1] * LOG2E, ((0, 0), (1, 1), (0, 0)))
    this, prev = rp[:, 1:], rp[:, :-1]
    z = jnp.zeros((h, 2 * NA_WIN_H, LANES // 2 - nb), F32)
    return jnp.concatenate([this[:, :, w:], z, prev, z, this[:, :, :w]], axis=-1)


def _state(y):
    return y.transpose(0, 3, 1, 2)


def kernel(x_prompt, x_sample, cache_a_k, cache_a_v, cache_b_k, cache_b_v, cache_c_k, cache_c_v,
           c, c_ctx, norm_gain, w_mod, b_mod, w_in_even, w_out_even, rpb_a, sink_b,
           w_in_odd, w_out_odd, q_norm_c, k_norm_c, w_gate_up, w_down, final_gain):
    batch, seq, d = x_prompt.shape
    dec_batch, dec_seq, _ = x_sample.shape
    depth = w_mod.shape[0]

    cvec = jnp.concatenate(
        [c_ctx[None, :], c, jnp.zeros((MOD_GROUPS - 1 - dec_batch, d), F32)], axis=0)
    mods = _modulation(cvec, w_mod, b_mod).reshape(depth, MOD_GROUPS, 1, 6 * d)
    rope = _rope_tables(dec_seq)
    w_in = {False: w_in_even.astype(BF16), True: w_in_odd.astype(BF16)}
    w_out = {False: w_out_even.astype(BF16), True: w_out_odd.astype(BF16)}
    w_gu = w_gate_up.astype(BF16)
    w_dn = w_down.astype(BF16)

    ctx = x_prompt.reshape(batch * seq, d)
    lat = x_sample.reshape(dec_batch * dec_seq, d)
    states = {name: [] for name in ("a_k", "a_v", "b_k", "b_v", "c_k", "c_v")}

    for layer in range(depth):
        odd = layer % 2 == 1
        sub = layer // 2
        gain1, gain2 = norm_gain[layer, 0], norm_gain[layer, 1]
        pre = functools.partial(_pre_attention, mods=mods, layer=layer, gain=gain1, w=w_in[odd],
                                sub=sub, odd=odd)
        if not odd:
            qk_c, ka, va, kb, vb = pre(ctx, is_lat=False, seq=seq)
            qk_l, vat_l, vbt_l = pre(lat, is_lat=True, seq=dec_seq, rope=rope)
            for name, y in (("a_k", ka), ("a_v", va), ("b_k", kb), ("b_v", vb)):
                states[name].append(_state(y))
            o_ctx = [_ctx_even_attention(qk_c, va, vb, sink_b[sub], seq)]
            o_lat = [
                _lat_a_attention(qk_l, vat_l, _feature_major(cache_a_k), _feature_major(cache_a_v),
                                 sub, _na_bias_sources(rpb_a[sub]), dec_seq),
                _lat_b_attention(qk_l, vbt_l, _feature_major(cache_b_k), _feature_major(cache_b_v),
                                 sub, sink_b[sub], dec_seq),
            ]
        else:
            per = LANES // HEAD_DIM
            head_gains = (jnp.tile(q_norm_c[sub], per).reshape(1, LANES),
                          jnp.tile(k_norm_c[sub], per).reshape(1, LANES))
            qk_c, kc, vc = pre(ctx, is_lat=False, seq=seq, head_gains=head_gains)
            qk_l, vct_l = pre(lat, is_lat=True, seq=dec_seq, rope=rope, head_gains=head_gains)
            states["c_k"].append(_state(kc))
            states["c_v"].append(_state(vc))
            o_ctx = [_ctx_odd_attention(qk_c, vc, seq)]
            o_lat = [_lat_c_attention(qk_l, vct_l, _feature_major(cache_c_k),
                                      _feature_major(cache_c_v), sub, dec_seq)]
        fg = final_gain if layer == depth - 1 else None
        post = functools.partial(_post_attention, mods=mods, layer=layer, gain=gain2,
                                 w_out=w_out[odd], sub=sub, w_gu=w_gu, w_down=w_dn, final_gain=fg)
        ctx = post(o_ctx, ctx, is_lat=False, seq=seq)
        lat = post(o_lat, lat, is_lat=True, seq=dec_seq)

    return (ctx.reshape(batch, seq, d), lat.reshape(dec_batch, dec_seq, d),
            jnp.stack(states["a_k"], axis=1), jnp.stack(states["a_v"], axis=1),
            jnp.stack(states["b_k"], axis=1), jnp.stack(states["b_v"], axis=1),
            jnp.stack(states["c_k"], axis=1), jnp.stack(states["c_v"], axis=1))
```

```python
import functools
import math

import jax
import jax.numpy as jnp
import numpy as np
from jax import lax
from jax.experimental import pallas as pl
from jax.experimental.pallas import tpu as pltpu

F32 = jnp.float32
BF16 = jnp.bfloat16

D_MODEL = 1024
GRID_W = 64
HEAD_DIM = 64
A_HEADS = 8
B_HEADS = 8
B_KV_HEADS = 2
C_HEADS = 16
C_KV_HEADS = 4
NA_WIN_H = 8
NA_WIN_W = 16
B_WINDOW = 128
ROPE_THETA = 10000.0
RMS_EPS = 1e-6
NEG_INF = -1e30
LOG2E = math.log2(math.e)
QK_SCALE = LOG2E / math.sqrt(HEAD_DIM)

LANES = 128
TOKEN_TILE = 512
Q_TILE = 256
CTX_BATCHES = 8
ONES_ROWS = 16
KEY_CHUNK = 256
FF_CHUNK = 256
MOD_GROUPS = 16
VMEM_LIMIT = 56 * 1024 * 1024


def _params(n_axes, vmem=VMEM_LIMIT):
    return pltpu.CompilerParams(
        dimension_semantics=("arbitrary",) * n_axes, vmem_limit_bytes=vmem)


def _resident(shape):
    nd = len(shape)
    return pl.BlockSpec(shape, lambda *_: (0,) * nd, pipeline_mode=pl.Buffered(1))


def _resident_layer(stacked, layer):
    return pl.BlockSpec((None,) + stacked.shape[1:], lambda *_: (layer, 0, 0),
                        pipeline_mode=pl.Buffered(1))


def _mod_kernel(c_ref, w_ref, b_ref, o_ref):
    c = c_ref[...]
    s = (c * jax.nn.sigmoid(c)).astype(BF16)
    o_ref[0] = jnp.dot(s, w_ref[0].astype(BF16), preferred_element_type=F32) + b_ref[0]


def _modulation(cvec, w_mod, b_mod):
    depth, d, n = w_mod.shape
    tn = 1536
    return pl.pallas_call(
        _mod_kernel,
        grid=(depth, n // tn),
        in_specs=[
            pl.BlockSpec((MOD_GROUPS, d), lambda l, j: (0, 0)),
            pl.BlockSpec((1, d, tn), lambda l, j: (l, 0, j)),
            pl.BlockSpec((1, 1, tn), lambda l, j: (l, 0, j)),
        ],
        out_specs=pl.BlockSpec((1, MOD_GROUPS, tn), lambda l, j: (l, 0, j)),
        out_shape=jax.ShapeDtypeStruct((depth, MOD_GROUPS, n), F32),
        compiler_params=_params(2),
        name="modulation",
    )(cvec, w_mod, b_mod.reshape(depth, 1, n))


def _mod_spec(layer, which, group_of_step):
    return pl.BlockSpec((1, 1, 1, D_MODEL), lambda i: (layer, group_of_step(i), 0, which))


def _adaln(x, gain, shift, scale):
    ms = jnp.mean(x * x, axis=-1, keepdims=True)
    return (x * lax.rsqrt(ms + RMS_EPS)) * gain * (1.0 + scale) + shift


def _rope(y, cos, sin_lo, sin_hi):
    outs = []
    for c in range(y.shape[1] // LANES):
        yc = y[:, c * LANES:(c + 1) * LANES]
        outs.append(yc * cos
                    + pltpu.roll(yc, LANES - 16, 1) * sin_lo
                    + pltpu.roll(yc, 16, 1) * sin_hi)
    return outs[0] if len(outs) == 1 else jnp.concatenate(outs, axis=1)


def _head_rms_norm(y, gain):
    first = lax.broadcasted_iota(jnp.int32, (1, LANES), 1) < HEAD_DIM
    outs = []
    for c in range(y.shape[1] // LANES):
        yc = y[:, c * LANES:(c + 1) * LANES]
        sq = yc * yc
        s0 = jnp.sum(jnp.where(first, sq, 0.0), axis=-1, keepdims=True)
        s1 = jnp.sum(jnp.where(first, 0.0, sq), axis=-1, keepdims=True)
        ms = jnp.where(first, s0, s1) * (1.0 / HEAD_DIM)
        outs.append(yc * lax.rsqrt(ms + RMS_EPS) * gain)
    return outs[0] if len(outs) == 1 else jnp.concatenate(outs, axis=1)


_NT = (((1,), (1,)), ((), ()))


def _with_ones_rows(vt):
    return jnp.concatenate([vt.astype(BF16), jnp.ones((ONES_ROWS, vt.shape[1]), BF16)], axis=0)


def _attend(segments, q, rows, group, sink=None):
    scores = []
    for k, _, bias, valid in segments:
        s = _scores(k, q)
        if bias is not None:
            s = s + bias
        if valid is not None:
            s = jnp.where(valid, s, NEG_INF)
        scores.append(s)
    return _softmax_values(scores, [seg[1] for seg in segments], rows, group, sink)


def _scores(k, q):
    return lax.dot_general(k, q, _NT, preferred_element_type=F32)


def _softmax_values(scores, vts, rows, group, sink=None):
    m = functools.reduce(jnp.maximum, [jnp.max(s, axis=0, keepdims=True) for s in scores])
    if sink is not None:
        m = jnp.maximum(m, sink)
    ot = None
    for s, vt_ones in zip(scores, vts):
        part = jnp.dot(vt_ones, jnp.exp2(s - m).astype(BF16), preferred_element_type=F32)
        ot = part if ot is None else ot + part
    denom = ot[HEAD_DIM:HEAD_DIM + 1, :]
    if sink is not None:
        denom = denom + jnp.exp2(sink - m)
    ot = ot[:HEAD_DIM, :] / denom
    return jnp.concatenate([ot[:, g * rows:(g + 1) * rows].T for g in range(group)], axis=1)


def _pipelined_item(cur, nxt, n_keys, *, q_next, keys_next, vt_cur, bias_next=None,
                    cols_of=None, sink_cur=None, sink_next=None):
    s_cur, m_cur = cur
    s_nxt, m_nxt = nxt
    width = s_nxt.shape[1]
    if cols_of is None:
        cols_of = lambda c0: (0, width)
    m = None if s_cur is None else m_cur[...]
    m_next = [None] * (width // Q_TILE)
    ot = [None] * (width // Q_TILE)
    for c0 in range(0, n_keys, KEY_CHUNK):
        keys = slice(c0, c0 + KEY_CHUNK)
        lo, hi = cols_of(c0)
        blocks = [(g, slice(g * Q_TILE - lo, (g + 1) * Q_TILE - lo))
                  for g in range(lo // Q_TILE, hi // Q_TILE)]
        if q_next is not None:
            s = _scores(keys_next(c0), q_next[lo:hi])
            bias = None if bias_next is None else bias_next(c0)
            if bias is not None:
                s = s + bias
            s_nxt[keys, lo:hi] = s
            m_c = jnp.max(s, axis=0, keepdims=True)
            for g, cols in blocks:
                m_next[g] = (m_c[:, cols] if m_next[g] is None
                             else jnp.maximum(m_next[g], m_c[:, cols]))
        if s_cur is not None:
            p = jnp.exp2(s_cur[keys, lo:hi] - m[:, lo:hi]).astype(BF16)
            part = jnp.dot(vt_cur(c0), p, preferred_element_type=F32)
            for g, cols in blocks:
                ot[g] = part[:, cols] if ot[g] is None else ot[g] + part[:, cols]
    if q_next is not None:
        m_next = jnp.concatenate(m_next, axis=1)
        if sink_next is not None:
            m_next = jnp.maximum(m_next, sink_next)
        m_nxt[...] = m_next
    if s_cur is None:
        return None
    ot = jnp.concatenate(ot, axis=1)
    denom = ot[HEAD_DIM:HEAD_DIM + 1, :]
    if sink_cur is not None:
        denom = denom + jnp.exp2(sink_cur - m)
    return ot[:HEAD_DIM, :] / denom


def _score_slots(n_keys, m):
    return [pltpu.VMEM((n_keys, m), F32), pltpu.VMEM((n_keys, m), F32),
            pltpu.VMEM((2, 1, m), F32)]


def _stack_group(ref, rows, col0, group):
    return jnp.concatenate(
        [ref[rows, col0 + g * HEAD_DIM: col0 + (g + 1) * HEAD_DIM] for g in range(group)], axis=0)


def _sink_row(sink_ref, h0, group, rows):
    return jnp.concatenate(
        [jnp.full((1, rows), sink_ref[h0 + g] * LOG2E, F32) for g in range(group)], axis=1)


def _store_feature_major(ref, y):
    nb, heads, _, seq = ref.shape
    yt = y.T
    for b in range(nb):
        for h in range(heads):
            ref[b, h] = yt[h * HEAD_DIM:(h + 1) * HEAD_DIM, b * seq:(b + 1) * seq].astype(ref.dtype)


def _pre_even_kernel(x_ref, g_ref, sh_ref, sc_ref, w_ref, *rest, is_lat):
    if is_lat:
        cos_ref, slo_ref, shi_ref, qk_ref, va_ref, vb_ref = rest
    else:
        qk_ref, ka_ref, va_ref, kb_ref, vb_ref = rest
    h = _adaln(x_ref[...], g_ref[...], sh_ref[0, 0], sc_ref[0, 0]).astype(BF16)

    def proj(c0, c1):
        return jnp.dot(h, w_ref[:, c0:c1], preferred_element_type=F32)

    na = A_HEADS * HEAD_DIM
    nb = B_HEADS * HEAD_DIM
    nkv = B_KV_HEADS * HEAD_DIM
    qk_ref[:, 0:na] = (proj(0, na) * QK_SCALE).astype(BF16)
    ka = proj(na, 2 * na)
    qk_ref[:, na:2 * na] = ka.astype(BF16)
    va = proj(2 * na, 3 * na)
    qb = proj(3 * na, 3 * na + nb)
    if is_lat:
        qb = _rope(qb, cos_ref[...], slo_ref[...], shi_ref[...])
    qk_ref[:, 2 * na:2 * na + nb] = (qb * QK_SCALE).astype(BF16)
    kvb = proj(3 * na + nb, 3 * na + nb + 2 * nkv)
    kb, vb = kvb[:, :nkv], kvb[:, nkv:]
    if is_lat:
        kb_out = _rope(kb, cos_ref[...], slo_ref[...], shi_ref[...])
    else:
        kb_out = kb
    qk_ref[:, 2 * na + nb:2 * na + nb + nkv] = kb_out.astype(BF16)
    _store_feature_major(va_ref, va)
    _store_feature_major(vb_ref, vb)
    if not is_lat:
        _store_feature_major(ka_ref, ka)
        _store_feature_major(kb_ref, kb)


def _pre_odd_kernel(x_ref, g_ref, sh_ref, sc_ref, w_ref, qn_ref, kn_ref, *rest, is_lat):
    if is_lat:
        cos_ref, slo_ref, shi_ref, qk_ref, vc_ref = rest
    else:
        qk_ref, kc_ref, vc_ref = rest
    h = _adaln(x_ref[...], g_ref[...], sh_ref[0, 0], sc_ref[0, 0]).astype(BF16)

    def proj(c0, c1):
        return jnp.dot(h, w_ref[:, c0:c1], preferred_element_type=F32)

    nq = C_HEADS * HEAD_DIM
    nkv = C_KV_HEADS * HEAD_DIM
    q = _head_rms_norm(proj(0, nq), qn_ref[...])
    k = _head_rms_norm(proj(nq, nq + nkv), kn_ref[...])
    v = proj(nq + nkv, nq + 2 * nkv)
    if is_lat:
        q = _rope(q, cos_ref[...], slo_ref[...], shi_ref[...])
        k_out = _rope(k, cos_ref[...], slo_ref[...], shi_ref[...])
    else:
        k_out = k
        _store_feature_major(kc_ref, k)
    _store_feature_major(vc_ref, v)
    qk_ref[:, 0:nq] = (q * QK_SCALE).astype(BF16)
    qk_ref[:, nq:nq + nkv] = k_out.astype(BF16)


def _pre_attention(x, mods, layer, gain, w, sub, *, odd, is_lat, seq, rope=None,
                   head_gains=None):
    t, d = x.shape
    tm = TOKEN_TILE
    per_seq = max(seq // tm, 1)
    per_tile = max(tm // seq, 1)
    group = (lambda i: 1 + i // per_seq) if is_lat else (lambda i: 0)
    row = lambda i: (i, 0)
    in_specs = [
        pl.BlockSpec((tm, d), row),
        _resident((1, d)),
        _mod_spec(layer, 0, group),
        _mod_spec(layer, 1, group),
        _resident_layer(w, sub),
    ]
    args = [x, gain.reshape(1, d), mods, mods, w]
    if odd:
        in_specs += [_resident(a.shape) for a in head_gains]
        args += list(head_gains)
    if is_lat:
        in_specs += [pl.BlockSpec((tm, LANES), lambda i: (i % per_seq, 0))] * 3
        args += list(rope)
    if odd:
        n_qk = (C_HEADS + C_KV_HEADS) * HEAD_DIM
        lat_heads, ctx_heads = [C_KV_HEADS], [C_KV_HEADS, C_KV_HEADS]
    else:
        n_qk = (2 * A_HEADS + B_HEADS + B_KV_HEADS) * HEAD_DIM
        lat_heads, ctx_heads = [A_HEADS, B_KV_HEADS], [A_HEADS, A_HEADS, B_KV_HEADS, B_KV_HEADS]
    out_specs = [pl.BlockSpec((tm, n_qk), row)]
    out_shape = [jax.ShapeDtypeStruct((t, n_qk), BF16)]
    for heads in (lat_heads if is_lat else ctx_heads):
        blk = (per_tile, heads, HEAD_DIM, min(tm, seq))
        out_specs.append(pl.BlockSpec(blk, lambda i: (i // per_seq, 0, 0, i % per_seq)))
        out_shape.append(jax.ShapeDtypeStruct((t // seq, heads, HEAD_DIM, seq),
                                              BF16 if is_lat else F32))
    body = _pre_odd_kernel if odd else _pre_even_kernel
    return pl.pallas_call(
        functools.partial(body, is_lat=is_lat),
        grid=(t // tm,),
        in_specs=in_specs,
        out_specs=out_specs,
        out_shape=out_shape,
        compiler_params=_params(1),
        name=f"pre_{'odd' if odd else 'even'}_{'lat' if is_lat else 'ctx'}",
    )(*args)


def _batch_spec(arr, nb=1):
    nd = arr.ndim
    return pl.BlockSpec((nb,) + arr.shape[1:], lambda b: (b,) + (0,) * (nd - 1))


def _cache_spec(cache, layer):
    blk = (1, 1) + cache.shape[2:]
    return pl.BlockSpec(blk, lambda b: (b, layer, 0, 0, 0))


def _feature_major(cache):
    return cache.transpose(0, 1, 3, 4, 2)


def _cache_keys(ref, h):
    return ref[0, 0, h].T.astype(BF16)


def _ctx_even_attn_kernel(sink_ref, qk_ref, vat_ref, vbt_ref, o_ref,
                          sa0_ref, sa1_ref, ma_ref, sb0_ref, sb1_ref, mb_ref):
    nb = vat_ref.shape[0]
    n = qk_ref.shape[0] // nb
    na = A_HEADS * HEAD_DIM
    group = B_HEADS // B_KV_HEADS
    qb0 = 2 * na
    kb0 = qb0 + B_HEADS * HEAD_DIM

    def rows_of(b, c0=0, size=n):
        if isinstance(b, int):
            return slice(b * n + c0, b * n + c0 + size)
        return pl.ds(pl.multiple_of(b * n + c0, size), size)

    def following(b, i, count):
        return (lax.rem(b + 1, nb), 0) if i + 1 == count else (b, i + 1)

    slots_a = ((sa0_ref, ma_ref.at[0]), (sa1_ref, ma_ref.at[1]))

    def item_a(cur, nxt, b, h, b_next, h_next):
        c = h_next * HEAD_DIM
        return _pipelined_item(
            cur, nxt, n, q_next=qk_ref[rows_of(b_next), c:c + HEAD_DIM],
            keys_next=lambda c0: qk_ref[rows_of(b_next, c0, KEY_CHUNK), na + c:na + c + HEAD_DIM],
            vt_cur=lambda c0: _with_ones_rows(vat_ref[b, h, :, c0:c0 + KEY_CHUNK]))

    item_a((None, None), slots_a[0], None, None, 0, 0)

    def sequence_a(b, carry):
        for pair in range(A_HEADS // 2):
            outs = []
            for h in (2 * pair, 2 * pair + 1):
                ot = item_a(slots_a[h % 2], slots_a[(h + 1) % 2], b, h, *following(b, h, A_HEADS))
                outs.append(ot.T)
            o_ref[rows_of(b), pair * LANES:(pair + 1) * LANES] = (
                jnp.concatenate(outs, axis=1).astype(BF16))
        return carry

    lax.fori_loop(0, nb, sequence_a, 0)

    slots_b = ((sb0_ref, mb_ref.at[0]), (sb1_ref, mb_ref.at[1]))
    sinks = [_sink_row(sink_ref, kv * group, group, n) for kv in range(B_KV_HEADS)]

    def item_b(cur, nxt, b, kv, b_next, kv_next):
        kcols = slice(kb0 + kv_next * HEAD_DIM, kb0 + (kv_next + 1) * HEAD_DIM)
        ot = _pipelined_item(
            cur, nxt, n,
            q_next=_stack_group(qk_ref, rows_of(b_next), qb0 + kv_next * group * HEAD_DIM, group),
            keys_next=lambda c0: qk_ref[rows_of(b_next, c0, KEY_CHUNK), kcols],
            vt_cur=lambda c0: _with_ones_rows(vbt_ref[b, kv, :, c0:c0 + KEY_CHUNK]),
            sink_cur=None if kv is None else sinks[kv], sink_next=sinks[kv_next])
        if ot is None:
            return None
        return jnp.concatenate([ot[:, g * n:(g + 1) * n].T for g in range(group)], axis=1)

    item_b((None, None), slots_b[0], None, None, 0, 0)

    def sequence_b(b, carry):
        for kv in range(B_KV_HEADS):
            o = item_b(slots_b[kv % 2], slots_b[(kv + 1) % 2], b, kv,
                       *following(b, kv, B_KV_HEADS))
            c0 = na + kv * group * HEAD_DIM
            o_ref[rows_of(b), c0:c0 + group * HEAD_DIM] = o.astype(BF16)
        return carry

    lax.fori_loop(0, nb, sequence_b, 0)


def _ctx_even_attention(qk, vat, vbt, sink, seq):
    t, n = qk.shape
    width = (A_HEADS + B_HEADS) * HEAD_DIM
    nb = CTX_BATCHES
    return pl.pallas_call(
        _ctx_even_attn_kernel,
        grid=(t // (nb * seq),),
        in_specs=[pl.BlockSpec(memory_space=pltpu.SMEM),
                  pl.BlockSpec((nb * seq, n), lambda b: (b, 0)),
                  _batch_spec(vat, nb), _batch_spec(vbt, nb)],
        out_specs=pl.BlockSpec((nb * seq, width), lambda b: (b, 0)),
        out_shape=jax.ShapeDtypeStruct((t, width), BF16),
        scratch_shapes=(_score_slots(seq, seq)
                        + _score_slots(seq, B_HEADS // B_KV_HEADS * seq)),
        compiler_params=_params(1),
        name="attn_even_ctx",
    )(sink, qk, vat, vbt)


def _na_bias_tiles(src_ref, h):
    kc = lax.broadcasted_iota(jnp.int32, (GRID_W, LANES), 0)
    qc = lax.rem(lax.broadcasted_iota(jnp.int32, (GRID_W, LANES), 1), GRID_W)
    cs = jnp.clip(qc - NA_WIN_W // 2, 0, GRID_W - NA_WIN_W)
    col_ok = (kc >= cs) & (kc < cs + NA_WIN_W)
    tiles = []
    for i in range(2 * NA_WIN_H):
        src = jnp.broadcast_to(src_ref[h, i:i + 1, :], (GRID_W, LANES))
        tiles.append(jnp.where(col_ok, pltpu.roll(src, 0, 1, stride=1, stride_axis=0), NEG_INF))
    return tiles


def _na_row_window(qr, grid_rows):
    wh = min(NA_WIN_H, grid_rows)
    start = min(max(qr - wh // 2, 0), grid_rows - wh)
    return start, start + wh


def _na_reachable_query_rows(key_rows, grid_rows):
    per_block = Q_TILE // GRID_W
    hit = [qr for qr in range(grid_rows)
           if any(_na_row_window(qr, grid_rows)[0] <= kr < _na_row_window(qr, grid_rows)[1]
                  for kr in key_rows)]
    return min(hit) // per_block * per_block, (max(hit) // per_block + 1) * per_block


def _na_bias_block(tiles, key_rows, q_rows, grid_rows):
    first_row = lax.broadcasted_iota(jnp.int32, (GRID_W, LANES), 1) < GRID_W

    def in_window(kr, qr):
        start, stop = _na_row_window(qr, grid_rows)
        return start <= kr < stop

    strips = []
    for kr in key_rows:
        parts = []
        for qr in range(q_rows.start, q_rows.stop, 2):
            ok0 = in_window(kr, qr)
            ok1 = in_window(kr, qr + 1)
            i = kr - qr + NA_WIN_H - 1
            if ok0 and ok1:
                parts.append(tiles[i])
            elif ok0:
                parts.append(jnp.where(first_row, tiles[i], NEG_INF))
            elif ok1:
                parts.append(jnp.where(first_row, NEG_INF, tiles[i]))
            else:
                parts.append(jnp.full((GRID_W, LANES), NEG_INF, F32))
        strips.append(jnp.concatenate(parts, axis=1))
    return jnp.concatenate(strips, axis=0)


def _lat_a_kernel(src_ref, q_ref, k_ref, vt_ref, ck_ref, cv_ref, o_ref,
                  qs_ref, kall_ref, vtall_ref, os_ref, s0_ref, s1_ref, m_ref):
    n = q_ref.shape[0]
    past = ck_ref.shape[4]
    grid_rows = n // GRID_W
    for h in range(A_HEADS):
        cols = slice(h * HEAD_DIM, (h + 1) * HEAD_DIM)
        qs_ref[h] = q_ref[:, cols]
        kall_ref[h, 0:n, :] = k_ref[:, cols]
        kall_ref[h, n:n + past, :] = _cache_keys(ck_ref, h)
        vtall_ref[h, 0:HEAD_DIM, 0:n] = vt_ref[0, h]
        vtall_ref[h, 0:HEAD_DIM, n:n + past] = cv_ref[0, 0, h].astype(BF16)
        vtall_ref[h, HEAD_DIM:, :] = jnp.ones((ONES_ROWS, n + past), BF16)

    slots = ((s0_ref, m_ref.at[0]), (s1_ref, m_ref.at[1]))

    def key_rows(c0):
        return range(c0 // GRID_W, (c0 + KEY_CHUNK) // GRID_W)

    def query_rows(c0):
        if c0 >= n:
            return range(0, grid_rows)
        return range(*_na_reachable_query_rows(key_rows(c0), grid_rows))

    def step(cur, nxt, h, h_next):
        tiles = None if h_next is None else _na_bias_tiles(src_ref, h_next)

        def bias(c0):
            if c0 >= n:
                return None
            return _na_bias_block(tiles, key_rows(c0), query_rows(c0), grid_rows)

        return _pipelined_item(
            cur, nxt, n + past, q_next=None if h_next is None else qs_ref[h_next],
            keys_next=lambda c0: kall_ref[h_next, c0:c0 + KEY_CHUNK, :],
            vt_cur=lambda c0: vtall_ref[h, :, c0:c0 + KEY_CHUNK], bias_next=bias,
            cols_of=lambda c0: (query_rows(c0).start * GRID_W, query_rows(c0).stop * GRID_W))

    step((None, None), slots[0], None, 0)

    def head_pair(p, carry=0):
        h0 = 2 * p
        following = None if isinstance(p, int) else h0 + 2
        os_ref[h0] = step(slots[0], slots[1], h0, h0 + 1).T.astype(BF16)
        os_ref[h0 + 1] = step(slots[1], slots[0], h0 + 1, following).T.astype(BF16)
        return carry

    lax.fori_loop(0, A_HEADS // 2 - 1, head_pair, 0)
    head_pair(A_HEADS // 2 - 1)
    for pair in range(A_HEADS // 2):
        o_ref[:, pair * LANES:(pair + 1) * LANES] = jnp.concatenate(
            [os_ref[2 * pair], os_ref[2 * pair + 1]], axis=1)


def _lat_a_attention(qk, vt, cache_k, cache_v, e, bias_src, seq):
    t = qk.shape[0]
    past = cache_k.shape[4]
    na = A_HEADS * HEAD_DIM
    return pl.pallas_call(
        _lat_a_kernel,
        grid=(t // seq,),
        in_specs=[
            _resident(bias_src.shape),
            pl.BlockSpec((seq, na), lambda b: (b, 0)),
            pl.BlockSpec((seq, na), lambda b: (b, 1)),
            _batch_spec(vt),
            _cache_spec(cache_k, e),
            _cache_spec(cache_v, e),
        ],
        out_specs=pl.BlockSpec((seq, na), lambda b: (b, 0)),
        out_shape=jax.ShapeDtypeStruct((t, na), BF16),
        scratch_shapes=[pltpu.VMEM((A_HEADS, seq, HEAD_DIM), BF16),
                        pltpu.VMEM((A_HEADS, seq + past, HEAD_DIM), BF16),
                        pltpu.VMEM((A_HEADS, HEAD_DIM + ONES_ROWS, seq + past), BF16),
                        pltpu.VMEM((A_HEADS, seq, HEAD_DIM), BF16)]
                       + _score_slots(seq + past, seq),
        compiler_params=_params(1),
        name="attn_even_lat_a",
    )(bias_src, qk, qk, vt, cache_k, cache_v)


def _lat_b_kernel(sink_ref, q_ref, k_ref, vt_ref, ck_ref, cv_ref, o_ref):
    n = q_ref.shape[0]
    group = B_HEADS // B_KV_HEADS
    win = 2 * Q_TILE
    ctx = [(_cache_keys(ck_ref, kv), _with_ones_rows(cv_ref[0, 0, kv]), None, None)
           for kv in range(B_KV_HEADS)]
    for j in range(n // Q_TILE):
        lo = min(max(Q_TILE * j - B_WINDOW, 0), n - win)
        q_rows = slice(j * Q_TILE, (j + 1) * Q_TILE)
        kpos = lo + lax.broadcasted_iota(jnp.int32, (win, 1), 0)
        qpos = j * Q_TILE + lax.broadcasted_iota(jnp.int32, (1, group * Q_TILE), 1) % Q_TILE
        valid = jnp.abs(qpos - kpos) <= B_WINDOW
        for kv in range(B_KV_HEADS):
            q = _stack_group(q_ref, q_rows, kv * group * HEAD_DIM, group)
            local = (k_ref[lo:lo + win, kv * HEAD_DIM:(kv + 1) * HEAD_DIM],
                     _with_ones_rows(vt_ref[0, kv, :, lo:lo + win]), None, valid)
            o = _attend([local, ctx[kv]], q, Q_TILE, group,
                        _sink_row(sink_ref, kv * group, group, Q_TILE))
            c0 = kv * group * HEAD_DIM
            o_ref[q_rows, c0:c0 + group * HEAD_DIM] = o.astype(BF16)


def _lat_b_attention(qk, vt, cache_k, cache_v, e, sink, seq):
    t = qk.shape[0]
    na = A_HEADS * HEAD_DIM
    nb = B_HEADS * HEAD_DIM
    nkv = B_KV_HEADS * HEAD_DIM
    return pl.pallas_call(
        _lat_b_kernel,
        grid=(t // seq,),
        in_specs=[
            pl.BlockSpec(memory_space=pltpu.SMEM),
            pl.BlockSpec((seq, nb), lambda b: (b, 2 * na // nb)),
            pl.BlockSpec((seq, nkv), lambda b: (b, (2 * na + nb) // nkv)),
            _batch_spec(vt),
            _cache_spec(cache_k, e),
            _cache_spec(cache_v, e),
        ],
        out_specs=pl.BlockSpec((seq, nb), lambda b: (b, 0)),
        out_shape=jax.ShapeDtypeStruct((t, nb), BF16),
        compiler_params=_params(1),
        name="attn_even_lat_b",
    )(sink, qk, qk, vt, cache_k, cache_v)


def _ctx_odd_attn_kernel(qk_ref, vt_ref, o_ref, s0_ref, s1_ref, m_ref):
    nb = vt_ref.shape[0]
    n = qk_ref.shape[0] // nb
    group = C_HEADS // C_KV_HEADS
    nq = C_HEADS * HEAD_DIM

    def rows_of(b, c0=0, size=n):
        if isinstance(b, int):
            return slice(b * n + c0, b * n + c0 + size)
        return pl.ds(pl.multiple_of(b * n + c0, size), size)

    slots = ((s0_ref, m_ref.at[0]), (s1_ref, m_ref.at[1]))

    def item(cur, nxt, b, kv, b_next, kv_next):
        kcols = slice(nq + kv_next * HEAD_DIM, nq + (kv_next + 1) * HEAD_DIM)
        ot = _pipelined_item(
            cur, nxt, n,
            q_next=_stack_group(qk_ref, rows_of(b_next), kv_next * group * HEAD_DIM, group),
            keys_next=lambda c0: qk_ref[rows_of(b_next, c0, KEY_CHUNK), kcols],
            vt_cur=lambda c0: _with_ones_rows(vt_ref[b, kv, :, c0:c0 + KEY_CHUNK]))
        if ot is None:
            return None
        return jnp.concatenate([ot[:, g * n:(g + 1) * n].T for g in range(group)], axis=1)

    item((None, None), slots[0], None, None, 0, 0)

    def sequence(b, carry):
        for kv in range(C_KV_HEADS):
            last = kv + 1 == C_KV_HEADS
            o = item(slots[kv % 2], slots[(kv + 1) % 2], b, kv,
                     lax.rem(b + 1, nb) if last else b, 0 if last else kv + 1)
            c0 = kv * group * HEAD_DIM
            o_ref[rows_of(b), c0:c0 + group * HEAD_DIM] = o.astype(BF16)
        return carry

    lax.fori_loop(0, nb, sequence, 0)


def _ctx_odd_attention(qk, vt, seq):
    t, n = qk.shape
    width = C_HEADS * HEAD_DIM
    nb = CTX_BATCHES
    return pl.pallas_call(
        _ctx_odd_attn_kernel,
        grid=(t // (nb * seq),),
        in_specs=[pl.BlockSpec((nb * seq, n), lambda b: (b, 0)), _batch_spec(vt, nb)],
        out_specs=pl.BlockSpec((nb * seq, width), lambda b: (b, 0)),
        out_shape=jax.ShapeDtypeStruct((t, width), BF16),
        scratch_shapes=_score_slots(seq, C_HEADS // C_KV_HEADS * seq),
        compiler_params=_params(1),
        name="attn_odd_ctx",
    )(qk, vt)


def _lat_c_kernel(qk_ref, vt_ref, ck_ref, cv_ref, o_ref, kall_ref, vtall_ref, s0_ref, s1_ref,
                  m_ref):
    n = qk_ref.shape[0]
    past = ck_ref.shape[4]
    group = C_HEADS // C_KV_HEADS
    nq = C_HEADS * HEAD_DIM
    n_blocks = n // Q_TILE
    for kv in range(C_KV_HEADS):
        kall_ref[kv, 0:past, :] = _cache_keys(ck_ref, kv)
        kall_ref[kv, past:past + n, :] = qk_ref[:, nq + kv * HEAD_DIM:nq + (kv + 1) * HEAD_DIM]
        vtall_ref[kv, 0:HEAD_DIM, 0:past] = cv_ref[0, 0, kv].astype(BF16)
        vtall_ref[kv, 0:HEAD_DIM, past:past + n] = vt_ref[0, kv]
        vtall_ref[kv, HEAD_DIM:, :] = jnp.ones((ONES_ROWS, past + n), BF16)

    def rows_of(j):
        if isinstance(j, int):
            return slice(j * Q_TILE, (j + 1) * Q_TILE)
        return pl.ds(pl.multiple_of(j * Q_TILE, Q_TILE), Q_TILE)

    slots = ((s0_ref, m_ref.at[0]), (s1_ref, m_ref.at[1]))

    def item(cur, nxt, kv, j_next, kv_next):
        ot = _pipelined_item(
            cur, nxt, past + n,
            q_next=None if j_next is None else _stack_group(
                qk_ref, rows_of(j_next), kv_next * group * HEAD_DIM, group),
            keys_next=lambda c0: kall_ref[kv_next, c0:c0 + KEY_CHUNK, :],
            vt_cur=lambda c0: vtall_ref[kv, :, c0:c0 + KEY_CHUNK])
        if ot is None:
            return None
        return jnp.concatenate(
            [ot[:, g * Q_TILE:(g + 1) * Q_TILE].T for g in range(group)], axis=1)

    item((None, None), slots[0], None, 0, 0)

    def q_block(j, carry=0):
        final = isinstance(j, int)
        for kv in range(C_KV_HEADS):
            last = kv + 1 == C_KV_HEADS
            o = item(slots[kv % 2], slots[(kv + 1) % 2], kv,
                     (None if final else j + 1) if last else j, 0 if last else kv + 1)
            c0 = kv * group * HEAD_DIM
            o_ref[rows_of(j), c0:c0 + group * HEAD_DIM] = o.astype(BF16)
        return carry

    lax.fori_loop(0, n_blocks - 1, q_block, 0)
    q_block(n_blocks - 1)


def _lat_c_attention(qk, vt, cache_k, cache_v, o, seq):
    t, n = qk.shape
    past = cache_k.shape[4]
    width = C_HEADS * HEAD_DIM
    return pl.pallas_call(
        _lat_c_kernel,
        grid=(t // seq,),
        in_specs=[
            pl.BlockSpec((seq, n), lambda b: (b, 0)),
            _batch_spec(vt),
            _cache_spec(cache_k, o),
            _cache_spec(cache_v, o),
        ],
        out_specs=pl.BlockSpec((seq, width), lambda b: (b, 0)),
        out_shape=jax.ShapeDtypeStruct((t, width), BF16),
        scratch_shapes=[pltpu.VMEM((C_KV_HEADS, past + seq, HEAD_DIM), BF16),
                        pltpu.VMEM((C_KV_HEADS, HEAD_DIM + ONES_ROWS, past + seq), BF16)]
                       + _score_slots(past + seq, C_HEADS // C_KV_HEADS * Q_TILE),
        compiler_params=_params(1),
        name="attn_odd_lat",
    )(qk, vt, cache_k, cache_v)


def _post_kernel(*refs, n_parts, final):
    o_refs = refs[:n_parts]
    (x_ref, wo_ref, g1_ref, sh_ref, sc_ref, g2_ref, gain_ref, wgu_ref, wd_ref) = refs[n_parts:n_parts + 9]
    rest = refs[n_parts + 9:]
    if final:
        fg_ref, out_ref, act_ref = rest
    else:
        out_ref, act_ref = rest
    mix = None
    r0 = 0
    for o_ref in o_refs:
        kk = o_ref.shape[1]
        part = jnp.dot(o_ref[...], wo_ref[r0:r0 + kk, :], preferred_element_type=F32)
        mix = part if mix is None else mix + part
        r0 += kk
    x1 = x_ref[...] + g1_ref[0, 0] * mix
    h = _adaln(x1, gain_ref[...], sh_ref[0, 0], sc_ref[0, 0]).astype(BF16)
    d_ff = wd_ref.shape[0]
    for j in range(d_ff // FF_CHUNK):
        c0 = j * FF_CHUNK
        gate = jnp.dot(h, wgu_ref[:, c0:c0 + FF_CHUNK], preferred_element_type=F32)
        up = jnp.dot(h, wgu_ref[:, d_ff + c0:d_ff + c0 + FF_CHUNK], preferred_element_type=F32)
        act_ref[:, c0:c0 + FF_CHUNK] = (gate * jax.nn.sigmoid(gate) * up).astype(BF16)
    ffn = jnp.dot(act_ref[...], wd_ref[...], preferred_element_type=F32)
    x2 = x1 + g2_ref[0, 0] * ffn
    if final:
        ms = jnp.mean(x2 * x2, axis=-1, keepdims=True)
        x2 = (x2 * lax.rsqrt(ms + RMS_EPS)) * fg_ref[...]
    out_ref[...] = x2


def _post_attention(o_parts, x, mods, layer, gain, w_out, sub, w_gu, w_down, *, is_lat, seq,
                    final_gain=None):
    t, d = x.shape
    tm = TOKEN_TILE
    per_seq = max(seq // tm, 1)
    group = (lambda i: 1 + i // per_seq) if is_lat else (lambda i: 0)
    row = lambda i: (i, 0)
    d_ff = w_down.shape[1]
    final = final_gain is not None
    in_specs = [pl.BlockSpec((tm, o.shape[1]), row) for o in o_parts]
    in_specs += [
        pl.BlockSpec((tm, d), row),
        _resident_layer(w_out, sub),
        _mod_spec(layer, 2, group),
        _mod_spec(layer, 3, group),
        _mod_spec(layer, 4, group),
        _mod_spec(layer, 5, group),
        _resident((1, d)),
        _resident_layer(w_gu, layer),
        _resident_layer(w_down, layer),
    ]
    args = list(o_parts) + [x, w_out, mods, mods, mods, mods, gain.reshape(1, d), w_gu, w_down]
    if final:
        in_specs.append(_resident((1, d)))
        args.append(final_gain.reshape(1, d))
    return pl.pallas_call(
        functools.partial(_post_kernel, n_parts=len(o_parts), final=final),
        grid=(t // tm,),
        in_specs=in_specs,
        out_specs=pl.BlockSpec((tm, d), row),
        out_shape=jax.ShapeDtypeStruct((t, d), F32),
        scratch_shapes=[pltpu.VMEM((tm, d_ff), BF16)],
        compiler_params=_params(1),
        name=f"post_{'lat' if is_lat else 'ctx'}{'_final' if final else ''}",
    )(*args)


def _rope_tables(n):
    t = np.arange(n)
    row = (t // GRID_W).astype(np.float32)
    col = (t % GRID_W).astype(np.float32)
    half = HEAD_DIM // 2
    inv_freq = np.float32(ROPE_THETA) ** (-np.arange(0, half, 2, dtype=np.float32) / np.float32(half))
    lane = np.arange(LANES)
    in_head = lane % HEAD_DIM
    pos = np.where((in_head < half)[None, :], row[:, None], col[:, None])
    ang = (pos * inv_freq[in_head % (half // 2)][None, :]).astype(np.float32)
    first = ((in_head % half) < half // 2)[None, :]
    cos = np.cos(ang)
    sin = np.sin(ang)
    zero = np.float32(0.0)
    return tuple(jnp.asarray(a, F32) for a in
                 (cos, np.where(first, -sin, zero), np.where(first, zero, sin)))


def _na_bias_sources(rpb):
    h, _, nb = rpb.shape
    w = NA_WIN_W - 1
    rp = jnp.pad(rpb[:, :, ::-1] * LOG2E, ((0, 0), (1, 1), (0, 0)))
    this, prev = rp[:, 1:], rp[:, :-1]
    z = jnp.zeros((h, 2 * NA_WIN_H, LANES // 2 - nb), F32)
    return jnp.concatenate([this[:, :, w:], z, prev, z, this[:, :, :w]], axis=-1)


def _state(y):
    return y.transpose(0, 3, 1, 2)


def kernel(x_prompt, x_sample, cache_a_k, cache_a_v, cache_b_k, cache_b_v, cache_c_k, cache_c_v,
           c, c_ctx, norm_gain, w_mod, b_mod, w_in_even, w_out_even, rpb_a, sink_b,
           w_in_odd, w_out_odd, q_norm_c, k_norm_c, w_gate_up, w_down, final_gain):
    batch, seq, d = x_prompt.shape
    dec_batch, dec_seq, _ = x_sample.shape
    depth = w_mod.shape[0]

    cvec = jnp.concatenate(
        [c_ctx[None, :], c, jnp.zeros((MOD_GROUPS - 1 - dec_batch, d), F32)], axis=0)
    mods = _modulation(cvec, w_mod, b_mod).reshape(depth, MOD_GROUPS, 1, 6 * d)
    rope = _rope_tables(dec_seq)
    w_in = {False: w_in_even.astype(BF16), True: w_in_odd.astype(BF16)}
    w_out = {False: w_out_even.astype(BF16), True: w_out_odd.astype(BF16)}
    w_gu = w_gate_up.astype(BF16)
    w_dn = w_down.astype(BF16)

    ctx = x_prompt.reshape(batch * seq, d)
    lat = x_sample.reshape(dec_batch * dec_seq, d)
    states = {name: [] for name in ("a_k", "a_v", "b_k", "b_v", "c_k", "c_v")}

    for layer in range(depth):
        odd = layer % 2 == 1
        sub = layer // 2
        gain1, gain2 = norm_gain[layer, 0], norm_gain[layer, 1]
        pre = functools.partial(_pre_attention, mods=mods, layer=layer, gain=gain1, w=w_in[odd],
                                sub=sub, odd=odd)
        if not odd:
            qk_c, ka, va, kb, vb = pre(ctx, is_lat=False, seq=seq)
            qk_l, vat_l, vbt_l = pre(lat, is_lat=True, seq=dec_seq, rope=rope)
            for name, y in (("a_k", ka), ("a_v", va), ("b_k", kb), ("b_v", vb)):
                states[name].append(_state(y))
            o_ctx = [_ctx_even_attention(qk_c, va, vb, sink_b[sub], seq)]
            o_lat = [
                _lat_a_attention(qk_l, vat_l, _feature_major(cache_a_k), _feature_major(cache_a_v),
                                 sub, _na_bias_sources(rpb_a[sub]), dec_seq),
                _lat_b_attention(qk_l, vbt_l, _feature_major(cache_b_k), _feature_major(cache_b_v),
                                 sub, sink_b[sub], dec_seq),
            ]
        else:
            per = LANES // HEAD_DIM
            head_gains = (jnp.tile(q_norm_c[sub], per).reshape(1, LANES),
                          jnp.tile(k_norm_c[sub], per).reshape(1, LANES))
            qk_c, kc, vc = pre(ctx, is_lat=False, seq=seq, head_gains=head_gains)
            qk_l, vct_l = pre(lat, is_lat=True, seq=dec_seq, rope=rope, head_gains=head_gains)
            states["c_k"].append(_state(kc))
            states["c_v"].append(_state(vc))
            o_ctx = [_ctx_odd_attention(qk_c, vc, seq)]
            o_lat = [_lat_c_attention(qk_l, vct_l, _feature_major(cache_c_k),
                                      _feature_major(cache_c_v), sub, dec_seq)]
        fg = final_gain if layer == depth - 1 else None
        post = functools.partial(_post_attention, mods=mods, layer=layer, gain=gain2,
                                 w_out=w_out[odd], sub=sub, w_gu=w_gu, w_down=w_dn, final_gain=fg)
        ctx = post(o_ctx, ctx, is_lat=False, seq=seq)
        lat = post(o_lat, lat, is_lat=True, seq=dec_seq)

    return (ctx.reshape(batch, seq, d), lat.reshape(dec_batch, dec_seq, d),
            jnp.stack(states["a_k"], axis=1), jnp.stack(states["a_v"], axis=1),
            jnp.stack(states["b_k"], axis=1), jnp.stack(states["b_v"], axis=1),
            jnp.stack(states["c_k"], axis=1), jnp.stack(states["c_v"], axis=1))
```

```python
import functools
import math

import jax
import jax.numpy as jnp
import numpy as np
from jax import lax
from jax.experimental import pallas as pl
from jax.experimental.pallas import tpu as pltpu

F32 = jnp.float32
BF16 = jnp.bfloat16

D_MODEL = 1024
GRID_W = 64
HEAD_DIM = 64
A_HEADS = 8
B_HEADS = 8
B_KV_HEADS = 2
C_HEADS = 16
C_KV_HEADS = 4
NA_WIN_H = 8
NA_WIN_W = 16
B_WINDOW = 128
ROPE_THETA = 10000.0
RMS_EPS = 1e-6
NEG_INF = -1e30
LOG2E = math.log2(math.e)
QK_SCALE = LOG2E / math.sqrt(HEAD_DIM)

LANES = 128
TOKEN_TILE = 512
Q_TILE = 256
CTX_BATCHES = 8
ONES_ROWS = 16
KEY_CHUNK = 256
FF_CHUNK = 256
MOD_GROUPS = 16
VMEM_LIMIT = 56 * 1024 * 1024


def _params(n_axes, vmem=VMEM_LIMIT):
    return pltpu.CompilerParams(
        dimension_semantics=("arbitrary",) * n_axes, vmem_limit_bytes=vmem)


def _resident(shape):
    nd = len(shape)
    return pl.BlockSpec(shape, lambda *_: (0,) * nd, pipeline_mode=pl.Buffered(1))


def _resident_layer(stacked, layer):
    return pl.BlockSpec((None,) + stacked.shape[1:], lambda *_: (layer, 0, 0),
                        pipeline_mode=pl.Buffered(1))


def _mod_kernel(c_ref, w_ref, b_ref, o_ref):
    c = c_ref[...]
    s = (c * jax.nn.sigmoid(c)).astype(BF16)
    o_ref[0] = jnp.dot(s, w_ref[0].astype(BF16), preferred_element_type=F32) + b_ref[0]


def _modulation(cvec, w_mod, b_mod):
    depth, d, n = w_mod.shape
    tn = 1536
    return pl.pallas_call(
        _mod_kernel,
        grid=(depth, n // tn),
        in_specs=[
            pl.BlockSpec((MOD_GROUPS, d), lambda l, j: (0, 0)),
            pl.BlockSpec((1, d, tn), lambda l, j: (l, 0, j)),
            pl.BlockSpec((1, 1, tn), lambda l, j: (l, 0, j)),
        ],
        out_specs=pl.BlockSpec((1, MOD_GROUPS, tn), lambda l, j: (l, 0, j)),
        out_shape=jax.ShapeDtypeStruct((depth, MOD_GROUPS, n), F32),
        compiler_params=_params(2),
        name="modulation",
    )(cvec, w_mod, b_mod.reshape(depth, 1, n))


def _mod_spec(layer, which, group_of_step):
    return pl.BlockSpec((1, 1, 1, D_MODEL), lambda i: (layer, group_of_step(i), 0, which))


def _adaln(x, gain, shift, scale):
    ms = jnp.mean(x * x, axis=-1, keepdims=True)
    return (x * lax.rsqrt(ms + RMS_EPS)) * gain * (1.0 + scale) + shift


def _rope(y, cos, sin_lo, sin_hi):
    outs = []
    for c in range(y.shape[1] // LANES):
        yc = y[:, c * LANES:(c + 1) * LANES]
        outs.append(yc * cos
                    + pltpu.roll(yc, LANES - 16, 1) * sin_lo
                    + pltpu.roll(yc, 16, 1) * sin_hi)
    return outs[0] if len(outs) == 1 else jnp.concatenate(outs, axis=1)


def _head_rms_norm(y, gain):
    first = lax.broadcasted_iota(jnp.int32, (1, LANES), 1) < HEAD_DIM
    outs = []
    for c in range(y.shape[1] // LANES):
        yc = y[:, c * LANES:(c + 1) * LANES]
        sq = yc * yc
        s0 = jnp.sum(jnp.where(first, sq, 0.0), axis=-1, keepdims=True)
        s1 = jnp.sum(jnp.where(first, 0.0, sq), axis=-1, keepdims=True)
        ms = jnp.where(first, s0, s1) * (1.0 / HEAD_DIM)
        outs.append(yc * lax.rsqrt(ms + RMS_EPS) * gain)
    return outs[0] if len(outs) == 1 else jnp.concatenate(outs, axis=1)


_NT = (((1,), (1,)), ((), ()))


def _with_ones_rows(vt):
    return jnp.concatenate([vt.astype(BF16), jnp.ones((ONES_ROWS, vt.shape[1]), BF16)], axis=0)


def _attend(segments, q, rows, group, sink=None):
    scores = []
    for k, _, bias, valid in segments:
        s = _scores(k, q)
        if bias is not None:
            s = s + bias
        if valid is not None:
            s = jnp.where(valid, s, NEG_INF)
        scores.append(s)
    return _softmax_values(scores, [seg[1] for seg in segments], rows, group, sink)


def _scores(k, q):
    return lax.dot_general(k, q, _NT, preferred_element_type=F32)


def _softmax_values(scores, vts, rows, group, sink=None):
    m = functools.reduce(jnp.maximum, [jnp.max(s, axis=0, keepdims=True) for s in scores])
    if sink is not None:
        m = jnp.maximum(m, sink)
    ot = None
    for s, vt_ones in zip(scores, vts):
        part = jnp.dot(vt_ones, jnp.exp2(s - m).astype(BF16), preferred_element_type=F32)
        ot = part if ot is None else ot + part
    denom = ot[HEAD_DIM:HEAD_DIM + 1, :]
    if sink is not None:
        denom = denom + jnp.exp2(sink - m)
    ot = ot[:HEAD_DIM, :] / denom
    return jnp.concatenate([ot[:, g * rows:(g + 1) * rows].T for g in range(group)], axis=1)


def _pipelined_item(cur, nxt, n_keys, *, q_next, keys_next, vt_cur, bias_next=None,
                    cols_of=None, sink_cur=None, sink_next=None):
    s_cur, m_cur = cur
    s_nxt, m_nxt = nxt
    width = s_nxt.shape[1]
    if cols_of is None:
        cols_of = lambda c0: (0, width)
    m = None if s_cur is None else m_cur[...]
    m_next = [None] * (width // Q_TILE)
    ot = [None] * (width // Q_TILE)
    for c0 in range(0, n_keys, KEY_CHUNK):
        keys = slice(c0, c0 + KEY_CHUNK)
        lo, hi = cols_of(c0)
        blocks = [(g, slice(g * Q_TILE - lo, (g + 1) * Q_TILE - lo))
                  for g in range(lo // Q_TILE, hi // Q_TILE)]
        if q_next is not None:
            s = _scores(keys_next(c0), q_next[lo:hi])
            bias = None if bias_next is None else bias_next(c0)
            if bias is not None:
                s = s + bias
            s_nxt[keys, lo:hi] = s
            m_c = jnp.max(s, axis=0, keepdims=True)
            for g, cols in blocks:
                m_next[g] = (m_c[:, cols] if m_next[g] is None
                             else jnp.maximum(m_next[g], m_c[:, cols]))
        if s_cur is not None:
            p = jnp.exp2(s_cur[keys, lo:hi] - m[:, lo:hi]).astype(BF16)
            part = jnp.dot(vt_cur(c0), p, preferred_element_type=F32)
            for g, cols in blocks:
                ot[g] = part[:, cols] if ot[g] is None else ot[g] + part[:, cols]
    if q_next is not None:
        m_next = jnp.concatenate(m_next, axis=1)
        if sink_next is not None:
            m_next = jnp.maximum(m_next, sink_next)
        m_nxt[...] = m_next
    if s_cur is None:
        return None
    ot = jnp.concatenate(ot, axis=1)
    denom = ot[HEAD_DIM:HEAD_DIM + 1, :]
    if sink_cur is not None:
        denom = denom + jnp.exp2(sink_cur - m)
    return ot[:HEAD_DIM, :] / denom


def _score_slots(n_keys, m):
    return [pltpu.VMEM((n_keys, m), F32), pltpu.VMEM((n_keys, m), F32),
            pltpu.VMEM((2, 1, m), F32)]


def _stack_group(ref, rows, col0, group):
    return jnp.concatenate(
        [ref[rows, col0 + g * HEAD_DIM: col0 + (g + 1) * HEAD_DIM] for g in range(group)], axis=0)


def _sink_row(sink_ref, h0, group, rows):
    return jnp.concatenate(
        [jnp.full((1, rows), sink_ref[h0 + g] * LOG2E, F32) for g in range(group)], axis=1)


def _store_feature_major(ref, y):
    nb, heads, _, seq = ref.shape
    yt = y.T
    for b in range(nb):
        for h in range(heads):
            ref[b, h] = yt[h * HEAD_DIM:(h + 1) * HEAD_DIM, b * seq:(b + 1) * seq].astype(ref.dtype)


def _pre_even_kernel(x_ref, g_ref, sh_ref, sc_ref, w_ref, *rest, is_lat):
    if is_lat:
        cos_ref, slo_ref, shi_ref, qk_ref, va_ref, vb_ref = rest
    else:
        qk_ref, ka_ref, va_ref, kb_ref, vb_ref = rest
    h = _adaln(x_ref[...], g_ref[...], sh_ref[0, 0], sc_ref[0, 0]).astype(BF16)

    def proj(c0, c1):
        return jnp.dot(h, w_ref[:, c0:c1], preferred_element_type=F32)

    na = A_HEADS * HEAD_DIM
    nb = B_HEADS * HEAD_DIM
    nkv = B_KV_HEADS * HEAD_DIM
    qk_ref[:, 0:na] = (proj(0, na) * QK_SCALE).astype(BF16)
    ka = proj(na, 2 * na)
    qk_ref[:, na:2 * na] = ka.astype(BF16)
    va = proj(2 * na, 3 * na)
    qb = proj(3 * na, 3 * na + nb)
    if is_lat:
        qb = _rope(qb, cos_ref[...], slo_ref[...], shi_ref[...])
    qk_ref[:, 2 * na:2 * na + nb] = (qb * QK_SCALE).astype(BF16)
    kvb = proj(3 * na + nb, 3 * na + nb + 2 * nkv)
    kb, vb = kvb[:, :nkv], kvb[:, nkv:]
    if is_lat:
        kb_out = _rope(kb, cos_ref[...], slo_ref[...], shi_ref[...])
    else:
        kb_out = kb
    qk_ref[:, 2 * na + nb:2 * na + nb + nkv] = kb_out.astype(BF16)
    _store_feature_major(va_ref, va)
    _store_feature_major(vb_ref, vb)
    if not is_lat:
        _store_feature_major(ka_ref, ka)
        _store_feature_major(kb_ref, kb)


def _pre_odd_kernel(x_ref, g_ref, sh_ref, sc_ref, w_ref, qn_ref, kn_ref, *rest, is_lat):
    if is_lat:
        cos_ref, slo_ref, shi_ref, qk_ref, vc_ref = rest
    else:
        qk_ref, kc_ref, vc_ref = rest
    h = _adaln(x_ref[...], g_ref[...], sh_ref[0, 0], sc_ref[0, 0]).astype(BF16)

    def proj(c0, c1):
        return jnp.dot(h, w_ref[:, c0:c1], preferred_element_type=F32)

    nq = C_HEADS * HEAD_DIM
    nkv = C_KV_HEADS * HEAD_DIM
    q = _head_rms_norm(proj(0, nq), qn_ref[...])
    k = _head_rms_norm(proj(nq, nq + nkv), kn_ref[...])
    v = proj(nq + nkv, nq + 2 * nkv)
    if is_lat:
        q = _rope(q, cos_ref[...], slo_ref[...], shi_ref[...])
        k_out = _rope(k, cos_ref[...], slo_ref[...], shi_ref[...])
    else:
        k_out = k
        _store_feature_major(kc_ref, k)
    _store_feature_major(vc_ref, v)
    qk_ref[:, 0:nq] = (q * QK_SCALE).astype(BF16)
    qk_ref[:, nq:nq + nkv] = k_out.astype(BF16)


def _pre_attention(x, mods, layer, gain, w, sub, *, odd, is_lat, seq, rope=None,
                   head_gains=None):
    t, d = x.shape
    tm = TOKEN_TILE
    per_seq = max(seq // tm, 1)
    per_tile = max(tm // seq, 1)
    group = (lambda i: 1 + i // per_seq) if is_lat else (lambda i: 0)
    row = lambda i: (i, 0)
    in_specs = [
        pl.BlockSpec((tm, d), row),
        _resident((1, d)),
        _mod_spec(layer, 0, group),
        _mod_spec(layer, 1, group),
        _resident_layer(w, sub),
    ]
    args = [x, gain.reshape(1, d), mods, mods, w]
    if odd:
        in_specs += [_resident(a.shape) for a in head_gains]
        args += list(head_gains)
    if is_lat:
        in_specs += [pl.BlockSpec((tm, LANES), lambda i: (i % per_seq, 0))] * 3
        args += list(rope)
    if odd:
        n_qk = (C_HEADS + C_KV_HEADS) * HEAD_DIM
        lat_heads, ctx_heads = [C_KV_HEADS], [C_KV_HEADS, C_KV_HEADS]
    else:
        n_qk = (2 * A_HEADS + B_HEADS + B_KV_HEADS) * HEAD_DIM
        lat_heads, ctx_heads = [A_HEADS, B_KV_HEADS], [A_HEADS, A_HEADS, B_KV_HEADS, B_KV_HEADS]
    out_specs = [pl.BlockSpec((tm, n_qk), row)]
    out_shape = [jax.ShapeDtypeStruct((t, n_qk), BF16)]
    for heads in (lat_heads if is_lat else ctx_heads):
        blk = (per_tile, heads, HEAD_DIM, min(tm, seq))
        out_specs.append(pl.BlockSpec(blk, lambda i: (i // per_seq, 0, 0, i % per_seq)))
        out_shape.append(jax.ShapeDtypeStruct((t // seq, heads, HEAD_DIM, seq),
                                              BF16 if is_lat else F32))
    body = _pre_odd_kernel if odd else _pre_even_kernel
    return pl.pallas_call(
        functools.partial(body, is_lat=is_lat),
        grid=(t // tm,),
        in_specs=in_specs,
        out_specs=out_specs,
        out_shape=out_shape,
        compiler_params=_params(1),
        name=f"pre_{'odd' if odd else 'even'}_{'lat' if is_lat else 'ctx'}",
    )(*args)


def _batch_spec(arr, nb=1):
    nd = arr.ndim
    return pl.BlockSpec((nb,) + arr.shape[1:], lambda b: (b,) + (0,) * (nd - 1))


def _cache_spec(cache, layer):
    blk = (1, 1) + cache.shape[2:]
    return pl.BlockSpec(blk, lambda b: (b, layer, 0, 0, 0))


def _feature_major(cache):
    return cache.transpose(0, 1, 3, 4, 2)


def _cache_keys(ref, h):
    return ref[0, 0, h].T.astype(BF16)


def _ctx_even_attn_kernel(sink_ref, qk_ref, vat_ref, vbt_ref, o_ref,
                          sa0_ref, sa1_ref, ma_ref, sb0_ref, sb1_ref, mb_ref):
    nb = vat_ref.shape[0]
    n = qk_ref.shape[0] // nb
    na = A_HEADS * HEAD_DIM
    group = B_HEADS // B_KV_HEADS
    qb0 = 2 * na
    kb0 = qb0 + B_HEADS * HEAD_DIM

    def rows_of(b, c0=0, size=n):
        if isinstance(b, int):
            return slice(b * n + c0, b * n + c0 + size)
        return pl.ds(pl.multiple_of(b * n + c0, size), size)

    def following(b, i, count):
        if i + 1 < count:
            return b, i + 1
        return (None, 0) if isinstance(b, int) else (b + 1, 0)

    slots_a = ((sa0_ref, ma_ref.at[0]), (sa1_ref, ma_ref.at[1]))

    def item_a(cur, nxt, b, h, b_next, h_next):
        c = h_next * HEAD_DIM
        return _pipelined_item(
            cur, nxt, n,
            q_next=None if b_next is None else qk_ref[rows_of(b_next), c:c + HEAD_DIM],
            keys_next=lambda c0: qk_ref[rows_of(b_next, c0, KEY_CHUNK), na + c:na + c + HEAD_DIM],
            vt_cur=lambda c0: _with_ones_rows(vat_ref[b, h, :, c0:c0 + KEY_CHUNK]))

    item_a((None, None), slots_a[0], None, None, 0, 0)

    def sequence_a(b, carry=0):
        for pair in range(A_HEADS // 2):
            outs = []
            for h in (2 * pair, 2 * pair + 1):
                ot = item_a(slots_a[h % 2], slots_a[(h + 1) % 2], b, h, *following(b, h, A_HEADS))
                outs.append(ot.T)
            o_ref[rows_of(b), pair * LANES:(pair + 1) * LANES] = (
                jnp.concatenate(outs, axis=1).astype(BF16))
        return carry

    lax.fori_loop(0, nb - 1, sequence_a, 0)
    sequence_a(nb - 1)

    slots_b = ((sb0_ref, mb_ref.at[0]), (sb1_ref, mb_ref.at[1]))
    sinks = [_sink_row(sink_ref, kv * group, group, n) for kv in range(B_KV_HEADS)]

    def item_b(cur, nxt, b, kv, b_next, kv_next):
        kcols = slice(kb0 + kv_next * HEAD_DIM, kb0 + (kv_next + 1) * HEAD_DIM)
        ot = _pipelined_item(
            cur, nxt, n,
            q_next=None if b_next is None else _stack_group(
                qk_ref, rows_of(b_next), qb0 + kv_next * group * HEAD_DIM, group),
            keys_next=lambda c0: qk_ref[rows_of(b_next, c0, KEY_CHUNK), kcols],
            vt_cur=lambda c0: _with_ones_rows(vbt_ref[b, kv, :, c0:c0 + KEY_CHUNK]),
            sink_cur=None if kv is None else sinks[kv], sink_next=sinks[kv_next])
        if ot is None:
            return None
        return jnp.concatenate([ot[:, g * n:(g + 1) * n].T for g in range(group)], axis=1)

    item_b((None, None), slots_b[0], None, None, 0, 0)

    def sequence_b(b, carry=0):
        for kv in range(B_KV_HEADS):
            o = item_b(slots_b[kv % 2], slots_b[(kv + 1) % 2], b, kv,
                       *following(b, kv, B_KV_HEADS))
            c0 = na + kv * group * HEAD_DIM
            o_ref[rows_of(b), c0:c0 + group * HEAD_DIM] = o.astype(BF16)
        return carry

    lax.fori_loop(0, nb - 1, sequence_b, 0)
    sequence_b(nb - 1)


def _ctx_even_attention(qk, vat, vbt, sink, seq):
    t, n = qk.shape
    width = (A_HEADS + B_HEADS) * HEAD_DIM
    nb = CTX_BATCHES
    return pl.pallas_call(
        _ctx_even_attn_kernel,
        grid=(t // (nb * seq),),
        in_specs=[pl.BlockSpec(memory_space=pltpu.SMEM),
                  pl.BlockSpec((nb * seq, n), lambda b: (b, 0)),
                  _batch_spec(vat, nb), _batch_spec(vbt, nb)],
        out_specs=pl.BlockSpec((nb * seq, width), lambda b: (b, 0)),
        out_shape=jax.ShapeDtypeStruct((t, width), BF16),
        scratch_shapes=(_score_slots(seq, seq)
                        + _score_slots(seq, B_HEADS // B_KV_HEADS * seq)),
        compiler_params=_params(1),
        name="attn_even_ctx",
    )(sink, qk, vat, vbt)


def _na_bias_tiles(src_ref, h):
    kc = lax.broadcasted_iota(jnp.int32, (GRID_W, LANES), 0)
    qc = lax.rem(lax.broadcasted_iota(jnp.int32, (GRID_W, LANES), 1), GRID_W)
    cs = jnp.clip(qc - NA_WIN_W // 2, 0, GRID_W - NA_WIN_W)
    col_ok = (kc >= cs) & (kc < cs + NA_WIN_W)
    tiles = []
    for i in range(2 * NA_WIN_H):
        src = jnp.broadcast_to(src_ref[h, i:i + 1, :], (GRID_W, LANES))
        tiles.append(jnp.where(col_ok, pltpu.roll(src, 0, 1, stride=1, stride_axis=0), NEG_INF))
    return tiles


def _na_row_window(qr, grid_rows):
    wh = min(NA_WIN_H, grid_rows)
    start = min(max(qr - wh // 2, 0), grid_rows - wh)
    return start, start + wh


def _na_reachable_query_rows(key_rows, grid_rows):
    per_block = Q_TILE // GRID_W
    hit = [qr for qr in range(grid_rows)
           if any(_na_row_window(qr, grid_rows)[0] <= kr < _na_row_window(qr, grid_rows)[1]
                  for kr in key_rows)]
    return min(hit) // per_block * per_block, (max(hit) // per_block + 1) * per_block


def _na_bias_block(tiles, key_rows, q_rows, grid_rows):
    first_row = lax.broadcasted_iota(jnp.int32, (GRID_W, LANES), 1) < GRID_W

    def in_window(kr, qr):
        start, stop = _na_row_window(qr, grid_rows)
        return start <= kr < stop

    strips = []
    for kr in key_rows:
        parts = []
        for qr in range(q_rows.start, q_rows.stop, 2):
            ok0 = in_window(kr, qr)
            ok1 = in_window(kr, qr + 1)
            i = kr - qr + NA_WIN_H - 1
            if ok0 and ok1:
                parts.append(tiles[i])
            elif ok0:
                parts.append(jnp.where(first_row, tiles[i], NEG_INF))
            elif ok1:
                parts.append(jnp.where(first_row, NEG_INF, tiles[i]))
            else:
                parts.append(jnp.full((GRID_W, LANES), NEG_INF, F32))
        strips.append(jnp.concatenate(parts, axis=1))
    return jnp.concatenate(strips, axis=0)


def _lat_a_kernel(src_ref, q_ref, k_ref, vt_ref, ck_ref, cv_ref, o_ref,
                  qs_ref, kall_ref, vtall_ref, os_ref, s0_ref, s1_ref, m_ref):
    n = q_ref.shape[0]
    past = ck_ref.shape[4]
    grid_rows = n // GRID_W
    for h in range(A_HEADS):
        cols = slice(h * HEAD_DIM, (h + 1) * HEAD_DIM)
        qs_ref[h] = q_ref[:, cols]
        kall_ref[h, 0:n, :] = k_ref[:, cols]
        kall_ref[h, n:n + past, :] = _cache_keys(ck_ref, h)
        vtall_ref[h, 0:HEAD_DIM, 0:n] = vt_ref[0, h]
        vtall_ref[h, 0:HEAD_DIM, n:n + past] = cv_ref[0, 0, h].astype(BF16)
        vtall_ref[h, HEAD_DIM:, :] = jnp.ones((ONES_ROWS, n + past), BF16)

    slots = ((s0_ref, m_ref.at[0]), (s1_ref, m_ref.at[1]))

    def key_rows(c0):
        return range(c0 // GRID_W, (c0 + KEY_CHUNK) // GRID_W)

    def query_rows(c0):
        if c0 >= n:
            return range(0, grid_rows)
        return range(*_na_reachable_query_rows(key_rows(c0), grid_rows))

    def step(cur, nxt, h, h_next):
        tiles = None if h_next is None else _na_bias_tiles(src_ref, h_next)

        def bias(c0):
            if c0 >= n:
                return None
            return _na_bias_block(tiles, key_rows(c0), query_rows(c0), grid_rows)

        return _pipelined_item(
            cur, nxt, n + past, q_next=None if h_next is None else qs_ref[h_next],
            keys_next=lambda c0: kall_ref[h_next, c0:c0 + KEY_CHUNK, :],
            vt_cur=lambda c0: vtall_ref[h, :, c0:c0 + KEY_CHUNK], bias_next=bias,
            cols_of=lambda c0: (query_rows(c0).start * GRID_W, query_rows(c0).stop * GRID_W))

    step((None, None), slots[0], None, 0)

    def head_pair(p, carry=0):
        h0 = 2 * p
        following = None if isinstance(p, int) else h0 + 2
        os_ref[h0] = step(slots[0], slots[1], h0, h0 + 1).T.astype(BF16)
        os_ref[h0 + 1] = step(slots[1], slots[0], h0 + 1, following).T.astype(BF16)
        return carry

    lax.fori_loop(0, A_HEADS // 2 - 1, head_pair, 0)
    head_pair(A_HEADS // 2 - 1)
    for pair in range(A_HEADS // 2):
        o_ref[:, pair * LANES:(pair + 1) * LANES] = jnp.concatenate(
            [os_ref[2 * pair], os_ref[2 * pair + 1]], axis=1)


def _lat_a_attention(qk, vt, cache_k, cache_v, e, bias_src, seq):
    t = qk.shape[0]
    past = cache_k.shape[4]
    na = A_HEADS * HEAD_DIM
    return pl.pallas_call(
        _lat_a_kernel,
        grid=(t // seq,),
        in_specs=[
            _resident(bias_src.shape),
            pl.BlockSpec((seq, na), lambda b: (b, 0)),
            pl.BlockSpec((seq, na), lambda b: (b, 1)),
            _batch_spec(vt),
            _cache_spec(cache_k, e),
            _cache_spec(cache_v, e),
        ],
        out_specs=pl.BlockSpec((seq, na), lambda b: (b, 0)),
        out_shape=jax.ShapeDtypeStruct((t, na), BF16),
        scratch_shapes=[pltpu.VMEM((A_HEADS, seq, HEAD_DIM), BF16),
                        pltpu.VMEM((A_HEADS, seq + past, HEAD_DIM), BF16),
                        pltpu.VMEM((A_HEADS, HEAD_DIM + ONES_ROWS, seq + past), BF16),
                        pltpu.VMEM((A_HEADS, seq, HEAD_DIM), BF16)]
                       + _score_slots(seq + past, seq),
        compiler_params=_params(1),
        name="attn_even_lat_a",
    )(bias_src, qk, qk, vt, cache_k, cache_v)


def _band_window_start(j, n):
    return min(max(Q_TILE * j - B_WINDOW, 0), n - 2 * Q_TILE)


def _band_bias(n, group):
    j = np.arange(n // Q_TILE)[:, None, None]
    lo = np.clip(Q_TILE * j - B_WINDOW, 0, n - 2 * Q_TILE)
    kpos = lo + np.arange(2 * Q_TILE)[None, :, None]
    qpos = Q_TILE * j + (np.arange(group * Q_TILE) % Q_TILE)[None, None, :]
    return jnp.asarray(np.where(np.abs(qpos - kpos) <= B_WINDOW, 0.0, NEG_INF), F32)


def _lat_b_kernel(sink_ref, band_ref, q_ref, k_ref, vt_ref, ck_ref, cv_ref, o_ref,
                  ckeys_ref, cvt_ref, vtw_ref, s0_ref, s1_ref, m_ref):
    n = q_ref.shape[0]
    past = ck_ref.shape[4]
    group = B_HEADS // B_KV_HEADS
    win = 2 * Q_TILE
    n_blocks = n // Q_TILE
    for kv in range(B_KV_HEADS):
        ckeys_ref[kv] = _cache_keys(ck_ref, kv)
        cvt_ref[kv] = _with_ones_rows(cv_ref[0, 0, kv])
        for j in range(n_blocks):
            lo = _band_window_start(j, n)
            vtw_ref[j, kv] = _with_ones_rows(vt_ref[0, kv, :, lo:lo + win])
    sinks = [_sink_row(sink_ref, kv * group, group, Q_TILE) for kv in range(B_KV_HEADS)]

    def rows_of(j):
        if isinstance(j, int):
            return slice(j * Q_TILE, (j + 1) * Q_TILE)
        return pl.ds(pl.multiple_of(j * Q_TILE, Q_TILE), Q_TILE)

    def window_rows(j, c0):
        if isinstance(j, int):
            lo = _band_window_start(j, n) + c0
            return slice(lo, lo + KEY_CHUNK)
        lo = jnp.clip(Q_TILE * j - B_WINDOW, 0, n - win) + c0
        return pl.ds(pl.multiple_of(lo, B_WINDOW), KEY_CHUNK)

    slots = ((s0_ref, m_ref.at[0]), (s1_ref, m_ref.at[1]))

    def item(cur, nxt, j, kv, j_next, kv_next):
        kcols = slice(kv_next * HEAD_DIM, (kv_next + 1) * HEAD_DIM)

        def keys_next(c0):
            if c0 < win:
                return k_ref[window_rows(j_next, c0), kcols]
            return ckeys_ref[kv_next, c0 - win:c0 - win + KEY_CHUNK, :]

        def vt_cur(c0):
            if c0 < win:
                return vtw_ref[j, kv, :, c0:c0 + KEY_CHUNK]
            return cvt_ref[kv, :, c0 - win:c0 - win + KEY_CHUNK]

        ot = _pipelined_item(
            cur, nxt, win + past,
            q_next=None if j_next is None else _stack_group(
                q_ref, rows_of(j_next), kv_next * group * HEAD_DIM, group),
            keys_next=keys_next, vt_cur=vt_cur,
            bias_next=lambda c0: band_ref[j_next, c0:c0 + KEY_CHUNK, :] if c0 < win else None,
            sink_cur=None if kv is None else sinks[kv], sink_next=sinks[kv_next])
        if ot is None:
            return None
        return jnp.concatenate(
            [ot[:, g * Q_TILE:(g + 1) * Q_TILE].T for g in range(group)], axis=1)

    item((None, None), slots[0], None, None, 0, 0)

    def q_block(j, carry=0):
        final = isinstance(j, int)
        for kv in range(B_KV_HEADS):
            last = kv + 1 == B_KV_HEADS
            o = item(slots[kv % 2], slots[(kv + 1) % 2], j, kv,
                     (None if final else j + 1) if last else j, 0 if last else kv + 1)
            c0 = kv * group * HEAD_DIM
            o_ref[rows_of(j), c0:c0 + group * HEAD_DIM] = o.astype(BF16)
        return carry

    lax.fori_loop(0, n_blocks - 1, q_block, 0)
    q_block(n_blocks - 1)


def _lat_b_attention(qk, vt, cache_k, cache_v, e, sink, seq):
    t = qk.shape[0]
    past = cache_k.shape[4]
    na = A_HEADS * HEAD_DIM
    nb = B_HEADS * HEAD_DIM
    nkv = B_KV_HEADS * HEAD_DIM
    group = B_HEADS // B_KV_HEADS
    band = _band_bias(seq, group)
    return pl.pallas_call(
        _lat_b_kernel,
        grid=(t // seq,),
        in_specs=[
            pl.BlockSpec(memory_space=pltpu.SMEM),
            _resident(band.shape),
            pl.BlockSpec((seq, nb), lambda b: (b, 2 * na // nb)),
            pl.BlockSpec((seq, nkv), lambda b: (b, (2 * na + nb) // nkv)),
            _batch_spec(vt),
            _cache_spec(cache_k, e),
            _cache_spec(cache_v, e),
        ],
        out_specs=pl.BlockSpec((seq, nb), lambda b: (b, 0)),
        out_shape=jax.ShapeDtypeStruct((t, nb), BF16),
        scratch_shapes=[pltpu.VMEM((B_KV_HEADS, past, HEAD_DIM), BF16),
                        pltpu.VMEM((B_KV_HEADS, HEAD_DIM + ONES_ROWS, past), BF16),
                        pltpu.VMEM((seq // Q_TILE, B_KV_HEADS, HEAD_DIM + ONES_ROWS, 2 * Q_TILE),
                                   BF16)]
                       + _score_slots(2 * Q_TILE + past, group * Q_TILE),
        compiler_params=_params(1),
        name="attn_even_lat_b",
    )(sink, band, qk, qk, vt, cache_k, cache_v)


def _ctx_odd_attn_kernel(qk_ref, vt_ref, o_ref, s0_ref, s1_ref, m_ref):
    nb = vt_ref.shape[0]
    n = qk_ref.shape[0] // nb
    group = C_HEADS // C_KV_HEADS
    nq = C_HEADS * HEAD_DIM

    def rows_of(b, c0=0, size=n):
        if isinstance(b, int):
            return slice(b * n + c0, b * n + c0 + size)
        return pl.ds(pl.multiple_of(b * n + c0, size), size)

    slots = ((s0_ref, m_ref.at[0]), (s1_ref, m_ref.at[1]))

    def item(cur, nxt, b, kv, b_next, kv_next):
        kcols = slice(nq + kv_next * HEAD_DIM, nq + (kv_next + 1) * HEAD_DIM)
        ot = _pipelined_item(
            cur, nxt, n,
            q_next=None if b_next is None else _stack_group(
                qk_ref, rows_of(b_next), kv_next * group * HEAD_DIM, group),
            keys_next=lambda c0: qk_ref[rows_of(b_next, c0, KEY_CHUNK), kcols],
            vt_cur=lambda c0: _with_ones_rows(vt_ref[b, kv, :, c0:c0 + KEY_CHUNK]))
        if ot is None:
            return None
        return jnp.concatenate([ot[:, g * n:(g + 1) * n].T for g in range(group)], axis=1)

    item((None, None), slots[0], None, None, 0, 0)

    def sequence(b, carry=0):
        final = isinstance(b, int)
        for kv in range(C_KV_HEADS):
            last = kv + 1 == C_KV_HEADS
            o = item(slots[kv % 2], slots[(kv + 1) % 2], b, kv,
                     (None if final else b + 1) if last else b, 0 if last else kv + 1)
            c0 = kv * group * HEAD_DIM
            o_ref[rows_of(b), c0:c0 + group * HEAD_DIM] = o.astype(BF16)
        return carry

    lax.fori_loop(0, nb - 1, sequence, 0)
    sequence(nb - 1)


def _ctx_odd_attention(qk, vt, seq):
    t, n = qk.shape
    width = C_HEADS * HEAD_DIM
    nb = CTX_BATCHES
    return pl.pallas_call(
        _ctx_odd_attn_kernel,
        grid=(t // (nb * seq),),
        in_specs=[pl.BlockSpec((nb * seq, n), lambda b: (b, 0)), _batch_spec(vt, nb)],
        out_specs=pl.BlockSpec((nb * seq, width), lambda b: (b, 0)),
        out_shape=jax.ShapeDtypeStruct((t, width), BF16),
        scratch_shapes=_score_slots(seq, C_HEADS // C_KV_HEADS * seq),
        compiler_params=_params(1),
        name="attn_odd_ctx",
    )(qk, vt)


def _lat_c_kernel(qk_ref, vt_ref, ck_ref, cv_ref, o_ref, kall_ref, vtall_ref, s0_ref, s1_ref,
                  m_ref):
    n = qk_ref.shape[0]
    past = ck_ref.shape[4]
    group = C_HEADS // C_KV_HEADS
    nq = C_HEADS * HEAD_DIM
    n_blocks = n // Q_TILE
    for kv in range(C_KV_HEADS):
        kall_ref[kv, 0:past, :] = _cache_keys(ck_ref, kv)
        kall_ref[kv, past:past + n, :] = qk_ref[:, nq + kv * HEAD_DIM:nq + (kv + 1) * HEAD_DIM]
        vtall_ref[kv, 0:HEAD_DIM, 0:past] = cv_ref[0, 0, kv].astype(BF16)
        vtall_ref[kv, 0:HEAD_DIM, past:past + n] = vt_ref[0, kv]
        vtall_ref[kv, HEAD_DIM:, :] = jnp.ones((ONES_ROWS, past + n), BF16)

    def rows_of(j):
        if isinstance(j, int):
            return slice(j * Q_TILE, (j + 1) * Q_TILE)
        return pl.ds(pl.multiple_of(j * Q_TILE, Q_TILE), Q_TILE)

    slots = ((s0_ref, m_ref.at[0]), (s1_ref, m_ref.at[1]))

    def item(cur, nxt, kv, j_next, kv_next):
        ot = _pipelined_item(
            cur, nxt, past + n,
            q_next=None if j_next is None else _stack_group(
                qk_ref, rows_of(j_next), kv_next * group * HEAD_DIM, group),
            keys_next=lambda c0: kall_ref[kv_next, c0:c0 + KEY_CHUNK, :],
            vt_cur=lambda c0: vtall_ref[kv, :, c0:c0 + KEY_CHUNK])
        if ot is None:
            return None
        return jnp.concatenate(
            [ot[:, g * Q_TILE:(g + 1) * Q_TILE].T for g in range(group)], axis=1)

    item((None, None), slots[0], None, 0, 0)

    def q_block(j, carry=0):
        final = isinstance(j, int)
        for kv in range(C_KV_HEADS):
            last = kv + 1 == C_KV_HEADS
            o = item(slots[kv % 2], slots[(kv + 1) % 2], kv,
                     (None if final else j + 1) if last else j, 0 if last else kv + 1)
            c0 = kv * group * HEAD_DIM
            o_ref[rows_of(j), c0:c0 + group * HEAD_DIM] = o.astype(BF16)
        return carry

    lax.fori_loop(0, n_blocks - 1, q_block, 0)
    q_block(n_blocks - 1)


def _lat_c_attention(qk, vt, cache_k, cache_v, o, seq):
    t, n = qk.shape
    past = cache_k.shape[4]
    width = C_HEADS * HEAD_DIM
    return pl.pallas_call(
        _lat_c_kernel,
        grid=(t // seq,),
        in_specs=[
            pl.BlockSpec((seq, n), lambda b: (b, 0)),
            _batch_spec(vt),
            _cache_spec(cache_k, o),
            _cache_spec(cache_v, o),
        ],
        out_specs=pl.BlockSpec((seq, width), lambda b: (b, 0)),
        out_shape=jax.ShapeDtypeStruct((t, width), BF16),
        scratch_shapes=[pltpu.VMEM((C_KV_HEADS, past + seq, HEAD_DIM), BF16),
                        pltpu.VMEM((C_KV_HEADS, HEAD_DIM + ONES_ROWS, past + seq), BF16)]
                       + _score_slots(past + seq, C_HEADS // C_KV_HEADS * Q_TILE),
        compiler_params=_params(1),
        name="attn_odd_lat",
    )(qk, vt, cache_k, cache_v)


def _post_kernel(*refs, n_parts, final):
    o_refs = refs[:n_parts]
    (x_ref, wo_ref, g1_ref, sh_ref, sc_ref, g2_ref, gain_ref, wgu_ref, wd_ref) = refs[n_parts:n_parts + 9]
    rest = refs[n_parts + 9:]
    if final:
        fg_ref, out_ref, act_ref = rest
    else:
        out_ref, act_ref = rest
    mix = None
    r0 = 0
    for o_ref in o_refs:
        kk = o_ref.shape[1]
        part = jnp.dot(o_ref[...], wo_ref[r0:r0 + kk, :], preferred_element_type=F32)
        mix = part if mix is None else mix + part
        r0 += kk
    x1 = x_ref[...] + g1_ref[0, 0] * mix
    h = _adaln(x1, gain_ref[...], sh_ref[0, 0], sc_ref[0, 0]).astype(BF16)
    d_ff = wd_ref.shape[0]
    for j in range(d_ff // FF_CHUNK):
        c0 = j * FF_CHUNK
        gate = jnp.dot(h, wgu_ref[:, c0:c0 + FF_CHUNK], preferred_element_type=F32)
        up = jnp.dot(h, wgu_ref[:, d_ff + c0:d_ff + c0 + FF_CHUNK], preferred_element_type=F32)
        act_ref[:, c0:c0 + FF_CHUNK] = (gate * jax.nn.sigmoid(gate) * up).astype(BF16)
    ffn = jnp.dot(act_ref[...], wd_ref[...], preferred_element_type=F32)
    x2 = x1 + g2_ref[0, 0] * ffn
    if final:
        ms = jnp.mean(x2 * x2, axis=-1, keepdims=True)
        x2 = (x2 * lax.rsqrt(ms + RMS_EPS)) * fg_ref[...]
    out_ref[...] = x2


def _post_attention(o_parts, x, mods, layer, gain, w_out, sub, w_gu, w_down, *, is_lat, seq,
                    final_gain=None):
    t, d = x.shape
    tm = TOKEN_TILE
    per_seq = max(seq // tm, 1)
    group = (lambda i: 1 + i // per_seq) if is_lat else (lambda i: 0)
    row = lambda i: (i, 0)
    d_ff = w_down.shape[1]
    final = final_gain is not None
    in_specs = [pl.BlockSpec((tm, o.shape[1]), row) for o in o_parts]
    in_specs += [
        pl.BlockSpec((tm, d), row),
        _resident_layer(w_out, sub),
        _mod_spec(layer, 2, group),
        _mod_spec(layer, 3, group),
        _mod_spec(layer, 4, group),
        _mod_spec(layer, 5, group),
        _resident((1, d)),
        _resident_layer(w_gu, layer),
        _resident_layer(w_down, layer),
    ]
    args = list(o_parts) + [x, w_out, mods, mods, mods, mods, gain.reshape(1, d), w_gu, w_down]
    if final:
        in_specs.append(_resident((1, d)))
        args.append(final_gain.reshape(1, d))
    return pl.pallas_call(
        functools.partial(_post_kernel, n_parts=len(o_parts), final=final),
        grid=(t // tm,),
        in_specs=in_specs,
        out_specs=pl.BlockSpec((tm, d), row),
        out_shape=jax.ShapeDtypeStruct((t, d), F32),
        scratch_shapes=[pltpu.VMEM((tm, d_ff), BF16)],
        compiler_params=_params(1),
        name=f"post_{'lat' if is_lat else 'ctx'}{'_final' if final else ''}",
    )(*args)


def _rope_tables(n):
    t = np.arange(n)
    row = (t // GRID_W).astype(np.float32)
    col = (t % GRID_W).astype(np.float32)
    half = HEAD_DIM // 2
    inv_freq = np.float32(ROPE_THETA) ** (-np.arange(0, half, 2, dtype=np.float32) / np.float32(half))
    lane = np.arange(LANES)
    in_head = lane % HEAD_DIM
    pos = np.where((in_head < half)[None, :], row[:, None], col[:, None])
    ang = (pos * inv_freq[in_head % (half // 2)][None, :]).astype(np.float32)
    first = ((in_head % half) < half // 2)[None, :]
    cos = np.cos(ang)
    sin = np.sin(ang)
    zero = np.float32(0.0)
    return tuple(jnp.asarray(a, F32) for a in
                 (cos, np.where(first, -sin, zero), np.where(first, zero, sin)))


def _na_bias_sources(rpb):
    h, _, nb = rpb.shape
    w = NA_WIN_W - 1
    rp = jnp.pad(rpb[:, :, ::-1] * LOG2E, ((0, 0), (1, 1), (0, 0)))
    this, prev = rp[:, 1:], rp[:, :-1]
    z = jnp.zeros((h, 2 * NA_WIN_H, LANES // 2 - nb), F32)
    return jnp.concatenate([this[:, :, w:], z, prev, z, this[:, :, :w]], axis=-1)


def _state(y):
    return y.transpose(0, 3, 1, 2)


def kernel(x_prompt, x_sample, cache_a_k, cache_a_v, cache_b_k, cache_b_v, cache_c_k, cache_c_v,
           c, c_ctx, norm_gain, w_mod, b_mod, w_in_even, w_out_even, rpb_a, sink_b,
           w_in_odd, w_out_odd, q_norm_c, k_norm_c, w_gate_up, w_down, final_gain):
    batch, seq, d = x_prompt.shape
    dec_batch, dec_seq, _ = x_sample.shape
    depth = w_mod.shape[0]

    cvec = jnp.concatenate(
        [c_ctx[None, :], c, jnp.zeros((MOD_GROUPS - 1 - dec_batch, d), F32)], axis=0)
    mods = _modulation(cvec, w_mod, b_mod).reshape(depth, MOD_GROUPS, 1, 6 * d)
    rope = _rope_tables(dec_seq)
    w_in = {False: w_in_even.astype(BF16), True: w_in_odd.astype(BF16)}
    w_out = {False: w_out_even.astype(BF16), True: w_out_odd.astype(BF16)}
    w_gu = w_gate_up.astype(BF16)
    w_dn = w_down.astype(BF16)

    ctx = x_prompt.reshape(batch * seq, d)
    lat = x_sample.reshape(dec_batch * dec_seq, d)
    states = {name: [] for name in ("a_k", "a_v", "b_k", "b_v", "c_k", "c_v")}

    for layer in range(depth):
        odd = layer % 2 == 1
        sub = layer // 2
        gain1, gain2 = norm_gain[layer, 0], norm_gain[layer, 1]
        pre = functools.partial(_pre_attention, mods=mods, layer=layer, gain=gain1, w=w_in[odd],
                                sub=sub, odd=odd)
        if not odd:
            qk_c, ka, va, kb, vb = pre(ctx, is_lat=False, seq=seq)
            qk_l, vat_l, vbt_l = pre(lat, is_lat=True, seq=dec_seq, rope=rope)
            for name, y in (("a_k", ka), ("a_v", va), ("b_k", kb), ("b_v", vb)):
                states[name].append(_state(y))
            o_ctx = [_ctx_even_attention(qk_c, va, vb, sink_b[sub], seq)]
            o_lat = [
                _lat_a_attention(qk_l, vat_l, _feature_major(cache_a_k), _feature_major(cache_a_v),
                                 sub, _na_bias_sources(rpb_a[sub]), dec_seq),
                _lat_b_attention(qk_l, vbt_l, _feature_major(cache_b_k), _feature_major(cache_b_v),
                                 sub, sink_b[sub], dec_seq),
            ]
        else:
            per = LANES // HEAD_DIM
            head_gains = (jnp.tile(q_norm_c[sub], per).reshape(1, LANES),
                          jnp.tile(k_norm_c[sub], per).reshape(1, LANES))
            qk_c, kc, vc = pre(ctx, is_lat=False, seq=seq, head_gains=head_gains)
            qk_l, vct_l = pre(lat, is_lat=True, seq=dec_seq, rope=rope, head_gains=head_gains)
            states["c_k"].append(_state(kc))
            states["c_v"].append(_state(vc))
            o_ctx = [_ctx_odd_attention(qk_c, vc, seq)]
            o_lat = [_lat_c_attention(qk_l, vct_l, _feature_major(cache_c_k),
                                      _feature_major(cache_c_v), sub, dec_seq)]
        fg = final_gain if layer == depth - 1 else None
        post = functools.partial(_post_attention, mods=mods, layer=layer, gain=gain2,
                                 w_out=w_out[odd], sub=sub, w_gu=w_gu, w_down=w_dn, final_gain=fg)
        ctx = post(o_ctx, ctx, is_lat=False, seq=seq)
        lat = post(o_lat, lat, is_lat=True, seq=dec_seq)

    return (ctx.reshape(batch, seq, d), lat.reshape(dec_batch, dec_seq, d),
            jnp.stack(states["a_k"], axis=1), jnp.stack(states["a_v"], axis=1),
            jnp.stack(states["b_k"], axis=1), jnp.stack(states["b_v"], axis=1),
            jnp.stack(states["c_k"], axis=1), jnp.stack(states["c_v"], axis=1))
```

```python
import functools
import math

import jax
import jax.numpy as jnp
import numpy as np
from jax import lax
from jax.experimental import pallas as pl
from jax.experimental.pallas import tpu as pltpu

F32 = jnp.float32
BF16 = jnp.bfloat16

D_MODEL = 1024
GRID_W = 64
HEAD_DIM = 64
A_HEADS = 8
B_HEADS = 8
B_KV_HEADS = 2
C_HEADS = 16
C_KV_HEADS = 4
NA_WIN_H = 8
NA_WIN_W = 16
B_WINDOW = 128
ROPE_THETA = 10000.0
RMS_EPS = 1e-6
NEG_INF = -1e30
LOG2E = math.log2(math.e)
QK_SCALE = LOG2E / math.sqrt(HEAD_DIM)

LANES = 128
TOKEN_TILE = 512
Q_TILE = 256
CTX_BATCHES = 8
ONES_ROWS = 16
KEY_CHUNK = 256
FF_CHUNK = 256
MOD_GROUPS = 16
VMEM_LIMIT = 56 * 1024 * 1024


def _params(n_axes, vmem=VMEM_LIMIT):
    return pltpu.CompilerParams(
        dimension_semantics=("arbitrary",) * n_axes, vmem_limit_bytes=vmem)


def _resident(shape):
    nd = len(shape)
    return pl.BlockSpec(shape, lambda *_: (0,) * nd, pipeline_mode=pl.Buffered(1))


def _resident_layer(stacked, layer):
    return pl.BlockSpec((None,) + stacked.shape[1:], lambda *_: (layer, 0, 0),
                        pipeline_mode=pl.Buffered(1))


def _mod_kernel(c_ref, w_ref, b_ref, o_ref):
    c = c_ref[...]
    s = (c * jax.nn.sigmoid(c)).astype(BF16)
    o_ref[0] = jnp.dot(s, w_ref[0].astype(BF16), preferred_element_type=F32) + b_ref[0]


def _modulation(cvec, w_mod, b_mod):
    depth, d, n = w_mod.shape
    tn = 1536
    return pl.pallas_call(
        _mod_kernel,
        grid=(depth, n // tn),
        in_specs=[
            pl.BlockSpec((MOD_GROUPS, d), lambda l, j: (0, 0)),
            pl.BlockSpec((1, d, tn), lambda l, j: (l, 0, j)),
            pl.BlockSpec((1, 1, tn), lambda l, j: (l, 0, j)),
        ],
        out_specs=pl.BlockSpec((1, MOD_GROUPS, tn), lambda l, j: (l, 0, j)),
        out_shape=jax.ShapeDtypeStruct((depth, MOD_GROUPS, n), F32),
        compiler_params=_params(2),
        name="modulation",
    )(cvec, w_mod, b_mod.reshape(depth, 1, n))


def _mod_spec(layer, which, group_of_step):
    return pl.BlockSpec((1, 1, 1, D_MODEL), lambda i: (layer, group_of_step(i), 0, which))


def _adaln(x, gain, shift, scale):
    ms = jnp.mean(x * x, axis=-1, keepdims=True)
    return (x * lax.rsqrt(ms + RMS_EPS)) * gain * (1.0 + scale) + shift


def _rope(y, cos, sin_lo, sin_hi):
    outs = []
    for c in range(y.shape[1] // LANES):
        yc = y[:, c * LANES:(c + 1) * LANES]
        outs.append(yc * cos
                    + pltpu.roll(yc, LANES - 16, 1) * sin_lo
                    + pltpu.roll(yc, 16, 1) * sin_hi)
    return outs[0] if len(outs) == 1 else jnp.concatenate(outs, axis=1)


def _head_rms_norm(y, gain):
    first = lax.broadcasted_iota(jnp.int32, (1, LANES), 1) < HEAD_DIM
    outs = []
    for c in range(y.shape[1] // LANES):
        yc = y[:, c * LANES:(c + 1) * LANES]
        sq = yc * yc
        s0 = jnp.sum(jnp.where(first, sq, 0.0), axis=-1, keepdims=True)
        s1 = jnp.sum(jnp.where(first, 0.0, sq), axis=-1, keepdims=True)
        ms = jnp.where(first, s0, s1) * (1.0 / HEAD_DIM)
        outs.append(yc * lax.rsqrt(ms + RMS_EPS) * gain)
    return outs[0] if len(outs) == 1 else jnp.concatenate(outs, axis=1)


_NT = (((1,), (1,)), ((), ()))


def _with_ones_rows(vt):
    return jnp.concatenate([vt.astype(BF16), jnp.ones((ONES_ROWS, vt.shape[1]), BF16)], axis=0)


def _scores(k, q):
    return lax.dot_general(k, q, _NT, preferred_element_type=F32)


def _pipelined_item(cur, nxt, n_keys, *, q_next, keys_next, vt_cur, bias_next=None,
                    cols_of=None, sink_cur=None, sink_next=None):
    s_cur, m_cur = cur
    s_nxt, m_nxt = nxt
    width = s_nxt.shape[1]
    if cols_of is None:
        cols_of = lambda c0: (0, width)
    m = None if s_cur is None else m_cur[...]
    m_next = [None] * (width // Q_TILE)
    ot = [None] * (width // Q_TILE)
    for c0 in range(0, n_keys, KEY_CHUNK):
        keys = slice(c0, c0 + KEY_CHUNK)
        lo, hi = cols_of(c0)
        blocks = [(g, slice(g * Q_TILE - lo, (g + 1) * Q_TILE - lo))
                  for g in range(lo // Q_TILE, hi // Q_TILE)]
        if q_next is not None:
            s = _scores(keys_next(c0), q_next[lo:hi])
            bias = None if bias_next is None else bias_next(c0)
            if bias is not None:
                s = s + bias
            s_nxt[keys, lo:hi] = s
            m_c = jnp.max(s, axis=0, keepdims=True)
            for g, cols in blocks:
                m_next[g] = (m_c[:, cols] if m_next[g] is None
                             else jnp.maximum(m_next[g], m_c[:, cols]))
        if s_cur is not None:
            p = jnp.exp2(s_cur[keys, lo:hi] - m[:, lo:hi]).astype(BF16)
            part = jnp.dot(vt_cur(c0), p, preferred_element_type=F32)
            for g, cols in blocks:
                ot[g] = part[:, cols] if ot[g] is None else ot[g] + part[:, cols]
    if q_next is not None:
        m_next = jnp.concatenate(m_next, axis=1)
        if sink_next is not None:
            m_next = jnp.maximum(m_next, sink_next)
        m_nxt[...] = m_next
    if s_cur is None:
        return None
    ot = jnp.concatenate(ot, axis=1)
    denom = ot[HEAD_DIM:HEAD_DIM + 1, :]
    if sink_cur is not None:
        denom = denom + jnp.exp2(sink_cur - m)
    return ot[:HEAD_DIM, :] / denom


def _score_slots(n_keys, m):
    return [pltpu.VMEM((n_keys, m), F32), pltpu.VMEM((n_keys, m), F32),
            pltpu.VMEM((2, 1, m), F32)]


def _stack_group(ref, rows, col0, group):
    return jnp.concatenate(
        [ref[rows, col0 + g * HEAD_DIM: col0 + (g + 1) * HEAD_DIM] for g in range(group)], axis=0)


def _sink_row(sink_ref, h0, group, rows):
    return jnp.concatenate(
        [jnp.full((1, rows), sink_ref[h0 + g] * LOG2E, F32) for g in range(group)], axis=1)


def _store_feature_major(ref, y):
    nb, heads, _, seq = ref.shape
    yt = y.T
    for b in range(nb):
        for h in range(heads):
            ref[b, h] = yt[h * HEAD_DIM:(h + 1) * HEAD_DIM, b * seq:(b + 1) * seq].astype(ref.dtype)


def _pre_even_kernel(x_ref, g_ref, sh_ref, sc_ref, w_ref, *rest, is_lat):
    if is_lat:
        cos_ref, slo_ref, shi_ref, qk_ref, va_ref, vb_ref = rest
    else:
        qk_ref, ka_ref, va_ref, kb_ref, vb_ref = rest
    h = _adaln(x_ref[...], g_ref[...], sh_ref[0, 0], sc_ref[0, 0]).astype(BF16)

    def proj(c0, c1):
        return jnp.dot(h, w_ref[:, c0:c1], preferred_element_type=F32)

    na = A_HEADS * HEAD_DIM
    nb = B_HEADS * HEAD_DIM
    nkv = B_KV_HEADS * HEAD_DIM
    qk_ref[:, 0:na] = (proj(0, na) * QK_SCALE).astype(BF16)
    ka = proj(na, 2 * na)
    qk_ref[:, na:2 * na] = ka.astype(BF16)
    va = proj(2 * na, 3 * na)
    qb = proj(3 * na, 3 * na + nb)
    if is_lat:
        qb = _rope(qb, cos_ref[...], slo_ref[...], shi_ref[...])
    qk_ref[:, 2 * na:2 * na + nb] = (qb * QK_SCALE).astype(BF16)
    kvb = proj(3 * na + nb, 3 * na + nb + 2 * nkv)
    kb, vb = kvb[:, :nkv], kvb[:, nkv:]
    if is_lat:
        kb_out = _rope(kb, cos_ref[...], slo_ref[...], shi_ref[...])
    else:
        kb_out = kb
    qk_ref[:, 2 * na + nb:2 * na + nb + nkv] = kb_out.astype(BF16)
    _store_feature_major(va_ref, va)
    _store_feature_major(vb_ref, vb)
    if not is_lat:
        _store_feature_major(ka_ref, ka)
        _store_feature_major(kb_ref, kb)


def _pre_odd_kernel(x_ref, g_ref, sh_ref, sc_ref, w_ref, qn_ref, kn_ref, *rest, is_lat):
    if is_lat:
        cos_ref, slo_ref, shi_ref, qk_ref, vc_ref = rest
    else:
        qk_ref, kc_ref, vc_ref = rest
    h = _adaln(x_ref[...], g_ref[...], sh_ref[0, 0], sc_ref[0, 0]).astype(BF16)

    def proj(c0, c1):
        return jnp.dot(h, w_ref[:, c0:c1], preferred_element_type=F32)

    nq = C_HEADS * HEAD_DIM
    nkv = C_KV_HEADS * HEAD_DIM
    q = _head_rms_norm(proj(0, nq), qn_ref[...])
    k = _head_rms_norm(proj(nq, nq + nkv), kn_ref[...])
    v = proj(nq + nkv, nq + 2 * nkv)
    if is_lat:
        q = _rope(q, cos_ref[...], slo_ref[...], shi_ref[...])
        k_out = _rope(k, cos_ref[...], slo_ref[...], shi_ref[...])
    else:
        k_out = k
        _store_feature_major(kc_ref, k)
    _store_feature_major(vc_ref, v)
    qk_ref[:, 0:nq] = (q * QK_SCALE).astype(BF16)
    qk_ref[:, nq:nq + nkv] = k_out.astype(BF16)


def _pre_attention(x, mods, layer, gain, w, sub, *, odd, is_lat, seq, rope=None,
                   head_gains=None):
    t, d = x.shape
    tm = TOKEN_TILE
    per_seq = max(seq // tm, 1)
    per_tile = max(tm // seq, 1)
    group = (lambda i: 1 + i // per_seq) if is_lat else (lambda i: 0)
    row = lambda i: (i, 0)
    in_specs = [
        pl.BlockSpec((tm, d), row),
        _resident((1, d)),
        _mod_spec(layer, 0, group),
        _mod_spec(layer, 1, group),
        _resident_layer(w, sub),
    ]
    args = [x, gain.reshape(1, d), mods, mods, w]
    if odd:
        in_specs += [_resident(a.shape) for a in head_gains]
        args += list(head_gains)
    if is_lat:
        in_specs += [pl.BlockSpec((tm, LANES), lambda i: (i % per_seq, 0))] * 3
        args += list(rope)
    if odd:
        n_qk = (C_HEADS + C_KV_HEADS) * HEAD_DIM
        lat_heads, ctx_heads = [C_KV_HEADS], [C_KV_HEADS, C_KV_HEADS]
    else:
        n_qk = (2 * A_HEADS + B_HEADS + B_KV_HEADS) * HEAD_DIM
        lat_heads, ctx_heads = [A_HEADS, B_KV_HEADS], [A_HEADS, A_HEADS, B_KV_HEADS, B_KV_HEADS]
    out_specs = [pl.BlockSpec((tm, n_qk), row)]
    out_shape = [jax.ShapeDtypeStruct((t, n_qk), BF16)]
    for heads in (lat_heads if is_lat else ctx_heads):
        blk = (per_tile, heads, HEAD_DIM, min(tm, seq))
        out_specs.append(pl.BlockSpec(blk, lambda i: (i // per_seq, 0, 0, i % per_seq)))
        out_shape.append(jax.ShapeDtypeStruct((t // seq, heads, HEAD_DIM, seq),
                                              BF16 if is_lat else F32))
    body = _pre_odd_kernel if odd else _pre_even_kernel
    return pl.pallas_call(
        functools.partial(body, is_lat=is_lat),
        grid=(t // tm,),
        in_specs=in_specs,
        out_specs=out_specs,
        out_shape=out_shape,
        compiler_params=_params(1),
        name=f"pre_{'odd' if odd else 'even'}_{'lat' if is_lat else 'ctx'}",
    )(*args)


def _batch_spec(arr, nb=1):
    nd = arr.ndim
    return pl.BlockSpec((nb,) + arr.shape[1:], lambda b: (b,) + (0,) * (nd - 1))


def _cache_spec(cache, layer):
    blk = (1, 1) + cache.shape[2:]
    return pl.BlockSpec(blk, lambda b: (b, layer, 0, 0, 0))


def _feature_major(cache):
    return cache.transpose(0, 1, 3, 4, 2)


def _cache_keys(ref, h):
    return ref[0, 0, h].T.astype(BF16)


def _ctx_even_attn_kernel(sink_ref, qk_ref, vat_ref, vbt_ref, o_ref,
                          sa0_ref, sa1_ref, ma_ref, sb0_ref, sb1_ref, mb_ref):
    nb = vat_ref.shape[0]
    n = qk_ref.shape[0] // nb
    na = A_HEADS * HEAD_DIM
    group = B_HEADS // B_KV_HEADS
    qb0 = 2 * na
    kb0 = qb0 + B_HEADS * HEAD_DIM

    def rows_of(b, c0=0, size=n):
        if isinstance(b, int):
            return slice(b * n + c0, b * n + c0 + size)
        return pl.ds(pl.multiple_of(b * n + c0, size), size)

    def following(b, i, count):
        if i + 1 < count:
            return b, i + 1
        return (None, 0) if isinstance(b, int) else (b + 1, 0)

    slots_a = ((sa0_ref, ma_ref.at[0]), (sa1_ref, ma_ref.at[1]))

    def item_a(cur, nxt, b, h, b_next, h_next):
        c = h_next * HEAD_DIM
        return _pipelined_item(
            cur, nxt, n,
            q_next=None if b_next is None else qk_ref[rows_of(b_next), c:c + HEAD_DIM],
            keys_next=lambda c0: qk_ref[rows_of(b_next, c0, KEY_CHUNK), na + c:na + c + HEAD_DIM],
            vt_cur=lambda c0: _with_ones_rows(vat_ref[b, h, :, c0:c0 + KEY_CHUNK]))

    item_a((None, None), slots_a[0], None, None, 0, 0)

    def sequence_a(b, carry=0):
        for pair in range(A_HEADS // 2):
            outs = []
            for h in (2 * pair, 2 * pair + 1):
                ot = item_a(slots_a[h % 2], slots_a[(h + 1) % 2], b, h, *following(b, h, A_HEADS))
                outs.append(ot.T)
            o_ref[rows_of(b), pair * LANES:(pair + 1) * LANES] = (
                jnp.concatenate(outs, axis=1).astype(BF16))
        return carry

    lax.fori_loop(0, nb - 1, sequence_a, 0)
    sequence_a(nb - 1)

    slots_b = ((sb0_ref, mb_ref.at[0]), (sb1_ref, mb_ref.at[1]))
    sinks = [_sink_row(sink_ref, kv * group, group, n) for kv in range(B_KV_HEADS)]

    def item_b(cur, nxt, b, kv, b_next, kv_next):
        kcols = slice(kb0 + kv_next * HEAD_DIM, kb0 + (kv_next + 1) * HEAD_DIM)
        ot = _pipelined_item(
            cur, nxt, n,
            q_next=None if b_next is None else _stack_group(
                qk_ref, rows_of(b_next), qb0 + kv_next * group * HEAD_DIM, group),
            keys_next=lambda c0: qk_ref[rows_of(b_next, c0, KEY_CHUNK), kcols],
            vt_cur=lambda c0: _with_ones_rows(vbt_ref[b, kv, :, c0:c0 + KEY_CHUNK]),
            sink_cur=None if kv is None else sinks[kv], sink_next=sinks[kv_next])
        if ot is None:
            return None
        return jnp.concatenate([ot[:, g * n:(g + 1) * n].T for g in range(group)], axis=1)

    item_b((None, None), slots_b[0], None, None, 0, 0)

    def sequence_b(b, carry=0):
        for kv in range(B_KV_HEADS):
            o = item_b(slots_b[kv % 2], slots_b[(kv + 1) % 2], b, kv,
                       *following(b, kv, B_KV_HEADS))
            c0 = na + kv * group * HEAD_DIM
            o_ref[rows_of(b), c0:c0 + group * HEAD_DIM] = o.astype(BF16)
        return carry

    lax.fori_loop(0, nb - 1, sequence_b, 0)
    sequence_b(nb - 1)


def _ctx_even_attention(qk, vat, vbt, sink, seq):
    t, n = qk.shape
    width = (A_HEADS + B_HEADS) * HEAD_DIM
    nb = CTX_BATCHES
    return pl.pallas_call(
        _ctx_even_attn_kernel,
        grid=(t // (nb * seq),),
        in_specs=[pl.BlockSpec(memory_space=pltpu.SMEM),
                  pl.BlockSpec((nb * seq, n), lambda b: (b, 0)),
                  _batch_spec(vat, nb), _batch_spec(vbt, nb)],
        out_specs=pl.BlockSpec((nb * seq, width), lambda b: (b, 0)),
        out_shape=jax.ShapeDtypeStruct((t, width), BF16),
        scratch_shapes=(_score_slots(seq, seq)
                        + _score_slots(seq, B_HEADS // B_KV_HEADS * seq)),
        compiler_params=_params(1),
        name="attn_even_ctx",
    )(sink, qk, vat, vbt)


def _na_bias_tiles(src_ref, h):
    kc = lax.broadcasted_iota(jnp.int32, (GRID_W, LANES), 0)
    qc = lax.rem(lax.broadcasted_iota(jnp.int32, (GRID_W, LANES), 1), GRID_W)
    cs = jnp.clip(qc - NA_WIN_W // 2, 0, GRID_W - NA_WIN_W)
    col_ok = (kc >= cs) & (kc < cs + NA_WIN_W)
    tiles = []
    for i in range(2 * NA_WIN_H):
        src = jnp.broadcast_to(src_ref[h, i:i + 1, :], (GRID_W, LANES))
        tiles.append(jnp.where(col_ok, pltpu.roll(src, 0, 1, stride=1, stride_axis=0), NEG_INF))
    return tiles


def _na_row_window(qr, grid_rows):
    wh = min(NA_WIN_H, grid_rows)
    start = min(max(qr - wh // 2, 0), grid_rows - wh)
    return start, start + wh


def _na_reachable_query_rows(key_rows, grid_rows):
    per_block = Q_TILE // GRID_W
    hit = [qr for qr in range(grid_rows)
           if any(_na_row_window(qr, grid_rows)[0] <= kr < _na_row_window(qr, grid_rows)[1]
                  for kr in key_rows)]
    return min(hit) // per_block * per_block, (max(hit) // per_block + 1) * per_block


def _na_bias_block(tiles, key_rows, q_rows, grid_rows):
    first_row = lax.broadcasted_iota(jnp.int32, (GRID_W, LANES), 1) < GRID_W

    def in_window(kr, qr):
        start, stop = _na_row_window(qr, grid_rows)
        return start <= kr < stop

    strips = []
    for kr in key_rows:
        parts = []
        for qr in range(q_rows.start, q_rows.stop, 2):
            ok0 = in_window(kr, qr)
            ok1 = in_window(kr, qr + 1)
            i = kr - qr + NA_WIN_H - 1
            if ok0 and ok1:
                parts.append(tiles[i])
            elif ok0:
                parts.append(jnp.where(first_row, tiles[i], NEG_INF))
            elif ok1:
                parts.append(jnp.where(first_row, NEG_INF, tiles[i]))
            else:
                parts.append(jnp.full((GRID_W, LANES), NEG_INF, F32))
        strips.append(jnp.concatenate(parts, axis=1))
    return jnp.concatenate(strips, axis=0)


def _lat_a_kernel(src_ref, q_ref, k_ref, vt_ref, ck_ref, cv_ref, o_ref,
                  qs_ref, kall_ref, vtall_ref, os_ref, s0_ref, s1_ref, m_ref):
    n = q_ref.shape[0]
    past = ck_ref.shape[4]
    grid_rows = n // GRID_W
    for h in range(A_HEADS):
        cols = slice(h * HEAD_DIM, (h + 1) * HEAD_DIM)
        qs_ref[h] = q_ref[:, cols]
        kall_ref[h, 0:n, :] = k_ref[:, cols]
        kall_ref[h, n:n + past, :] = _cache_keys(ck_ref, h)
        vtall_ref[h, 0:HEAD_DIM, 0:n] = vt_ref[0, h]
        vtall_ref[h, 0:HEAD_DIM, n:n + past] = cv_ref[0, 0, h].astype(BF16)
        vtall_ref[h, HEAD_DIM:, :] = jnp.ones((ONES_ROWS, n + past), BF16)

    slots = ((s0_ref, m_ref.at[0]), (s1_ref, m_ref.at[1]))

    def key_rows(c0):
        return range(c0 // GRID_W, (c0 + KEY_CHUNK) // GRID_W)

    def query_rows(c0):
        if c0 >= n:
            return range(0, grid_rows)
        return range(*_na_reachable_query_rows(key_rows(c0), grid_rows))

    def step(cur, nxt, h, h_next):
        tiles = None if h_next is None else _na_bias_tiles(src_ref, h_next)

        def bias(c0):
            if c0 >= n:
                return None
            return _na_bias_block(tiles, key_rows(c0), query_rows(c0), grid_rows)

        return _pipelined_item(
            cur, nxt, n + past, q_next=None if h_next is None else qs_ref[h_next],
            keys_next=lambda c0: kall_ref[h_next, c0:c0 + KEY_CHUNK, :],
            vt_cur=lambda c0: vtall_ref[h, :, c0:c0 + KEY_CHUNK], bias_next=bias,
            cols_of=lambda c0: (query_rows(c0).start * GRID_W, query_rows(c0).stop * GRID_W))

    step((None, None), slots[0], None, 0)

    def head_pair(p, carry=0):
        h0 = 2 * p
        following = None if isinstance(p, int) else h0 + 2
        os_ref[h0] = step(slots[0], slots[1], h0, h0 + 1).T.astype(BF16)
        os_ref[h0 + 1] = step(slots[1], slots[0], h0 + 1, following).T.astype(BF16)
        return carry

    lax.fori_loop(0, A_HEADS // 2 - 1, head_pair, 0)
    head_pair(A_HEADS // 2 - 1)
    for pair in range(A_HEADS // 2):
        o_ref[:, pair * LANES:(pair + 1) * LANES] = jnp.concatenate(
            [os_ref[2 * pair], os_ref[2 * pair + 1]], axis=1)


def _lat_a_attention(qk, vt, cache_k, cache_v, e, bias_src, seq):
    t = qk.shape[0]
    past = cache_k.shape[4]
    na = A_HEADS * HEAD_DIM
    return pl.pallas_call(
        _lat_a_kernel,
        grid=(t // seq,),
        in_specs=[
            _resident(bias_src.shape),
            pl.BlockSpec((seq, na), lambda b: (b, 0)),
            pl.BlockSpec((seq, na), lambda b: (b, 1)),
            _batch_spec(vt),
            _cache_spec(cache_k, e),
            _cache_spec(cache_v, e),
        ],
        out_specs=pl.BlockSpec((seq, na), lambda b: (b, 0)),
        out_shape=jax.ShapeDtypeStruct((t, na), BF16),
        scratch_shapes=[pltpu.VMEM((A_HEADS, seq, HEAD_DIM), BF16),
                        pltpu.VMEM((A_HEADS, seq + past, HEAD_DIM), BF16),
                        pltpu.VMEM((A_HEADS, HEAD_DIM + ONES_ROWS, seq + past), BF16),
                        pltpu.VMEM((A_HEADS, seq, HEAD_DIM), BF16)]
                       + _score_slots(seq + past, seq),
        compiler_params=_params(1),
        name="attn_even_lat_a",
    )(bias_src, qk, qk, vt, cache_k, cache_v)


def _band_window_start(j, n):
    return min(max(Q_TILE * j - B_WINDOW, 0), n - 2 * Q_TILE)


def _band_bias(n, group):
    j = np.arange(n // Q_TILE)[:, None, None]
    lo = np.clip(Q_TILE * j - B_WINDOW, 0, n - 2 * Q_TILE)
    kpos = lo + np.arange(2 * Q_TILE)[None, :, None]
    qpos = Q_TILE * j + (np.arange(group * Q_TILE) % Q_TILE)[None, None, :]
    return jnp.asarray(np.where(np.abs(qpos - kpos) <= B_WINDOW, 0.0, NEG_INF), F32)


def _lat_b_kernel(sink_ref, band_ref, q_ref, k_ref, vt_ref, ck_ref, cv_ref, o_ref,
                  ckeys_ref, cvt_ref, kw_ref, vtw_ref, s0_ref, s1_ref, m_ref):
    n = q_ref.shape[0]
    past = ck_ref.shape[4]
    group = B_HEADS // B_KV_HEADS
    win = 2 * Q_TILE
    n_blocks = n // Q_TILE
    for kv in range(B_KV_HEADS):
        ckeys_ref[kv] = _cache_keys(ck_ref, kv)
        cvt_ref[kv] = _with_ones_rows(cv_ref[0, 0, kv])
        for j in range(n_blocks):
            lo = _band_window_start(j, n)
            kw_ref[j, kv] = k_ref[lo:lo + win, kv * HEAD_DIM:(kv + 1) * HEAD_DIM]
            vtw_ref[j, kv] = _with_ones_rows(vt_ref[0, kv, :, lo:lo + win])
    sinks = [_sink_row(sink_ref, kv * group, group, Q_TILE) for kv in range(B_KV_HEADS)]

    def rows_of(j):
        if isinstance(j, int):
            return slice(j * Q_TILE, (j + 1) * Q_TILE)
        return pl.ds(pl.multiple_of(j * Q_TILE, Q_TILE), Q_TILE)

    slots = ((s0_ref, m_ref.at[0]), (s1_ref, m_ref.at[1]))

    def item(cur, nxt, j, kv, j_next, kv_next):
        def keys_next(c0):
            if c0 < win:
                return kw_ref[j_next, kv_next, c0:c0 + KEY_CHUNK, :]
            return ckeys_ref[kv_next, c0 - win:c0 - win + KEY_CHUNK, :]

        def vt_cur(c0):
            if c0 < win:
                return vtw_ref[j, kv, :, c0:c0 + KEY_CHUNK]
            return cvt_ref[kv, :, c0 - win:c0 - win + KEY_CHUNK]

        ot = _pipelined_item(
            cur, nxt, win + past,
            q_next=None if j_next is None else _stack_group(
                q_ref, rows_of(j_next), kv_next * group * HEAD_DIM, group),
            keys_next=keys_next, vt_cur=vt_cur,
            bias_next=lambda c0: band_ref[j_next, c0:c0 + KEY_CHUNK, :] if c0 < win else None,
            sink_cur=None if kv is None else sinks[kv], sink_next=sinks[kv_next])
        if ot is None:
            return None
        return jnp.concatenate(
            [ot[:, g * Q_TILE:(g + 1) * Q_TILE].T for g in range(group)], axis=1)

    item((None, None), slots[0], None, None, 0, 0)

    def q_block(j, carry=0):
        final = isinstance(j, int)
        for kv in range(B_KV_HEADS):
            last = kv + 1 == B_KV_HEADS
            o = item(slots[kv % 2], slots[(kv + 1) % 2], j, kv,
                     (None if final else j + 1) if last else j, 0 if last else kv + 1)
            c0 = kv * group * HEAD_DIM
            o_ref[rows_of(j), c0:c0 + group * HEAD_DIM] = o.astype(BF16)
        return carry

    lax.fori_loop(0, n_blocks - 1, q_block, 0)
    q_block(n_blocks - 1)


def _lat_b_attention(qk, vt, cache_k, cache_v, e, sink, seq):
    t = qk.shape[0]
    past = cache_k.shape[4]
    na = A_HEADS * HEAD_DIM
    nb = B_HEADS * HEAD_DIM
    nkv = B_KV_HEADS * HEAD_DIM
    group = B_HEADS // B_KV_HEADS
    band = _band_bias(seq, group)
    return pl.pallas_call(
        _lat_b_kernel,
        grid=(t // seq,),
        in_specs=[
            pl.BlockSpec(memory_space=pltpu.SMEM),
            _resident(band.shape),
            pl.BlockSpec((seq, nb), lambda b: (b, 2 * na // nb)),
            pl.BlockSpec((seq, nkv), lambda b: (b, (2 * na + nb) // nkv)),
            _batch_spec(vt),
            _cache_spec(cache_k, e),
            _cache_spec(cache_v, e),
        ],
        out_specs=pl.BlockSpec((seq, nb), lambda b: (b, 0)),
        out_shape=jax.ShapeDtypeStruct((t, nb), BF16),
        scratch_shapes=[pltpu.VMEM((B_KV_HEADS, past, HEAD_DIM), BF16),
                        pltpu.VMEM((B_KV_HEADS, HEAD_DIM + ONES_ROWS, past), BF16),
                        pltpu.VMEM((seq // Q_TILE, B_KV_HEADS, 2 * Q_TILE, HEAD_DIM), BF16),
                        pltpu.VMEM((seq // Q_TILE, B_KV_HEADS, HEAD_DIM + ONES_ROWS, 2 * Q_TILE),
                                   BF16)]
                       + _score_slots(2 * Q_TILE + past, group * Q_TILE),
        compiler_params=_params(1),
        name="attn_even_lat_b",
    )(sink, band, qk, qk, vt, cache_k, cache_v)


def _ctx_odd_attn_kernel(qk_ref, vt_ref, o_ref, s0_ref, s1_ref, m_ref):
    nb = vt_ref.shape[0]
    n = qk_ref.shape[0] // nb
    group = C_HEADS // C_KV_HEADS
    nq = C_HEADS * HEAD_DIM

    def rows_of(b, c0=0, size=n):
        if isinstance(b, int):
            return slice(b * n + c0, b * n + c0 + size)
        return pl.ds(pl.multiple_of(b * n + c0, size), size)

    slots = ((s0_ref, m_ref.at[0]), (s1_ref, m_ref.at[1]))

    def item(cur, nxt, b, kv, b_next, kv_next):
        kcols = slice(nq + kv_next * HEAD_DIM, nq + (kv_next + 1) * HEAD_DIM)
        ot = _pipelined_item(
            cur, nxt, n,
            q_next=None if b_next is None else _stack_group(
                qk_ref, rows_of(b_next), kv_next * group * HEAD_DIM, group),
            keys_next=lambda c0: qk_ref[rows_of(b_next, c0, KEY_CHUNK), kcols],
            vt_cur=lambda c0: _with_ones_rows(vt_ref[b, kv, :, c0:c0 + KEY_CHUNK]))
        if ot is None:
            return None
        return jnp.concatenate([ot[:, g * n:(g + 1) * n].T for g in range(group)], axis=1)

    item((None, None), slots[0], None, None, 0, 0)

    def sequence(b, carry=0):
        final = isinstance(b, int)
        for kv in range(C_KV_HEADS):
            last = kv + 1 == C_KV_HEADS
            o = item(slots[kv % 2], slots[(kv + 1) % 2], b, kv,
                     (None if final else b + 1) if last else b, 0 if last else kv + 1)
            c0 = kv * group * HEAD_DIM
            o_ref[rows_of(b), c0:c0 + group * HEAD_DIM] = o.astype(BF16)
        return carry

    lax.fori_loop(0, nb - 1, sequence, 0)
    sequence(nb - 1)


def _ctx_odd_attention(qk, vt, seq):
    t, n = qk.shape
    width = C_HEADS * HEAD_DIM
    nb = CTX_BATCHES
    return pl.pallas_call(
        _ctx_odd_attn_kernel,
        grid=(t // (nb * seq),),
        in_specs=[pl.BlockSpec((nb * seq, n), lambda b: (b, 0)), _batch_spec(vt, nb)],
        out_specs=pl.BlockSpec((nb * seq, width), lambda b: (b, 0)),
        out_shape=jax.ShapeDtypeStruct((t, width), BF16),
        scratch_shapes=_score_slots(seq, C_HEADS // C_KV_HEADS * seq),
        compiler_params=_params(1),
        name="attn_odd_ctx",
    )(qk, vt)


def _lat_c_kernel(qk_ref, vt_ref, ck_ref, cv_ref, o_ref, kall_ref, vtall_ref, s0_ref, s1_ref,
                  m_ref):
    n = qk_ref.shape[0]
    past = ck_ref.shape[4]
    group = C_HEADS // C_KV_HEADS
    nq = C_HEADS * HEAD_DIM
    n_blocks = n // Q_TILE
    for kv in range(C_KV_HEADS):
        kall_ref[kv, 0:past, :] = _cache_keys(ck_ref, kv)
        kall_ref[kv, past:past + n, :] = qk_ref[:, nq + kv * HEAD_DIM:nq + (kv + 1) * HEAD_DIM]
        vtall_ref[kv, 0:HEAD_DIM, 0:past] = cv_ref[0, 0, kv].astype(BF16)
        vtall_ref[kv, 0:HEAD_DIM, past:past + n] = vt_ref[0, kv]
        vtall_ref[kv, HEAD_DIM:, :] = jnp.ones((ONES_ROWS, past + n), BF16)

    def rows_of(j):
        if isinstance(j, int):
            return slice(j * Q_TILE, (j + 1) * Q_TILE)
        return pl.ds(pl.multiple_of(j * Q_TILE, Q_TILE), Q_TILE)

    slots = ((s0_ref, m_ref.at[0]), (s1_ref, m_ref.at[1]))

    def item(cur, nxt, kv, j_next, kv_next):
        ot = _pipelined_item(
            cur, nxt, past + n,
            q_next=None if j_next is None else _stack_group(
                qk_ref, rows_of(j_next), kv_next * group * HEAD_DIM, group),
            keys_next=lambda c0: kall_ref[kv_next, c0:c0 + KEY_CHUNK, :],
            vt_cur=lambda c0: vtall_ref[kv, :, c0:c0 + KEY_CHUNK])
        if ot is None:
            return None
        return jnp.concatenate(
            [ot[:, g * Q_TILE:(g + 1) * Q_TILE].T for g in range(group)], axis=1)

    item((None, None), slots[0], None, 0, 0)

    def q_block(j, carry=0):
        final = isinstance(j, int)
        for kv in range(C_KV_HEADS):
            last = kv + 1 == C_KV_HEADS
            o = item(slots[kv % 2], slots[(kv + 1) % 2], kv,
                     (None if final else j + 1) if last else j, 0 if last else kv + 1)
            c0 = kv * group * HEAD_DIM
            o_ref[rows_of(j), c0:c0 + group * HEAD_DIM] = o.astype(BF16)
        return carry

    lax.fori_loop(0, n_blocks - 1, q_block, 0)
    q_block(n_blocks - 1)


def _lat_c_attention(qk, vt, cache_k, cache_v, o, seq):
    t, n = qk.shape
    past = cache_k.shape[4]
    width = C_HEADS * HEAD_DIM
    return pl.pallas_call(
        _lat_c_kernel,
        grid=(t // seq,),
        in_specs=[
            pl.BlockSpec((seq, n), lambda b: (b, 0)),
            _batch_spec(vt),
            _cache_spec(cache_k, o),
            _cache_spec(cache_v, o),
        ],
        out_specs=pl.BlockSpec((seq, width), lambda b: (b, 0)),
        out_shape=jax.ShapeDtypeStruct((t, width), BF16),
        scratch_shapes=[pltpu.VMEM((C_KV_HEADS, past + seq, HEAD_DIM), BF16),
                        pltpu.VMEM((C_KV_HEADS, HEAD_DIM + ONES_ROWS, past + seq), BF16)]
                       + _score_slots(past + seq, C_HEADS // C_KV_HEADS * Q_TILE),
        compiler_params=_params(1),
        name="attn_odd_lat",
    )(qk, vt, cache_k, cache_v)


def _post_kernel(*refs, n_parts, final):
    o_refs = refs[:n_parts]
    (x_ref, wo_ref, g1_ref, sh_ref, sc_ref, g2_ref, gain_ref, wgu_ref, wd_ref) = refs[n_parts:n_parts + 9]
    rest = refs[n_parts + 9:]
    if final:
        fg_ref, out_ref, act_ref = rest
    else:
        out_ref, act_ref = rest
    mix = None
    r0 = 0
    for o_ref in o_refs:
        kk = o_ref.shape[1]
        part = jnp.dot(o_ref[...], wo_ref[r0:r0 + kk, :], preferred_element_type=F32)
        mix = part if mix is None else mix + part
        r0 += kk
    x1 = x_ref[...] + g1_ref[0, 0] * mix
    h = _adaln(x1, gain_ref[...], sh_ref[0, 0], sc_ref[0, 0]).astype(BF16)
    d_ff = wd_ref.shape[0]
    for j in range(d_ff // FF_CHUNK):
        c0 = j * FF_CHUNK
        gate = jnp.dot(h, wgu_ref[:, c0:c0 + FF_CHUNK], preferred_element_type=F32)
        up = jnp.dot(h, wgu_ref[:, d_ff + c0:d_ff + c0 + FF_CHUNK], preferred_element_type=F32)
        act_ref[:, c0:c0 + FF_CHUNK] = (gate * jax.nn.sigmoid(gate) * up).astype(BF16)
    ffn = jnp.dot(act_ref[...], wd_ref[...], preferred_element_type=F32)
    x2 = x1 + g2_ref[0, 0] * ffn
    if final:
        ms = jnp.mean(x2 * x2, axis=-1, keepdims=True)
        x2 = (x2 * lax.rsqrt(ms + RMS_EPS)) * fg_ref[...]
    out_ref[...] = x2


def _post_attention(o_parts, x, mods, layer, gain, w_out, sub, w_gu, w_down, *, is_lat, seq,
                    final_gain=None):
    t, d = x.shape
    tm = TOKEN_TILE
    per_seq = max(seq // tm, 1)
    group = (lambda i: 1 + i // per_seq) if is_lat else (lambda i: 0)
    row = lambda i: (i, 0)
    d_ff = w_down.shape[1]
    final = final_gain is not None
    in_specs = [pl.BlockSpec((tm, o.shape[1]), row) for o in o_parts]
    in_specs += [
        pl.BlockSpec((tm, d), row),
        _resident_layer(w_out, sub),
        _mod_spec(layer, 2, group),
        _mod_spec(layer, 3, group),
        _mod_spec(layer, 4, group),
        _mod_spec(layer, 5, group),
        _resident((1, d)),
        _resident_layer(w_gu, layer),
        _resident_layer(w_down, layer),
    ]
    args = list(o_parts) + [x, w_out, mods, mods, mods, mods, gain.reshape(1, d), w_gu, w_down]
    if final:
        in_specs.append(_resident((1, d)))
        args.append(final_gain.reshape(1, d))
    return pl.pallas_call(
        functools.partial(_post_kernel, n_parts=len(o_parts), final=final),
        grid=(t // tm,),
        in_specs=in_specs,
        out_specs=pl.BlockSpec((tm, d), row),
        out_shape=jax.ShapeDtypeStruct((t, d), F32),
        scratch_shapes=[pltpu.VMEM((tm, d_ff), BF16)],
        compiler_params=_params(1),
        name=f"post_{'lat' if is_lat else 'ctx'}{'_final' if final else ''}",
    )(*args)


def _rope_tables(n):
    t = np.arange(n)
    row = (t // GRID_W).astype(np.float32)
    col = (t % GRID_W).astype(np.float32)
    half = HEAD_DIM // 2
    inv_freq = np.float32(ROPE_THETA) ** (-np.arange(0, half, 2, dtype=np.float32) / np.float32(half))
    lane = np.arange(LANES)
    in_head = lane % HEAD_DIM
    pos = np.where((in_head < half)[None, :], row[:, None], col[:, None])
    ang = (pos * inv_freq[in_head % (half // 2)][None, :]).astype(np.float32)
    first = ((in_head % half) < half // 2)[None, :]
    cos = np.cos(ang)
    sin = np.sin(ang)
    zero = np.float32(0.0)
    return tuple(jnp.asarray(a, F32) for a in
                 (cos, np.where(first, -sin, zero), np.where(first, zero, sin)))


def _na_bias_sources(rpb):
    h, _, nb = rpb.shape
    w = NA_WIN_W - 1
    rp = jnp.pad(rpb[:, :, ::-1] * LOG2E, ((0, 0), (1, 1), (0, 0)))
    this, prev = rp[:, 1:], rp[:, :-1]
    z = jnp.zeros((h, 2 * NA_WIN_H, LANES // 2 - nb), F32)
    return jnp.concatenate([this[:, :, w:], z, prev, z, this[:, :, :w]], axis=-1)


def _state(y):
    return y.transpose(0, 3, 1, 2)


def kernel(x_prompt, x_sample, cache_a_k, cache_a_v, cache_b_k, cache_b_v, cache_c_k, cache_c_v,
           c, c_ctx, norm_gain, w_mod, b_mod, w_in_even, w_out_even, rpb_a, sink_b,
           w_in_odd, w_out_odd, q_norm_c, k_norm_c, w_gate_up, w_down, final_gain):
    batch, seq, d = x_prompt.shape
    dec_batch, dec_seq, _ = x_sample.shape
    depth = w_mod.shape[0]

    cvec = jnp.concatenate(
        [c_ctx[None, :], c, jnp.zeros((MOD_GROUPS - 1 - dec_batch, d), F32)], axis=0)
    mods = _modulation(cvec, w_mod, b_mod).reshape(depth, MOD_GROUPS, 1, 6 * d)
    rope = _rope_tables(dec_seq)
    w_in = {False: w_in_even.astype(BF16), True: w_in_odd.astype(BF16)}
    w_out = {False: w_out_even.astype(BF16), True: w_out_odd.astype(BF16)}
    w_gu = w_gate_up.astype(BF16)
    w_dn = w_down.astype(BF16)

    ctx = x_prompt.reshape(batch * seq, d)
    lat = x_sample.reshape(dec_batch * dec_seq, d)
    states = {name: [] for name in ("a_k", "a_v", "b_k", "b_v", "c_k", "c_v")}

    for layer in range(depth):
        odd = layer % 2 == 1
        sub = layer // 2
        gain1, gain2 = norm_gain[layer, 0], norm_gain[layer, 1]
        pre = functools.partial(_pre_attention, mods=mods, layer=layer, gain=gain1, w=w_in[odd],
                                sub=sub, odd=odd)
        if not odd:
            qk_c, ka, va, kb, vb = pre(ctx, is_lat=False, seq=seq)
            qk_l, vat_l, vbt_l = pre(lat, is_lat=True, seq=dec_seq, rope=rope)
            for name, y in (("a_k", ka), ("a_v", va), ("b_k", kb), ("b_v", vb)):
                states[name].append(_state(y))
            o_ctx = [_ctx_even_attention(qk_c, va, vb, sink_b[sub], seq)]
            o_lat = [
                _lat_a_attention(qk_l, vat_l, _feature_major(cache_a_k), _feature_major(cache_a_v),
                                 sub, _na_bias_sources(rpb_a[sub]), dec_seq),
                _lat_b_attention(qk_l, vbt_l, _feature_major(cache_b_k), _feature_major(cache_b_v),
                                 sub, sink_b[sub], dec_seq),
            ]
        else:
            per = LANES // HEAD_DIM
            head_gains = (jnp.tile(q_norm_c[sub], per).reshape(1, LANES),
                          jnp.tile(k_norm_c[sub], per).reshape(1, LANES))
            qk_c, kc, vc = pre(ctx, is_lat=False, seq=seq, head_gains=head_gains)
            qk_l, vct_l = pre(lat, is_lat=True, seq=dec_seq, rope=rope, head_gains=head_gains)
            states["c_k"].append(_state(kc))
            states["c_v"].append(_state(vc))
            o_ctx = [_ctx_odd_attention(qk_c, vc, seq)]
            o_lat = [_lat_c_attention(qk_l, vct_l, _feature_major(cache_c_k),
                                      _feature_major(cache_c_v), sub, dec_seq)]
        fg = final_gain if layer == depth - 1 else None
        post = functools.partial(_post_attention, mods=mods, layer=layer, gain=gain2,
                                 w_out=w_out[odd], sub=sub, w_gu=w_gu, w_down=w_dn, final_gain=fg)
        ctx = post(o_ctx, ctx, is_lat=False, seq=seq)
        lat = post(o_lat, lat, is_lat=True, seq=dec_seq)

    return (ctx.reshape(batch, seq, d), lat.reshape(dec_batch, dec_seq, d),
            jnp.stack(states["a_k"], axis=1), jnp.stack(states["a_v"], axis=1),
            jnp.stack(states["b_k"], axis=1), jnp.stack(states["b_v"], axis=1),
            jnp.stack(states["c_k"], axis=1), jnp.stack(states["c_v"], axis=1))
```

```python
import functools
import math

import jax
import jax.numpy as jnp
import numpy as np
from jax import lax
from jax.experimental import pallas as pl
from jax.experimental.pallas import tpu as pltpu

F32 = jnp.float32
BF16 = jnp.bfloat16

D_MODEL = 1024
GRID_W = 64
HEAD_DIM = 64
A_HEADS = 8
B_HEADS = 8
B_KV_HEADS = 2
C_HEADS = 16
C_KV_HEADS = 4
NA_WIN_H = 8
NA_WIN_W = 16
B_WINDOW = 128
ROPE_THETA = 10000.0
RMS_EPS = 1e-6
NEG_INF = -1e30
LOG2E = math.log2(math.e)
QK_SCALE = LOG2E / math.sqrt(HEAD_DIM)

LANES = 128
TOKEN_TILE = 512
Q_TILE = 256
CTX_BATCHES = 8
ONES_ROWS = 16
KEY_CHUNK = 256
FF_CHUNK = 256
MOD_GROUPS = 16
VMEM_LIMIT = 56 * 1024 * 1024


def _params(n_axes, vmem=VMEM_LIMIT):
    return pltpu.CompilerParams(
        dimension_semantics=("arbitrary",) * n_axes, vmem_limit_bytes=vmem)


def _resident(shape):
    nd = len(shape)
    return pl.BlockSpec(shape, lambda *_: (0,) * nd, pipeline_mode=pl.Buffered(1))


def _resident_layer(stacked, layer):
    return pl.BlockSpec((None,) + stacked.shape[1:], lambda *_: (layer, 0, 0),
                        pipeline_mode=pl.Buffered(1))


def _mod_kernel(c_ref, w_ref, b_ref, o_ref):
    c = c_ref[...]
    s = (c * jax.nn.sigmoid(c)).astype(BF16)
    o_ref[0] = jnp.dot(s, w_ref[0].astype(BF16), preferred_element_type=F32) + b_ref[0]


def _modulation(cvec, w_mod, b_mod):
    depth, d, n = w_mod.shape
    tn = 1536
    return pl.pallas_call(
        _mod_kernel,
        grid=(depth, n // tn),
        in_specs=[
            pl.BlockSpec((MOD_GROUPS, d), lambda l, j: (0, 0)),
            pl.BlockSpec((1, d, tn), lambda l, j: (l, 0, j)),
            pl.BlockSpec((1, 1, tn), lambda l, j: (l, 0, j)),
        ],
        out_specs=pl.BlockSpec((1, MOD_GROUPS, tn), lambda l, j: (l, 0, j)),
        out_shape=jax.ShapeDtypeStruct((depth, MOD_GROUPS, n), F32),
        compiler_params=_params(2),
        name="modulation",
    )(cvec, w_mod, b_mod.reshape(depth, 1, n))


def _mod_spec(layer, which, group_of_step):
    return pl.BlockSpec((1, 1, 1, D_MODEL), lambda i: (layer, group_of_step(i), 0, which))


def _adaln(x, gain, shift, scale):
    ms = jnp.mean(x * x, axis=-1, keepdims=True)
    return (x * lax.rsqrt(ms + RMS_EPS)) * gain * (1.0 + scale) + shift


def _rope(y, cos, sin_lo, sin_hi):
    outs = []
    for c in range(y.shape[1] // LANES):
        yc = y[:, c * LANES:(c + 1) * LANES]
        outs.append(yc * cos
                    + pltpu.roll(yc, LANES - 16, 1) * sin_lo
                    + pltpu.roll(yc, 16, 1) * sin_hi)
    return outs[0] if len(outs) == 1 else jnp.concatenate(outs, axis=1)


def _head_rms_norm(y, gain):
    first = lax.broadcasted_iota(jnp.int32, (1, LANES), 1) < HEAD_DIM
    outs = []
    for c in range(y.shape[1] // LANES):
        yc = y[:, c * LANES:(c + 1) * LANES]
        sq = yc * yc
        s0 = jnp.sum(jnp.where(first, sq, 0.0), axis=-1, keepdims=True)
        s1 = jnp.sum(jnp.where(first, 0.0, sq), axis=-1, keepdims=True)
        ms = jnp.where(first, s0, s1) * (1.0 / HEAD_DIM)
        outs.append(yc * lax.rsqrt(ms + RMS_EPS) * gain)
    return outs[0] if len(outs) == 1 else jnp.concatenate(outs, axis=1)


_NT = (((1,), (1,)), ((), ()))


def _with_ones_rows(vt):
    return jnp.concatenate([vt.astype(BF16), jnp.ones((ONES_ROWS, vt.shape[1]), BF16)], axis=0)


def _scores(k, q):
    return lax.dot_general(k, q, _NT, preferred_element_type=F32)


def _pipelined_item(cur, nxt, n_keys, *, q_next, keys_next, vt_cur, bias_next=None,
                    cols_of=None, sink_cur=None, sink_next=None):
    s_cur, m_cur = cur
    s_nxt, m_nxt = nxt
    width = s_nxt.shape[1]
    if cols_of is None:
        cols_of = lambda c0: (0, width)
    m = None if s_cur is None else m_cur[...]
    m_next = [None] * (width // Q_TILE)
    ot = [None] * (width // Q_TILE)
    for c0 in range(0, n_keys, KEY_CHUNK):
        keys = slice(c0, c0 + KEY_CHUNK)
        lo, hi = cols_of(c0)
        blocks = [(g, slice(g * Q_TILE - lo, (g + 1) * Q_TILE - lo))
                  for g in range(lo // Q_TILE, hi // Q_TILE)]
        if q_next is not None:
            s = _scores(keys_next(c0), q_next[lo:hi])
            bias = None if bias_next is None else bias_next(c0)
            if bias is not None:
                s = s + bias
            s_nxt[keys, lo:hi] = s
            m_c = jnp.max(s, axis=0, keepdims=True)
            for g, cols in blocks:
                m_next[g] = (m_c[:, cols] if m_next[g] is None
                             else jnp.maximum(m_next[g], m_c[:, cols]))
        if s_cur is not None:
            p = jnp.exp2(s_cur[keys, lo:hi] - m[:, lo:hi]).astype(BF16)
            part = jnp.dot(vt_cur(c0), p, preferred_element_type=F32)
            for g, cols in blocks:
                ot[g] = part[:, cols] if ot[g] is None else ot[g] + part[:, cols]
    if q_next is not None:
        m_next = jnp.concatenate(m_next, axis=1)
        if sink_next is not None:
            m_next = jnp.maximum(m_next, sink_next)
        m_nxt[...] = m_next
    if s_cur is None:
        return None
    ot = jnp.concatenate(ot, axis=1)
    denom = ot[HEAD_DIM:HEAD_DIM + 1, :]
    if sink_cur is not None:
        denom = denom + jnp.exp2(sink_cur - m)
    return ot[:HEAD_DIM, :] / denom


def _score_slots(n_keys, m):
    return [pltpu.VMEM((n_keys, m), F32), pltpu.VMEM((n_keys, m), F32),
            pltpu.VMEM((2, 1, m), F32)]


def _stack_group(ref, rows, col0, group):
    return jnp.concatenate(
        [ref[rows, col0 + g * HEAD_DIM: col0 + (g + 1) * HEAD_DIM] for g in range(group)], axis=0)


def _sink_row(sink_ref, h0, group, rows):
    return jnp.concatenate(
        [jnp.full((1, rows), sink_ref[h0 + g] * LOG2E, F32) for g in range(group)], axis=1)


def _store_feature_major(ref, y):
    nb, heads, _, seq = ref.shape
    yt = y.T
    for b in range(nb):
        for h in range(heads):
            ref[b, h] = yt[h * HEAD_DIM:(h + 1) * HEAD_DIM, b * seq:(b + 1) * seq].astype(ref.dtype)


def _pre_even_kernel(x_ref, g_ref, sh_ref, sc_ref, w_ref, *rest, is_lat):
    if is_lat:
        cos_ref, slo_ref, shi_ref, qk_ref, va_ref, vb_ref = rest
    else:
        qk_ref, ka_ref, va_ref, kb_ref, vb_ref = rest
    h = _adaln(x_ref[...], g_ref[...], sh_ref[0, 0], sc_ref[0, 0]).astype(BF16)

    def proj(c0, c1):
        return jnp.dot(h, w_ref[:, c0:c1], preferred_element_type=F32)

    na = A_HEADS * HEAD_DIM
    nb = B_HEADS * HEAD_DIM
    nkv = B_KV_HEADS * HEAD_DIM
    qk_ref[:, 0:na] = (proj(0, na) * QK_SCALE).astype(BF16)
    ka = proj(na, 2 * na)
    qk_ref[:, na:2 * na] = ka.astype(BF16)
    va = proj(2 * na, 3 * na)
    qb = proj(3 * na, 3 * na + nb)
    if is_lat:
        qb = _rope(qb, cos_ref[...], slo_ref[...], shi_ref[...])
    qk_ref[:, 2 * na:2 * na + nb] = (qb * QK_SCALE).astype(BF16)
    kvb = proj(3 * na + nb, 3 * na + nb + 2 * nkv)
    kb, vb = kvb[:, :nkv], kvb[:, nkv:]
    if is_lat:
        kb_out = _rope(kb, cos_ref[...], slo_ref[...], shi_ref[...])
    else:
        kb_out = kb
    qk_ref[:, 2 * na + nb:2 * na + nb + nkv] = kb_out.astype(BF16)
    _store_feature_major(va_ref, va)
    _store_feature_major(vb_ref, vb)
    if not is_lat:
        _store_feature_major(ka_ref, ka)
        _store_feature_major(kb_ref, kb)


def _pre_odd_kernel(x_ref, g_ref, sh_ref, sc_ref, w_ref, qn_ref, kn_ref, *rest, is_lat):
    if is_lat:
        cos_ref, slo_ref, shi_ref, qk_ref, vc_ref = rest
    else:
        qk_ref, kc_ref, vc_ref = rest
    h = _adaln(x_ref[...], g_ref[...], sh_ref[0, 0], sc_ref[0, 0]).astype(BF16)

    def proj(c0, c1):
        return jnp.dot(h, w_ref[:, c0:c1], preferred_element_type=F32)

    nq = C_HEADS * HEAD_DIM
    nkv = C_KV_HEADS * HEAD_DIM
    q = _head_rms_norm(proj(0, nq), qn_ref[...])
    k = _head_rms_norm(proj(nq, nq + nkv), kn_ref[...])
    v = proj(nq + nkv, nq + 2 * nkv)
    if is_lat:
        q = _rope(q, cos_ref[...], slo_ref[...], shi_ref[...])
        k_out = _rope(k, cos_ref[...], slo_ref[...], shi_ref[...])
    else:
        k_out = k
        _store_feature_major(kc_ref, k)
    _store_feature_major(vc_ref, v)
    qk_ref[:, 0:nq] = (q * QK_SCALE).astype(BF16)
    qk_ref[:, nq:nq + nkv] = k_out.astype(BF16)


def _pre_attention(x, mods, layer, gain, w, sub, *, odd, is_lat, seq, rope=None,
                   head_gains=None):
    t, d = x.shape
    tm = TOKEN_TILE
    per_seq = max(seq // tm, 1)
    per_tile = max(tm // seq, 1)
    group = (lambda i: 1 + i // per_seq) if is_lat else (lambda i: 0)
    row = lambda i: (i, 0)
    in_specs = [
        pl.BlockSpec((tm, d), row),
        _resident((1, d)),
        _mod_spec(layer, 0, group),
        _mod_spec(layer, 1, group),
        _resident_layer(w, sub),
    ]
    args = [x, gain.reshape(1, d), mods, mods, w]
    if odd:
        in_specs += [_resident(a.shape) for a in head_gains]
        args += list(head_gains)
    if is_lat:
        in_specs += [pl.BlockSpec((tm, LANES), lambda i: (i % per_seq, 0))] * 3
        args += list(rope)
    if odd:
        n_qk = (C_HEADS + C_KV_HEADS) * HEAD_DIM
        lat_heads, ctx_heads = [C_KV_HEADS], [C_KV_HEADS, C_KV_HEADS]
    else:
        n_qk = (2 * A_HEADS + B_HEADS + B_KV_HEADS) * HEAD_DIM
        lat_heads, ctx_heads = [A_HEADS, B_KV_HEADS], [A_HEADS, A_HEADS, B_KV_HEADS, B_KV_HEADS]
    out_specs = [pl.BlockSpec((tm, n_qk), row)]
    out_shape = [jax.ShapeDtypeStruct((t, n_qk), BF16)]
    for heads in (lat_heads if is_lat else ctx_heads):
        blk = (per_tile, heads, HEAD_DIM, min(tm, seq))
        out_specs.append(pl.BlockSpec(blk, lambda i: (i // per_seq, 0, 0, i % per_seq)))
        out_shape.append(jax.ShapeDtypeStruct((t // seq, heads, HEAD_DIM, seq),
                                              BF16 if is_lat else F32))
    body = _pre_odd_kernel if odd else _pre_even_kernel
    return pl.pallas_call(
        functools.partial(body, is_lat=is_lat),
        grid=(t // tm,),
        in_specs=in_specs,
        out_specs=out_specs,
        out_shape=out_shape,
        compiler_params=_params(1),
        name=f"pre_{'odd' if odd else 'even'}_{'lat' if is_lat else 'ctx'}",
    )(*args)


def _batch_spec(arr, nb=1):
    nd = arr.ndim
    return pl.BlockSpec((nb,) + arr.shape[1:], lambda b: (b,) + (0,) * (nd - 1))


def _cache_spec(cache, layer):
    blk = (1, 1) + cache.shape[2:]
    return pl.BlockSpec(blk, lambda b: (b, layer, 0, 0, 0))


def _feature_major(cache):
    return cache.transpose(0, 1, 3, 4, 2)


def _cache_keys(ref, h):
    return ref[0, 0, h].T.astype(BF16)


def _ctx_even_attn_kernel(sink_ref, qk_ref, vat_ref, vbt_ref, o_ref,
                          sa0_ref, sa1_ref, ma_ref, sb0_ref, sb1_ref, mb_ref):
    nb = vat_ref.shape[0]
    n = qk_ref.shape[0] // nb
    na = A_HEADS * HEAD_DIM
    group = B_HEADS // B_KV_HEADS
    qb0 = 2 * na
    kb0 = qb0 + B_HEADS * HEAD_DIM

    def rows_of(b, c0=0, size=n):
        if isinstance(b, int):
            return slice(b * n + c0, b * n + c0 + size)
        return pl.ds(pl.multiple_of(b * n + c0, size), size)

    def following(b, i, count):
        if i + 1 < count:
            return b, i + 1
        return (None, 0) if isinstance(b, int) else (b + 1, 0)

    slots_a = ((sa0_ref, ma_ref.at[0]), (sa1_ref, ma_ref.at[1]))

    def item_a(cur, nxt, b, h, b_next, h_next):
        c = h_next * HEAD_DIM
        return _pipelined_item(
            cur, nxt, n,
            q_next=None if b_next is None else qk_ref[rows_of(b_next), c:c + HEAD_DIM],
            keys_next=lambda c0: qk_ref[rows_of(b_next, c0, KEY_CHUNK), na + c:na + c + HEAD_DIM],
            vt_cur=lambda c0: _with_ones_rows(vat_ref[b, h, :, c0:c0 + KEY_CHUNK]))

    item_a((None, None), slots_a[0], None, None, 0, 0)

    def sequence_a(b, carry=0):
        for pair in range(A_HEADS // 2):
            outs = []
            for h in (2 * pair, 2 * pair + 1):
                ot = item_a(slots_a[h % 2], slots_a[(h + 1) % 2], b, h, *following(b, h, A_HEADS))
                outs.append(ot.T)
            o_ref[rows_of(b), pair * LANES:(pair + 1) * LANES] = (
                jnp.concatenate(outs, axis=1).astype(BF16))
        return carry

    lax.fori_loop(0, nb - 1, sequence_a, 0)
    sequence_a(nb - 1)

    slots_b = ((sb0_ref, mb_ref.at[0]), (sb1_ref, mb_ref.at[1]))
    sinks = [_sink_row(sink_ref, kv * group, group, n) for kv in range(B_KV_HEADS)]

    def item_b(cur, nxt, b, kv, b_next, kv_next):
        kcols = slice(kb0 + kv_next * HEAD_DIM, kb0 + (kv_next + 1) * HEAD_DIM)
        ot = _pipelined_item(
            cur, nxt, n,
            q_next=None if b_next is None else _stack_group(
                qk_ref, rows_of(b_next), qb0 + kv_next * group * HEAD_DIM, group),
            keys_next=lambda c0: qk_ref[rows_of(b_next, c0, KEY_CHUNK), kcols],
            vt_cur=lambda c0: _with_ones_rows(vbt_ref[b, kv, :, c0:c0 + KEY_CHUNK]),
            sink_cur=None if kv is None else sinks[kv], sink_next=sinks[kv_next])
        if ot is None:
            return None
        return jnp.concatenate([ot[:, g * n:(g + 1) * n].T for g in range(group)], axis=1)

    item_b((None, None), slots_b[0], None, None, 0, 0)

    def sequence_b(b, carry=0):
        for kv in range(B_KV_HEADS):
            o = item_b(slots_b[kv % 2], slots_b[(kv + 1) % 2], b, kv,
                       *following(b, kv, B_KV_HEADS))
            c0 = na + kv * group * HEAD_DIM
            o_ref[rows_of(b), c0:c0 + group * HEAD_DIM] = o.astype(BF16)
        return carry

    lax.fori_loop(0, nb - 1, sequence_b, 0)
    sequence_b(nb - 1)


def _ctx_even_attention(qk, vat, vbt, sink, seq):
    t, n = qk.shape
    width = (A_HEADS + B_HEADS) * HEAD_DIM
    nb = CTX_BATCHES
    return pl.pallas_call(
        _ctx_even_attn_kernel,
        grid=(t // (nb * seq),),
        in_specs=[pl.BlockSpec(memory_space=pltpu.SMEM),
                  pl.BlockSpec((nb * seq, n), lambda b: (b, 0)),
                  _batch_spec(vat, nb), _batch_spec(vbt, nb)],
        out_specs=pl.BlockSpec((nb * seq, width), lambda b: (b, 0)),
        out_shape=jax.ShapeDtypeStruct((t, width), BF16),
        scratch_shapes=(_score_slots(seq, seq)
                        + _score_slots(seq, B_HEADS // B_KV_HEADS * seq)),
        compiler_params=_params(1),
        name="attn_even_ctx",
    )(sink, qk, vat, vbt)


def _na_bias_tiles(src_ref, h):
    kc = lax.broadcasted_iota(jnp.int32, (GRID_W, LANES), 0)
    qc = lax.rem(lax.broadcasted_iota(jnp.int32, (GRID_W, LANES), 1), GRID_W)
    cs = jnp.clip(qc - NA_WIN_W // 2, 0, GRID_W - NA_WIN_W)
    col_ok = (kc >= cs) & (kc < cs + NA_WIN_W)
    tiles = []
    for i in range(2 * NA_WIN_H):
        src = jnp.broadcast_to(src_ref[h, i:i + 1, :], (GRID_W, LANES))
        tiles.append(jnp.where(col_ok, pltpu.roll(src, 0, 1, stride=1, stride_axis=0), NEG_INF))
    return tiles


def _na_row_window(qr, grid_rows):
    wh = min(NA_WIN_H, grid_rows)
    start = min(max(qr - wh // 2, 0), grid_rows - wh)
    return start, start + wh


def _na_reachable_query_rows(key_rows, grid_rows):
    per_block = Q_TILE // GRID_W
    hit = [qr for qr in range(grid_rows)
           if any(_na_row_window(qr, grid_rows)[0] <= kr < _na_row_window(qr, grid_rows)[1]
                  for kr in key_rows)]
    return min(hit) // per_block * per_block, (max(hit) // per_block + 1) * per_block


def _na_bias_block(tiles, key_rows, q_rows, grid_rows):
    first_row = lax.broadcasted_iota(jnp.int32, (GRID_W, LANES), 1) < GRID_W

    def in_window(kr, qr):
        start, stop = _na_row_window(qr, grid_rows)
        return start <= kr < stop

    strips = []
    for kr in key_rows:
        parts = []
        for qr in range(q_rows.start, q_rows.stop, 2):
            ok0 = in_window(kr, qr)
            ok1 = in_window(kr, qr + 1)
            i = kr - qr + NA_WIN_H - 1
            if ok0 and ok1:
                parts.append(tiles[i])
            elif ok0:
                parts.append(jnp.where(first_row, tiles[i], NEG_INF))
            elif ok1:
                parts.append(jnp.where(first_row, NEG_INF, tiles[i]))
            else:
                parts.append(jnp.full((GRID_W, LANES), NEG_INF, F32))
        strips.append(jnp.concatenate(parts, axis=1))
    return jnp.concatenate(strips, axis=0)


def _lat_a_kernel(src_ref, q_ref, k_ref, vt_ref, ck_ref, cv_ref, o_ref,
                  qs_ref, kall_ref, vtall_ref, os_ref, s0_ref, s1_ref, m_ref):
    n = q_ref.shape[0]
    past = ck_ref.shape[4]
    grid_rows = n // GRID_W
    for h in range(A_HEADS):
        cols = slice(h * HEAD_DIM, (h + 1) * HEAD_DIM)
        qs_ref[h] = q_ref[:, cols]
        kall_ref[h, 0:n, :] = k_ref[:, cols]
        kall_ref[h, n:n + past, :] = _cache_keys(ck_ref, h)
        vtall_ref[h, 0:HEAD_DIM, 0:n] = vt_ref[0, h]
        vtall_ref[h, 0:HEAD_DIM, n:n + past] = cv_ref[0, 0, h].astype(BF16)
        vtall_ref[h, HEAD_DIM:, :] = jnp.ones((ONES_ROWS, n + past), BF16)

    slots = ((s0_ref, m_ref.at[0]), (s1_ref, m_ref.at[1]))

    def key_rows(c0):
        return range(c0 // GRID_W, (c0 + KEY_CHUNK) // GRID_W)

    def query_rows(c0):
        if c0 >= n:
            return range(0, grid_rows)
        return range(*_na_reachable_query_rows(key_rows(c0), grid_rows))

    def step(cur, nxt, h, h_next):
        tiles = None if h_next is None else _na_bias_tiles(src_ref, h_next)

        def bias(c0):
            if c0 >= n:
                return None
            return _na_bias_block(tiles, key_rows(c0), query_rows(c0), grid_rows)

        return _pipelined_item(
            cur, nxt, n + past, q_next=None if h_next is None else qs_ref[h_next],
            keys_next=lambda c0: kall_ref[h_next, c0:c0 + KEY_CHUNK, :],
            vt_cur=lambda c0: vtall_ref[h, :, c0:c0 + KEY_CHUNK], bias_next=bias,
            cols_of=lambda c0: (query_rows(c0).start * GRID_W, query_rows(c0).stop * GRID_W))

    step((None, None), slots[0], None, 0)

    def head_pair(p, carry=0):
        h0 = 2 * p
        following = None if isinstance(p, int) else h0 + 2
        os_ref[h0] = step(slots[0], slots[1], h0, h0 + 1).T.astype(BF16)
        os_ref[h0 + 1] = step(slots[1], slots[0], h0 + 1, following).T.astype(BF16)
        return carry

    lax.fori_loop(0, A_HEADS // 2 - 1, head_pair, 0)
    head_pair(A_HEADS // 2 - 1)
    for pair in range(A_HEADS // 2):
        o_ref[:, pair * LANES:(pair + 1) * LANES] = jnp.concatenate(
            [os_ref[2 * pair], os_ref[2 * pair + 1]], axis=1)


def _lat_a_attention(qk, vt, cache_k, cache_v, e, bias_src, seq):
    t = qk.shape[0]
    past = cache_k.shape[4]
    na = A_HEADS * HEAD_DIM
    return pl.pallas_call(
        _lat_a_kernel,
        grid=(t // seq,),
        in_specs=[
            _resident(bias_src.shape),
            pl.BlockSpec((seq, na), lambda b: (b, 0)),
            pl.BlockSpec((seq, na), lambda b: (b, 1)),
            _batch_spec(vt),
            _cache_spec(cache_k, e),
            _cache_spec(cache_v, e),
        ],
        out_specs=pl.BlockSpec((seq, na), lambda b: (b, 0)),
        out_shape=jax.ShapeDtypeStruct((t, na), BF16),
        scratch_shapes=[pltpu.VMEM((A_HEADS, seq, HEAD_DIM), BF16),
                        pltpu.VMEM((A_HEADS, seq + past, HEAD_DIM), BF16),
                        pltpu.VMEM((A_HEADS, HEAD_DIM + ONES_ROWS, seq + past), BF16),
                        pltpu.VMEM((A_HEADS, seq, HEAD_DIM), BF16)]
                       + _score_slots(seq + past, seq),
        compiler_params=_params(1),
        name="attn_even_lat_a",
    )(bias_src, qk, qk, vt, cache_k, cache_v)


def _band_window_start(j, n):
    return min(max(Q_TILE * j - B_WINDOW, 0), n - 2 * Q_TILE)


def _band_bias(n, group):
    j = np.arange(n // Q_TILE)[:, None, None]
    lo = np.clip(Q_TILE * j - B_WINDOW, 0, n - 2 * Q_TILE)
    kpos = lo + np.arange(2 * Q_TILE)[None, :, None]
    qpos = Q_TILE * j + (np.arange(group * Q_TILE) % Q_TILE)[None, None, :]
    return jnp.asarray(np.where(np.abs(qpos - kpos) <= B_WINDOW, 0.0, NEG_INF), F32)


def _lat_b_kernel(sink_ref, band_ref, q_ref, k_ref, vt_ref, ck_ref, cv_ref, o_ref,
                  ckeys_ref, cvt_ref, vtw_ref, s0_ref, s1_ref, m_ref):
    n = q_ref.shape[0]
    past = ck_ref.shape[4]
    group = B_HEADS // B_KV_HEADS
    win = 2 * Q_TILE
    n_blocks = n // Q_TILE
    for kv in range(B_KV_HEADS):
        ckeys_ref[kv] = _cache_keys(ck_ref, kv)
        cvt_ref[kv] = _with_ones_rows(cv_ref[0, 0, kv])
        for j in range(n_blocks):
            lo = _band_window_start(j, n)
            vtw_ref[j, kv] = _with_ones_rows(vt_ref[0, kv, :, lo:lo + win])
    sinks = [_sink_row(sink_ref, kv * group, group, Q_TILE) for kv in range(B_KV_HEADS)]

    def rows_of(j):
        if isinstance(j, int):
            return slice(j * Q_TILE, (j + 1) * Q_TILE)
        return pl.ds(pl.multiple_of(j * Q_TILE, Q_TILE), Q_TILE)

    def window_rows(j, c0):
        if isinstance(j, int):
            lo = _band_window_start(j, n) + c0
            return slice(lo, lo + KEY_CHUNK)
        lo = jnp.clip(Q_TILE * j - B_WINDOW, 0, n - win) + c0
        return pl.ds(pl.multiple_of(lo, B_WINDOW), KEY_CHUNK)

    slots = ((s0_ref, m_ref.at[0]), (s1_ref, m_ref.at[1]))

    def item(cur, nxt, j, kv, j_next, kv_next):
        kcols = slice(kv_next * HEAD_DIM, (kv_next + 1) * HEAD_DIM)

        def keys_next(c0):
            if c0 < win:
                return k_ref[window_rows(j_next, c0), kcols]
            return ckeys_ref[kv_next, c0 - win:c0 - win + KEY_CHUNK, :]

        def vt_cur(c0):
            if c0 < win:
                return vtw_ref[j, kv, :, c0:c0 + KEY_CHUNK]
            return cvt_ref[kv, :, c0 - win:c0 - win + KEY_CHUNK]

        ot = _pipelined_item(
            cur, nxt, win + past,
            q_next=None if j_next is None else _stack_group(
                q_ref, rows_of(j_next), kv_next * group * HEAD_DIM, group),
            keys_next=keys_next, vt_cur=vt_cur,
            bias_next=lambda c0: band_ref[j_next, c0:c0 + KEY_CHUNK, :] if c0 < win else None,
            sink_cur=None if kv is None else sinks[kv], sink_next=sinks[kv_next])
        if ot is None:
            return None
        return jnp.concatenate(
            [ot[:, g * Q_TILE:(g + 1) * Q_TILE].T for g in range(group)], axis=1)

    item((None, None), slots[0], None, None, 0, 0)

    def q_block(j, carry=0):
        final = isinstance(j, int)
        for kv in range(B_KV_HEADS):
            last = kv + 1 == B_KV_HEADS
            o = item(slots[kv % 2], slots[(kv + 1) % 2], j, kv,
                     (None if final else j + 1) if last else j, 0 if last else kv + 1)
            c0 = kv * group * HEAD_DIM
            o_ref[rows_of(j), c0:c0 + group * HEAD_DIM] = o.astype(BF16)
        return carry

    lax.fori_loop(0, n_blocks - 1, q_block, 0)
    q_block(n_blocks - 1)


def _lat_b_attention(qk, vt, cache_k, cache_v, e, sink, seq):
    t = qk.shape[0]
    past = cache_k.shape[4]
    na = A_HEADS * HEAD_DIM
    nb = B_HEADS * HEAD_DIM
    nkv = B_KV_HEADS * HEAD_DIM
    group = B_HEADS // B_KV_HEADS
    band = _band_bias(seq, group)
    return pl.pallas_call(
        _lat_b_kernel,
        grid=(t // seq,),
        in_specs=[
            pl.BlockSpec(memory_space=pltpu.SMEM),
            _resident(band.shape),
            pl.BlockSpec((seq, nb), lambda b: (b, 2 * na // nb)),
            pl.BlockSpec((seq, nkv), lambda b: (b, (2 * na + nb) // nkv)),
            _batch_spec(vt),
            _cache_spec(cache_k, e),
            _cache_spec(cache_v, e),
        ],
        out_specs=pl.BlockSpec((seq, nb), lambda b: (b, 0)),
        out_shape=jax.ShapeDtypeStruct((t, nb), BF16),
        scratch_shapes=[pltpu.VMEM((B_KV_HEADS, past, HEAD_DIM), BF16),
                        pltpu.VMEM((B_KV_HEADS, HEAD_DIM + ONES_ROWS, past), BF16),
                        pltpu.VMEM((seq // Q_TILE, B_KV_HEADS, HEAD_DIM + ONES_ROWS, 2 * Q_TILE),
                                   BF16)]
                       + _score_slots(2 * Q_TILE + past, group * Q_TILE),
        compiler_params=_params(1),
        name="attn_even_lat_b",
    )(sink, band, qk, qk, vt, cache_k, cache_v)


def _ctx_odd_attn_kernel(qk_ref, vt_ref, o_ref, s0_ref, s1_ref, m_ref):
    nb = vt_ref.shape[0]
    n = qk_ref.shape[0] // nb
    group = C_HEADS // C_KV_HEADS
    nq = C_HEADS * HEAD_DIM

    def rows_of(b, c0=0, size=n):
        if isinstance(b, int):
            return slice(b * n + c0, b * n + c0 + size)
        return pl.ds(pl.multiple_of(b * n + c0, size), size)

    slots = ((s0_ref, m_ref.at[0]), (s1_ref, m_ref.at[1]))

    def item(cur, nxt, b, kv, b_next, kv_next):
        kcols = slice(nq + kv_next * HEAD_DIM, nq + (kv_next + 1) * HEAD_DIM)
        ot = _pipelined_item(
            cur, nxt, n,
            q_next=None if b_next is None else _stack_group(
                qk_ref, rows_of(b_next), kv_next * group * HEAD_DIM, group),
            keys_next=lambda c0: qk_ref[rows_of(b_next, c0, KEY_CHUNK), kcols],
            vt_cur=lambda c0: _with_ones_rows(vt_ref[b, kv, :, c0:c0 + KEY_CHUNK]))
        if ot is None:
            return None
        return jnp.concatenate([ot[:, g * n:(g + 1) * n].T for g in range(group)], axis=1)

    item((None, None), slots[0], None, None, 0, 0)

    def sequence(b, carry=0):
        final = isinstance(b, int)
        for kv in range(C_KV_HEADS):
            last = kv + 1 == C_KV_HEADS
            o = item(slots[kv % 2], slots[(kv + 1) % 2], b, kv,
                     (None if final else b + 1) if last else b, 0 if last else kv + 1)
            c0 = kv * group * HEAD_DIM
            o_ref[rows_of(b), c0:c0 + group * HEAD_DIM] = o.astype(BF16)
        return carry

    lax.fori_loop(0, nb - 1, sequence, 0)
    sequence(nb - 1)


def _ctx_odd_attention(qk, vt, seq):
    t, n = qk.shape
    width = C_HEADS * HEAD_DIM
    nb = CTX_BATCHES
    return pl.pallas_call(
        _ctx_odd_attn_kernel,
        grid=(t // (nb * seq),),
        in_specs=[pl.BlockSpec((nb * seq, n), lambda b: (b, 0)), _batch_spec(vt, nb)],
        out_specs=pl.BlockSpec((nb * seq, width), lambda b: (b, 0)),
        out_shape=jax.ShapeDtypeStruct((t, width), BF16),
        scratch_shapes=_score_slots(seq, C_HEADS // C_KV_HEADS * seq),
        compiler_params=_params(1),
        name="attn_odd_ctx",
    )(qk, vt)


def _lat_c_kernel(qk_ref, vt_ref, ck_ref, cv_ref, o_ref, kall_ref, vtall_ref, s0_ref, s1_ref,
                  m_ref):
    n = qk_ref.shape[0]
    past = ck_ref.shape[4]
    group = C_HEADS // C_KV_HEADS
    nq = C_HEADS * HEAD_DIM
    n_blocks = n // Q_TILE
    for kv in range(C_KV_HEADS):
        kall_ref[kv, 0:past, :] = _cache_keys(ck_ref, kv)
        kall_ref[kv, past:past + n, :] = qk_ref[:, nq + kv * HEAD_DIM:nq + (kv + 1) * HEAD_DIM]
        vtall_ref[kv, 0:HEAD_DIM, 0:past] = cv_ref[0, 0, kv].astype(BF16)
        vtall_ref[kv, 0:HEAD_DIM, past:past + n] = vt_ref[0, kv]
        vtall_ref[kv, HEAD_DIM:, :] = jnp.ones((ONES_ROWS, past + n), BF16)

    def rows_of(j):
        if isinstance(j, int):
            return slice(j * Q_TILE, (j + 1) * Q_TILE)
        return pl.ds(pl.multiple_of(j * Q_TILE, Q_TILE), Q_TILE)

    slots = ((s0_ref, m_ref.at[0]), (s1_ref, m_ref.at[1]))

    def item(cur, nxt, kv, j_next, kv_next):
        ot = _pipelined_item(
            cur, nxt, past + n,
            q_next=None if j_next is None else _stack_group(
                qk_ref, rows_of(j_next), kv_next * group * HEAD_DIM, group),
            keys_next=lambda c0: kall_ref[kv_next, c0:c0 + KEY_CHUNK, :],
            vt_cur=lambda c0: vtall_ref[kv, :, c0:c0 + KEY_CHUNK])
        if ot is None:
            return None
        return jnp.concatenate(
            [ot[:, g * Q_TILE:(g + 1) * Q_TILE].T for g in range(group)], axis=1)

    item((None, None), slots[0], None, 0, 0)

    def q_block(j, carry=0):
        final = isinstance(j, int)
        for kv in range(C_KV_HEADS):
            last = kv + 1 == C_KV_HEADS
            o = item(slots[kv % 2], slots[(kv + 1) % 2], kv,
                     (None if final else j + 1) if last else j, 0 if last else kv + 1)
            c0 = kv * group * HEAD_DIM
            o_ref[rows_of(j), c0:c0 + group * HEAD_DIM] = o.astype(BF16)
        return carry

    lax.fori_loop(0, n_blocks - 1, q_block, 0)
    q_block(n_blocks - 1)


def _lat_c_attention(qk, vt, cache_k, cache_v, o, seq):
    t, n = qk.shape
    past = cache_k.shape[4]
    width = C_HEADS * HEAD_DIM
    return pl.pallas_call(
        _lat_c_kernel,
        grid=(t // seq,),
        in_specs=[
            pl.BlockSpec((seq, n), lambda b: (b, 0)),
            _batch_spec(vt),
            _cache_spec(cache_k, o),
            _cache_spec(cache_v, o),
        ],
        out_specs=pl.BlockSpec((seq, width), lambda b: (b, 0)),
        out_shape=jax.ShapeDtypeStruct((t, width), BF16),
        scratch_shapes=[pltpu.VMEM((C_KV_HEADS, past + seq, HEAD_DIM), BF16),
                        pltpu.VMEM((C_KV_HEADS, HEAD_DIM + ONES_ROWS, past + seq), BF16)]
                       + _score_slots(past + seq, C_HEADS // C_KV_HEADS * Q_TILE),
        compiler_params=_params(1),
        name="attn_odd_lat",
    )(qk, vt, cache_k, cache_v)


def _post_kernel(*refs, n_parts, final):
    o_refs = refs[:n_parts]
    (x_ref, wo_ref, g1_ref, sh_ref, sc_ref, g2_ref, gain_ref, wgu_ref, wd_ref) = refs[n_parts:n_parts + 9]
    rest = refs[n_parts + 9:]
    if final:
        fg_ref, out_ref, act_ref = rest
    else:
        out_ref, act_ref = rest
    mix = None
    r0 = 0
    for o_ref in o_refs:
        kk = o_ref.shape[1]
        part = jnp.dot(o_ref[...], wo_ref[r0:r0 + kk, :], preferred_element_type=F32)
        mix = part if mix is None else mix + part
        r0 += kk
    x1 = x_ref[...] + g1_ref[0, 0] * mix
    h = _adaln(x1, gain_ref[...], sh_ref[0, 0], sc_ref[0, 0]).astype(BF16)
    d_ff = wd_ref.shape[0]
    for j in range(d_ff // FF_CHUNK):
        c0 = j * FF_CHUNK
        gate = jnp.dot(h, wgu_ref[:, c0:c0 + FF_CHUNK], preferred_element_type=F32)
        up = jnp.dot(h, wgu_ref[:, d_ff + c0:d_ff + c0 + FF_CHUNK], preferred_element_type=F32)
        act_ref[:, c0:c0 + FF_CHUNK] = (gate * jax.nn.sigmoid(gate) * up).astype(BF16)
    ffn = jnp.dot(act_ref[...], wd_ref[...], preferred_element_type=F32)
    x2 = x1 + g2_ref[0, 0] * ffn
    if final:
        ms = jnp.mean(x2 * x2, axis=-1, keepdims=True)
        x2 = (x2 * lax.rsqrt(ms + RMS_EPS)) * fg_ref[...]
    out_ref[...] = x2


def _post_attention(o_parts, x, mods, layer, gain, w_out, sub, w_gu, w_down, *, is_lat, seq,
                    final_gain=None):
    t, d = x.shape
    tm = TOKEN_TILE
    per_seq = max(seq // tm, 1)
    group = (lambda i: 1 + i // per_seq) if is_lat else (lambda i: 0)
    row = lambda i: (i, 0)
    d_ff = w_down.shape[1]
    final = final_gain is not None
    in_specs = [pl.BlockSpec((tm, o.shape[1]), row) for o in o_parts]
    in_specs += [
        pl.BlockSpec((tm, d), row),
        _resident_layer(w_out, sub),
        _mod_spec(layer, 2, group),
        _mod_spec(layer, 3, group),
        _mod_spec(layer, 4, group),
        _mod_spec(layer, 5, group),
        _resident((1, d)),
        _resident_layer(w_gu, layer),
        _resident_layer(w_down, layer),
    ]
    args = list(o_parts) + [x, w_out, mods, mods, mods, mods, gain.reshape(1, d), w_gu, w_down]
    if final:
        in_specs.append(_resident((1, d)))
        args.append(final_gain.reshape(1, d))
    return pl.pallas_call(
        functools.partial(_post_kernel, n_parts=len(o_parts), final=final),
        grid=(t // tm,),
        in_specs=in_specs,
        out_specs=pl.BlockSpec((tm, d), row),
        out_shape=jax.ShapeDtypeStruct((t, d), F32),
        scratch_shapes=[pltpu.VMEM((tm, d_ff), BF16)],
        compiler_params=_params(1),
        name=f"post_{'lat' if is_lat else 'ctx'}{'_final' if final else ''}",
    )(*args)


def _rope_tables(n):
    t = np.arange(n)
    row = (t // GRID_W).astype(np.float32)
    col = (t % GRID_W).astype(np.float32)
    half = HEAD_DIM // 2
    inv_freq = np.float32(ROPE_THETA) ** (-np.arange(0, half, 2, dtype=np.float32) / np.float32(half))
    lane = np.arange(LANES)
    in_head = lane % HEAD_DIM
    pos = np.where((in_head < half)[None, :], row[:, None], col[:, None])
    ang = (pos * inv_freq[in_head % (half // 2)][None, :]).astype(np.float32)
    first = ((in_head % half) < half // 2)[None, :]
    cos = np.cos(ang)
    sin = np.sin(ang)
    zero = np.float32(0.0)
    return tuple(jnp.asarray(a, F32) for a in
                 (cos, np.where(first, -sin, zero), np.where(first, zero, sin)))


def _na_bias_sources(rpb):
    h, _, nb = rpb.shape
    w = NA_WIN_W - 1
    rp = jnp.pad(rpb[:, :, ::-1] * LOG2E, ((0, 0), (1, 1), (0, 0)))
    this, prev = rp[:, 1:], rp[:, :-1]
    z = jnp.zeros((h, 2 * NA_WIN_H, LANES // 2 - nb), F32)
    return jnp.concatenate([this[:, :, w:], z, prev, z, this[:, :, :w]], axis=-1)


def _state(y):
    return y.transpose(0, 3, 1, 2)


def kernel(x_prompt, x_sample, cache_a_k, cache_a_v, cache_b_k, cache_b_v, cache_c_k, cache_c_v,
           c, c_ctx, norm_gain, w_mod, b_mod, w_in_even, w_out_even, rpb_a, sink_b,
           w_in_odd, w_out_odd, q_norm_c, k_norm_c, w_gate_up, w_down, final_gain):
    batch, seq, d = x_prompt.shape
    dec_batch, dec_seq, _ = x_sample.shape
    depth = w_mod.shape[0]

    cvec = jnp.concatenate(
        [c_ctx[None, :], c, jnp.zeros((MOD_GROUPS - 1 - dec_batch, d), F32)], axis=0)
    mods = _modulation(cvec, w_mod, b_mod).reshape(depth, MOD_GROUPS, 1, 6 * d)
    rope = _rope_tables(dec_seq)
    w_in = {False: w_in_even.astype(BF16), True: w_in_odd.astype(BF16)}
    w_out = {False: w_out_even.astype(BF16), True: w_out_odd.astype(BF16)}
    w_gu = w_gate_up.astype(BF16)
    w_dn = w_down.astype(BF16)

    ctx = x_prompt.reshape(batch * seq, d)
    lat = x_sample.reshape(dec_batch * dec_seq, d)
    states = {name: [] for name in ("a_k", "a_v", "b_k", "b_v", "c_k", "c_v")}

    for layer in range(depth):
        odd = layer % 2 == 1
        sub = layer // 2
        gain1, gain2 = norm_gain[layer, 0], norm_gain[layer, 1]
        pre = functools.partial(_pre_attention, mods=mods, layer=layer, gain=gain1, w=w_in[odd],
                                sub=sub, odd=odd)
        if not odd:
            qk_c, ka, va, kb, vb = pre(ctx, is_lat=False, seq=seq)
            qk_l, vat_l, vbt_l = pre(lat, is_lat=True, seq=dec_seq, rope=rope)
            for name, y in (("a_k", ka), ("a_v", va), ("b_k", kb), ("b_v", vb)):
                states[name].append(_state(y))
            o_ctx = [_ctx_even_attention(qk_c, va, vb, sink_b[sub], seq)]
            o_lat = [
                _lat_a_attention(qk_l, vat_l, _feature_major(cache_a_k), _feature_major(cache_a_v),
                                 sub, _na_bias_sources(rpb_a[sub]), dec_seq),
                _lat_b_attention(qk_l, vbt_l, _feature_major(cache_b_k), _feature_major(cache_b_v),
                                 sub, sink_b[sub], dec_seq),
            ]
        else:
            per = LANES // HEAD_DIM
            head_gains = (jnp.tile(q_norm_c[sub], per).reshape(1, LANES),
                          jnp.tile(k_norm_c[sub], per).reshape(1, LANES))
            qk_c, kc, vc = pre(ctx, is_lat=False, seq=seq, head_gains=head_gains)
            qk_l, vct_l = pre(lat, is_lat=True, seq=dec_seq, rope=rope, head_gains=head_gains)
            states["c_k"].append(_state(kc))
            states["c_v"].append(_state(vc))
            o_ctx = [_ctx_odd_attention(qk_c, vc, seq)]
            o_lat = [_lat_c_attention(qk_l, vct_l, _feature_major(cache_c_k),
                                      _feature_major(cache_c_v), sub, dec_seq)]
        fg = final_gain if layer == depth - 1 else None
        post = functools.partial(_post_attention, mods=mods, layer=layer, gain=gain2,
                                 w_out=w_out[odd], sub=sub, w_gu=w_gu, w_down=w_dn, final_gain=fg)
        ctx = post(o_ctx, ctx, is_lat=False, seq=seq)
        lat = post(o_lat, lat, is_lat=True, seq=dec_seq)

    return (ctx.reshape(batch, seq, d), lat.reshape(dec_batch, dec_seq, d),
            jnp.stack(states["a_k"], axis=1), jnp.stack(states["a_v"], axis=1),
            jnp.stack(states["b_k"], axis=1), jnp.stack(states["b_v"], axis=1),
            jnp.stack(states["c_k"], axis=1), jnp.stack(states["c_v"], axis=1))
```

```python
import functools
import math

import jax
import jax.numpy as jnp
import numpy as np
from jax import lax
from jax.experimental import pallas as pl
from jax.experimental.pallas import tpu as pltpu

F32 = jnp.float32
BF16 = jnp.bfloat16

D_MODEL = 1024
GRID_W = 64
HEAD_DIM = 64
A_HEADS = 8
B_HEADS = 8
B_KV_HEADS = 2
C_HEADS = 16
C_KV_HEADS = 4
NA_WIN_H = 8
NA_WIN_W = 16
B_WINDOW = 128
ROPE_THETA = 10000.0
RMS_EPS = 1e-6
NEG_INF = -1e30
LOG2E = math.log2(math.e)
QK_SCALE = LOG2E / math.sqrt(HEAD_DIM)

LANES = 128
TOKEN_TILE = 512
Q_TILE = 256
CTX_BATCHES = 8
ONES_ROWS = 16
KEY_CHUNK = 256
FF_CHUNK = 256
MOD_GROUPS = 16
VMEM_LIMIT = 56 * 1024 * 1024


def _params(n_axes, vmem=VMEM_LIMIT):
    return pltpu.CompilerParams(
        dimension_semantics=("arbitrary",) * n_axes, vmem_limit_bytes=vmem)


def _resident(shape):
    nd = len(shape)
    return pl.BlockSpec(shape, lambda *_: (0,) * nd, pipeline_mode=pl.Buffered(1))


def _resident_layer(stacked, layer):
    return pl.BlockSpec((None,) + stacked.shape[1:], lambda *_: (layer, 0, 0),
                        pipeline_mode=pl.Buffered(1))


def _mod_kernel(c_ref, w_ref, b_ref, o_ref):
    c = c_ref[...]
    s = (c * jax.nn.sigmoid(c)).astype(BF16)
    o_ref[0] = jnp.dot(s, w_ref[0].astype(BF16), preferred_element_type=F32) + b_ref[0]


def _modulation(cvec, w_mod, b_mod):
    depth, d, n = w_mod.shape
    tn = 1536
    return pl.pallas_call(
        _mod_kernel,
        grid=(depth, n // tn),
        in_specs=[
            pl.BlockSpec((MOD_GROUPS, d), lambda l, j: (0, 0)),
            pl.BlockSpec((1, d, tn), lambda l, j: (l, 0, j)),
            pl.BlockSpec((1, 1, tn), lambda l, j: (l, 0, j)),
        ],
        out_specs=pl.BlockSpec((1, MOD_GROUPS, tn), lambda l, j: (l, 0, j)),
        out_shape=jax.ShapeDtypeStruct((depth, MOD_GROUPS, n), F32),
        compiler_params=_params(2),
        name="modulation",
    )(cvec, w_mod, b_mod.reshape(depth, 1, n))


def _mod_spec(layer, which, group_of_step):
    return pl.BlockSpec((1, 1, 1, D_MODEL), lambda i: (layer, group_of_step(i), 0, which))


def _adaln(x, gain, shift, scale):
    ms = jnp.mean(x * x, axis=-1, keepdims=True)
    return (x * lax.rsqrt(ms + RMS_EPS)) * gain * (1.0 + scale) + shift


def _rope(y, cos, sin_lo, sin_hi):
    outs = []
    for c in range(y.shape[1] // LANES):
        yc = y[:, c * LANES:(c + 1) * LANES]
        outs.append(yc * cos
                    + pltpu.roll(yc, LANES - 16, 1) * sin_lo
                    + pltpu.roll(yc, 16, 1) * sin_hi)
    return outs[0] if len(outs) == 1 else jnp.concatenate(outs, axis=1)


def _head_rms_norm(y, gain):
    first = lax.broadcasted_iota(jnp.int32, (1, LANES), 1) < HEAD_DIM
    outs = []
    for c in range(y.shape[1] // LANES):
        yc = y[:, c * LANES:(c + 1) * LANES]
        sq = yc * yc
        s0 = jnp.sum(jnp.where(first, sq, 0.0), axis=-1, keepdims=True)
        s1 = jnp.sum(jnp.where(first, 0.0, sq), axis=-1, keepdims=True)
        ms = jnp.where(first, s0, s1) * (1.0 / HEAD_DIM)
        outs.append(yc * lax.rsqrt(ms + RMS_EPS) * gain)
    return outs[0] if len(outs) == 1 else jnp.concatenate(outs, axis=1)


_NT = (((1,), (1,)), ((), ()))


def _with_ones_rows(vt):
    return jnp.concatenate([vt.astype(BF16), jnp.ones((ONES_ROWS, vt.shape[1]), BF16)], axis=0)


def _scores(k, q):
    return lax.dot_general(k, q, _NT, preferred_element_type=F32)


def _pipelined_item(cur, nxt, n_keys, *, q_next, keys_next, vt_cur, bias_next=None,
                    cols_of=None, sink_cur=None, sink_next=None):
    s_cur, m_cur = cur
    s_nxt, m_nxt = nxt
    width = s_nxt.shape[1]
    if cols_of is None:
        cols_of = lambda c0: (0, width)
    m = None if s_cur is None else m_cur[...]
    m_next = [None] * (width // Q_TILE)
    ot = [None] * (width // Q_TILE)
    for c0 in range(0, n_keys, KEY_CHUNK):
        keys = slice(c0, c0 + KEY_CHUNK)
        lo, hi = cols_of(c0)
        blocks = [(g, slice(g * Q_TILE - lo, (g + 1) * Q_TILE - lo))
                  for g in range(lo // Q_TILE, hi // Q_TILE)]
        if q_next is not None:
            s = _scores(keys_next(c0), q_next[lo:hi])
            bias = None if bias_next is None else bias_next(c0)
            if bias is not None:
                s = s + bias
            s_nxt[keys, lo:hi] = s
            m_c = jnp.max(s, axis=0, keepdims=True)
            for g, cols in blocks:
                m_next[g] = (m_c[:, cols] if m_next[g] is None
                             else jnp.maximum(m_next[g], m_c[:, cols]))
        if s_cur is not None:
            p = jnp.exp2(s_cur[keys, lo:hi] - m[:, lo:hi]).astype(BF16)
            part = jnp.dot(vt_cur(c0), p, preferred_element_type=F32)
            for g, cols in blocks:
                ot[g] = part[:, cols] if ot[g] is None else ot[g] + part[:, cols]
    if q_next is not None:
        m_next = jnp.concatenate(m_next, axis=1)
        if sink_next is not None:
            m_next = jnp.maximum(m_next, sink_next)
        m_nxt[...] = m_next
    if s_cur is None:
        return None
    ot = jnp.concatenate(ot, axis=1)
    denom = ot[HEAD_DIM:HEAD_DIM + 1, :]
    if sink_cur is not None:
        denom = denom + jnp.exp2(sink_cur - m)
    return ot[:HEAD_DIM, :] / denom


def _score_slots(n_keys, m):
    return [pltpu.VMEM((n_keys, m), F32), pltpu.VMEM((n_keys, m), F32),
            pltpu.VMEM((2, 1, m), F32)]


def _stack_group(ref, rows, col0, group):
    return jnp.concatenate(
        [ref[rows, col0 + g * HEAD_DIM: col0 + (g + 1) * HEAD_DIM] for g in range(group)], axis=0)


def _sink_row(sink_ref, h0, group, rows):
    return jnp.concatenate(
        [jnp.full((1, rows), sink_ref[h0 + g] * LOG2E, F32) for g in range(group)], axis=1)


def _store_feature_major(ref, y):
    nb, heads, _, seq = ref.shape
    yt = y.T
    for b in range(nb):
        for h in range(heads):
            ref[b, h] = yt[h * HEAD_DIM:(h + 1) * HEAD_DIM, b * seq:(b + 1) * seq].astype(ref.dtype)


def _pre_even_kernel(x_ref, g_ref, sh_ref, sc_ref, w_ref, *rest, is_lat):
    if is_lat:
        cos_ref, slo_ref, shi_ref, qk_ref, va_ref, vb_ref = rest
    else:
        qk_ref, ka_ref, va_ref, kb_ref, vb_ref = rest
    h = _adaln(x_ref[...], g_ref[...], sh_ref[0, 0], sc_ref[0, 0]).astype(BF16)

    def proj(c0, c1):
        return jnp.dot(h, w_ref[:, c0:c1], preferred_element_type=F32)

    na = A_HEADS * HEAD_DIM
    nb = B_HEADS * HEAD_DIM
    nkv = B_KV_HEADS * HEAD_DIM
    qk_ref[:, 0:na] = (proj(0, na) * QK_SCALE).astype(BF16)
    ka = proj(na, 2 * na)
    qk_ref[:, na:2 * na] = ka.astype(BF16)
    va = proj(2 * na, 3 * na)
    qb = proj(3 * na, 3 * na + nb)
    if is_lat:
        qb = _rope(qb, cos_ref[...], slo_ref[...], shi_ref[...])
    qk_ref[:, 2 * na:2 * na + nb] = (qb * QK_SCALE).astype(BF16)
    kvb = proj(3 * na + nb, 3 * na + nb + 2 * nkv)
    kb, vb = kvb[:, :nkv], kvb[:, nkv:]
    if is_lat:
        kb_out = _rope(kb, cos_ref[...], slo_ref[...], shi_ref[...])
    else:
        kb_out = kb
    qk_ref[:, 2 * na + nb:2 * na + nb + nkv] = kb_out.astype(BF16)
    _store_feature_major(va_ref, va)
    _store_feature_major(vb_ref, vb)
    if not is_lat:
        _store_feature_major(ka_ref, ka)
        _store_feature_major(kb_ref, kb)


def _pre_odd_kernel(x_ref, g_ref, sh_ref, sc_ref, w_ref, qn_ref, kn_ref, *rest, is_lat):
    if is_lat:
        cos_ref, slo_ref, shi_ref, qk_ref, vc_ref, ya_ref, yb_ref = rest
    else:
        qk_ref, kc_ref, vc_ref, ya_ref, yb_ref = rest
    nq = C_HEADS * HEAD_DIM
    nkv = C_KV_HEADS * HEAD_DIM
    width = 2 * LANES
    assert nkv == width
    i = pl.program_id(0)

    @pl.when(i == 0)
    def _():
        yb_ref[...] = jnp.zeros_like(yb_ref)

    def step(y_w, y_r):
        h = _adaln(x_ref[...], g_ref[...], sh_ref[0, 0], sc_ref[0, 0]).astype(BF16)
        for c0 in range(0, nq + 2 * nkv, width):
            cols = slice(c0, c0 + width)
            y = y_r[:, cols]
            if c0 < nq:
                y = _head_rms_norm(y, qn_ref[...])
                if is_lat:
                    y = _rope(y, cos_ref[...], slo_ref[...], shi_ref[...])
                qk_ref[:, cols] = (y * QK_SCALE).astype(BF16)
            elif c0 < nq + nkv:
                k = _head_rms_norm(y, kn_ref[...])
                if is_lat:
                    k_out = _rope(k, cos_ref[...], slo_ref[...], shi_ref[...])
                else:
                    k_out = k
                    _store_feature_major(kc_ref, k)
                qk_ref[:, cols] = k_out.astype(BF16)
            else:
                _store_feature_major(vc_ref, y)
            y_w[:, cols] = jnp.dot(h, w_ref[:, cols], preferred_element_type=F32)

    @pl.when(lax.rem(i, 2) == 0)
    def _():
        step(ya_ref, yb_ref)

    @pl.when(lax.rem(i, 2) == 1)
    def _():
        step(yb_ref, ya_ref)


def _pre_attention(x, mods, layer, gain, w, sub, *, odd, is_lat, seq, rope=None,
                   head_gains=None):
    t, d = x.shape
    tm = TOKEN_TILE
    per_seq = max(seq // tm, 1)
    per_tile = max(tm // seq, 1)
    n_tiles = t // tm
    if odd:
        cur = lambda i: jnp.minimum(i, n_tiles - 1)
        done = lambda i: jnp.maximum(i - 1, 0)
    else:
        cur = done = lambda i: i
    group = (lambda i: 1 + cur(i) // per_seq) if is_lat else (lambda i: 0)
    row = lambda i: (done(i), 0)
    in_specs = [
        pl.BlockSpec((tm, d), lambda i: (cur(i), 0)),
        _resident((1, d)),
        _mod_spec(layer, 0, group),
        _mod_spec(layer, 1, group),
        _resident_layer(w, sub),
    ]
    args = [x, gain.reshape(1, d), mods, mods, w]
    if odd:
        in_specs += [_resident(a.shape) for a in head_gains]
        args += list(head_gains)
    if is_lat:
        in_specs += [pl.BlockSpec((tm, LANES), lambda i: (done(i) % per_seq, 0))] * 3
        args += list(rope)
    if odd:
        n_qk = (C_HEADS + C_KV_HEADS) * HEAD_DIM
        lat_heads, ctx_heads = [C_KV_HEADS], [C_KV_HEADS, C_KV_HEADS]
    else:
        n_qk = (2 * A_HEADS + B_HEADS + B_KV_HEADS) * HEAD_DIM
        lat_heads, ctx_heads = [A_HEADS, B_KV_HEADS], [A_HEADS, A_HEADS, B_KV_HEADS, B_KV_HEADS]
    out_specs = [pl.BlockSpec((tm, n_qk), row)]
    out_shape = [jax.ShapeDtypeStruct((t, n_qk), BF16)]
    for heads in (lat_heads if is_lat else ctx_heads):
        blk = (per_tile, heads, HEAD_DIM, min(tm, seq))
        out_specs.append(pl.BlockSpec(
            blk, lambda i: (done(i) // per_seq, 0, 0, done(i) % per_seq)))
        out_shape.append(jax.ShapeDtypeStruct((t // seq, heads, HEAD_DIM, seq),
                                              BF16 if is_lat else F32))
    body = _pre_odd_kernel if odd else _pre_even_kernel
    return pl.pallas_call(
        functools.partial(body, is_lat=is_lat),
        grid=(n_tiles + 1 if odd else n_tiles,),
        in_specs=in_specs,
        out_specs=out_specs,
        out_shape=out_shape,
        scratch_shapes=[pltpu.VMEM((tm, w.shape[2]), F32)] * 2 if odd else [],
        compiler_params=_params(1),
        name=f"pre_{'odd' if odd else 'even'}_{'lat' if is_lat else 'ctx'}",
    )(*args)


def _batch_spec(arr, nb=1):
    nd = arr.ndim
    return pl.BlockSpec((nb,) + arr.shape[1:], lambda b: (b,) + (0,) * (nd - 1))


def _cache_spec(cache, layer):
    blk = (1, 1) + cache.shape[2:]
    return pl.BlockSpec(blk, lambda b: (b, layer, 0, 0, 0))


def _feature_major(cache):
    return cache.transpose(0, 1, 3, 4, 2)


def _cache_keys(ref, h):
    return ref[0, 0, h].T.astype(BF16)


def _ctx_even_attn_kernel(sink_ref, qk_ref, vat_ref, vbt_ref, o_ref,
                          sa0_ref, sa1_ref, ma_ref, sb0_ref, sb1_ref, mb_ref):
    nb = vat_ref.shape[0]
    n = qk_ref.shape[0] // nb
    na = A_HEADS * HEAD_DIM
    group = B_HEADS // B_KV_HEADS
    qb0 = 2 * na
    kb0 = qb0 + B_HEADS * HEAD_DIM

    def rows_of(b, c0=0, size=n):
        if isinstance(b, int):
            return slice(b * n + c0, b * n + c0 + size)
        return pl.ds(pl.multiple_of(b * n + c0, size), size)

    def following(b, i, count):
        if i + 1 < count:
            return b, i + 1
        return (None, 0) if isinstance(b, int) else (b + 1, 0)

    slots_a = ((sa0_ref, ma_ref.at[0]), (sa1_ref, ma_ref.at[1]))

    def item_a(cur, nxt, b, h, b_next, h_next):
        c = h_next * HEAD_DIM
        return _pipelined_item(
            cur, nxt, n,
            q_next=None if b_next is None else qk_ref[rows_of(b_next), c:c + HEAD_DIM],
            keys_next=lambda c0: qk_ref[rows_of(b_next, c0, KEY_CHUNK), na + c:na + c + HEAD_DIM],
            vt_cur=lambda c0: _with_ones_rows(vat_ref[b, h, :, c0:c0 + KEY_CHUNK]))

    item_a((None, None), slots_a[0], None, None, 0, 0)

    def sequence_a(b, carry=0):
        for pair in range(A_HEADS // 2):
            outs = []
            for h in (2 * pair, 2 * pair + 1):
                ot = item_a(slots_a[h % 2], slots_a[(h + 1) % 2], b, h, *following(b, h, A_HEADS))
                outs.append(ot.T)
            o_ref[rows_of(b), pair * LANES:(pair + 1) * LANES] = (
                jnp.concatenate(outs, axis=1).astype(BF16))
        return carry

    lax.fori_loop(0, nb - 1, sequence_a, 0)
    sequence_a(nb - 1)

    slots_b = ((sb0_ref, mb_ref.at[0]), (sb1_ref, mb_ref.at[1]))
    sinks = [_sink_row(sink_ref, kv * group, group, n) for kv in range(B_KV_HEADS)]

    def item_b(cur, nxt, b, kv, b_next, kv_next):
        kcols = slice(kb0 + kv_next * HEAD_DIM, kb0 + (kv_next + 1) * HEAD_DIM)
        ot = _pipelined_item(
            cur, nxt, n,
            q_next=None if b_next is None else _stack_group(
                qk_ref, rows_of(b_next), qb0 + kv_next * group * HEAD_DIM, group),
            keys_next=lambda c0: qk_ref[rows_of(b_next, c0, KEY_CHUNK), kcols],
            vt_cur=lambda c0: _with_ones_rows(vbt_ref[b, kv, :, c0:c0 + KEY_CHUNK]),
            sink_cur=None if kv is None else sinks[kv], sink_next=sinks[kv_next])
        if ot is None:
            return None
        return jnp.concatenate([ot[:, g * n:(g + 1) * n].T for g in range(group)], axis=1)

    item_b((None, None), slots_b[0], None, None, 0, 0)

    def sequence_b(b, carry=0):
        for kv in range(B_KV_HEADS):
            o = item_b(slots_b[kv % 2], slots_b[(kv + 1) % 2], b, kv,
                       *following(b, kv, B_KV_HEADS))
            c0 = na + kv * group * HEAD_DIM
            o_ref[rows_of(b), c0:c0 + group * HEAD_DIM] = o.astype(BF16)
        return carry

    lax.fori_loop(0, nb - 1, sequence_b, 0)
    sequence_b(nb - 1)


def _ctx_even_attention(qk, vat, vbt, sink, seq):
    t, n = qk.shape
    width = (A_HEADS + B_HEADS) * HEAD_DIM
    nb = CTX_BATCHES
    return pl.pallas_call(
        _ctx_even_attn_kernel,
        grid=(t // (nb * seq),),
        in_specs=[pl.BlockSpec(memory_space=pltpu.SMEM),
                  pl.BlockSpec((nb * seq, n), lambda b: (b, 0)),
                  _batch_spec(vat, nb), _batch_spec(vbt, nb)],
        out_specs=pl.BlockSpec((nb * seq, width), lambda b: (b, 0)),
        out_shape=jax.ShapeDtypeStruct((t, width), BF16),
        scratch_shapes=(_score_slots(seq, seq)
                        + _score_slots(seq, B_HEADS // B_KV_HEADS * seq)),
        compiler_params=_params(1),
        name="attn_even_ctx",
    )(sink, qk, vat, vbt)


def _na_bias_tiles(src_ref, h):
    kc = lax.broadcasted_iota(jnp.int32, (GRID_W, LANES), 0)
    qc = lax.rem(lax.broadcasted_iota(jnp.int32, (GRID_W, LANES), 1), GRID_W)
    cs = jnp.clip(qc - NA_WIN_W // 2, 0, GRID_W - NA_WIN_W)
    col_ok = (kc >= cs) & (kc < cs + NA_WIN_W)
    tiles = []
    for i in range(2 * NA_WIN_H):
        src = jnp.broadcast_to(src_ref[h, i:i + 1, :], (GRID_W, LANES))
        tiles.append(jnp.where(col_ok, pltpu.roll(src, 0, 1, stride=1, stride_axis=0), NEG_INF))
    return tiles


def _na_row_window(qr, grid_rows):
    wh = min(NA_WIN_H, grid_rows)
    start = min(max(qr - wh // 2, 0), grid_rows - wh)
    return start, start + wh


def _na_reachable_query_rows(key_rows, grid_rows):
    per_block = Q_TILE // GRID_W
    hit = [qr for qr in range(grid_rows)
           if any(_na_row_window(qr, grid_rows)[0] <= kr < _na_row_window(qr, grid_rows)[1]
                  for kr in key_rows)]
    return min(hit) // per_block * per_block, (max(hit) // per_block + 1) * per_block


def _na_bias_block(tiles, key_rows, q_rows, grid_rows):
    first_row = lax.broadcasted_iota(jnp.int32, (GRID_W, LANES), 1) < GRID_W

    def in_window(kr, qr):
        start, stop = _na_row_window(qr, grid_rows)
        return start <= kr < stop

    strips = []
    for kr in key_rows:
        parts = []
        for qr in range(q_rows.start, q_rows.stop, 2):
            ok0 = in_window(kr, qr)
            ok1 = in_window(kr, qr + 1)
            i = kr - qr + NA_WIN_H - 1
            if ok0 and ok1:
                parts.append(tiles[i])
            elif ok0:
                parts.append(jnp.where(first_row, tiles[i], NEG_INF))
            elif ok1:
                parts.append(jnp.where(first_row, NEG_INF, tiles[i]))
            else:
                parts.append(jnp.full((GRID_W, LANES), NEG_INF, F32))
        strips.append(jnp.concatenate(parts, axis=1))
    return jnp.concatenate(strips, axis=0)


def _lat_a_kernel(src_ref, q_ref, k_ref, vt_ref, ck_ref, cv_ref, o_ref,
                  qs_ref, kall_ref, vtall_ref, os_ref, s0_ref, s1_ref, m_ref):
    n = q_ref.shape[0]
    past = ck_ref.shape[4]
    grid_rows = n // GRID_W
    for h in range(A_HEADS):
        cols = slice(h * HEAD_DIM, (h + 1) * HEAD_DIM)
        qs_ref[h] = q_ref[:, cols]
        kall_ref[h, 0:n, :] = k_ref[:, cols]
        kall_ref[h, n:n + past, :] = _cache_keys(ck_ref, h)
        vtall_ref[h, 0:HEAD_DIM, 0:n] = vt_ref[0, h]
        vtall_ref[h, 0:HEAD_DIM, n:n + past] = cv_ref[0, 0, h].astype(BF16)
        vtall_ref[h, HEAD_DIM:, :] = jnp.ones((ONES_ROWS, n + past), BF16)

    slots = ((s0_ref, m_ref.at[0]), (s1_ref, m_ref.at[1]))

    def key_rows(c0):
        return range(c0 // GRID_W, (c0 + KEY_CHUNK) // GRID_W)

    def query_rows(c0):
        if c0 >= n:
            return range(0, grid_rows)
        return range(*_na_reachable_query_rows(key_rows(c0), grid_rows))

    def step(cur, nxt, h, h_next):
        tiles = None if h_next is None else _na_bias_tiles(src_ref, h_next)

        def bias(c0):
            if c0 >= n:
                return None
            return _na_bias_block(tiles, key_rows(c0), query_rows(c0), grid_rows)

        return _pipelined_item(
            cur, nxt, n + past, q_next=None if h_next is None else qs_ref[h_next],
            keys_next=lambda c0: kall_ref[h_next, c0:c0 + KEY_CHUNK, :],
            vt_cur=lambda c0: vtall_ref[h, :, c0:c0 + KEY_CHUNK], bias_next=bias,
            cols_of=lambda c0: (query_rows(c0).start * GRID_W, query_rows(c0).stop * GRID_W))

    step((None, None), slots[0], None, 0)

    def head_pair(p, carry=0):
        h0 = 2 * p
        following = None if isinstance(p, int) else h0 + 2
        os_ref[h0] = step(slots[0], slots[1], h0, h0 + 1).T.astype(BF16)
        os_ref[h0 + 1] = step(slots[1], slots[0], h0 + 1, following).T.astype(BF16)
        return carry

    lax.fori_loop(0, A_HEADS // 2 - 1, head_pair, 0)
    head_pair(A_HEADS // 2 - 1)
    for pair in range(A_HEADS // 2):
        o_ref[:, pair * LANES:(pair + 1) * LANES] = jnp.concatenate(
            [os_ref[2 * pair], os_ref[2 * pair + 1]], axis=1)


def _lat_a_attention(qk, vt, cache_k, cache_v, e, bias_src, seq):
    t = qk.shape[0]
    past = cache_k.shape[4]
    na = A_HEADS * HEAD_DIM
    return pl.pallas_call(
        _lat_a_kernel,
        grid=(t // seq,),
        in_specs=[
            _resident(bias_src.shape),
            pl.BlockSpec((seq, na), lambda b: (b, 0)),
            pl.BlockSpec((seq, na), lambda b: (b, 1)),
            _batch_spec(vt),
            _cache_spec(cache_k, e),
            _cache_spec(cache_v, e),
        ],
        out_specs=pl.BlockSpec((seq, na), lambda b: (b, 0)),
        out_shape=jax.ShapeDtypeStruct((t, na), BF16),
        scratch_shapes=[pltpu.VMEM((A_HEADS, seq, HEAD_DIM), BF16),
                        pltpu.VMEM((A_HEADS, seq + past, HEAD_DIM), BF16),
                        pltpu.VMEM((A_HEADS, HEAD_DIM + ONES_ROWS, seq + past), BF16),
                        pltpu.VMEM((A_HEADS, seq, HEAD_DIM), BF16)]
                       + _score_slots(seq + past, seq),
        compiler_params=_params(1),
        name="attn_even_lat_a",
    )(bias_src, qk, qk, vt, cache_k, cache_v)


def _band_window_start(j, n):
    return min(max(Q_TILE * j - B_WINDOW, 0), n - 2 * Q_TILE)


def _band_bias(n, group):
    j = np.arange(n // Q_TILE)[:, None, None]
    lo = np.clip(Q_TILE * j - B_WINDOW, 0, n - 2 * Q_TILE)
    kpos = lo + np.arange(2 * Q_TILE)[None, :, None]
    qpos = Q_TILE * j + (np.arange(group * Q_TILE) % Q_TILE)[None, None, :]
    return jnp.asarray(np.where(np.abs(qpos - kpos) <= B_WINDOW, 0.0, NEG_INF), F32)


def _lat_b_kernel(sink_ref, band_ref, q_ref, k_ref, vt_ref, ck_ref, cv_ref, o_ref,
                  ckeys_ref, cvt_ref, vtw_ref, s0_ref, s1_ref, m_ref):
    n = q_ref.shape[0]
    past = ck_ref.shape[4]
    group = B_HEADS // B_KV_HEADS
    win = 2 * Q_TILE
    n_blocks = n // Q_TILE
    for kv in range(B_KV_HEADS):
        ckeys_ref[kv] = _cache_keys(ck_ref, kv)
        cvt_ref[kv] = _with_ones_rows(cv_ref[0, 0, kv])
        for j in range(n_blocks):
            lo = _band_window_start(j, n)
            vtw_ref[j, kv] = _with_ones_rows(vt_ref[0, kv, :, lo:lo + win])
    sinks = [_sink_row(sink_ref, kv * group, group, Q_TILE) for kv in range(B_KV_HEADS)]

    def rows_of(j):
        if isinstance(j, int):
            return slice(j * Q_TILE, (j + 1) * Q_TILE)
        return pl.ds(pl.multiple_of(j * Q_TILE, Q_TILE), Q_TILE)

    def window_rows(j, c0):
        if isinstance(j, int):
            lo = _band_window_start(j, n) + c0
            return slice(lo, lo + KEY_CHUNK)
        lo = jnp.clip(Q_TILE * j - B_WINDOW, 0, n - win) + c0
        return pl.ds(pl.multiple_of(lo, B_WINDOW), KEY_CHUNK)

    slots = ((s0_ref, m_ref.at[0]), (s1_ref, m_ref.at[1]))

    def item(cur, nxt, j, kv, j_next, kv_next):
        kcols = slice(kv_next * HEAD_DIM, (kv_next + 1) * HEAD_DIM)

        def keys_next(c0):
            if c0 < win:
                return k_ref[window_rows(j_next, c0), kcols]
            return ckeys_ref[kv_next, c0 - win:c0 - win + KEY_CHUNK, :]

        def vt_cur(c0):
            if c0 < win:
                return vtw_ref[j, kv, :, c0:c0 + KEY_CHUNK]
            return cvt_ref[kv, :, c0 - win:c0 - win + KEY_CHUNK]

        ot = _pipelined_item(
            cur, nxt, win + past,
            q_next=None if j_next is None else _stack_group(
                q_ref, rows_of(j_next), kv_next * group * HEAD_DIM, group),
            keys_next=keys_next, vt_cur=vt_cur,
            bias_next=lambda c0: band_ref[j_next, c0:c0 + KEY_CHUNK, :] if c0 < win else None,
            sink_cur=None if kv is None else sinks[kv], sink_next=sinks[kv_next])
        if ot is None:
            return None
        return jnp.concatenate(
            [ot[:, g * Q_TILE:(g + 1) * Q_TILE].T for g in range(group)], axis=1)

    item((None, None), slots[0], None, None, 0, 0)

    def q_block(j, carry=0):
        final = isinstance(j, int)
        for kv in range(B_KV_HEADS):
            last = kv + 1 == B_KV_HEADS
            o = item(slots[kv % 2], slots[(kv + 1) % 2], j, kv,
                     (None if final else j + 1) if last else j, 0 if last else kv + 1)
            c0 = kv * group * HEAD_DIM
            o_ref[rows_of(j), c0:c0 + group * HEAD_DIM] = o.astype(BF16)
        return carry

    lax.fori_loop(0, n_blocks - 1, q_block, 0)
    q_block(n_blocks - 1)


def _lat_b_attention(qk, vt, cache_k, cache_v, e, sink, seq):
    t = qk.shape[0]
    past = cache_k.shape[4]
    na = A_HEADS * HEAD_DIM
    nb = B_HEADS * HEAD_DIM
    nkv = B_KV_HEADS * HEAD_DIM
    group = B_HEADS // B_KV_HEADS
    band = _band_bias(seq, group)
    return pl.pallas_call(
        _lat_b_kernel,
        grid=(t // seq,),
        in_specs=[
            pl.BlockSpec(memory_space=pltpu.SMEM),
            _resident(band.shape),
            pl.BlockSpec((seq, nb), lambda b: (b, 2 * na // nb)),
            pl.BlockSpec((seq, nkv), lambda b: (b, (2 * na + nb) // nkv)),
            _batch_spec(vt),
            _cache_spec(cache_k, e),
            _cache_spec(cache_v, e),
        ],
        out_specs=pl.BlockSpec((seq, nb), lambda b: (b, 0)),
        out_shape=jax.ShapeDtypeStruct((t, nb), BF16),
        scratch_shapes=[pltpu.VMEM((B_KV_HEADS, past, HEAD_DIM), BF16),
                        pltpu.VMEM((B_KV_HEADS, HEAD_DIM + ONES_ROWS, past), BF16),
                        pltpu.VMEM((seq // Q_TILE, B_KV_HEADS, HEAD_DIM + ONES_ROWS, 2 * Q_TILE),
                                   BF16)]
                       + _score_slots(2 * Q_TILE + past, group * Q_TILE),
        compiler_params=_params(1),
        name="attn_even_lat_b",
    )(sink, band, qk, qk, vt, cache_k, cache_v)


def _ctx_odd_attn_kernel(qk_ref, vt_ref, o_ref, s0_ref, s1_ref, m_ref):
    nb = vt_ref.shape[0]
    n = qk_ref.shape[0] // nb
    group = C_HEADS // C_KV_HEADS
    nq = C_HEADS * HEAD_DIM

    def rows_of(b, c0=0, size=n):
        if isinstance(b, int):
            return slice(b * n + c0, b * n + c0 + size)
        return pl.ds(pl.multiple_of(b * n + c0, size), size)

    slots = ((s0_ref, m_ref.at[0]), (s1_ref, m_ref.at[1]))

    def item(cur, nxt, b, kv, b_next, kv_next):
        kcols = slice(nq + kv_next * HEAD_DIM, nq + (kv_next + 1) * HEAD_DIM)
        ot = _pipelined_item(
            cur, nxt, n,
            q_next=None if b_next is None else _stack_group(
                qk_ref, rows_of(b_next), kv_next * group * HEAD_DIM, group),
            keys_next=lambda c0: qk_ref[rows_of(b_next, c0, KEY_CHUNK), kcols],
            vt_cur=lambda c0: _with_ones_rows(vt_ref[b, kv, :, c0:c0 + KEY_CHUNK]))
        if ot is None:
            return None
        return jnp.concatenate([ot[:, g * n:(g + 1) * n].T for g in range(group)], axis=1)

    item((None, None), slots[0], None, None, 0, 0)

    def sequence(b, carry=0):
        final = isinstance(b, int)
        for kv in range(C_KV_HEADS):
            last = kv + 1 == C_KV_HEADS
            o = item(slots[kv % 2], slots[(kv + 1) % 2], b, kv,
                     (None if final else b + 1) if last else b, 0 if last else kv + 1)
            c0 = kv * group * HEAD_DIM
            o_ref[rows_of(b), c0:c0 + group * HEAD_DIM] = o.astype(BF16)
        return carry

    lax.fori_loop(0, nb - 1, sequence, 0)
    sequence(nb - 1)


def _ctx_odd_attention(qk, vt, seq):
    t, n = qk.shape
    width = C_HEADS * HEAD_DIM
    nb = CTX_BATCHES
    return pl.pallas_call(
        _ctx_odd_attn_kernel,
        grid=(t // (nb * seq),),
        in_specs=[pl.BlockSpec((nb * seq, n), lambda b: (b, 0)), _batch_spec(vt, nb)],
        out_specs=pl.BlockSpec((nb * seq, width), lambda b: (b, 0)),
        out_shape=jax.ShapeDtypeStruct((t, width), BF16),
        scratch_shapes=_score_slots(seq, C_HEADS // C_KV_HEADS * seq),
        compiler_params=_params(1),
        name="attn_odd_ctx",
    )(qk, vt)


def _lat_c_kernel(qk_ref, vt_ref, ck_ref, cv_ref, o_ref, kall_ref, vtall_ref, s0_ref, s1_ref,
                  m_ref):
    n = qk_ref.shape[0]
    past = ck_ref.shape[4]
    group = C_HEADS // C_KV_HEADS
    nq = C_HEADS * HEAD_DIM
    n_blocks = n // Q_TILE
    for kv in range(C_KV_HEADS):
        kall_ref[kv, 0:past, :] = _cache_keys(ck_ref, kv)
        kall_ref[kv, past:past + n, :] = qk_ref[:, nq + kv * HEAD_DIM:nq + (kv + 1) * HEAD_DIM]
        vtall_ref[kv, 0:HEAD_DIM, 0:past] = cv_ref[0, 0, kv].astype(BF16)
        vtall_ref[kv, 0:HEAD_DIM, past:past + n] = vt_ref[0, kv]
        vtall_ref[kv, HEAD_DIM:, :] = jnp.ones((ONES_ROWS, past + n), BF16)

    def rows_of(j):
        if isinstance(j, int):
            return slice(j * Q_TILE, (j + 1) * Q_TILE)
        return pl.ds(pl.multiple_of(j * Q_TILE, Q_TILE), Q_TILE)

    slots = ((s0_ref, m_ref.at[0]), (s1_ref, m_ref.at[1]))

    def item(cur, nxt, kv, j_next, kv_next):
        ot = _pipelined_item(
            cur, nxt, past + n,
            q_next=None if j_next is None else _stack_group(
                qk_ref, rows_of(j_next), kv_next * group * HEAD_DIM, group),
            keys_next=lambda c0: kall_ref[kv_next, c0:c0 + KEY_CHUNK, :],
            vt_cur=lambda c0: vtall_ref[kv, :, c0:c0 + KEY_CHUNK])
        if ot is None:
            return None
        return jnp.concatenate(
            [ot[:, g * Q_TILE:(g + 1) * Q_TILE].T for g in range(group)], axis=1)

    item((None, None), slots[0], None, 0, 0)

    def q_block(j, carry=0):
        final = isinstance(j, int)
        for kv in range(C_KV_HEADS):
            last = kv + 1 == C_KV_HEADS
            o = item(slots[kv % 2], slots[(kv + 1) % 2], kv,
                     (None if final else j + 1) if last else j, 0 if last else kv + 1)
            c0 = kv * group * HEAD_DIM
            o_ref[rows_of(j), c0:c0 + group * HEAD_DIM] = o.astype(BF16)
        return carry

    lax.fori_loop(0, n_blocks - 1, q_block, 0)
    q_block(n_blocks - 1)


def _lat_c_attention(qk, vt, cache_k, cache_v, o, seq):
    t, n = qk.shape
    past = cache_k.shape[4]
    width = C_HEADS * HEAD_DIM
    return pl.pallas_call(
        _lat_c_kernel,
        grid=(t // seq,),
        in_specs=[
            pl.BlockSpec((seq, n), lambda b: (b, 0)),
            _batch_spec(vt),
            _cache_spec(cache_k, o),
            _cache_spec(cache_v, o),
        ],
        out_specs=pl.BlockSpec((seq, width), lambda b: (b, 0)),
        out_shape=jax.ShapeDtypeStruct((t, width), BF16),
        scratch_shapes=[pltpu.VMEM((C_KV_HEADS, past + seq, HEAD_DIM), BF16),
                        pltpu.VMEM((C_KV_HEADS, HEAD_DIM + ONES_ROWS, past + seq), BF16)]
                       + _score_slots(past + seq, C_HEADS // C_KV_HEADS * Q_TILE),
        compiler_params=_params(1),
        name="attn_odd_lat",
    )(qk, vt, cache_k, cache_v)


def _post_kernel(*refs, n_parts, final):
    o_refs = refs[:n_parts]
    (x_ref, wo_ref, g1_ref, sh_ref, sc_ref, g2_ref, gain_ref, wgu_ref, wd_ref) = refs[n_parts:n_parts + 9]
    rest = refs[n_parts + 9:]
    if final:
        fg_ref, out_ref, act_ref = rest
    else:
        out_ref, act_ref = rest
    mix = None
    r0 = 0
    for o_ref in o_refs:
        kk = o_ref.shape[1]
        part = jnp.dot(o_ref[...], wo_ref[r0:r0 + kk, :], preferred_element_type=F32)
        mix = part if mix is None else mix + part
        r0 += kk
    x1 = x_ref[...] + g1_ref[0, 0] * mix
    h = _adaln(x1, gain_ref[...], sh_ref[0, 0], sc_ref[0, 0]).astype(BF16)
    d_ff = wd_ref.shape[0]
    for j in range(d_ff // FF_CHUNK):
        c0 = j * FF_CHUNK
        gate = jnp.dot(h, wgu_ref[:, c0:c0 + FF_CHUNK], preferred_element_type=F32)
        up = jnp.dot(h, wgu_ref[:, d_ff + c0:d_ff + c0 + FF_CHUNK], preferred_element_type=F32)
        act_ref[:, c0:c0 + FF_CHUNK] = (gate * jax.nn.sigmoid(gate) * up).astype(BF16)
    ffn = jnp.dot(act_ref[...], wd_ref[...], preferred_element_type=F32)
    x2 = x1 + g2_ref[0, 0] * ffn
    if final:
        ms = jnp.mean(x2 * x2, axis=-1, keepdims=True)
        x2 = (x2 * lax.rsqrt(ms + RMS_EPS)) * fg_ref[...]
    out_ref[...] = x2


def _post_attention(o_parts, x, mods, layer, gain, w_out, sub, w_gu, w_down, *, is_lat, seq,
                    final_gain=None):
    t, d = x.shape
    tm = TOKEN_TILE
    per_seq = max(seq // tm, 1)
    group = (lambda i: 1 + i // per_seq) if is_lat else (lambda i: 0)
    row = lambda i: (i, 0)
    d_ff = w_down.shape[1]
    final = final_gain is not None
    in_specs = [pl.BlockSpec((tm, o.shape[1]), row) for o in o_parts]
    in_specs += [
        pl.BlockSpec((tm, d), row),
        _resident_layer(w_out, sub),
        _mod_spec(layer, 2, group),
        _mod_spec(layer, 3, group),
        _mod_spec(layer, 4, group),
        _mod_spec(layer, 5, group),
        _resident((1, d)),
        _resident_layer(w_gu, layer),
        _resident_layer(w_down, layer),
    ]
    args = list(o_parts) + [x, w_out, mods, mods, mods, mods, gain.reshape(1, d), w_gu, w_down]
    if final:
        in_specs.append(_resident((1, d)))
        args.append(final_gain.reshape(1, d))
    return pl.pallas_call(
        functools.partial(_post_kernel, n_parts=len(o_parts), final=final),
        grid=(t // tm,),
        in_specs=in_specs,
        out_specs=pl.BlockSpec((tm, d), row),
        out_shape=jax.ShapeDtypeStruct((t, d), F32),
        scratch_shapes=[pltpu.VMEM((tm, d_ff), BF16)],
        compiler_params=_params(1),
        name=f"post_{'lat' if is_lat else 'ctx'}{'_final' if final else ''}",
    )(*args)


def _rope_tables(n):
    t = np.arange(n)
    row = (t // GRID_W).astype(np.float32)
    col = (t % GRID_W).astype(np.float32)
    half = HEAD_DIM // 2
    inv_freq = np.float32(ROPE_THETA) ** (-np.arange(0, half, 2, dtype=np.float32) / np.float32(half))
    lane = np.arange(LANES)
    in_head = lane % HEAD_DIM
    pos = np.where((in_head < half)[None, :], row[:, None], col[:, None])
    ang = (pos * inv_freq[in_head % (half // 2)][None, :]).astype(np.float32)
    first = ((in_head % half) < half // 2)[None, :]
    cos = np.cos(ang)
    sin = np.sin(ang)
    zero = np.float32(0.0)
    return tuple(jnp.asarray(a, F32) for a in
                 (cos, np.where(first, -sin, zero), np.where(first, zero, sin)))


def _na_bias_sources(rpb):
    h, _, nb = rpb.shape
    w = NA_WIN_W - 1
    rp = jnp.pad(rpb[:, :, ::-1] * LOG2E, ((0, 0), (1, 1), (0, 0)))
    this, prev = rp[:, 1:], rp[:, :-1]
    z = jnp.zeros((h, 2 * NA_WIN_H, LANES // 2 - nb), F32)
    return jnp.concatenate([this[:, :, w:], z, prev, z, this[:, :, :w]], axis=-1)


def _state(y):
    return y.transpose(0, 3, 1, 2)


def kernel(x_prompt, x_sample, cache_a_k, cache_a_v, cache_b_k, cache_b_v, cache_c_k, cache_c_v,
           c, c_ctx, norm_gain, w_mod, b_mod, w_in_even, w_out_even, rpb_a, sink_b,
           w_in_odd, w_out_odd, q_norm_c, k_norm_c, w_gate_up, w_down, final_gain):
    batch, seq, d = x_prompt.shape
    dec_batch, dec_seq, _ = x_sample.shape
    depth = w_mod.shape[0]

    cvec = jnp.concatenate(
        [c_ctx[None, :], c, jnp.zeros((MOD_GROUPS - 1 - dec_batch, d), F32)], axis=0)
    mods = _modulation(cvec, w_mod, b_mod).reshape(depth, MOD_GROUPS, 1, 6 * d)
    rope = _rope_tables(dec_seq)
    w_in = {False: w_in_even.astype(BF16), True: w_in_odd.astype(BF16)}
    w_out = {False: w_out_even.astype(BF16), True: w_out_odd.astype(BF16)}
    w_gu = w_gate_up.astype(BF16)
    w_dn = w_down.astype(BF16)

    ctx = x_prompt.reshape(batch * seq, d)
    lat = x_sample.reshape(dec_batch * dec_seq, d)
    states = {name: [] for name in ("a_k", "a_v", "b_k", "b_v", "c_k", "c_v")}

    for layer in range(depth):
        odd = layer % 2 == 1
        sub = layer // 2
        gain1, gain2 = norm_gain[layer, 0], norm_gain[layer, 1]
        pre = functools.partial(_pre_attention, mods=mods, layer=layer, gain=gain1, w=w_in[odd],
                                sub=sub, odd=odd)
        if not odd:
            qk_c, ka, va, kb, vb = pre(ctx, is_lat=False, seq=seq)
            qk_l, vat_l, vbt_l = pre(lat, is_lat=True, seq=dec_seq, rope=rope)
            for name, y in (("a_k", ka), ("a_v", va), ("b_k", kb), ("b_v", vb)):
                states[name].append(_state(y))
            o_ctx = [_ctx_even_attention(qk_c, va, vb, sink_b[sub], seq)]
            o_lat = [
                _lat_a_attention(qk_l, vat_l, _feature_major(cache_a_k), _feature_major(cache_a_v),
                                 sub, _na_bias_sources(rpb_a[sub]), dec_seq),
                _lat_b_attention(qk_l, vbt_l, _feature_major(cache_b_k), _feature_major(cache_b_v),
                                 sub, sink_b[sub], dec_seq),
            ]
        else:
            per = LANES // HEAD_DIM
            head_gains = (jnp.tile(q_norm_c[sub], per).reshape(1, LANES),
                          jnp.tile(k_norm_c[sub], per).reshape(1, LANES))
            qk_c, kc, vc = pre(ctx, is_lat=False, seq=seq, head_gains=head_gains)
            qk_l, vct_l = pre(lat, is_lat=True, seq=dec_seq, rope=rope, head_gains=head_gains)
            states["c_k"].append(_state(kc))
            states["c_v"].append(_state(vc))
            o_ctx = [_ctx_odd_attention(qk_c, vc, seq)]
            o_lat = [_lat_c_attention(qk_l, vct_l, _feature_major(cache_c_k),
                                      _feature_major(cache_c_v), sub, dec_seq)]
        fg = final_gain if layer == depth - 1 else None
        post = functools.partial(_post_attention, mods=mods, layer=layer, gain=gain2,
                                 w_out=w_out[odd], sub=sub, w_gu=w_gu, w_down=w_dn, final_gain=fg)
        ctx = post(o_ctx, ctx, is_lat=False, seq=seq)
        lat = post(o_lat, lat, is_lat=True, seq=dec_seq)

    return (ctx.reshape(batch, seq, d), lat.reshape(dec_batch, dec_seq, d),
            jnp.stack(states["a_k"], axis=1), jnp.stack(states["a_v"], axis=1),
            jnp.stack(states["b_k"], axis=1), jnp.stack(states["b_v"], axis=1),
            jnp.stack(states["c_k"], axis=1), jnp.stack(states["c_v"], axis=1))
```

```python
import functools
import math

import jax
import jax.numpy as jnp
import numpy as np
from jax import lax
from jax.experimental import pallas as pl
from jax.experimental.pallas import tpu as pltpu

F32 = jnp.float32
BF16 = jnp.bfloat16

D_MODEL = 1024
GRID_W = 64
HEAD_DIM = 64
A_HEADS = 8
B_HEADS = 8
B_KV_HEADS = 2
C_HEADS = 16
C_KV_HEADS = 4
NA_WIN_H = 8
NA_WIN_W = 16
B_WINDOW = 128
ROPE_THETA = 10000.0
RMS_EPS = 1e-6
NEG_INF = -1e30
LOG2E = math.log2(math.e)
QK_SCALE = LOG2E / math.sqrt(HEAD_DIM)

LANES = 128
TOKEN_TILE = 512
Q_TILE = 256
CTX_BATCHES = 8
ONES_ROWS = 16
KEY_CHUNK = 256
FF_CHUNK = 256
MOD_GROUPS = 16
VMEM_LIMIT = 56 * 1024 * 1024


def _params(n_axes, vmem=VMEM_LIMIT):
    return pltpu.CompilerParams(
        dimension_semantics=("arbitrary",) * n_axes, vmem_limit_bytes=vmem)


def _resident(shape):
    nd = len(shape)
    return pl.BlockSpec(shape, lambda *_: (0,) * nd, pipeline_mode=pl.Buffered(1))


def _resident_layer(stacked, layer):
    return pl.BlockSpec((None,) + stacked.shape[1:], lambda *_: (layer, 0, 0),
                        pipeline_mode=pl.Buffered(1))


def _mod_kernel(c_ref, w_ref, b_ref, o_ref):
    c = c_ref[...]
    s = (c * jax.nn.sigmoid(c)).astype(BF16)
    o_ref[0] = jnp.dot(s, w_ref[0].astype(BF16), preferred_element_type=F32) + b_ref[0]


def _modulation(cvec, w_mod, b_mod):
    depth, d, n = w_mod.shape
    tn = 1536
    return pl.pallas_call(
        _mod_kernel,
        grid=(depth, n // tn),
        in_specs=[
            pl.BlockSpec((MOD_GROUPS, d), lambda l, j: (0, 0)),
            pl.BlockSpec((1, d, tn), lambda l, j: (l, 0, j)),
            pl.BlockSpec((1, 1, tn), lambda l, j: (l, 0, j)),
        ],
        out_specs=pl.BlockSpec((1, MOD_GROUPS, tn), lambda l, j: (l, 0, j)),
        out_shape=jax.ShapeDtypeStruct((depth, MOD_GROUPS, n), F32),
        compiler_params=_params(2),
        name="modulation",
    )(cvec, w_mod, b_mod.reshape(depth, 1, n))


def _mod_spec(layer, which, group_of_step):
    return pl.BlockSpec((1, 1, 1, D_MODEL), lambda i: (layer, group_of_step(i), 0, which))


def _adaln(x, gain, shift, scale):
    ms = jnp.mean(x * x, axis=-1, keepdims=True)
    return (x * lax.rsqrt(ms + RMS_EPS)) * gain * (1.0 + scale) + shift


def _rope(y, cos, sin_lo, sin_hi):
    outs = []
    for c in range(y.shape[1] // LANES):
        yc = y[:, c * LANES:(c + 1) * LANES]
        outs.append(yc * cos
                    + pltpu.roll(yc, LANES - 16, 1) * sin_lo
                    + pltpu.roll(yc, 16, 1) * sin_hi)
    return outs[0] if len(outs) == 1 else jnp.concatenate(outs, axis=1)


def _head_rms_norm(y, gain):
    first = lax.broadcasted_iota(jnp.int32, (1, LANES), 1) < HEAD_DIM
    outs = []
    for c in range(y.shape[1] // LANES):
        yc = y[:, c * LANES:(c + 1) * LANES]
        sq = yc * yc
        s0 = jnp.sum(jnp.where(first, sq, 0.0), axis=-1, keepdims=True)
        s1 = jnp.sum(jnp.where(first, 0.0, sq), axis=-1, keepdims=True)
        ms = jnp.where(first, s0, s1) * (1.0 / HEAD_DIM)
        outs.append(yc * lax.rsqrt(ms + RMS_EPS) * gain)
    return outs[0] if len(outs) == 1 else jnp.concatenate(outs, axis=1)


_NT = (((1,), (1,)), ((), ()))


def _with_ones_rows(vt):
    return jnp.concatenate([vt.astype(BF16), jnp.ones((ONES_ROWS, vt.shape[1]), BF16)], axis=0)


def _scores(k, q):
    return lax.dot_general(k, q, _NT, preferred_element_type=F32)


def _pipelined_item(cur, nxt, n_keys, *, q_next, keys_next, vt_cur, bias_next=None,
                    cols_of=None, sink_cur=None, sink_next=None):
    s_cur, m_cur = cur
    s_nxt, m_nxt = nxt
    width = s_nxt.shape[1]
    if cols_of is None:
        cols_of = lambda c0: (0, width)
    m = None if s_cur is None else m_cur[...]
    m_next = [None] * (width // Q_TILE)
    ot = [None] * (width // Q_TILE)
    for c0 in range(0, n_keys, KEY_CHUNK):
        keys = slice(c0, c0 + KEY_CHUNK)
        lo, hi = cols_of(c0)
        blocks = [(g, slice(g * Q_TILE - lo, (g + 1) * Q_TILE - lo))
                  for g in range(lo // Q_TILE, hi // Q_TILE)]
        if q_next is not None:
            s = _scores(keys_next(c0), q_next[lo:hi])
            bias = None if bias_next is None else bias_next(c0)
            if bias is not None:
                s = s + bias
            s_nxt[keys, lo:hi] = s
            m_c = jnp.max(s, axis=0, keepdims=True)
            for g, cols in blocks:
                m_next[g] = (m_c[:, cols] if m_next[g] is None
                             else jnp.maximum(m_next[g], m_c[:, cols]))
        if s_cur is not None:
            p = jnp.exp2(s_cur[keys, lo:hi] - m[:, lo:hi]).astype(BF16)
            part = jnp.dot(vt_cur(c0), p, preferred_element_type=F32)
            for g, cols in blocks:
                ot[g] = part[:, cols] if ot[g] is None else ot[g] + part[:, cols]
    if q_next is not None:
        m_next = jnp.concatenate(m_next, axis=1)
        if sink_next is not None:
            m_next = jnp.maximum(m_next, sink_next)
        m_nxt[...] = m_next
    if s_cur is None:
        return None
    ot = jnp.concatenate(ot, axis=1)
    denom = ot[HEAD_DIM:HEAD_DIM + 1, :]
    if sink_cur is not None:
        denom = denom + jnp.exp2(sink_cur - m)
    return ot[:HEAD_DIM, :] / denom


def _score_slots(n_keys, m):
    return [pltpu.VMEM((n_keys, m), F32), pltpu.VMEM((n_keys, m), F32),
            pltpu.VMEM((2, 1, m), F32)]


def _stack_group(ref, rows, col0, group):
    return jnp.concatenate(
        [ref[rows, col0 + g * HEAD_DIM: col0 + (g + 1) * HEAD_DIM] for g in range(group)], axis=0)


def _sink_row(sink_ref, h0, group, rows):
    return jnp.concatenate(
        [jnp.full((1, rows), sink_ref[h0 + g] * LOG2E, F32) for g in range(group)], axis=1)


def _store_feature_major(ref, y):
    nb, heads, _, seq = ref.shape
    yt = y.T
    for b in range(nb):
        for h in range(heads):
            ref[b, h] = yt[h * HEAD_DIM:(h + 1) * HEAD_DIM, b * seq:(b + 1) * seq].astype(ref.dtype)


def _pre_even_kernel(x_ref, g_ref, sh_ref, sc_ref, w_ref, *rest, is_lat):
    if is_lat:
        cos_ref, slo_ref, shi_ref, qk_ref, va_ref, vb_ref, ya_ref, yb_ref = rest
    else:
        qk_ref, ka_ref, va_ref, kb_ref, vb_ref, ya_ref, yb_ref = rest
    na = A_HEADS * HEAD_DIM
    nb = B_HEADS * HEAD_DIM
    nkv = B_KV_HEADS * HEAD_DIM
    i = pl.program_id(0)

    @pl.when(i == 0)
    def _():
        yb_ref[...] = jnp.zeros_like(yb_ref)

    def step(y_w, y_r):
        h = _adaln(x_ref[...], g_ref[...], sh_ref[0, 0], sc_ref[0, 0]).astype(BF16)

        def proj(c0, c1):
            y_w[:, c0:c1] = jnp.dot(h, w_ref[:, c0:c1], preferred_element_type=F32)

        qk_ref[:, 0:na] = (y_r[:, 0:na] * QK_SCALE).astype(BF16)
        proj(0, na)
        ka = y_r[:, na:2 * na]
        qk_ref[:, na:2 * na] = ka.astype(BF16)
        if not is_lat:
            _store_feature_major(ka_ref, ka)
        proj(na, 2 * na)
        _store_feature_major(va_ref, y_r[:, 2 * na:3 * na])
        proj(2 * na, 3 * na)
        qb = y_r[:, 3 * na:3 * na + nb]
        if is_lat:
            qb = _rope(qb, cos_ref[...], slo_ref[...], shi_ref[...])
        qk_ref[:, 2 * na:2 * na + nb] = (qb * QK_SCALE).astype(BF16)
        proj(3 * na, 3 * na + nb)
        kvb = y_r[:, 3 * na + nb:3 * na + nb + 2 * nkv]
        kb, vb = kvb[:, :nkv], kvb[:, nkv:]
        if is_lat:
            kb_out = _rope(kb, cos_ref[...], slo_ref[...], shi_ref[...])
        else:
            kb_out = kb
            _store_feature_major(kb_ref, kb)
        qk_ref[:, 2 * na + nb:2 * na + nb + nkv] = kb_out.astype(BF16)
        _store_feature_major(vb_ref, vb)
        proj(3 * na + nb, 3 * na + nb + 2 * nkv)

    @pl.when(lax.rem(i, 2) == 0)
    def _():
        step(ya_ref, yb_ref)

    @pl.when(lax.rem(i, 2) == 1)
    def _():
        step(yb_ref, ya_ref)


def _pre_odd_kernel(x_ref, g_ref, sh_ref, sc_ref, w_ref, qn_ref, kn_ref, *rest, is_lat):
    if is_lat:
        cos_ref, slo_ref, shi_ref, qk_ref, vc_ref, ya_ref, yb_ref = rest
    else:
        qk_ref, kc_ref, vc_ref, ya_ref, yb_ref = rest
    nq = C_HEADS * HEAD_DIM
    nkv = C_KV_HEADS * HEAD_DIM
    width = 2 * LANES
    assert nkv == width
    i = pl.program_id(0)

    @pl.when(i == 0)
    def _():
        yb_ref[...] = jnp.zeros_like(yb_ref)

    def step(y_w, y_r):
        h = _adaln(x_ref[...], g_ref[...], sh_ref[0, 0], sc_ref[0, 0]).astype(BF16)
        for c0 in range(0, nq + 2 * nkv, width):
            cols = slice(c0, c0 + width)
            y = y_r[:, cols]
            if c0 < nq:
                y = _head_rms_norm(y, qn_ref[...])
                if is_lat:
                    y = _rope(y, cos_ref[...], slo_ref[...], shi_ref[...])
                qk_ref[:, cols] = (y * QK_SCALE).astype(BF16)
            elif c0 < nq + nkv:
                k = _head_rms_norm(y, kn_ref[...])
                if is_lat:
                    k_out = _rope(k, cos_ref[...], slo_ref[...], shi_ref[...])
                else:
                    k_out = k
                    _store_feature_major(kc_ref, k)
                qk_ref[:, cols] = k_out.astype(BF16)
            else:
                _store_feature_major(vc_ref, y)
            y_w[:, cols] = jnp.dot(h, w_ref[:, cols], preferred_element_type=F32)

    @pl.when(lax.rem(i, 2) == 0)
    def _():
        step(ya_ref, yb_ref)

    @pl.when(lax.rem(i, 2) == 1)
    def _():
        step(yb_ref, ya_ref)


def _pre_attention(x, mods, layer, gain, w, sub, *, odd, is_lat, seq, rope=None,
                   head_gains=None):
    t, d = x.shape
    tm = TOKEN_TILE
    per_seq = max(seq // tm, 1)
    per_tile = max(tm // seq, 1)
    n_tiles = t // tm
    cur = lambda i: jnp.minimum(i, n_tiles - 1)
    done = lambda i: jnp.maximum(i - 1, 0)
    group = (lambda i: 1 + cur(i) // per_seq) if is_lat else (lambda i: 0)
    row = lambda i: (done(i), 0)
    in_specs = [
        pl.BlockSpec((tm, d), lambda i: (cur(i), 0)),
        _resident((1, d)),
        _mod_spec(layer, 0, group),
        _mod_spec(layer, 1, group),
        _resident_layer(w, sub),
    ]
    args = [x, gain.reshape(1, d), mods, mods, w]
    if odd:
        in_specs += [_resident(a.shape) for a in head_gains]
        args += list(head_gains)
    if is_lat:
        in_specs += [pl.BlockSpec((tm, LANES), lambda i: (done(i) % per_seq, 0))] * 3
        args += list(rope)
    if odd:
        n_qk = (C_HEADS + C_KV_HEADS) * HEAD_DIM
        lat_heads, ctx_heads = [C_KV_HEADS], [C_KV_HEADS, C_KV_HEADS]
    else:
        n_qk = (2 * A_HEADS + B_HEADS + B_KV_HEADS) * HEAD_DIM
        lat_heads, ctx_heads = [A_HEADS, B_KV_HEADS], [A_HEADS, A_HEADS, B_KV_HEADS, B_KV_HEADS]
    out_specs = [pl.BlockSpec((tm, n_qk), row)]
    out_shape = [jax.ShapeDtypeStruct((t, n_qk), BF16)]
    for heads in (lat_heads if is_lat else ctx_heads):
        blk = (per_tile, heads, HEAD_DIM, min(tm, seq))
        out_specs.append(pl.BlockSpec(
            blk, lambda i: (done(i) // per_seq, 0, 0, done(i) % per_seq)))
        out_shape.append(jax.ShapeDtypeStruct((t // seq, heads, HEAD_DIM, seq),
                                              BF16 if is_lat else F32))
    body = _pre_odd_kernel if odd else _pre_even_kernel
    return pl.pallas_call(
        functools.partial(body, is_lat=is_lat),
        grid=(n_tiles + 1,),
        in_specs=in_specs,
        out_specs=out_specs,
        out_shape=out_shape,
        scratch_shapes=[pltpu.VMEM((tm, w.shape[2]), F32)] * 2,
        compiler_params=_params(1),
        name=f"pre_{'odd' if odd else 'even'}_{'lat' if is_lat else 'ctx'}",
    )(*args)


def _batch_spec(arr, nb=1):
    nd = arr.ndim
    return pl.BlockSpec((nb,) + arr.shape[1:], lambda b: (b,) + (0,) * (nd - 1))


def _cache_spec(cache, layer):
    blk = (1, 1) + cache.shape[2:]
    return pl.BlockSpec(blk, lambda b: (b, layer, 0, 0, 0))


def _feature_major(cache):
    return cache.transpose(0, 1, 3, 4, 2)


def _cache_keys(ref, h):
    return ref[0, 0, h].T.astype(BF16)


def _ctx_even_attn_kernel(sink_ref, qk_ref, vat_ref, vbt_ref, o_ref,
                          sa0_ref, sa1_ref, ma_ref, sb0_ref, sb1_ref, mb_ref):
    nb = vat_ref.shape[0]
    n = qk_ref.shape[0] // nb
    na = A_HEADS * HEAD_DIM
    group = B_HEADS // B_KV_HEADS
    qb0 = 2 * na
    kb0 = qb0 + B_HEADS * HEAD_DIM

    def rows_of(b, c0=0, size=n):
        if isinstance(b, int):
            return slice(b * n + c0, b * n + c0 + size)
        return pl.ds(pl.multiple_of(b * n + c0, size), size)

    def following(b, i, count):
        if i + 1 < count:
            return b, i + 1
        return (None, 0) if isinstance(b, int) else (b + 1, 0)

    slots_a = ((sa0_ref, ma_ref.at[0]), (sa1_ref, ma_ref.at[1]))

    def item_a(cur, nxt, b, h, b_next, h_next):
        c = h_next * HEAD_DIM
        return _pipelined_item(
            cur, nxt, n,
            q_next=None if b_next is None else qk_ref[rows_of(b_next), c:c + HEAD_DIM],
            keys_next=lambda c0: qk_ref[rows_of(b_next, c0, KEY_CHUNK), na + c:na + c + HEAD_DIM],
            vt_cur=lambda c0: _with_ones_rows(vat_ref[b, h, :, c0:c0 + KEY_CHUNK]))

    item_a((None, None), slots_a[0], None, None, 0, 0)

    def sequence_a(b, carry=0):
        for pair in range(A_HEADS // 2):
            outs = []
            for h in (2 * pair, 2 * pair + 1):
                ot = item_a(slots_a[h % 2], slots_a[(h + 1) % 2], b, h, *following(b, h, A_HEADS))
                outs.append(ot.T)
            o_ref[rows_of(b), pair * LANES:(pair + 1) * LANES] = (
                jnp.concatenate(outs, axis=1).astype(BF16))
        return carry

    lax.fori_loop(0, nb - 1, sequence_a, 0)
    sequence_a(nb - 1)

    slots_b = ((sb0_ref, mb_ref.at[0]), (sb1_ref, mb_ref.at[1]))
    sinks = [_sink_row(sink_ref, kv * group, group, n) for kv in range(B_KV_HEADS)]

    def item_b(cur, nxt, b, kv, b_next, kv_next):
        kcols = slice(kb0 + kv_next * HEAD_DIM, kb0 + (kv_next + 1) * HEAD_DIM)
        ot = _pipelined_item(
            cur, nxt, n,
            q_next=None if b_next is None else _stack_group(
                qk_ref, rows_of(b_next), qb0 + kv_next * group * HEAD_DIM, group),
            keys_next=lambda c0: qk_ref[rows_of(b_next, c0, KEY_CHUNK), kcols],
            vt_cur=lambda c0: _with_ones_rows(vbt_ref[b, kv, :, c0:c0 + KEY_CHUNK]),
            sink_cur=None if kv is None else sinks[kv], sink_next=sinks[kv_next])
        if ot is None:
            return None
        return jnp.concatenate([ot[:, g * n:(g + 1) * n].T for g in range(group)], axis=1)

    item_b((None, None), slots_b[0], None, None, 0, 0)

    def sequence_b(b, carry=0):
        for kv in range(B_KV_HEADS):
            o = item_b(slots_b[kv % 2], slots_b[(kv + 1) % 2], b, kv,
                       *following(b, kv, B_KV_HEADS))
            c0 = na + kv * group * HEAD_DIM
            o_ref[rows_of(b), c0:c0 + group * HEAD_DIM] = o.astype(BF16)
        return carry

    lax.fori_loop(0, nb - 1, sequence_b, 0)
    sequence_b(nb - 1)


def _ctx_even_attention(qk, vat, vbt, sink, seq):
    t, n = qk.shape
    width = (A_HEADS + B_HEADS) * HEAD_DIM
    nb = CTX_BATCHES
    return pl.pallas_call(
        _ctx_even_attn_kernel,
        grid=(t // (nb * seq),),
        in_specs=[pl.BlockSpec(memory_space=pltpu.SMEM),
                  pl.BlockSpec((nb * seq, n), lambda b: (b, 0)),
                  _batch_spec(vat, nb), _batch_spec(vbt, nb)],
        out_specs=pl.BlockSpec((nb * seq, width), lambda b: (b, 0)),
        out_shape=jax.ShapeDtypeStruct((t, width), BF16),
        scratch_shapes=(_score_slots(seq, seq)
                        + _score_slots(seq, B_HEADS // B_KV_HEADS * seq)),
        compiler_params=_params(1),
        name="attn_even_ctx",
    )(sink, qk, vat, vbt)


def _na_bias_tiles(src_ref, h):
    kc = lax.broadcasted_iota(jnp.int32, (GRID_W, LANES), 0)
    qc = lax.rem(lax.broadcasted_iota(jnp.int32, (GRID_W, LANES), 1), GRID_W)
    cs = jnp.clip(qc - NA_WIN_W // 2, 0, GRID_W - NA_WIN_W)
    col_ok = (kc >= cs) & (kc < cs + NA_WIN_W)
    tiles = []
    for i in range(2 * NA_WIN_H):
        src = jnp.broadcast_to(src_ref[h, i:i + 1, :], (GRID_W, LANES))
        tiles.append(jnp.where(col_ok, pltpu.roll(src, 0, 1, stride=1, stride_axis=0), NEG_INF))
    return tiles


def _na_row_window(qr, grid_rows):
    wh = min(NA_WIN_H, grid_rows)
    start = min(max(qr - wh // 2, 0), grid_rows - wh)
    return start, start + wh


def _na_reachable_query_rows(key_rows, grid_rows):
    per_block = Q_TILE // GRID_W
    hit = [qr for qr in range(grid_rows)
           if any(_na_row_window(qr, grid_rows)[0] <= kr < _na_row_window(qr, grid_rows)[1]
                  for kr in key_rows)]
    return min(hit) // per_block * per_block, (max(hit) // per_block + 1) * per_block


def _na_bias_block(tiles, key_rows, q_rows, grid_rows):
    first_row = lax.broadcasted_iota(jnp.int32, (GRID_W, LANES), 1) < GRID_W

    def in_window(kr, qr):
        start, stop = _na_row_window(qr, grid_rows)
        return start <= kr < stop

    strips = []
    for kr in key_rows:
        parts = []
        for qr in range(q_rows.start, q_rows.stop, 2):
            ok0 = in_window(kr, qr)
            ok1 = in_window(kr, qr + 1)
            i = kr - qr + NA_WIN_H - 1
            if ok0 and ok1:
                parts.append(tiles[i])
            elif ok0:
                parts.append(jnp.where(first_row, tiles[i], NEG_INF))
            elif ok1:
                parts.append(jnp.where(first_row, NEG_INF, tiles[i]))
            else:
                parts.append(jnp.full((GRID_W, LANES), NEG_INF, F32))
        strips.append(jnp.concatenate(parts, axis=1))
    return jnp.concatenate(strips, axis=0)


def _lat_a_kernel(src_ref, q_ref, k_ref, vt_ref, ck_ref, cv_ref, o_ref,
                  qs_ref, kall_ref, vtall_ref, os_ref, s0_ref, s1_ref, m_ref):
    n = q_ref.shape[0]
    past = ck_ref.shape[4]
    grid_rows = n // GRID_W
    for h in range(A_HEADS):
        cols = slice(h * HEAD_DIM, (h + 1) * HEAD_DIM)
        qs_ref[h] = q_ref[:, cols]
        kall_ref[h, 0:n, :] = k_ref[:, cols]
        kall_ref[h, n:n + past, :] = _cache_keys(ck_ref, h)
        vtall_ref[h, 0:HEAD_DIM, 0:n] = vt_ref[0, h]
        vtall_ref[h, 0:HEAD_DIM, n:n + past] = cv_ref[0, 0, h].astype(BF16)
        vtall_ref[h, HEAD_DIM:, :] = jnp.ones((ONES_ROWS, n + past), BF16)

    slots = ((s0_ref, m_ref.at[0]), (s1_ref, m_ref.at[1]))

    def key_rows(c0):
        return range(c0 // GRID_W, (c0 + KEY_CHUNK) // GRID_W)

    def query_rows(c0):
        if c0 >= n:
            return range(0, grid_rows)
        return range(*_na_reachable_query_rows(key_rows(c0), grid_rows))

    def step(cur, nxt, h, h_next):
        tiles = None if h_next is None else _na_bias_tiles(src_ref, h_next)

        def bias(c0):
            if c0 >= n:
                return None
            return _na_bias_block(tiles, key_rows(c0), query_rows(c0), grid_rows)

        return _pipelined_item(
            cur, nxt, n + past, q_next=None if h_next is None else qs_ref[h_next],
            keys_next=lambda c0: kall_ref[h_next, c0:c0 + KEY_CHUNK, :],
            vt_cur=lambda c0: vtall_ref[h, :, c0:c0 + KEY_CHUNK], bias_next=bias,
            cols_of=lambda c0: (query_rows(c0).start * GRID_W, query_rows(c0).stop * GRID_W))

    step((None, None), slots[0], None, 0)

    def head_pair(p, carry=0):
        h0 = 2 * p
        following = None if isinstance(p, int) else h0 + 2
        os_ref[h0] = step(slots[0], slots[1], h0, h0 + 1).T.astype(BF16)
        os_ref[h0 + 1] = step(slots[1], slots[0], h0 + 1, following).T.astype(BF16)
        return carry

    lax.fori_loop(0, A_HEADS // 2 - 1, head_pair, 0)
    head_pair(A_HEADS // 2 - 1)
    for pair in range(A_HEADS // 2):
        o_ref[:, pair * LANES:(pair + 1) * LANES] = jnp.concatenate(
            [os_ref[2 * pair], os_ref[2 * pair + 1]], axis=1)


def _lat_a_attention(qk, vt, cache_k, cache_v, e, bias_src, seq):
    t = qk.shape[0]
    past = cache_k.shape[4]
    na = A_HEADS * HEAD_DIM
    return pl.pallas_call(
        _lat_a_kernel,
        grid=(t // seq,),
        in_specs=[
            _resident(bias_src.shape),
            pl.BlockSpec((seq, na), lambda b: (b, 0)),
            pl.BlockSpec((seq, na), lambda b: (b, 1)),
            _batch_spec(vt),
            _cache_spec(cache_k, e),
            _cache_spec(cache_v, e),
        ],
        out_specs=pl.BlockSpec((seq, na), lambda b: (b, 0)),
        out_shape=jax.ShapeDtypeStruct((t, na), BF16),
        scratch_shapes=[pltpu.VMEM((A_HEADS, seq, HEAD_DIM), BF16),
                        pltpu.VMEM((A_HEADS, seq + past, HEAD_DIM), BF16),
                        pltpu.VMEM((A_HEADS, HEAD_DIM + ONES_ROWS, seq + past), BF16),
                        pltpu.VMEM((A_HEADS, seq, HEAD_DIM), BF16)]
                       + _score_slots(seq + past, seq),
        compiler_params=_params(1),
        name="attn_even_lat_a",
    )(bias_src, qk, qk, vt, cache_k, cache_v)


def _band_window_start(j, n):
    return min(max(Q_TILE * j - B_WINDOW, 0), n - 2 * Q_TILE)


def _band_bias(n, group):
    j = np.arange(n // Q_TILE)[:, None, None]
    lo = np.clip(Q_TILE * j - B_WINDOW, 0, n - 2 * Q_TILE)
    kpos = lo + np.arange(2 * Q_TILE)[None, :, None]
    qpos = Q_TILE * j + (np.arange(group * Q_TILE) % Q_TILE)[None, None, :]
    return jnp.asarray(np.where(np.abs(qpos - kpos) <= B_WINDOW, 0.0, NEG_INF), F32)


def _lat_b_kernel(sink_ref, band_ref, q_ref, k_ref, vt_ref, ck_ref, cv_ref, o_ref,
                  ckeys_ref, cvt_ref, vtw_ref, s0_ref, s1_ref, m_ref):
    n = q_ref.shape[0]
    past = ck_ref.shape[4]
    group = B_HEADS // B_KV_HEADS
    win = 2 * Q_TILE
    n_blocks = n // Q_TILE
    for kv in range(B_KV_HEADS):
        ckeys_ref[kv] = _cache_keys(ck_ref, kv)
        cvt_ref[kv] = _with_ones_rows(cv_ref[0, 0, kv])
        for j in range(n_blocks):
            lo = _band_window_start(j, n)
            vtw_ref[j, kv] = _with_ones_rows(vt_ref[0, kv, :, lo:lo + win])
    sinks = [_sink_row(sink_ref, kv * group, group, Q_TILE) for kv in range(B_KV_HEADS)]

    def rows_of(j):
        if isinstance(j, int):
            return slice(j * Q_TILE, (j + 1) * Q_TILE)
        return pl.ds(pl.multiple_of(j * Q_TILE, Q_TILE), Q_TILE)

    def window_rows(j, c0):
        if isinstance(j, int):
            lo = _band_window_start(j, n) + c0
            return slice(lo, lo + KEY_CHUNK)
        lo = jnp.clip(Q_TILE * j - B_WINDOW, 0, n - win) + c0
        return pl.ds(pl.multiple_of(lo, B_WINDOW), KEY_CHUNK)

    slots = ((s0_ref, m_ref.at[0]), (s1_ref, m_ref.at[1]))

    def item(cur, nxt, j, kv, j_next, kv_next):
        kcols = slice(kv_next * HEAD_DIM, (kv_next + 1) * HEAD_DIM)

        def keys_next(c0):
            if c0 < win:
                return k_ref[window_rows(j_next, c0), kcols]
            return ckeys_ref[kv_next, c0 - win:c0 - win + KEY_CHUNK, :]

        def vt_cur(c0):
            if c0 < win:
                return vtw_ref[j, kv, :, c0:c0 + KEY_CHUNK]
            return cvt_ref[kv, :, c0 - win:c0 - win + KEY_CHUNK]

        ot = _pipelined_item(
            cur, nxt, win + past,
            q_next=None if j_next is None else _stack_group(
                q_ref, rows_of(j_next), kv_next * group * HEAD_DIM, group),
            keys_next=keys_next, vt_cur=vt_cur,
            bias_next=lambda c0: band_ref[j_next, c0:c0 + KEY_CHUNK, :] if c0 < win else None,
            sink_cur=None if kv is None else sinks[kv], sink_next=sinks[kv_next])
        if ot is None:
            return None
        return jnp.concatenate(
            [ot[:, g * Q_TILE:(g + 1) * Q_TILE].T for g in range(group)], axis=1)

    item((None, None), slots[0], None, None, 0, 0)

    def q_block(j, carry=0):
        final = isinstance(j, int)
        for kv in range(B_KV_HEADS):
            last = kv + 1 == B_KV_HEADS
            o = item(slots[kv % 2], slots[(kv + 1) % 2], j, kv,
                     (None if final else j + 1) if last else j, 0 if last else kv + 1)
            c0 = kv * group * HEAD_DIM
            o_ref[rows_of(j), c0:c0 + group * HEAD_DIM] = o.astype(BF16)
        return carry

    lax.fori_loop(0, n_blocks - 1, q_block, 0)
    q_block(n_blocks - 1)


def _lat_b_attention(qk, vt, cache_k, cache_v, e, sink, seq):
    t = qk.shape[0]
    past = cache_k.shape[4]
    na = A_HEADS * HEAD_DIM
    nb = B_HEADS * HEAD_DIM
    nkv = B_KV_HEADS * HEAD_DIM
    group = B_HEADS // B_KV_HEADS
    band = _band_bias(seq, group)
    return pl.pallas_call(
        _lat_b_kernel,
        grid=(t // seq,),
        in_specs=[
            pl.BlockSpec(memory_space=pltpu.SMEM),
            _resident(band.shape),
            pl.BlockSpec((seq, nb), lambda b: (b, 2 * na // nb)),
            pl.BlockSpec((seq, nkv), lambda b: (b, (2 * na + nb) // nkv)),
            _batch_spec(vt),
            _cache_spec(cache_k, e),
            _cache_spec(cache_v, e),
        ],
        out_specs=pl.BlockSpec((seq, nb), lambda b: (b, 0)),
        out_shape=jax.ShapeDtypeStruct((t, nb), BF16),
        scratch_shapes=[pltpu.VMEM((B_KV_HEADS, past, HEAD_DIM), BF16),
                        pltpu.VMEM((B_KV_HEADS, HEAD_DIM + ONES_ROWS, past), BF16),
                        pltpu.VMEM((seq // Q_TILE, B_KV_HEADS, HEAD_DIM + ONES_ROWS, 2 * Q_TILE),
                                   BF16)]
                       + _score_slots(2 * Q_TILE + past, group * Q_TILE),
        compiler_params=_params(1),
        name="attn_even_lat_b",
    )(sink, band, qk, qk, vt, cache_k, cache_v)


def _ctx_odd_attn_kernel(qk_ref, vt_ref, o_ref, s0_ref, s1_ref, m_ref):
    nb = vt_ref.shape[0]
    n = qk_ref.shape[0] // nb
    group = C_HEADS // C_KV_HEADS
    nq = C_HEADS * HEAD_DIM

    def rows_of(b, c0=0, size=n):
        if isinstance(b, int):
            return slice(b * n + c0, b * n + c0 + size)
        return pl.ds(pl.multiple_of(b * n + c0, size), size)

    slots = ((s0_ref, m_ref.at[0]), (s1_ref, m_ref.at[1]))

    def item(cur, nxt, b, kv, b_next, kv_next):
        kcols = slice(nq + kv_next * HEAD_DIM, nq + (kv_next + 1) * HEAD_DIM)
        ot = _pipelined_item(
            cur, nxt, n,
            q_next=None if b_next is None else _stack_group(
                qk_ref, rows_of(b_next), kv_next * group * HEAD_DIM, group),
            keys_next=lambda c0: qk_ref[rows_of(b_next, c0, KEY_CHUNK), kcols],
            vt_cur=lambda c0: _with_ones_rows(vt_ref[b, kv, :, c0:c0 + KEY_CHUNK]))
        if ot is None:
            return None
        return jnp.concatenate([ot[:, g * n:(g + 1) * n].T for g in range(group)], axis=1)

    item((None, None), slots[0], None, None, 0, 0)

    def sequence(b, carry=0):
        final = isinstance(b, int)
        for kv in range(C_KV_HEADS):
            last = kv + 1 == C_KV_HEADS
            o = item(slots[kv % 2], slots[(kv + 1) % 2], b, kv,
                     (None if final else b + 1) if last else b, 0 if last else kv + 1)
            c0 = kv * group * HEAD_DIM
            o_ref[rows_of(b), c0:c0 + group * HEAD_DIM] = o.astype(BF16)
        return carry

    lax.fori_loop(0, nb - 1, sequence, 0)
    sequence(nb - 1)


def _ctx_odd_attention(qk, vt, seq):
    t, n = qk.shape
    width = C_HEADS * HEAD_DIM
    nb = CTX_BATCHES
    return pl.pallas_call(
        _ctx_odd_attn_kernel,
        grid=(t // (nb * seq),),
        in_specs=[pl.BlockSpec((nb * seq, n), lambda b: (b, 0)), _batch_spec(vt, nb)],
        out_specs=pl.BlockSpec((nb * seq, width), lambda b: (b, 0)),
        out_shape=jax.ShapeDtypeStruct((t, width), BF16),
        scratch_shapes=_score_slots(seq, C_HEADS // C_KV_HEADS * seq),
        compiler_params=_params(1),
        name="attn_odd_ctx",
    )(qk, vt)


def _lat_c_kernel(qk_ref, vt_ref, ck_ref, cv_ref, o_ref, kall_ref, vtall_ref, s0_ref, s1_ref,
                  m_ref):
    n = qk_ref.shape[0]
    past = ck_ref.shape[4]
    group = C_HEADS // C_KV_HEADS
    nq = C_HEADS * HEAD_DIM
    n_blocks = n // Q_TILE
    for kv in range(C_KV_HEADS):
        kall_ref[kv, 0:past, :] = _cache_keys(ck_ref, kv)
        kall_ref[kv, past:past + n, :] = qk_ref[:, nq + kv * HEAD_DIM:nq + (kv + 1) * HEAD_DIM]
        vtall_ref[kv, 0:HEAD_DIM, 0:past] = cv_ref[0, 0, kv].astype(BF16)
        vtall_ref[kv, 0:HEAD_DIM, past:past + n] = vt_ref[0, kv]
        vtall_ref[kv, HEAD_DIM:, :] = jnp.ones((ONES_ROWS, past + n), BF16)

    def rows_of(j):
        if isinstance(j, int):
            return slice(j * Q_TILE, (j + 1) * Q_TILE)
        return pl.ds(pl.multiple_of(j * Q_TILE, Q_TILE), Q_TILE)

    slots = ((s0_ref, m_ref.at[0]), (s1_ref, m_ref.at[1]))

    def item(cur, nxt, kv, j_next, kv_next):
        ot = _pipelined_item(
            cur, nxt, past + n,
            q_next=None if j_next is None else _stack_group(
                qk_ref, rows_of(j_next), kv_next * group * HEAD_DIM, group),
            keys_next=lambda c0: kall_ref[kv_next, c0:c0 + KEY_CHUNK, :],
            vt_cur=lambda c0: vtall_ref[kv, :, c0:c0 + KEY_CHUNK])
        if ot is None:
            return None
        return jnp.concatenate(
            [ot[:, g * Q_TILE:(g + 1) * Q_TILE].T for g in range(group)], axis=1)

    item((None, None), slots[0], None, 0, 0)

    def q_block(j, carry=0):
        final = isinstance(j, int)
        for kv in range(C_KV_HEADS):
            last = kv + 1 == C_KV_HEADS
            o = item(slots[kv % 2], slots[(kv + 1) % 2], kv,
                     (None if final else j + 1) if last else j, 0 if last else kv + 1)
            c0 = kv * group * HEAD_DIM
            o_ref[rows_of(j), c0:c0 + group * HEAD_DIM] = o.astype(BF16)
        return carry

    lax.fori_loop(0, n_blocks - 1, q_block, 0)
    q_block(n_blocks - 1)


def _lat_c_attention(qk, vt, cache_k, cache_v, o, seq):
    t, n = qk.shape
    past = cache_k.shape[4]
    width = C_HEADS * HEAD_DIM
    return pl.pallas_call(
        _lat_c_kernel,
        grid=(t // seq,),
        in_specs=[
            pl.BlockSpec((seq, n), lambda b: (b, 0)),
            _batch_spec(vt),
            _cache_spec(cache_k, o),
            _cache_spec(cache_v, o),
        ],
        out_specs=pl.BlockSpec((seq, width), lambda b: (b, 0)),
        out_shape=jax.ShapeDtypeStruct((t, width), BF16),
        scratch_shapes=[pltpu.VMEM((C_KV_HEADS, past + seq, HEAD_DIM), BF16),
                        pltpu.VMEM((C_KV_HEADS, HEAD_DIM + ONES_ROWS, past + seq), BF16)]
                       + _score_slots(past + seq, C_HEADS // C_KV_HEADS * Q_TILE),
        compiler_params=_params(1),
        name="attn_odd_lat",
    )(qk, vt, cache_k, cache_v)


def _post_kernel(*refs, n_parts, final):
    o_refs = refs[:n_parts]
    (x_ref, wo_ref, g1_ref, sh_ref, sc_ref, g2_ref, gain_ref, wgu_ref, wd_ref) = refs[n_parts:n_parts + 9]
    rest = refs[n_parts + 9:]
    if final:
        fg_ref, out_ref, act_ref = rest
    else:
        out_ref, act_ref = rest
    mix = None
    r0 = 0
    for o_ref in o_refs:
        kk = o_ref.shape[1]
        part = jnp.dot(o_ref[...], wo_ref[r0:r0 + kk, :], preferred_element_type=F32)
        mix = part if mix is None else mix + part
        r0 += kk
    x1 = x_ref[...] + g1_ref[0, 0] * mix
    h = _adaln(x1, gain_ref[...], sh_ref[0, 0], sc_ref[0, 0]).astype(BF16)
    d_ff = wd_ref.shape[0]
    for j in range(d_ff // FF_CHUNK):
        c0 = j * FF_CHUNK
        gate = jnp.dot(h, wgu_ref[:, c0:c0 + FF_CHUNK], preferred_element_type=F32)
        up = jnp.dot(h, wgu_ref[:, d_ff + c0:d_ff + c0 + FF_CHUNK], preferred_element_type=F32)
        act_ref[:, c0:c0 + FF_CHUNK] = (gate * jax.nn.sigmoid(gate) * up).astype(BF16)
    ffn = jnp.dot(act_ref[...], wd_ref[...], preferred_element_type=F32)
    x2 = x1 + g2_ref[0, 0] * ffn
    if final:
        ms = jnp.mean(x2 * x2, axis=-1, keepdims=True)
        x2 = (x2 * lax.rsqrt(ms + RMS_EPS)) * fg_ref[...]
    out_ref[...] = x2


def _post_attention(o_parts, x, mods, layer, gain, w_out, sub, w_gu, w_down, *, is_lat, seq,
                    final_gain=None):
    t, d = x.shape
    tm = TOKEN_TILE
    per_seq = max(seq // tm, 1)
    group = (lambda i: 1 + i // per_seq) if is_lat else (lambda i: 0)
    row = lambda i: (i, 0)
    d_ff = w_down.shape[1]
    final = final_gain is not None
    in_specs = [pl.BlockSpec((tm, o.shape[1]), row) for o in o_parts]
    in_specs += [
        pl.BlockSpec((tm, d), row),
        _resident_layer(w_out, sub),
        _mod_spec(layer, 2, group),
        _mod_spec(layer, 3, group),
        _mod_spec(layer, 4, group),
        _mod_spec(layer, 5, group),
        _resident((1, d)),
        _resident_layer(w_gu, layer),
        _resident_layer(w_down, layer),
    ]
    args = list(o_parts) + [x, w_out, mods, mods, mods, mods, gain.reshape(1, d), w_gu, w_down]
    if final:
        in_specs.append(_resident((1, d)))
        args.append(final_gain.reshape(1, d))
    return pl.pallas_call(
        functools.partial(_post_kernel, n_parts=len(o_parts), final=final),
        grid=(t // tm,),
        in_specs=in_specs,
        out_specs=pl.BlockSpec((tm, d), row),
        out_shape=jax.ShapeDtypeStruct((t, d), F32),
        scratch_shapes=[pltpu.VMEM((tm, d_ff), BF16)],
        compiler_params=_params(1),
        name=f"post_{'lat' if is_lat else 'ctx'}{'_final' if final else ''}",
    )(*args)


def _rope_tables(n):
    t = np.arange(n)
    row = (t // GRID_W).astype(np.float32)
    col = (t % GRID_W).astype(np.float32)
    half = HEAD_DIM // 2
    inv_freq = np.float32(ROPE_THETA) ** (-np.arange(0, half, 2, dtype=np.float32) / np.float32(half))
    lane = np.arange(LANES)
    in_head = lane % HEAD_DIM
    pos = np.where((in_head < half)[None, :], row[:, None], col[:, None])
    ang = (pos * inv_freq[in_head % (half // 2)][None, :]).astype(np.float32)
    first = ((in_head % half) < half // 2)[None, :]
    cos = np.cos(ang)
    sin = np.sin(ang)
    zero = np.float32(0.0)
    return tuple(jnp.asarray(a, F32) for a in
                 (cos, np.where(first, -sin, zero), np.where(first, zero, sin)))


def _na_bias_sources(rpb):
    h, _, nb = rpb.shape
    w = NA_WIN_W - 1
    rp = jnp.pad(rpb[:, :, ::-1] * LOG2E, ((0, 0), (1, 1), (0, 0)))
    this, prev = rp[:, 1:], rp[:, :-1]
    z = jnp.zeros((h, 2 * NA_WIN_H, LANES // 2 - nb), F32)
    return jnp.concatenate([this[:, :, w:], z, prev, z, this[:, :, :w]], axis=-1)


def _state(y):
    return y.transpose(0, 3, 1, 2)


def kernel(x_prompt, x_sample, cache_a_k, cache_a_v, cache_b_k, cache_b_v, cache_c_k, cache_c_v,
           c, c_ctx, norm_gain, w_mod, b_mod, w_in_even, w_out_even, rpb_a, sink_b,
           w_in_odd, w_out_odd, q_norm_c, k_norm_c, w_gate_up, w_down, final_gain):
    batch, seq, d = x_prompt.shape
    dec_batch, dec_seq, _ = x_sample.shape
    depth = w_mod.shape[0]

    cvec = jnp.concatenate(
        [c_ctx[None, :], c, jnp.zeros((MOD_GROUPS - 1 - dec_batch, d), F32)], axis=0)
    mods = _modulation(cvec, w_mod, b_mod).reshape(depth, MOD_GROUPS, 1, 6 * d)
    rope = _rope_tables(dec_seq)
    w_in = {False: w_in_even.astype(BF16), True: w_in_odd.astype(BF16)}
    w_out = {False: w_out_even.astype(BF16), True: w_out_odd.astype(BF16)}
    w_gu = w_gate_up.astype(BF16)
    w_dn = w_down.astype(BF16)

    ctx = x_prompt.reshape(batch * seq, d)
    lat = x_sample.reshape(dec_batch * dec_seq, d)
    states = {name: [] for name in ("a_k", "a_v", "b_k", "b_v", "c_k", "c_v")}

    for layer in range(depth):
        odd = layer % 2 == 1
        sub = layer // 2
        gain1, gain2 = norm_gain[layer, 0], norm_gain[layer, 1]
        pre = functools.partial(_pre_attention, mods=mods, layer=layer, gain=gain1, w=w_in[odd],
                                sub=sub, odd=odd)
        if not odd:
            qk_c, ka, va, kb, vb = pre(ctx, is_lat=False, seq=seq)
            qk_l, vat_l, vbt_l = pre(lat, is_lat=True, seq=dec_seq, rope=rope)
            for name, y in (("a_k", ka), ("a_v", va), ("b_k", kb), ("b_v", vb)):
                states[name].append(_state(y))
            o_ctx = [_ctx_even_attention(qk_c, va, vb, sink_b[sub], seq)]
            o_lat = [
                _lat_a_attention(qk_l, vat_l, _feature_major(cache_a_k), _feature_major(cache_a_v),
                                 sub, _na_bias_sources(rpb_a[sub]), dec_seq),
                _lat_b_attention(qk_l, vbt_l, _feature_major(cache_b_k), _feature_major(cache_b_v),
                                 sub, sink_b[sub], dec_seq),
            ]
        else:
            per = LANES // HEAD_DIM
            head_gains = (jnp.tile(q_norm_c[sub], per).reshape(1, LANES),
                          jnp.tile(k_norm_c[sub], per).reshape(1, LANES))
            qk_c, kc, vc = pre(ctx, is_lat=False, seq=seq, head_gains=head_gains)
            qk_l, vct_l = pre(lat, is_lat=True, seq=dec_seq, rope=rope, head_gains=head_gains)
            states["c_k"].append(_state(kc))
            states["c_v"].append(_state(vc))
            o_ctx = [_ctx_odd_attention(qk_c, vc, seq)]
            o_lat = [_lat_c_attention(qk_l, vct_l, _feature_major(cache_c_k),
                                      _feature_major(cache_c_v), sub, dec_seq)]
        fg = final_gain if layer == depth - 1 else None
        post = functools.partial(_post_attention, mods=mods, layer=layer, gain=gain2,
                                 w_out=w_out[odd], sub=sub, w_gu=w_gu, w_down=w_dn, final_gain=fg)
        ctx = post(o_ctx, ctx, is_lat=False, seq=seq)
        lat = post(o_lat, lat, is_lat=True, seq=dec_seq)

    return (ctx.reshape(batch, seq, d), lat.reshape(dec_batch, dec_seq, d),
            jnp.stack(states["a_k"], axis=1), jnp.stack(states["a_v"], axis=1),
            jnp.stack(states["b_k"], axis=1), jnp.stack(states["b_v"], axis=1),
            jnp.stack(states["c_k"], axis=1), jnp.stack(states["c_v"], axis=1))
```
